```python
import math
import jax, jax.numpy as jnp
from jax import lax
import numpy as np

D_MODEL = 1024
BATCH = 8
SEQ = 8192
DEPTH = 1

CHUNK = 64
EPS = 1e-6
ATT_HEADS = 8
ATT_KV_HEADS = 2
ATT_HEAD_DIM = 64
IDX_HEADS = 8
IDX_HEAD_DIM = 64
TOPK_MAX = 256
Q_BLOCK = 128
MLSTM_HEADS = 4
MLSTM_QK_DIM = 64
MLSTM_V_DIM = 128
CONV_WIDTH = 4
MIX_A = ATT_HEADS * ATT_HEAD_DIM
MIX_B = MLSTM_HEADS * MLSTM_V_DIM
D_FF = 2816
SPLIT_WIDTHS = (
    ATT_HEADS * ATT_HEAD_DIM,
    ATT_KV_HEADS * ATT_HEAD_DIM,
    ATT_KV_HEADS * ATT_HEAD_DIM,
    IDX_HEADS * IDX_HEAD_DIM,
    IDX_HEAD_DIM,
    IDX_HEADS,
    MIX_B,
    MIX_B,
    MLSTM_HEADS,
    MLSTM_HEADS,
    MIX_B,
    D_MODEL,
    D_MODEL,
)
D_IN = sum(SPLIT_WIDTHS)

kernel_name = "hybrid_dsa_mlstm_macaron_block"


def rms_norm(x, g):
    xf = x.astype(jnp.float32)
    y = xf * lax.rsqrt(jnp.mean(xf * xf, axis=-1, keepdims=True) + EPS)
    return (y * g.astype(jnp.float32)).astype(x.dtype)


def swiglu_ffn(x, g, w_gate, w_up, w_down):
    h = rms_norm(x, g)
    return (jax.nn.silu(h @ w_gate) * (h @ w_up)) @ w_down


def split_columns(p):
    parts, off = [], 0
    for w in SPLIT_WIDTHS:
        parts.append(p[..., off:off + w])
        off += w
    return parts


def dsa_attention(q, k, v, iq, ik, iw):
    B, S = q.shape[0], q.shape[1]
    k_sel = min(TOPK_MAX, S // 4)
    nb = S // Q_BLOCK
    reps = ATT_HEADS // ATT_KV_HEADS
    idx_scale = (IDX_HEAD_DIM ** -0.5) * (IDX_HEADS ** -0.5)
    att_scale = ATT_HEAD_DIM ** -0.5
    key_pos = jnp.arange(S)

    qb = q.reshape(B, nb, Q_BLOCK, ATT_KV_HEADS, reps, ATT_HEAD_DIM).transpose(1, 0, 2, 3, 4, 5)
    iqb = iq.reshape(B, nb, Q_BLOCK, IDX_HEADS, IDX_HEAD_DIM).transpose(1, 0, 2, 3, 4)
    iwb = iw.reshape(B, nb, Q_BLOCK, IDX_HEADS).transpose(1, 0, 2, 3)
    ids = jnp.arange(nb, dtype=jnp.int32)

    def block(args):
        q_blk, iq_blk, iw_blk, bi = args
        t = bi * Q_BLOCK + jnp.arange(Q_BLOCK)
        logits = jnp.einsum('bqhd,bsd->bqhs', iq_blk, ik)
        score = jnp.einsum('bqhs,bqh->bqs', jax.nn.relu(logits), iw_blk).astype(jnp.float32) * idx_scale
        limit = (t // CHUNK + 1) * CHUNK
        allowed = key_pos[None, :] < limit[:, None]
        score = jnp.where(allowed[None], score, -jnp.inf)
        top_s, top_i = lax.top_k(score, k_sel)
        valid = jnp.isfinite(top_s)
        kg = jax.vmap(lambda kb, ib: kb[ib])(k, top_i)
        vg = jax.vmap(lambda vb, ib: vb[ib])(v, top_i)
        s = jnp.einsum('bqgrd,bqkgd->bqgrk', q_blk, kg).astype(jnp.float32) * att_scale
        s = jnp.where(valid[:, :, None, None, :], s, -jnp.inf)
        p = jax.nn.softmax(s, axis=-1).astype(vg.dtype)
        return jnp.einsum('bqgrk,bqkgd->bqgrd', p, vg)

    o = lax.map(block, (qb, iqb, iwb, ids))
    return o.transpose(1, 0, 2, 3, 4, 5).reshape(B, S, MIX_A)


def mlstm_branch(xm, vm, ig, fg, o_pre, conv_w, conv_b, w_mq, w_mk, b_i, b_f, head_g):
    B, S, _ = xm.shape
    H, L = MLSTM_HEADS, CHUNK
    nc = S // L
    xpad = jnp.pad(xm, ((0, 0), (CONV_WIDTH - 1, 0), (0, 0)))
    xc = conv_b + xpad[:, 0:S] * conv_w[0]
    for j in range(1, CONV_WIDTH):
        xc = xc + xpad[:, j:j + S] * conv_w[j]
    xc = jax.nn.silu(xc).reshape(B, S, H, MLSTM_V_DIM)
    q = jnp.einsum('bshc,hcd->bshd', xc, w_mq).astype(jnp.float32) * (MLSTM_QK_DIM ** -0.5)
    k = jnp.einsum('bshc,hcd->bshd', xc, w_mk).astype(jnp.float32)
    v = vm.reshape(B, S, H, MLSTM_V_DIM).astype(jnp.float32)
    log_i = (ig + b_i).astype(jnp.float32)
    log_f = jax.nn.log_sigmoid((fg + b_f).astype(jnp.float32))

    def to_chunks(a):
        a = a.reshape((B, nc, L) + a.shape[2:])
        perm = (1, 0, 3, 2) + tuple(range(4, a.ndim))
        return a.transpose(perm)

    tril = jnp.tril(jnp.ones((L, L), dtype=bool))

    def step(carry, inp):
        C, n, m = carry
        qc, kc, vc, ic, fc = inp
        b = jnp.cumsum(fc, axis=-1)
        D = b[..., :, None] - b[..., None, :] + ic[..., None, :]
        D = jnp.where(tril, D, -jnp.inf)
        inter = b + m[..., None]
        m_t = jnp.maximum(jnp.max(D, axis=-1), inter)
        W = jnp.einsum('bhtd,bhsd->bhts', qc, kc) * jnp.exp(D - m_t[..., None])
        w_int = jnp.exp(inter - m_t)
        num = jnp.einsum('bhts,bhsv->bhtv', W, vc) + w_int[..., None] * jnp.einsum('bhtd,bhdv->bhtv', qc, C)
        den = jnp.sum(W, axis=-1) + w_int * jnp.einsum('bhtd,bhd->bht', qc, n)
        h = num / jnp.maximum(jnp.abs(den), jnp.exp(-m_t))[..., None]
        bL = b[..., -1]
        wk = bL[..., None] - b + ic
        m_new = jnp.maximum(bL + m, jnp.max(wk, axis=-1))
        ws = jnp.exp(wk - m_new[..., None])
        wc = jnp.exp(bL + m - m_new)
        C_new = wc[..., None, None] * C + jnp.einsum('bhs,bhsd,bhsv->bhdv', ws, kc, vc)
        n_new = wc[..., None] * n + jnp.einsum('bhs,bhsd->bhd', ws, kc)
        return (C_new, n_new, m_new), h

    init = (jnp.zeros((B, H, MLSTM_QK_DIM, MLSTM_V_DIM), jnp.float32),
            jnp.zeros((B, H, MLSTM_QK_DIM), jnp.float32),
            jnp.zeros((B, H), jnp.float32))
    _, hs = lax.scan(step, init, (to_chunks(q), to_chunks(k), to_chunks(v), to_chunks(log_i), to_chunks(log_f)))
    h = hs.transpose(1, 0, 3, 2, 4).reshape(B, S, H, MLSTM_V_DIM)
    h = rms_norm(h, head_g.reshape(H, MLSTM_V_DIM)).reshape(B, S, MIX_B).astype(xm.dtype)
    return h * jax.nn.sigmoid(o_pre)


def setup_inputs(seed: int = 0) -> dict:
    key = jax.random.key(seed)
    ks = jax.random.split(key, 32)
    f32 = jnp.float32

    def nrm(k, shape, fan_in):
        return jax.random.normal(k, shape, f32) * (fan_in ** -0.5)

    def gain(k, shape):
        return 1.0 + 0.02 * jax.random.normal(k, shape, f32)

    Ld = DEPTH
    return {
        "x": jax.random.normal(ks[0], (BATCH, SEQ, D_MODEL), f32),
        "ffn1_norm": gain(ks[1], (Ld, D_MODEL)),
        "ffn1_w_gate": nrm(ks[2], (Ld, D_MODEL, D_FF), D_MODEL),
        "ffn1_w_up": nrm(ks[3], (Ld, D_MODEL, D_FF), D_MODEL),
        "ffn1_w_down": nrm(ks[4], (Ld, D_FF, D_MODEL), D_FF),
        "mix_norm": gain(ks[5], (Ld, D_MODEL)),
        "w_in": nrm(ks[6], (Ld, D_MODEL, D_IN), D_MODEL),
        "q_norm": gain(ks[7], (Ld, ATT_HEAD_DIM)),
        "k_norm": gain(ks[8], (Ld, ATT_HEAD_DIM)),
        "idx_k_norm": gain(ks[9], (Ld, IDX_HEAD_DIM)),
        "conv_w": nrm(ks[10], (Ld, CONV_WIDTH, MIX_B), CONV_WIDTH),
        "conv_b": 0.02 * jax.random.normal(ks[11], (Ld, MIX_B), f32),
        "w_mq": nrm(ks[12], (Ld, MLSTM_HEADS, MLSTM_V_DIM, MLSTM_QK_DIM), MLSTM_V_DIM),
        "w_mk": nrm(ks[13], (Ld, MLSTM_HEADS, MLSTM_V_DIM, MLSTM_QK_DIM), MLSTM_V_DIM),
        "b_i": 0.1 * jax.random.normal(ks[14], (Ld, MLSTM_HEADS), f32),
        "b_f": 3.0 + 0.1 * jax.random.normal(ks[15], (Ld, MLSTM_HEADS), f32),
        "m_head_norm": gain(ks[16], (Ld, MIX_B)),
        "w_proj_a": nrm(ks[17], (Ld, MIX_A, D_MODEL), MIX_A),
        "w_proj_b": nrm(ks[18], (Ld, MIX_B, D_MODEL), MIX_B),
        "w_out": nrm(ks[19], (Ld, D_MODEL, D_MODEL), D_MODEL),
        "ffn2_norm": gain(ks[20], (Ld, D_MODEL)),
        "ffn2_w_gate": nrm(ks[21], (Ld, D_MODEL, D_FF), D_MODEL),
        "ffn2_w_up": nrm(ks[22], (Ld, D_MODEL, D_FF), D_MODEL),
        "ffn2_w_down": nrm(ks[23], (Ld, D_FF, D_MODEL), D_FF),
    }


def reference(x, ffn1_norm, ffn1_w_gate, ffn1_w_up, ffn1_w_down, mix_norm, w_in,
              q_norm, k_norm, idx_k_norm, conv_w, conv_b, w_mq, w_mk, b_i, b_f,
              m_head_norm, w_proj_a, w_proj_b, w_out, ffn2_norm, ffn2_w_gate,
              ffn2_w_up, ffn2_w_down):
    B, S, _ = x.shape
    for l in range(DEPTH):
        x = x + 0.5 * swiglu_ffn(x, ffn1_norm[l], ffn1_w_gate[l], ffn1_w_up[l], ffn1_w_down[l])
        h = rms_norm(x, mix_norm[l])
        (aq, ak, av, iq, ik, iw, mx, mv, mi, mf, mo, ga, gb) = split_columns(h @ w_in[l])
        q = rms_norm(aq.reshape(B, S, ATT_HEADS, ATT_HEAD_DIM), q_norm[l])
        k = rms_norm(ak.reshape(B, S, ATT_KV_HEADS, ATT_HEAD_DIM), k_norm[l])
        v = av.reshape(B, S, ATT_KV_HEADS, ATT_HEAD_DIM)
        iq = iq.reshape(B, S, IDX_HEADS, IDX_HEAD_DIM)
        ik = rms_norm(ik, idx_k_norm[l])
        ya = dsa_attention(q, k, v, iq, ik, iw)
        yb = mlstm_branch(mx, mv, mi, mf, mo, conv_w[l], conv_b[l], w_mq[l], w_mk[l],
                          b_i[l], b_f[l], m_head_norm[l])
        merged = jax.nn.sigmoid(ga) * (ya @ w_proj_a[l]) + jax.nn.sigmoid(gb) * (yb @ w_proj_b[l])
        x = x + merged @ w_out[l]
        x = x + 0.5 * swiglu_ffn(x, ffn2_norm[l], ffn2_w_gate[l], ffn2_w_up[l], ffn2_w_down[l])
    return x
```

```python
import functools

import jax
import jax.numpy as jnp
from jax import lax
from jax.experimental import pallas as pl
from jax.experimental.pallas import tpu as pltpu

F32 = jnp.float32
BF16 = jnp.bfloat16

D_MODEL = 1024
D_FF = 2816
EPS = 1e-6
ATT_HEADS = 8
ATT_KV_HEADS = 2
ATT_HEAD_DIM = 64
IDX_HEADS = 8
IDX_HEAD_DIM = 64
TOPK_MAX = 256
Q_BLOCK = 128
CHUNK = 64
MLSTM_HEADS = 4
MLSTM_QK_DIM = 64
MLSTM_V_DIM = 128
CONV_WIDTH = 4
MIX_A = ATT_HEADS * ATT_HEAD_DIM
MIX_B = MLSTM_HEADS * MLSTM_V_DIM

LANES = 128
VMEM_LIMIT = 56 * 1024 * 1024
INT_MIN = -2147483648
NEG_BIG = -1e30

COL_Q, COL_IQ, COL_K, COL_IK, COL_V = 0, 512, 1024, 1152, 1280
A_WIDTH = 1408
G_WIDTH = 128
M_WIDTH = 3 * MIX_B
T_WIDTH = 2 * D_MODEL
W_TOTAL = A_WIDTH + G_WIDTH + M_WIDTH + T_WIDTH


def _rms(x, g):
    return x * lax.rsqrt(jnp.mean(x * x, axis=-1, keepdims=True) + EPS) * g


def _dot(a, b):
    return jnp.dot(a, b, preferred_element_type=F32)


def _dot_nt(a, b):
    return lax.dot_general(a, b, (((1,), (1,)), ((), ())), preferred_element_type=F32)


def _dot_tn(a, b):
    return lax.dot_general(a, b, (((0,), (0,)), ((), ())), preferred_element_type=F32)


FFN_TM = 512
FFN_TF = 1408


def _ffn_kernel(x_ref, g_ref, wg_ref, wu_ref, wd_ref, o_ref, h_ref, acc_ref):
    j = pl.program_id(1)

    @pl.when(j == 0)
    def _():
        h_ref[...] = _rms(x_ref[...], g_ref[...]).astype(BF16)
        acc_ref[...] = jnp.zeros_like(acc_ref)

    h = h_ref[...]
    a = _dot(h, wg_ref[...])
    u = _dot(h, wu_ref[...])
    act = (a * jax.nn.sigmoid(a) * u).astype(BF16)
    acc_ref[...] += _dot(act, wd_ref[...])

    @pl.when(j == pl.num_programs(1) - 1)
    def _():
        o_ref[...] = x_ref[...] + 0.5 * acc_ref[...]


def _ffn(x2d, g, wg, wu, wd):
    m = x2d.shape[0]
    return pl.pallas_call(
        _ffn_kernel,
        grid=(m // FFN_TM, D_FF // FFN_TF),
        in_specs=[
            pl.BlockSpec((FFN_TM, D_MODEL), lambda i, j: (i, 0)),
            pl.BlockSpec((1, D_MODEL), lambda i, j: (0, 0)),
            pl.BlockSpec((D_MODEL, FFN_TF), lambda i, j: (0, j)),
            pl.BlockSpec((D_MODEL, FFN_TF), lambda i, j: (0, j)),
            pl.BlockSpec((FFN_TF, D_MODEL), lambda i, j: (j, 0)),
        ],
        out_specs=pl.BlockSpec((FFN_TM, D_MODEL), lambda i, j: (i, 0)),
        out_shape=jax.ShapeDtypeStruct((m, D_MODEL), F32),
        scratch_shapes=[pltpu.VMEM((FFN_TM, D_MODEL), BF16), pltpu.VMEM((FFN_TM, D_MODEL), F32)],
        compiler_params=pltpu.CompilerParams(
            dimension_semantics=("parallel", "arbitrary"), vmem_limit_bytes=VMEM_LIMIT),
        name="ffn",
    )(x2d, g.reshape(1, D_MODEL), wg, wu, wd)


PROJ_TM = 256


def _proj_kernel(x_ref, g_ref, w_ref, bdq_ref, bdk_ref, gq_ref, gk_ref,
                 a_ref, g_out_ref, m_ref, t_ref):
    h = _rms(x_ref[...], g_ref[...]).astype(BF16)

    def head_norm(p, bd_ref, gain_ref):
        ss = _dot((p * p).astype(BF16), bd_ref[...]) * (1.0 / ATT_HEAD_DIM)
        return p * lax.rsqrt(ss + EPS) * gain_ref[...]

    pq = _dot(h, w_ref[:, COL_Q:COL_IQ])
    a_ref[:, COL_Q:COL_IQ] = head_norm(pq, bdq_ref, gq_ref).astype(BF16)
    a_ref[:, COL_IQ:COL_K] = _dot(h, w_ref[:, COL_IQ:COL_K]).astype(BF16)
    pk = _dot(h, w_ref[:, COL_K:COL_V])
    a_ref[:, COL_K:COL_V] = head_norm(pk, bdk_ref, gk_ref).astype(BF16)
    a_ref[:, COL_V:A_WIDTH] = _dot(h, w_ref[:, COL_V:A_WIDTH]).astype(BF16)
    o = A_WIDTH
    g_out_ref[...] = _dot(h, w_ref[:, o:o + G_WIDTH])
    o += G_WIDTH
    for c in range(3):
        m_ref[:, c * MIX_B:(c + 1) * MIX_B] = _dot(h, w_ref[:, o + c * MIX_B:o + (c + 1) * MIX_B])
    o += M_WIDTH
    for c in range(2):
        t_ref[:, c * D_MODEL:(c + 1) * D_MODEL] = _dot(h, w_ref[:, o + c * D_MODEL:o + (c + 1) * D_MODEL])


def _proj(x2d, g, w_all, bdq, bdk, gq, gk):
    m = x2d.shape[0]
    const = lambda i: (0, 0)
    row = lambda i: (i, 0)
    return pl.pallas_call(
        _proj_kernel,
        grid=(m // PROJ_TM,),
        in_specs=[
            pl.BlockSpec((PROJ_TM, D_MODEL), row),
            pl.BlockSpec((1, D_MODEL), const),
            pl.BlockSpec((D_MODEL, W_TOTAL), const),
            pl.BlockSpec((512, 512), const),
            pl.BlockSpec((256, 256), const),
            pl.BlockSpec((1, 512), const),
            pl.BlockSpec((1, 256), const),
        ],
        out_specs=[
            pl.BlockSpec((PROJ_TM, A_WIDTH), row),
            pl.BlockSpec((PROJ_TM, G_WIDTH), row),
            pl.BlockSpec((PROJ_TM, M_WIDTH), row),
            pl.BlockSpec((PROJ_TM, T_WIDTH), row),
        ],
        out_shape=[
            jax.ShapeDtypeStruct((m, A_WIDTH), BF16),
            jax.ShapeDtypeStruct((m, G_WIDTH), F32),
            jax.ShapeDtypeStruct((m, M_WIDTH), F32),
            jax.ShapeDtypeStruct((m, T_WIDTH), F32),
        ],
        compiler_params=pltpu.CompilerParams(
            dimension_semantics=("parallel",), vmem_limit_bytes=VMEM_LIMIT),
        name="proj",
    )(x2d, g.reshape(1, D_MODEL), w_all, bdq, bdk, gq, gk)


DSA_KB = 256


def _dsa_kernel(q_ref, iq_ref, k_ref, ik_ref, vt_ref, iwt_ref, o_ref, keys_ref, acc_ref,
                *, seq, k_sel, idx_scale):
    kb_sz = DSA_KB
    bi = pl.program_id(1)
    nkb = (bi * Q_BLOCK + Q_BLOCK + kb_sz - 1) // kb_sz
    lane = lax.broadcasted_iota(jnp.int32, (1, LANES), 1)
    limit = bi * Q_BLOCK + jnp.where(lane < CHUNK, CHUNK, 2 * CHUNK)
    row_iota = lax.broadcasted_iota(jnp.int32, (kb_sz, LANES), 0)

    iq = iq_ref[...]
    iq_all = jnp.concatenate(
        [iq[:, h * IDX_HEAD_DIM:(h + 1) * IDX_HEAD_DIM] for h in range(IDX_HEADS)], axis=0)
    w = iwt_ref[...]

    def score_block(kb, _):
        off = pl.multiple_of(kb * kb_sz, kb_sz)
        ikb = ik_ref[pl.ds(off, kb_sz), :][:, :IDX_HEAD_DIM]
        logits = _dot_nt(ikb, iq_all)
        sc = jnp.zeros((kb_sz, LANES), F32)
        for h in range(IDX_HEADS):
            sc = sc + jnp.maximum(logits[:, h * LANES:(h + 1) * LANES], 0.0) * w[h:h + 1, :]
        sc = sc * idx_scale
        bits = pltpu.bitcast(sc, jnp.int32)
        key = bits ^ ((bits >> 31) & 0x7FFFFFFF)
        key = jnp.where(sc == 0.0, 0, key)
        key = jnp.where(off + row_iota < limit, key, INT_MIN)
        keys_ref[pl.ds(off, kb_sz), :] = key
        return 0

    lax.fori_loop(0, nkb, score_block, 0)

    def count(pred):
        def body(kb, c):
            off = pl.multiple_of(kb * kb_sz, kb_sz)
            ind = jnp.where(pred(keys_ref[pl.ds(off, kb_sz), :], off + row_iota), 1, 0)
            return c + jnp.sum(ind.reshape(kb_sz // 8, 8, LANES), axis=0)
        c8 = lax.fori_loop(0, nkb, body, jnp.zeros((8, LANES), jnp.int32))
        return jnp.sum(c8, axis=0, keepdims=True)

    def bit_body(i, t_u):
        cand_u = t_u | jnp.left_shift(jnp.int32(1), 31 - i)
        cand = cand_u ^ INT_MIN
        c = count(lambda kk, pos: kk >= cand)
        return jnp.where(c >= k_sel, cand_u, t_u)

    t_u = lax.fori_loop(0, 32, bit_body, jnp.zeros((1, LANES), jnp.int32))
    thr = t_u ^ INT_MIN
    few = thr == INT_MIN
    thr = jnp.maximum(thr, INT_MIN + 1)

    c_gt = count(lambda kk, pos: kk > thr)
    c_ge = count(lambda kk, pos: kk >= thr)
    need = k_sel - c_gt
    excess = jnp.where(few, 0, c_ge - k_sel)
    any_excess = jnp.max(excess) > 0
    pos_bits = seq.bit_length() - 1

    def tie_cut():
        def jbit(i, cut):
            cand = cut | jnp.left_shift(jnp.int32(1), pos_bits - 1 - i)
            c = count(lambda kk, pos: (kk == thr) & (pos < cand))
            return jnp.where(c <= need, cand, cut)
        cut = lax.fori_loop(0, pos_bits, jbit, jnp.zeros((1, LANES), jnp.int32))
        return jnp.where(excess > 0, cut, seq)

    cut = lax.cond(any_excess, tie_cut, lambda: jnp.full((1, LANES), seq, jnp.int32))

    q = q_ref[...]
    reps = ATT_HEADS // ATT_KV_HEADS
    q_g = [jnp.concatenate(
        [q[:, (g * reps + r) * ATT_HEAD_DIM:(g * reps + r + 1) * ATT_HEAD_DIM] for r in range(reps)],
        axis=0) for g in range(ATT_KV_HEADS)]
    acc_ref[...] = jnp.zeros_like(acc_ref)
    width = reps * LANES

    def att_block(kb, carry):
        off = pl.multiple_of(kb * kb_sz, kb_sz)
        kk = keys_ref[pl.ds(off, kb_sz), :]
        sel = (kk > thr) | ((kk == thr) & (off + row_iota < cut))
        bias = jnp.where(sel, 0.0, NEG_BIG)
        bias = jnp.concatenate([bias] * reps, axis=1)
        kblk = k_ref[pl.ds(off, kb_sz), :]
        out = []
        for g in range(ATT_KV_HEADS):
            m_old, l_old = carry[2 * g], carry[2 * g + 1]
            s = _dot_nt(kblk[:, g * ATT_HEAD_DIM:(g + 1) * ATT_HEAD_DIM], q_g[g]) + bias
            m_new = jnp.maximum(m_old, jnp.max(s, axis=0, keepdims=True))
            alpha = jnp.exp(m_old - m_new)
            p = jnp.exp(s - m_new)
            l_new = alpha * l_old + jnp.sum(p, axis=0, keepdims=True)
            vt = vt_ref[g * ATT_HEAD_DIM:(g + 1) * ATT_HEAD_DIM, pl.ds(off, kb_sz)]
            acc_ref[g] = alpha * acc_ref[g] + _dot(vt, p.astype(BF16))
            out += [m_new, l_new]
        return tuple(out)

    init = (jnp.full((1, width), NEG_BIG, F32), jnp.zeros((1, width), F32)) * ATT_KV_HEADS
    fin = lax.fori_loop(0, nkb, att_block, init)

    pieces = []
    for g in range(ATT_KV_HEADS):
        o = acc_ref[g] / fin[2 * g + 1]
        pieces += [o[:, r * LANES:(r + 1) * LANES] for r in range(reps)]
    o_ref[...] = jnp.concatenate(pieces, axis=0).T.astype(o_ref.dtype)


def _dsa(a3, vt, iwt):
    b, s, _ = a3.shape
    nb = s // Q_BLOCK
    k_sel = min(TOPK_MAX, s // 4)
    idx_scale = (IDX_HEAD_DIM ** -0.5) * (IDX_HEADS ** -0.5)
    kern = functools.partial(_dsa_kernel, seq=s, k_sel=k_sel, idx_scale=idx_scale)
    return pl.pallas_call(
        kern,
        grid=(b, nb),
        in_specs=[
            pl.BlockSpec((None, Q_BLOCK, 512), lambda bb, i: (bb, i, COL_Q // 512)),
            pl.BlockSpec((None, Q_BLOCK, 512), lambda bb, i: (bb, i, COL_IQ // 512)),
            pl.BlockSpec((None, s, LANES), lambda bb, i: (bb, 0, COL_K // LANES)),
            pl.BlockSpec((None, s, LANES), lambda bb, i: (bb, 0, COL_IK // LANES)),
            pl.BlockSpec((None, LANES, s), lambda bb, i: (bb, 0, 0)),
            pl.BlockSpec((None, None, IDX_HEADS, LANES), lambda bb, i: (bb, i, 0, 0)),
        ],
        out_specs=pl.BlockSpec((None, Q_BLOCK, MIX_A), lambda bb, i: (bb, i, 0)),
        out_shape=jax.ShapeDtypeStruct((b, s, MIX_A), BF16),
        scratch_shapes=[
            pltpu.VMEM((s, LANES), jnp.int32),
            pltpu.VMEM((ATT_KV_HEADS, ATT_HEAD_DIM, (ATT_HEADS // ATT_KV_HEADS) * LANES), F32),
        ],
        compiler_params=pltpu.CompilerParams(
            dimension_semantics=("parallel", "arbitrary"), vmem_limit_bytes=VMEM_LIMIT),
        name="dsa",
    )(a3, a3, a3, a3, vt, iwt)


ML_L = 128
ML_TS = 512
ML_PAD = 8


def _mlstm_kernel(mx_ref, mv_ref, mo_ref, gc_ref, gr_ref, cw_ref, cb_ref, wq_ref, wk_ref,
                  bcol_ref, brow_ref, hg_ref, o_ref,
                  xbuf_ref, q_ref, k_ref, c_ref, n_ref, m_ref):
    t_idx = pl.program_id(1)
    nh, dk, dv, L = MLSTM_HEADS, MLSTM_QK_DIM, MLSTM_V_DIM, ML_L

    @pl.when(t_idx == 0)
    def _():
        xbuf_ref[0:ML_PAD, :] = jnp.zeros((ML_PAD, MIX_B), F32)
        c_ref[...] = jnp.zeros_like(c_ref)
        n_ref[...] = jnp.zeros_like(n_ref)
        m_ref[...] = jnp.zeros_like(m_ref)

    xbuf_ref[ML_PAD:ML_PAD + ML_TS, :] = mx_ref[...]
    xc = cb_ref[...] + jnp.zeros((ML_TS, MIX_B), F32)
    for j in range(CONV_WIDTH):
        s0 = ML_PAD - (CONV_WIDTH - 1) + j
        xc = xc + xbuf_ref[s0:s0 + ML_TS, :] * cw_ref[j:j + 1, :]
    xbuf_ref[0:ML_PAD, :] = mx_ref[ML_TS - ML_PAD:ML_TS, :]
    xc = (xc * jax.nn.sigmoid(xc)).astype(BF16)
    for h in range(nh):
        xh = xc[:, h * dv:(h + 1) * dv]
        q_ref[h] = (_dot(xh, wq_ref[h]) * (dk ** -0.5)).astype(BF16)
        k_ref[h] = _dot(xh, wk_ref[h])

    gc = gc_ref[...]
    li_c = gc[:, 8:8 + nh] + bcol_ref[0:1, 0:nh]
    lf_c = jax.nn.log_sigmoid(gc[:, 8 + nh:8 + 2 * nh] + bcol_ref[0:1, nh:2 * nh])
    gr = gr_ref[...]
    li_r = gr[0:nh, :] + brow_ref[0:nh, :]
    lf_r = jax.nn.log_sigmoid(gr[nh:2 * nh, :] + brow_ref[nh:2 * nh, :])

    ri = lax.broadcasted_iota(jnp.int32, (L, L), 0)
    ci = lax.broadcasted_iota(jnp.int32, (L, L), 1)
    causal = ri >= ci
    tril = jnp.where(causal, 1.0, 0.0).astype(F32)
    triu = jnp.where(ri <= ci, 1.0, 0.0).astype(F32)
    hi = lax.Precision.HIGHEST

    for c in range(ML_TS // L):
        r0 = c * L
        b_c = jnp.dot(tril, lf_c[r0:r0 + L, :], precision=hi, preferred_element_type=F32)
        b_r = jnp.dot(lf_r[:, r0:r0 + L], triu, precision=hi, preferred_element_type=F32)
        outs = []
        for h in range(nh):
            bc = b_c[:, h:h + 1]
            br = b_r[h:h + 1, :]
            ir = li_r[h:h + 1, r0:r0 + L]
            ic = li_c[r0:r0 + L, h:h + 1]
            m_prev = m_ref[h]
            d = jnp.where(causal, bc - br + ir, NEG_BIG)
            inter = bc + m_prev
            m_t = jnp.maximum(jnp.max(d, axis=-1, keepdims=True), inter)
            qh = q_ref[h, r0:r0 + L, :]
            kh = k_ref[h, r0:r0 + L, :]
            vh = mv_ref[r0:r0 + L, h * dv:(h + 1) * dv].astype(BF16)
            wt = _dot_nt(qh, kh.astype(BF16)) * jnp.exp(d - m_t)
            w_int = jnp.exp(inter - m_t)
            num = _dot(wt.astype(BF16), vh) + w_int * _dot(qh, c_ref[h].astype(BF16))
            qn = jnp.sum(qh.astype(F32) * n_ref[h], axis=-1, keepdims=True)
            den = jnp.sum(wt, axis=-1, keepdims=True) + w_int * qn
            outs.append(num / jnp.maximum(jnp.abs(den), jnp.exp(-m_t)))
            b_last = bc[L - 1:L, :]
            wk_r = b_last - br + ir
            m_new = jnp.maximum(b_last + m_prev, jnp.max(wk_r, axis=-1, keepdims=True))
            ws_c = jnp.exp(b_last - bc + ic - m_new)
            wc = jnp.exp(b_last + m_prev - m_new)
            kw = kh * ws_c
            c_ref[h] = wc * c_ref[h] + _dot_tn(kw.astype(BF16), vh)
            n_ref[h] = wc * n_ref[h] + jnp.sum(kw, axis=0, keepdims=True)
            m_ref[h] = m_new
        hh = jnp.concatenate(
            [_rms(outs[h], hg_ref[0:1, h * dv:(h + 1) * dv]) for h in range(nh)], axis=1)
        o_ref[r0:r0 + L, :] = (hh * jax.nn.sigmoid(mo_ref[r0:r0 + L, :])).astype(o_ref.dtype)


def _mlstm(m3, g3, gt, conv_w, conv_b, wq, wk, bcol, brow, head_g):
    b, s, _ = m3.shape
    nh = MLSTM_HEADS
    tile = lambda c: pl.BlockSpec((None, ML_TS, MIX_B), lambda bb, t: (bb, t, c))
    const2 = lambda shape: pl.BlockSpec(shape, lambda bb, t: (0, 0))
    const3 = lambda shape: pl.BlockSpec(shape, lambda bb, t: (0, 0, 0))
    return pl.pallas_call(
        _mlstm_kernel,
        grid=(b, s // ML_TS),
        in_specs=[
            tile(0), tile(1), tile(2),
            pl.BlockSpec((None, ML_TS, G_WIDTH), lambda bb, t: (bb, t, 0)),
            pl.BlockSpec((None, 2 * nh, ML_TS), lambda bb, t: (bb, 0, t)),
            const2((CONV_WIDTH, MIX_B)), const2((1, MIX_B)),
            const3((nh, MLSTM_V_DIM, MLSTM_QK_DIM)), const3((nh, MLSTM_V_DIM, MLSTM_QK_DIM)),
            const2((1, 2 * nh)), const2((2 * nh, 1)), const2((1, MIX_B)),
        ],
        out_specs=pl.BlockSpec((None, ML_TS, MIX_B), lambda bb, t: (bb, t, 0)),
        out_shape=jax.ShapeDtypeStruct((b, s, MIX_B), BF16),
        scratch_shapes=[
            pltpu.VMEM((ML_PAD + ML_TS, MIX_B), F32),
            pltpu.VMEM((nh, ML_TS, MLSTM_QK_DIM), BF16),
            pltpu.VMEM((nh, ML_TS, MLSTM_QK_DIM), F32),
            pltpu.VMEM((nh, MLSTM_QK_DIM, MLSTM_V_DIM), F32),
            pltpu.VMEM((nh, 1, MLSTM_QK_DIM), F32),
            pltpu.VMEM((nh, 1, 1), F32),
        ],
        compiler_params=pltpu.CompilerParams(
            dimension_semantics=("parallel", "arbitrary"), vmem_limit_bytes=VMEM_LIMIT),
        name="mlstm",
    )(m3, m3, m3, g3, gt, conv_w, conv_b, wq, wk, bcol, brow, head_g)


MERGE_TM = 512


def _merge_kernel(x_ref, ya_ref, yb_ref, ga_ref, gb_ref, wa_ref, wb_ref, wo_ref, o_ref):
    merged = (jax.nn.sigmoid(ga_ref[...]) * _dot(ya_ref[...], wa_ref[...])
              + jax.nn.sigmoid(gb_ref[...]) * _dot(yb_ref[...], wb_ref[...]))
    o_ref[...] = x_ref[...] + _dot(merged.astype(BF16), wo_ref[...])


def _merge(x2d, ya, yb, gates, wa, wb, wo):
    m = x2d.shape[0]
    row = lambda i: (i, 0)
    const = lambda i: (0, 0)
    return pl.pallas_call(
        _merge_kernel,
        grid=(m // MERGE_TM,),
        in_specs=[
            pl.BlockSpec((MERGE_TM, D_MODEL), row),
            pl.BlockSpec((MERGE_TM, MIX_A), row),
            pl.BlockSpec((MERGE_TM, MIX_B), row),
            pl.BlockSpec((MERGE_TM, D_MODEL), lambda i: (i, 0)),
            pl.BlockSpec((MERGE_TM, D_MODEL), lambda i: (i, 1)),
            pl.BlockSpec((MIX_A, D_MODEL), const),
            pl.BlockSpec((MIX_B, D_MODEL), const),
            pl.BlockSpec((D_MODEL, D_MODEL), const),
        ],
        out_specs=pl.BlockSpec((MERGE_TM, D_MODEL), row),
        out_shape=jax.ShapeDtypeStruct((m, D_MODEL), F32),
        compiler_params=pltpu.CompilerParams(
            dimension_semantics=("parallel",), vmem_limit_bytes=VMEM_LIMIT),
        name="merge",
    )(x2d, ya, yb, gates, gates, wa, wb, wo)


def _block_diag_ones(n, blk):
    r = jnp.arange(n) // blk
    return (r[:, None] == r[None, :]).astype(BF16)


def _arrange_w_in(w_in):
    widths = (512, 128, 128, 512, 64, 8, 512, 512, 4, 4, 512, 1024, 1024)
    parts, off = [], 0
    for wd in widths:
        parts.append(w_in[:, off:off + wd])
        off += wd
    aq, ak, av, iq, ik, iw, mx, mv, mi, mf, mo, ga, gb = parts
    z = lambda n: jnp.zeros((w_in.shape[0], n), w_in.dtype)
    cols = [aq, iq, ak, ik, z(64), av,
            iw, mi, mf, z(G_WIDTH - 16),
            mx, mv, mo, ga, gb]
    return jnp.concatenate(cols, axis=1).astype(BF16)


def kernel(x, ffn1_norm, ffn1_w_gate, ffn1_w_up, ffn1_w_down, mix_norm, w_in, q_norm, k_norm,
           idx_k_norm, conv_w, conv_b, w_mq, w_mk, b_i, b_f, m_head_norm, w_proj_a, w_proj_b,
           w_out, ffn2_norm, ffn2_w_gate, ffn2_w_up, ffn2_w_down):
    b, s, _ = x.shape
    m = b * s
    nb = s // Q_BLOCK
    nh = MLSTM_HEADS
    x2d = x.reshape(m, D_MODEL)
    for l in range(ffn1_norm.shape[0]):
        x2d = _ffn(x2d, ffn1_norm[l], ffn1_w_gate[l].astype(BF16), ffn1_w_up[l].astype(BF16),
                   ffn1_w_down[l].astype(BF16))

        gq = (jnp.tile(q_norm[l], ATT_HEADS) * (ATT_HEAD_DIM ** -0.5)).reshape(1, 512)
        gk = jnp.concatenate([jnp.tile(k_norm[l], ATT_KV_HEADS), idx_k_norm[l],
                              jnp.zeros((64,), F32)]).reshape(1, 256)
        a2d, g2d, m2d, t2d = _proj(x2d, mix_norm[l], _arrange_w_in(w_in[l]),
                                   _block_diag_ones(512, ATT_HEAD_DIM),
                                   _block_diag_ones(256, ATT_HEAD_DIM), gq, gk)

        a3 = a2d.reshape(b, s, A_WIDTH)
        vt = jnp.swapaxes(a3[:, :, COL_V:COL_V + ATT_KV_HEADS * ATT_HEAD_DIM], 1, 2)
        g3 = g2d.reshape(b, s, G_WIDTH)
        iwt = jnp.swapaxes(g3[:, :, 0:IDX_HEADS].reshape(b, nb, Q_BLOCK, IDX_HEADS), 2, 3)
        ya = _dsa(a3, vt, iwt)

        gt = jnp.swapaxes(g3[:, :, IDX_HEADS:IDX_HEADS + 2 * nh], 1, 2)
        bias = jnp.concatenate([b_i[l], b_f[l]])
        yb = _mlstm(m2d.reshape(b, s, M_WIDTH), g3, gt, conv_w[l], conv_b[l].reshape(1, MIX_B),
                    w_mq[l].astype(BF16), w_mk[l].astype(BF16),
                    bias.reshape(1, 2 * nh), bias.reshape(2 * nh, 1),
                    m_head_norm[l].reshape(1, MIX_B))

        x2d = _merge(x2d, ya.reshape(m, MIX_A), yb.reshape(m, MIX_B), t2d,
                     w_proj_a[l].astype(BF16), w_proj_b[l].astype(BF16), w_out[l].astype(BF16))

        x2d = _ffn(x2d, ffn2_norm[l], ffn2_w_gate[l].astype(BF16), ffn2_w_up[l].astype(BF16),
                   ffn2_w_down[l].astype(BF16))
    return x2d.reshape(b, s, D_MODEL)
```

```python
import functools

import jax
import jax.numpy as jnp
from jax import lax
from jax.experimental import pallas as pl
from jax.experimental.pallas import tpu as pltpu

F32 = jnp.float32
BF16 = jnp.bfloat16

D_MODEL = 1024
D_FF = 2816
EPS = 1e-6
ATT_HEADS = 8
ATT_KV_HEADS = 2
ATT_HEAD_DIM = 64
IDX_HEADS = 8
IDX_HEAD_DIM = 64
TOPK_MAX = 256
Q_BLOCK = 128
CHUNK = 64
MLSTM_HEADS = 4
MLSTM_QK_DIM = 64
MLSTM_V_DIM = 128
CONV_WIDTH = 4
MIX_A = ATT_HEADS * ATT_HEAD_DIM
MIX_B = MLSTM_HEADS * MLSTM_V_DIM

LANES = 128
VMEM_LIMIT = 56 * 1024 * 1024
INT_MIN = -2147483648
NEG_BIG = -1e30
LOG2_E = 1.4426950408889634

COL_Q, COL_IQ, COL_K, COL_IK, COL_V = 0, 512, 1024, 1152, 1280
A_WIDTH = 1408
G_WIDTH = 128
M_WIDTH = 3 * MIX_B
T_WIDTH = 2 * D_MODEL
W_TOTAL = A_WIDTH + G_WIDTH + M_WIDTH + T_WIDTH


def _rms(x, g):
    return x * lax.rsqrt(jnp.mean(x * x, axis=-1, keepdims=True) + EPS) * g


def _dot(a, b):
    return jnp.dot(a, b, preferred_element_type=F32)


def _dot_nt(a, b):
    return lax.dot_general(a, b, (((1,), (1,)), ((), ())), preferred_element_type=F32)


def _dot_tn(a, b):
    return lax.dot_general(a, b, (((0,), (0,)), ((), ())), preferred_element_type=F32)


FFN_TM = 512
FFN_TF = 1408


def _ffn_kernel(x_ref, g_ref, wg_ref, wu_ref, wd_ref, o_ref, h_ref, acc_ref):
    j = pl.program_id(1)

    @pl.when(j == 0)
    def _():
        h_ref[...] = _rms(x_ref[...], g_ref[...]).astype(BF16)
        acc_ref[...] = jnp.zeros_like(acc_ref)

    h = h_ref[...]
    a = _dot(h, wg_ref[...])
    u = _dot(h, wu_ref[...])
    act = (a * jax.nn.sigmoid(a) * u).astype(BF16)
    acc_ref[...] += _dot(act, wd_ref[...])

    @pl.when(j == pl.num_programs(1) - 1)
    def _():
        o_ref[...] = x_ref[...] + 0.5 * acc_ref[...]


def _ffn(x2d, g, wg, wu, wd):
    m = x2d.shape[0]
    return pl.pallas_call(
        _ffn_kernel,
        grid=(m // FFN_TM, D_FF // FFN_TF),
        in_specs=[
            pl.BlockSpec((FFN_TM, D_MODEL), lambda i, j: (i, 0)),
            pl.BlockSpec((1, D_MODEL), lambda i, j: (0, 0)),
            pl.BlockSpec((D_MODEL, FFN_TF), lambda i, j: (0, j)),
            pl.BlockSpec((D_MODEL, FFN_TF), lambda i, j: (0, j)),
            pl.BlockSpec((FFN_TF, D_MODEL), lambda i, j: (j, 0)),
        ],
        out_specs=pl.BlockSpec((FFN_TM, D_MODEL), lambda i, j: (i, 0)),
        out_shape=jax.ShapeDtypeStruct((m, D_MODEL), F32),
        scratch_shapes=[pltpu.VMEM((FFN_TM, D_MODEL), BF16), pltpu.VMEM((FFN_TM, D_MODEL), F32)],
        compiler_params=pltpu.CompilerParams(
            dimension_semantics=("parallel", "arbitrary"), vmem_limit_bytes=VMEM_LIMIT),
        name="ffn",
    )(x2d, g.reshape(1, D_MODEL), wg, wu, wd)


PROJ_TM = 256


def _proj_kernel(x_ref, g_ref, w_ref, bdq_ref, bdk_ref, gq_ref, gk_ref,
                 a_ref, g_out_ref, m_ref, t_ref):
    h = _rms(x_ref[...], g_ref[...]).astype(BF16)

    def head_norm(p, bd_ref, gain_ref):
        ss = _dot((p * p).astype(BF16), bd_ref[...]) * (1.0 / ATT_HEAD_DIM)
        return p * lax.rsqrt(ss + EPS) * gain_ref[...]

    pq = _dot(h, w_ref[:, COL_Q:COL_IQ])
    a_ref[:, COL_Q:COL_IQ] = head_norm(pq, bdq_ref, gq_ref).astype(BF16)
    a_ref[:, COL_IQ:COL_K] = _dot(h, w_ref[:, COL_IQ:COL_K]).astype(BF16)
    pk = _dot(h, w_ref[:, COL_K:COL_V])
    a_ref[:, COL_K:COL_V] = head_norm(pk, bdk_ref, gk_ref).astype(BF16)
    a_ref[:, COL_V:A_WIDTH] = _dot(h, w_ref[:, COL_V:A_WIDTH]).astype(BF16)
    o = A_WIDTH
    g_out_ref[...] = _dot(h, w_ref[:, o:o + G_WIDTH])
    o += G_WIDTH
    for c in range(3):
        m_ref[:, c * MIX_B:(c + 1) * MIX_B] = _dot(h, w_ref[:, o + c * MIX_B:o + (c + 1) * MIX_B])
    o += M_WIDTH
    for c in range(2):
        t_ref[:, c * D_MODEL:(c + 1) * D_MODEL] = _dot(h, w_ref[:, o + c * D_MODEL:o + (c + 1) * D_MODEL])


def _proj(x2d, g, w_all, bdq, bdk, gq, gk):
    m = x2d.shape[0]
    const = lambda i: (0, 0)
    row = lambda i: (i, 0)
    return pl.pallas_call(
        _proj_kernel,
        grid=(m // PROJ_TM,),
        in_specs=[
            pl.BlockSpec((PROJ_TM, D_MODEL), row),
            pl.BlockSpec((1, D_MODEL), const),
            pl.BlockSpec((D_MODEL, W_TOTAL), const),
            pl.BlockSpec((512, 512), const),
            pl.BlockSpec((256, 256), const),
            pl.BlockSpec((1, 512), const),
            pl.BlockSpec((1, 256), const),
        ],
        out_specs=[
            pl.BlockSpec((PROJ_TM, A_WIDTH), row),
            pl.BlockSpec((PROJ_TM, G_WIDTH), row),
            pl.BlockSpec((PROJ_TM, M_WIDTH), row),
            pl.BlockSpec((PROJ_TM, T_WIDTH), row),
        ],
        out_shape=[
            jax.ShapeDtypeStruct((m, A_WIDTH), BF16),
            jax.ShapeDtypeStruct((m, G_WIDTH), F32),
            jax.ShapeDtypeStruct((m, M_WIDTH), F32),
            jax.ShapeDtypeStruct((m, T_WIDTH), F32),
        ],
        compiler_params=pltpu.CompilerParams(
            dimension_semantics=("parallel",), vmem_limit_bytes=VMEM_LIMIT),
        name="proj",
    )(x2d, g.reshape(1, D_MODEL), w_all, bdq, bdk, gq, gk)


DSA_KB = 512
DSA_FIXED_PASSES = 12


def _heads_to_lanes(x, n_heads, dh):
    xf = x.astype(F32)
    return jnp.concatenate([xf[:, h * dh:(h + 1) * dh].T for h in range(n_heads)], axis=1).astype(BF16)


def _dsa_kernel(q_ref, iq_ref, k_ref, ik_ref, vt_ref, iwt_ref, o_ref, keys_ref, acc_ref,
                *, seq, k_sel, idx_scale):
    kb_sz = DSA_KB
    bi = pl.program_id(1)
    nkb = (bi * Q_BLOCK + Q_BLOCK + kb_sz - 1) // kb_sz
    lane = lax.broadcasted_iota(jnp.int32, (1, LANES), 1)
    limit = bi * Q_BLOCK + jnp.where(lane < CHUNK, CHUNK, 2 * CHUNK)
    row_iota = lax.broadcasted_iota(jnp.int32, (kb_sz, LANES), 0)

    iq_t = _heads_to_lanes(iq_ref[...], IDX_HEADS, IDX_HEAD_DIM)
    w = iwt_ref[...]

    def score_block(kb, _):
        off = pl.multiple_of(kb * kb_sz, kb_sz)
        ikb = ik_ref[pl.ds(off, kb_sz), :][:, :IDX_HEAD_DIM]
        logits = _dot(ikb, iq_t)
        sc = jnp.zeros((kb_sz, LANES), F32)
        for h in range(IDX_HEADS):
            sc = sc + jnp.maximum(logits[:, h * LANES:(h + 1) * LANES], 0.0) * w[h:h + 1, :]
        sc = sc * idx_scale
        bits = pltpu.bitcast(sc, jnp.int32)
        key = bits ^ ((bits >> 31) & 0x7FFFFFFF)
        key = jnp.where(sc == 0.0, 0, key)
        key = jnp.where(off + row_iota < limit, key, INT_MIN)
        keys_ref[pl.ds(off, kb_sz), :] = key
        return 0

    lax.fori_loop(0, nkb, score_block, 0)

    def count(pred):
        def body(kb, c):
            off = pl.multiple_of(kb * kb_sz, kb_sz)
            ind = jnp.where(pred(keys_ref[pl.ds(off, kb_sz), :], off + row_iota), 1, 0)
            return c + jnp.sum(ind.reshape(kb_sz // 8, 8, LANES), axis=0)
        c8 = lax.fori_loop(0, nkb, body, jnp.zeros((8, LANES), jnp.int32))
        return jnp.sum(c8, axis=0, keepdims=True)

    def bit_step(i, t_u, done):
        cand_u = t_u | jnp.left_shift(jnp.int32(1), 31 - i)
        c = count(lambda kk, pos: kk >= (cand_u ^ INT_MIN))
        t_u = jnp.where((c >= k_sel) & (done == 0), cand_u, t_u)
        return t_u, done | jnp.where(c == k_sel, 1, 0)

    zeros = jnp.zeros((1, LANES), jnp.int32)
    t_u, done = lax.fori_loop(0, DSA_FIXED_PASSES, lambda i, s: bit_step(i, *s), (zeros, zeros))
    _, t_u, done = lax.while_loop(
        lambda s: (s[0] < 32) & (jnp.min(s[2]) == 0),
        lambda s: (s[0] + 1,) + bit_step(s[0], s[1], s[2]),
        (jnp.int32(DSA_FIXED_PASSES), t_u, done))
    thr = jnp.maximum(t_u ^ INT_MIN, INT_MIN + 1)
    pos_bits = seq.bit_length() - 1
    no_cut = jnp.full((1, LANES), seq, jnp.int32)

    def tie_path():
        c_gt = count(lambda kk, pos: kk > thr)
        c_ge = count(lambda kk, pos: kk >= thr)
        need = k_sel - c_gt
        excess = jnp.where(done == 0, c_ge - k_sel, 0)

        def tie_cut():
            def jbit(i, cut):
                cand = cut | jnp.left_shift(jnp.int32(1), pos_bits - 1 - i)
                c = count(lambda kk, pos: (kk == thr) & (pos < cand))
                return jnp.where(c <= need, cand, cut)
            cut = lax.fori_loop(0, pos_bits, jbit, zeros)
            return jnp.where(excess > 0, cut, seq)

        return lax.cond(jnp.max(excess) > 0, tie_cut, lambda: no_cut)

    cut = lax.cond(jnp.min(done) == 0, tie_path, lambda: no_cut)

    reps = ATT_HEADS // ATT_KV_HEADS
    width = reps * LANES
    q_t = _heads_to_lanes(q_ref[...], ATT_HEADS, ATT_HEAD_DIM)
    acc_ref[...] = jnp.zeros_like(acc_ref)

    def att_block(kb, carry):
        m_old, l_old = carry
        off = pl.multiple_of(kb * kb_sz, kb_sz)
        kk = keys_ref[pl.ds(off, kb_sz), :]
        sel = (kk > thr) | ((kk == thr) & (off + row_iota < cut))
        bias = jnp.where(sel, 0.0, NEG_BIG)
        bias = jnp.concatenate([bias] * reps, axis=1)
        kblk = k_ref[pl.ds(off, kb_sz), :]
        s = jnp.concatenate(
            [_dot(kblk[:, g * ATT_HEAD_DIM:(g + 1) * ATT_HEAD_DIM], q_t[:, g * width:(g + 1) * width]) + bias
             for g in range(ATT_KV_HEADS)], axis=1)
        m_new = jnp.maximum(m_old, jnp.max(s, axis=0, keepdims=True))
        alpha = jnp.exp2(m_old - m_new)
        p = jnp.exp2(s - m_new)
        l_new = alpha * l_old + jnp.sum(p, axis=0, keepdims=True)
        p = p.astype(BF16)
        for g in range(ATT_KV_HEADS):
            vt = vt_ref[g * ATT_HEAD_DIM:(g + 1) * ATT_HEAD_DIM, pl.ds(off, kb_sz)]
            acc_ref[g] = (alpha[:, g * width:(g + 1) * width] * acc_ref[g]
                          + _dot(vt, p[:, g * width:(g + 1) * width]))
        return m_new, l_new

    init = (jnp.full((1, ATT_HEADS * LANES), NEG_BIG, F32), jnp.zeros((1, ATT_HEADS * LANES), F32))
    _, l_fin = lax.fori_loop(0, nkb, att_block, init)

    pieces = []
    for g in range(ATT_KV_HEADS):
        o = acc_ref[g] / l_fin[:, g * width:(g + 1) * width]
        pieces += [o[:, r * LANES:(r + 1) * LANES] for r in range(reps)]
    o_ref[...] = jnp.concatenate(pieces, axis=0).T.astype(o_ref.dtype)


def _dsa(a3, vt, iwt):
    b, s, _ = a3.shape
    nb = s // Q_BLOCK
    k_sel = min(TOPK_MAX, s // 4)
    idx_scale = (IDX_HEAD_DIM ** -0.5) * (IDX_HEADS ** -0.5)
    kern = functools.partial(_dsa_kernel, seq=s, k_sel=k_sel, idx_scale=idx_scale)
    return pl.pallas_call(
        kern,
        grid=(b, nb),
        in_specs=[
            pl.BlockSpec((None, Q_BLOCK, 512), lambda bb, i: (bb, i, COL_Q // 512)),
            pl.BlockSpec((None, Q_BLOCK, 512), lambda bb, i: (bb, i, COL_IQ // 512)),
            pl.BlockSpec((None, s, LANES), lambda bb, i: (bb, 0, COL_K // LANES)),
            pl.BlockSpec((None, s, LANES), lambda bb, i: (bb, 0, COL_IK // LANES)),
            pl.BlockSpec((None, LANES, s), lambda bb, i: (bb, 0, 0)),
            pl.BlockSpec((None, None, IDX_HEADS, LANES), lambda bb, i: (bb, i, 0, 0)),
        ],
        out_specs=pl.BlockSpec((None, Q_BLOCK, MIX_A), lambda bb, i: (bb, i, 0)),
        out_shape=jax.ShapeDtypeStruct((b, s, MIX_A), BF16),
        scratch_shapes=[
            pltpu.VMEM((s, LANES), jnp.int32),
            pltpu.VMEM((ATT_KV_HEADS, ATT_HEAD_DIM, (ATT_HEADS // ATT_KV_HEADS) * LANES), F32),
        ],
        compiler_params=pltpu.CompilerParams(
            dimension_semantics=("parallel", "arbitrary"), vmem_limit_bytes=VMEM_LIMIT),
        name="dsa",
    )(a3, a3, a3, a3, vt, iwt)


ML_L = 128
ML_TS = 512
ML_PAD = 8


def _mlstm_kernel(mx_ref, mv_ref, mo_ref, gc_ref, gr_ref, cw_ref, cb_ref, wq_ref, wk_ref,
                  bcol_ref, brow_ref, hg_ref, o_ref,
                  xbuf_ref, q_ref, k_ref, c_ref, n_ref, m_ref):
    t_idx = pl.program_id(1)
    nh, dk, dv, L = MLSTM_HEADS, MLSTM_QK_DIM, MLSTM_V_DIM, ML_L

    @pl.when(t_idx == 0)
    def _():
        xbuf_ref[0:ML_PAD, :] = jnp.zeros((ML_PAD, MIX_B), F32)
        c_ref[...] = jnp.zeros_like(c_ref)
        n_ref[...] = jnp.zeros_like(n_ref)
        m_ref[...] = jnp.zeros_like(m_ref)

    xbuf_ref[ML_PAD:ML_PAD + ML_TS, :] = mx_ref[...]
    xc = cb_ref[...] + jnp.zeros((ML_TS, MIX_B), F32)
    for j in range(CONV_WIDTH):
        s0 = ML_PAD - (CONV_WIDTH - 1) + j
        xc = xc + xbuf_ref[s0:s0 + ML_TS, :] * cw_ref[j:j + 1, :]
    xbuf_ref[0:ML_PAD, :] = mx_ref[ML_TS - ML_PAD:ML_TS, :]
    xc = (xc * jax.nn.sigmoid(xc)).astype(BF16)
    for h in range(nh):
        xh = xc[:, h * dv:(h + 1) * dv]
        q_ref[h] = (_dot(xh, wq_ref[h]) * (dk ** -0.5)).astype(BF16)
        k_ref[h] = _dot(xh, wk_ref[h])

    gc = gc_ref[...]
    li_c = gc[:, 8:8 + nh] + bcol_ref[0:1, 0:nh]
    lf_c = jax.nn.log_sigmoid(gc[:, 8 + nh:8 + 2 * nh] + bcol_ref[0:1, nh:2 * nh])
    gr = gr_ref[...]
    li_r = gr[0:nh, :] + brow_ref[0:nh, :]
    lf_r = jax.nn.log_sigmoid(gr[nh:2 * nh, :] + brow_ref[nh:2 * nh, :])

    ri = lax.broadcasted_iota(jnp.int32, (L, L), 0)
    ci = lax.broadcasted_iota(jnp.int32, (L, L), 1)
    causal = ri >= ci
    tril = jnp.where(causal, 1.0, 0.0).astype(F32)
    triu = jnp.where(ri <= ci, 1.0, 0.0).astype(F32)
    hi = lax.Precision.HIGHEST

    for c in range(ML_TS // L):
        r0 = c * L
        b_c = jnp.dot(tril, lf_c[r0:r0 + L, :], precision=hi, preferred_element_type=F32)
        b_r = jnp.dot(lf_r[:, r0:r0 + L], triu, precision=hi, preferred_element_type=F32)
        outs = []
        for h in range(nh):
            bc = b_c[:, h:h + 1]
            br = b_r[h:h + 1, :]
            ir = li_r[h:h + 1, r0:r0 + L]
            ic = li_c[r0:r0 + L, h:h + 1]
            m_prev = m_ref[h]
            d = jnp.where(causal, bc - br + ir, NEG_BIG)
            inter = bc + m_prev
            m_t = jnp.maximum(jnp.max(d, axis=-1, keepdims=True), inter)
            qh = q_ref[h, r0:r0 + L, :]
            kh = k_ref[h, r0:r0 + L, :]
            vh = mv_ref[r0:r0 + L, h * dv:(h + 1) * dv].astype(BF16)
            wt = _dot_nt(qh, kh.astype(BF16)) * jnp.exp(d - m_t)
            w_int = jnp.exp(inter - m_t)
            num = _dot(wt.astype(BF16), vh) + w_int * _dot(qh, c_ref[h].astype(BF16))
            qn = jnp.sum(qh.astype(F32) * n_ref[h], axis=-1, keepdims=True)
            den = jnp.sum(wt, axis=-1, keepdims=True) + w_int * qn
            outs.append(num / jnp.maximum(jnp.abs(den), jnp.exp(-m_t)))
            b_last = bc[L - 1:L, :]
            wk_r = b_last - br + ir
            m_new = jnp.maximum(b_last + m_prev, jnp.max(wk_r, axis=-1, keepdims=True))
            ws_c = jnp.exp(b_last - bc + ic - m_new)
            wc = jnp.exp(b_last + m_prev - m_new)
            kw = kh * ws_c
            c_ref[h] = wc * c_ref[h] + _dot_tn(kw.astype(BF16), vh)
            n_ref[h] = wc * n_ref[h] + jnp.sum(kw, axis=0, keepdims=True)
            m_ref[h] = m_new
        hh = jnp.concatenate(
            [_rms(outs[h], hg_ref[0:1, h * dv:(h + 1) * dv]) for h in range(nh)], axis=1)
        o_ref[r0:r0 + L, :] = (hh * jax.nn.sigmoid(mo_ref[r0:r0 + L, :])).astype(o_ref.dtype)


def _mlstm(m3, g3, gt, conv_w, conv_b, wq, wk, bcol, brow, head_g):
    b, s, _ = m3.shape
    nh = MLSTM_HEADS
    tile = lambda c: pl.BlockSpec((None, ML_TS, MIX_B), lambda bb, t: (bb, t, c))
    const2 = lambda shape: pl.BlockSpec(shape, lambda bb, t: (0, 0))
    const3 = lambda shape: pl.BlockSpec(shape, lambda bb, t: (0, 0, 0))
    return pl.pallas_call(
        _mlstm_kernel,
        grid=(b, s // ML_TS),
        in_specs=[
            tile(0), tile(1), tile(2),
            pl.BlockSpec((None, ML_TS, G_WIDTH), lambda bb, t: (bb, t, 0)),
            pl.BlockSpec((None, 2 * nh, ML_TS), lambda bb, t: (bb, 0, t)),
            const2((CONV_WIDTH, MIX_B)), const2((1, MIX_B)),
            const3((nh, MLSTM_V_DIM, MLSTM_QK_DIM)), const3((nh, MLSTM_V_DIM, MLSTM_QK_DIM)),
            const2((1, 2 * nh)), const2((2 * nh, 1)), const2((1, MIX_B)),
        ],
        out_specs=pl.BlockSpec((None, ML_TS, MIX_B), lambda bb, t: (bb, t, 0)),
        out_shape=jax.ShapeDtypeStruct((b, s, MIX_B), BF16),
        scratch_shapes=[
            pltpu.VMEM((ML_PAD + ML_TS, MIX_B), F32),
            pltpu.VMEM((nh, ML_TS, MLSTM_QK_DIM), BF16),
            pltpu.VMEM((nh, ML_TS, MLSTM_QK_DIM), F32),
            pltpu.VMEM((nh, MLSTM_QK_DIM, MLSTM_V_DIM), F32),
            pltpu.VMEM((nh, 1, MLSTM_QK_DIM), F32),
            pltpu.VMEM((nh, 1, 1), F32),
        ],
        compiler_params=pltpu.CompilerParams(
            dimension_semantics=("parallel", "arbitrary"), vmem_limit_bytes=VMEM_LIMIT),
        name="mlstm",
    )(m3, m3, m3, g3, gt, conv_w, conv_b, wq, wk, bcol, brow, head_g)


MERGE_TM = 512


def _merge_kernel(x_ref, ya_ref, yb_ref, ga_ref, gb_ref, wa_ref, wb_ref, wo_ref, o_ref):
    merged = (jax.nn.sigmoid(ga_ref[...]) * _dot(ya_ref[...], wa_ref[...])
              + jax.nn.sigmoid(gb_ref[...]) * _dot(yb_ref[...], wb_ref[...]))
    o_ref[...] = x_ref[...] + _dot(merged.astype(BF16), wo_ref[...])


def _merge(x2d, ya, yb, gates, wa, wb, wo):
    m = x2d.shape[0]
    row = lambda i: (i, 0)
    const = lambda i: (0, 0)
    return pl.pallas_call(
        _merge_kernel,
        grid=(m // MERGE_TM,),
        in_specs=[
            pl.BlockSpec((MERGE_TM, D_MODEL), row),
            pl.BlockSpec((MERGE_TM, MIX_A), row),
            pl.BlockSpec((MERGE_TM, MIX_B), row),
            pl.BlockSpec((MERGE_TM, D_MODEL), lambda i: (i, 0)),
            pl.BlockSpec((MERGE_TM, D_MODEL), lambda i: (i, 1)),
            pl.BlockSpec((MIX_A, D_MODEL), const),
            pl.BlockSpec((MIX_B, D_MODEL), const),
            pl.BlockSpec((D_MODEL, D_MODEL), const),
        ],
        out_specs=pl.BlockSpec((MERGE_TM, D_MODEL), row),
        out_shape=jax.ShapeDtypeStruct((m, D_MODEL), F32),
        compiler_params=pltpu.CompilerParams(
            dimension_semantics=("parallel",), vmem_limit_bytes=VMEM_LIMIT),
        name="merge",
    )(x2d, ya, yb, gates, gates, wa, wb, wo)


def _block_diag_ones(n, blk):
    r = jnp.arange(n) // blk
    return (r[:, None] == r[None, :]).astype(BF16)


def _arrange_w_in(w_in):
    widths = (512, 128, 128, 512, 64, 8, 512, 512, 4, 4, 512, 1024, 1024)
    parts, off = [], 0
    for wd in widths:
        parts.append(w_in[:, off:off + wd])
        off += wd
    aq, ak, av, iq, ik, iw, mx, mv, mi, mf, mo, ga, gb = parts
    z = lambda n: jnp.zeros((w_in.shape[0], n), w_in.dtype)
    cols = [aq, iq, ak, ik, z(64), av,
            iw, mi, mf, z(G_WIDTH - 16),
            mx, mv, mo, ga, gb]
    return jnp.concatenate(cols, axis=1).astype(BF16)


def kernel(x, ffn1_norm, ffn1_w_gate, ffn1_w_up, ffn1_w_down, mix_norm, w_in, q_norm, k_norm,
           idx_k_norm, conv_w, conv_b, w_mq, w_mk, b_i, b_f, m_head_norm, w_proj_a, w_proj_b,
           w_out, ffn2_norm, ffn2_w_gate, ffn2_w_up, ffn2_w_down):
    b, s, _ = x.shape
    m = b * s
    nb = s // Q_BLOCK
    nh = MLSTM_HEADS
    x2d = x.reshape(m, D_MODEL)
    for l in range(ffn1_norm.shape[0]):
        x2d = _ffn(x2d, ffn1_norm[l], ffn1_w_gate[l].astype(BF16), ffn1_w_up[l].astype(BF16),
                   ffn1_w_down[l].astype(BF16))

        gq = (jnp.tile(q_norm[l], ATT_HEADS) * (ATT_HEAD_DIM ** -0.5 * LOG2_E)).reshape(1, 512)
        gk = jnp.concatenate([jnp.tile(k_norm[l], ATT_KV_HEADS), idx_k_norm[l],
                              jnp.zeros((64,), F32)]).reshape(1, 256)
        a2d, g2d, m2d, t2d = _proj(x2d, mix_norm[l], _arrange_w_in(w_in[l]),
                                   _block_diag_ones(512, ATT_HEAD_DIM),
                                   _block_diag_ones(256, ATT_HEAD_DIM), gq, gk)

        a3 = a2d.reshape(b, s, A_WIDTH)
        vt = jnp.swapaxes(a3[:, :, COL_V:COL_V + ATT_KV_HEADS * ATT_HEAD_DIM], 1, 2)
        g3 = g2d.reshape(b, s, G_WIDTH)
        iwt = jnp.swapaxes(g3[:, :, 0:IDX_HEADS].reshape(b, nb, Q_BLOCK, IDX_HEADS), 2, 3)
        ya = _dsa(a3, vt, iwt)

        gt = jnp.swapaxes(g3[:, :, IDX_HEADS:IDX_HEADS + 2 * nh], 1, 2)
        bias = jnp.concatenate([b_i[l], b_f[l]])
        yb = _mlstm(m2d.reshape(b, s, M_WIDTH), g3, gt, conv_w[l], conv_b[l].reshape(1, MIX_B),
                    w_mq[l].astype(BF16), w_mk[l].astype(BF16),
                    bias.reshape(1, 2 * nh), bias.reshape(2 * nh, 1),
                    m_head_norm[l].reshape(1, MIX_B))

        x2d = _merge(x2d, ya.reshape(m, MIX_A), yb.reshape(m, MIX_B), t2d,
                     w_proj_a[l].astype(BF16), w_proj_b[l].astype(BF16), w_out[l].astype(BF16))

        x2d = _ffn(x2d, ffn2_norm[l], ffn2_w_gate[l].astype(BF16), ffn2_w_up[l].astype(BF16),
                   ffn2_w_down[l].astype(BF16))
    return x2d.reshape(b, s, D_MODEL)
```

```python
import functools

import jax
import jax.numpy as jnp
from jax import lax
from jax.experimental import pallas as pl
from jax.experimental.pallas import tpu as pltpu

F32 = jnp.float32
BF16 = jnp.bfloat16

D_MODEL = 1024
D_FF = 2816
EPS = 1e-6
ATT_HEADS = 8
ATT_KV_HEADS = 2
ATT_HEAD_DIM = 64
IDX_HEADS = 8
IDX_HEAD_DIM = 64
TOPK_MAX = 256
Q_BLOCK = 128
CHUNK = 64
MLSTM_HEADS = 4
MLSTM_QK_DIM = 64
MLSTM_V_DIM = 128
CONV_WIDTH = 4
MIX_A = ATT_HEADS * ATT_HEAD_DIM
MIX_B = MLSTM_HEADS * MLSTM_V_DIM

LANES = 128
VMEM_LIMIT = 56 * 1024 * 1024
INT_MIN = -2147483648
HALF = 32768
NEG_BIG = -1e30
LOG2_E = 1.4426950408889634

COL_Q, COL_IQ, COL_K, COL_IK, COL_V = 0, 512, 1024, 1152, 1280
A_WIDTH = 1408
G_WIDTH = 128
M_WIDTH = 3 * MIX_B
T_WIDTH = 2 * D_MODEL
W_TOTAL = A_WIDTH + G_WIDTH + M_WIDTH + T_WIDTH


def _rms(x, g):
    return x * lax.rsqrt(jnp.mean(x * x, axis=-1, keepdims=True) + EPS) * g


def _dot(a, b):
    return jnp.dot(a, b, preferred_element_type=F32)


def _dot_nt(a, b):
    return lax.dot_general(a, b, (((1,), (1,)), ((), ())), preferred_element_type=F32)


def _dot_tn(a, b):
    return lax.dot_general(a, b, (((0,), (0,)), ((), ())), preferred_element_type=F32)


FFN_TM = 512
FFN_TF = 1408


def _ffn_kernel(x_ref, g_ref, wg_ref, wu_ref, wd_ref, o_ref, h_ref, acc_ref):
    j = pl.program_id(1)

    @pl.when(j == 0)
    def _():
        h_ref[...] = _rms(x_ref[...], g_ref[...]).astype(BF16)
        acc_ref[...] = jnp.zeros_like(acc_ref)

    h = h_ref[...]
    a = _dot(h, wg_ref[...])
    u = _dot(h, wu_ref[...])
    act = (a * jax.nn.sigmoid(a) * u).astype(BF16)
    acc_ref[...] += _dot(act, wd_ref[...])

    @pl.when(j == pl.num_programs(1) - 1)
    def _():
        o_ref[...] = x_ref[...] + 0.5 * acc_ref[...]


def _ffn(x2d, g, wg, wu, wd):
    m = x2d.shape[0]
    return pl.pallas_call(
        _ffn_kernel,
        grid=(m // FFN_TM, D_FF // FFN_TF),
        in_specs=[
            pl.BlockSpec((FFN_TM, D_MODEL), lambda i, j: (i, 0)),
            pl.BlockSpec((1, D_MODEL), lambda i, j: (0, 0)),
            pl.BlockSpec((D_MODEL, FFN_TF), lambda i, j: (0, j)),
            pl.BlockSpec((D_MODEL, FFN_TF), lambda i, j: (0, j)),
            pl.BlockSpec((FFN_TF, D_MODEL), lambda i, j: (j, 0)),
        ],
        out_specs=pl.BlockSpec((FFN_TM, D_MODEL), lambda i, j: (i, 0)),
        out_shape=jax.ShapeDtypeStruct((m, D_MODEL), F32),
        scratch_shapes=[pltpu.VMEM((FFN_TM, D_MODEL), BF16), pltpu.VMEM((FFN_TM, D_MODEL), F32)],
        compiler_params=pltpu.CompilerParams(
            dimension_semantics=("parallel", "arbitrary"), vmem_limit_bytes=VMEM_LIMIT),
        name="ffn",
    )(x2d, g.reshape(1, D_MODEL), wg, wu, wd)


PROJ_TM = 256


def _proj_kernel(x_ref, g_ref, w_ref, bdq_ref, bdk_ref, gq_ref, gk_ref,
                 a_ref, g_out_ref, m_ref, t_ref):
    h = _rms(x_ref[...], g_ref[...]).astype(BF16)

    def head_norm(p, bd_ref, gain_ref):
        ss = _dot((p * p).astype(BF16), bd_ref[...]) * (1.0 / ATT_HEAD_DIM)
        return p * lax.rsqrt(ss + EPS) * gain_ref[...]

    pq = _dot(h, w_ref[:, COL_Q:COL_IQ])
    a_ref[:, COL_Q:COL_IQ] = head_norm(pq, bdq_ref, gq_ref).astype(BF16)
    a_ref[:, COL_IQ:COL_K] = _dot(h, w_ref[:, COL_IQ:COL_K]).astype(BF16)
    pk = _dot(h, w_ref[:, COL_K:COL_V])
    a_ref[:, COL_K:COL_V] = head_norm(pk, bdk_ref, gk_ref).astype(BF16)
    a_ref[:, COL_V:A_WIDTH] = _dot(h, w_ref[:, COL_V:A_WIDTH]).astype(BF16)
    o = A_WIDTH
    g_out_ref[...] = _dot(h, w_ref[:, o:o + G_WIDTH])
    o += G_WIDTH
    for c in range(3):
        m_ref[:, c * MIX_B:(c + 1) * MIX_B] = _dot(h, w_ref[:, o + c * MIX_B:o + (c + 1) * MIX_B])
    o += M_WIDTH
    for c in range(2):
        t_ref[:, c * D_MODEL:(c + 1) * D_MODEL] = _dot(h, w_ref[:, o + c * D_MODEL:o + (c + 1) * D_MODEL])


def _proj(x2d, g, w_all, bdq, bdk, gq, gk):
    m = x2d.shape[0]
    const = lambda i: (0, 0)
    row = lambda i: (i, 0)
    return pl.pallas_call(
        _proj_kernel,
        grid=(m // PROJ_TM,),
        in_specs=[
            pl.BlockSpec((PROJ_TM, D_MODEL), row),
            pl.BlockSpec((1, D_MODEL), const),
            pl.BlockSpec((D_MODEL, W_TOTAL), const),
            pl.BlockSpec((512, 512), const),
            pl.BlockSpec((256, 256), const),
            pl.BlockSpec((1, 512), const),
            pl.BlockSpec((1, 256), const),
        ],
        out_specs=[
            pl.BlockSpec((PROJ_TM, A_WIDTH), row),
            pl.BlockSpec((PROJ_TM, G_WIDTH), row),
            pl.BlockSpec((PROJ_TM, M_WIDTH), row),
            pl.BlockSpec((PROJ_TM, T_WIDTH), row),
        ],
        out_shape=[
            jax.ShapeDtypeStruct((m, A_WIDTH), BF16),
            jax.ShapeDtypeStruct((m, G_WIDTH), F32),
            jax.ShapeDtypeStruct((m, M_WIDTH), F32),
            jax.ShapeDtypeStruct((m, T_WIDTH), F32),
        ],
        compiler_params=pltpu.CompilerParams(
            dimension_semantics=("parallel",), vmem_limit_bytes=VMEM_LIMIT),
        name="proj",
    )(x2d, g.reshape(1, D_MODEL), w_all, bdq, bdk, gq, gk)


DSA_KB = 512
DSA_KA = 256
DSA_VROWS = 80

def _heads_to_lanes(x, n_heads, dh):
    xf = x.astype(F32)
    return jnp.concatenate([xf[:, h * dh:(h + 1) * dh].T for h in range(n_heads)], axis=1).astype(BF16)


def _dsa_kernel(q_ref, iq_ref, k_ref, ik_ref, vt_ref, iwt_ref, o_ref,
                keys_ref, hi_ref, lo_ref, lob_ref, acc_ref, sa_ref, sb_ref, *, seq, k_sel, idx_scale):
    kb_sz = DSA_KB
    bi = pl.program_id(1)
    nkb = (bi * Q_BLOCK + Q_BLOCK + kb_sz - 1) // kb_sz
    lane = lax.broadcasted_iota(jnp.int32, (1, LANES), 1)
    limit = bi * Q_BLOCK + jnp.where(lane < CHUNK, CHUNK, 2 * CHUNK)
    row_iota = lax.broadcasted_iota(jnp.int32, (kb_sz, LANES), 0)

    iq_t = _heads_to_lanes(iq_ref[...], IDX_HEADS, IDX_HEAD_DIM)
    w = iwt_ref[...]

    def score_block(kb, _):
        off = pl.multiple_of(kb * kb_sz, kb_sz)
        ikb = ik_ref[pl.ds(off, kb_sz), :][:, :IDX_HEAD_DIM]
        logits = _dot(ikb, iq_t)
        sc = jnp.zeros((kb_sz, LANES), F32)
        for h in range(IDX_HEADS):
            sc = sc + jnp.maximum(logits[:, h * LANES:(h + 1) * LANES], 0.0) * w[h:h + 1, :]
        sc = sc * idx_scale
        bits = pltpu.bitcast(sc, jnp.int32)
        key = bits ^ ((bits >> 31) & 0x7FFFFFFF)
        key = jnp.where(sc == 0.0, 0, key)
        key = jnp.where(off + row_iota < limit, key, INT_MIN)
        keys_ref[pl.ds(off, kb_sz), :] = key
        hi_ref[pl.ds(off, kb_sz), :] = (key >> 16).astype(jnp.int16)
        lo_ref[pl.ds(off, kb_sz), :] = ((key & 0xFFFF) - HALF).astype(jnp.int16)
        return 0

    lax.fori_loop(0, nkb, score_block, 0)

    one_bf, zero_bf = jnp.ones((), BF16), jnp.zeros((), BF16)
    rows16 = 16

    def count16(ref, cand):
        cand16 = cand.astype(jnp.int16)

        def body(kb, c):
            off = pl.multiple_of(kb * kb_sz, kb_sz)
            ind = jnp.where(ref[pl.ds(off, kb_sz), :] >= cand16, one_bf, zero_bf)
            ind = ind.reshape(kb_sz // rows16, rows16, LANES)
            parts = [ind[r] for r in range(kb_sz // rows16)]
            while len(parts) > 1:
                parts = [parts[a] + parts[a + 1] for a in range(0, len(parts), 2)]
            return c + parts[0].astype(F32)
        c = lax.fori_loop(0, nkb, body, jnp.zeros((rows16, LANES), F32))
        return jnp.sum(c, axis=0, keepdims=True).astype(jnp.int32)

    def count(pred):
        def body(kb, c):
            off = pl.multiple_of(kb * kb_sz, kb_sz)
            ind = jnp.where(pred(keys_ref[pl.ds(off, kb_sz), :], off + row_iota), 1, 0)
            return c + jnp.sum(ind.reshape(kb_sz // 8, 8, LANES), axis=0)
        c8 = lax.fori_loop(0, nkb, body, jnp.zeros((8, LANES), jnp.int32))
        return jnp.sum(c8, axis=0, keepdims=True)

    zeros = jnp.zeros((1, LANES), jnp.int32)

    def hi_step(i, t_hi):
        cand_u = t_hi | jnp.left_shift(jnp.int32(1), 15 - i)
        return jnp.where(count16(hi_ref, cand_u - HALF) >= k_sel, cand_u, t_hi)

    t_hi = lax.fori_loop(0, 16, hi_step, zeros)
    c_above = jnp.where(t_hi == 2 * HALF - 1, 0, count16(hi_ref, t_hi + 1 - HALF))
    need_lo = k_sel - c_above
    thi16 = (t_hi - HALF).astype(jnp.int16)

    def bucket_block(kb, _):
        off = pl.multiple_of(kb * kb_sz, kb_sz)
        in_bucket = hi_ref[pl.ds(off, kb_sz), :] == thi16
        lob_ref[pl.ds(off, kb_sz), :] = jnp.where(in_bucket, lo_ref[pl.ds(off, kb_sz), :], jnp.int16(-HALF))
        return 0

    lax.fori_loop(0, nkb, bucket_block, 0)

    def lo_step(i, t_lo, done):
        cand_u = t_lo | jnp.left_shift(jnp.int32(1), 15 - i)
        c = count16(lob_ref, cand_u - HALF)
        t_lo = jnp.where((c >= need_lo) & (done == 0), cand_u, t_lo)
        return t_lo, done | jnp.where(c == need_lo, 1, 0)

    _, t_lo, done = lax.while_loop(
        lambda s: (s[0] < 16) & (jnp.min(s[2]) == 0),
        lambda s: (s[0] + 1,) + lo_step(s[0], s[1], s[2]),
        (jnp.int32(0), zeros, zeros))
    t_u = jnp.left_shift(t_hi, 16) | t_lo
    thr = jnp.maximum(t_u ^ INT_MIN, INT_MIN + 1)
    pos_bits = seq.bit_length() - 1

    def tie_path():
        c_gt = count(lambda kk, pos: kk > thr)
        c_ge = count(lambda kk, pos: kk >= thr)
        need = k_sel - c_gt
        excess = jnp.where(done == 0, c_ge - k_sel, 0)

        @pl.when(jnp.max(excess) > 0)
        def _():
            def jbit(i, cut):
                cand = cut | jnp.left_shift(jnp.int32(1), pos_bits - 1 - i)
                c = count(lambda kk, pos: (kk == thr) & (pos < cand))
                return jnp.where(c <= need, cand, cut)
            cut = lax.fori_loop(0, pos_bits, jbit, zeros)
            cut = jnp.where(excess > 0, cut, seq)

            def drop_block(kb, _):
                off = pl.multiple_of(kb * kb_sz, kb_sz)
                kk = keys_ref[pl.ds(off, kb_sz), :]
                keys_ref[pl.ds(off, kb_sz), :] = jnp.where(
                    (kk == thr) & (off + row_iota >= cut), INT_MIN, kk)
                return 0

            lax.fori_loop(0, nkb, drop_block, 0)

    pl.when(jnp.min(done) == 0)(tie_path)

    reps = ATT_HEADS // ATT_KV_HEADS
    width = reps * LANES
    q_t = _heads_to_lanes(q_ref[...], ATT_HEADS, ATT_HEAD_DIM)
    acc_ref[...] = jnp.zeros_like(acc_ref)

    ka = DSA_KA

    def masked_scores(blk):
        off = pl.multiple_of(blk * ka, ka)
        bias = jnp.where(keys_ref[pl.ds(off, ka), :] >= thr, 0.0, NEG_BIG)
        bias = jnp.concatenate([bias] * reps, axis=1)
        kblk = k_ref[pl.ds(off, ka), :]
        return jnp.concatenate(
            [_dot(kblk[:, g * ATT_HEAD_DIM:(g + 1) * ATT_HEAD_DIM], q_t[:, g * width:(g + 1) * width]) + bias
             for g in range(ATT_KV_HEADS)], axis=1)

    def consume(s_ref, blk, m_old):
        off = pl.multiple_of(blk * ka, ka)
        s = s_ref[...]
        m_new = jnp.maximum(m_old, jnp.max(s, axis=0, keepdims=True))
        alpha = jnp.exp2(m_old - m_new)
        p = jnp.exp2(s - m_new).astype(BF16)
        for g in range(ATT_KV_HEADS):
            vt = vt_ref[g, :, pl.ds(off, ka)]
            acc_ref[g] = (alpha[:, g * width:(g + 1) * width] * acc_ref[g]
                          + _dot(vt, p[:, g * width:(g + 1) * width]))
        return m_new

    n_sub = nkb * (kb_sz // ka)
    sa_ref[...] = masked_scores(0)

    def att_pair(j, m_run):
        sb_ref[...] = masked_scores(2 * j + 1)
        m_run = consume(sa_ref, 2 * j, m_run)
        sa_ref[...] = masked_scores(jnp.minimum(2 * j + 2, n_sub - 1))
        return consume(sb_ref, 2 * j + 1, m_run)

    lax.fori_loop(0, n_sub // 2, att_pair, jnp.full((1, ATT_HEADS * LANES), NEG_BIG, F32))

    pieces = []
    for g in range(ATT_KV_HEADS):
        a = acc_ref[g]
        o = a[0:ATT_HEAD_DIM, :] / a[ATT_HEAD_DIM:ATT_HEAD_DIM + 1, :]
        pieces += [o[:, r * LANES:(r + 1) * LANES] for r in range(reps)]
    o_ref[...] = jnp.concatenate(pieces, axis=0).T.astype(o_ref.dtype)


def _v_transposed(v):
    b, s, _ = v.shape
    vt = jnp.swapaxes(v.reshape(b, s, ATT_KV_HEADS, ATT_HEAD_DIM), 1, 3)
    vt = jnp.swapaxes(vt, 1, 2)
    ones = jnp.ones((b, ATT_KV_HEADS, 1, s), v.dtype)
    pad = jnp.zeros((b, ATT_KV_HEADS, DSA_VROWS - ATT_HEAD_DIM - 1, s), v.dtype)
    return jnp.concatenate([vt, ones, pad], axis=2)


def _dsa(a3, vt, iwt):
    b, s, _ = a3.shape
    nb = s // Q_BLOCK
    k_sel = min(TOPK_MAX, s // 4)
    idx_scale = (IDX_HEAD_DIM ** -0.5) * (IDX_HEADS ** -0.5)
    kern = functools.partial(_dsa_kernel, seq=s, k_sel=k_sel, idx_scale=idx_scale)
    return pl.pallas_call(
        kern,
        grid=(b, nb),
        in_specs=[
            pl.BlockSpec((None, Q_BLOCK, 512), lambda bb, i: (bb, i, COL_Q // 512)),
            pl.BlockSpec((None, Q_BLOCK, 512), lambda bb, i: (bb, i, COL_IQ // 512)),
            pl.BlockSpec((None, s, LANES), lambda bb, i: (bb, 0, COL_K // LANES)),
            pl.BlockSpec((None, s, LANES), lambda bb, i: (bb, 0, COL_IK // LANES)),
            pl.BlockSpec((None, ATT_KV_HEADS, DSA_VROWS, s), lambda bb, i: (bb, 0, 0, 0)),
            pl.BlockSpec((None, None, IDX_HEADS, LANES), lambda bb, i: (bb, i, 0, 0)),
        ],
        out_specs=pl.BlockSpec((None, Q_BLOCK, MIX_A), lambda bb, i: (bb, i, 0)),
        out_shape=jax.ShapeDtypeStruct((b, s, MIX_A), BF16),
        scratch_shapes=[
            pltpu.VMEM((s, LANES), jnp.int32),
            pltpu.VMEM((s, LANES), jnp.int16),
            pltpu.VMEM((s, LANES), jnp.int16),
            pltpu.VMEM((s, LANES), jnp.int16),
            pltpu.VMEM((ATT_KV_HEADS, DSA_VROWS, (ATT_HEADS // ATT_KV_HEADS) * LANES), F32),
            pltpu.VMEM((DSA_KA, ATT_HEADS * LANES), F32),
            pltpu.VMEM((DSA_KA, ATT_HEADS * LANES), F32),
        ],
        compiler_params=pltpu.CompilerParams(
            dimension_semantics=("parallel", "arbitrary"), vmem_limit_bytes=VMEM_LIMIT),
        name="dsa",
    )(a3, a3, a3, a3, vt, iwt)


ML_L = 128
ML_TS = 512
ML_PAD = 8


def _mlstm_kernel(mx_ref, mv_ref, mo_ref, gc_ref, gr_ref, cw_ref, cb_ref, wq_ref, wk_ref,
                  bcol_ref, brow_ref, hg_ref, o_ref,
                  xbuf_ref, q_ref, k_ref, c_ref, n_ref, m_ref):
    t_idx = pl.program_id(1)
    nh, dk, dv, L = MLSTM_HEADS, MLSTM_QK_DIM, MLSTM_V_DIM, ML_L

    @pl.when(t_idx == 0)
    def _():
        xbuf_ref[0:ML_PAD, :] = jnp.zeros((ML_PAD, MIX_B), F32)
        c_ref[...] = jnp.zeros_like(c_ref)
        n_ref[...] = jnp.zeros_like(n_ref)
        m_ref[...] = jnp.zeros_like(m_ref)

    xbuf_ref[ML_PAD:ML_PAD + ML_TS, :] = mx_ref[...]
    xc = cb_ref[...] + jnp.zeros((ML_TS, MIX_B), F32)
    for j in range(CONV_WIDTH):
        s0 = ML_PAD - (CONV_WIDTH - 1) + j
        xc = xc + xbuf_ref[s0:s0 + ML_TS, :] * cw_ref[j:j + 1, :]
    xbuf_ref[0:ML_PAD, :] = mx_ref[ML_TS - ML_PAD:ML_TS, :]
    xc = (xc * jax.nn.sigmoid(xc)).astype(BF16)
    for h in range(nh):
        xh = xc[:, h * dv:(h + 1) * dv]
        q_ref[h] = (_dot(xh, wq_ref[h]) * (dk ** -0.5)).astype(BF16)
        k_ref[h] = _dot(xh, wk_ref[h])

    gc = gc_ref[...]
    li_c = gc[:, 8:8 + nh] + bcol_ref[0:1, 0:nh]
    lf_c = jax.nn.log_sigmoid(gc[:, 8 + nh:8 + 2 * nh] + bcol_ref[0:1, nh:2 * nh])
    gr = gr_ref[...]
    li_r = gr[0:nh, :] + brow_ref[0:nh, :]
    lf_r = jax.nn.log_sigmoid(gr[nh:2 * nh, :] + brow_ref[nh:2 * nh, :])

    ri = lax.broadcasted_iota(jnp.int32, (L, L), 0)
    ci = lax.broadcasted_iota(jnp.int32, (L, L), 1)
    causal = ri >= ci
    tril = jnp.where(causal, 1.0, 0.0).astype(F32)
    triu = jnp.where(ri <= ci, 1.0, 0.0).astype(F32)
    hi = lax.Precision.HIGHEST

    for c in range(ML_TS // L):
        r0 = c * L
        b_c = jnp.dot(tril, lf_c[r0:r0 + L, :], precision=hi, preferred_element_type=F32)
        b_r = jnp.dot(lf_r[:, r0:r0 + L], triu, precision=hi, preferred_element_type=F32)
        outs = []
        for h in range(nh):
            bc = b_c[:, h:h + 1]
            br = b_r[h:h + 1, :]
            ir = li_r[h:h + 1, r0:r0 + L]
            ic = li_c[r0:r0 + L, h:h + 1]
            m_prev = m_ref[h]
            d = jnp.where(causal, bc - br + ir, NEG_BIG)
            inter = bc + m_prev
            m_t = jnp.maximum(jnp.max(d, axis=-1, keepdims=True), inter)
            qh = q_ref[h, r0:r0 + L, :]
            kh = k_ref[h, r0:r0 + L, :]
            vh = mv_ref[r0:r0 + L, h * dv:(h + 1) * dv].astype(BF16)
            wt = _dot_nt(qh, kh.astype(BF16)) * jnp.exp(d - m_t)
            w_int = jnp.exp(inter - m_t)
            num = _dot(wt.astype(BF16), vh) + w_int * _dot(qh, c_ref[h].astype(BF16))
            qn = jnp.sum(qh.astype(F32) * n_ref[h], axis=-1, keepdims=True)
            den = jnp.sum(wt, axis=-1, keepdims=True) + w_int * qn
            outs.append(num / jnp.maximum(jnp.abs(den), jnp.exp(-m_t)))
            b_last = bc[L - 1:L, :]
            wk_r = b_last - br + ir
            m_new = jnp.maximum(b_last + m_prev, jnp.max(wk_r, axis=-1, keepdims=True))
            ws_c = jnp.exp(b_last - bc + ic - m_new)
            wc = jnp.exp(b_last + m_prev - m_new)
            kw = kh * ws_c
            c_ref[h] = wc * c_ref[h] + _dot_tn(kw.astype(BF16), vh)
            n_ref[h] = wc * n_ref[h] + jnp.sum(kw, axis=0, keepdims=True)
            m_ref[h] = m_new
        hh = jnp.concatenate(
            [_rms(outs[h], hg_ref[0:1, h * dv:(h + 1) * dv]) for h in range(nh)], axis=1)
        o_ref[r0:r0 + L, :] = (hh * jax.nn.sigmoid(mo_ref[r0:r0 + L, :])).astype(o_ref.dtype)


def _mlstm(m3, g3, gt, conv_w, conv_b, wq, wk, bcol, brow, head_g):
    b, s, _ = m3.shape
    nh = MLSTM_HEADS
    tile = lambda c: pl.BlockSpec((None, ML_TS, MIX_B), lambda bb, t: (bb, t, c))
    const2 = lambda shape: pl.BlockSpec(shape, lambda bb, t: (0, 0))
    const3 = lambda shape: pl.BlockSpec(shape, lambda bb, t: (0, 0, 0))
    return pl.pallas_call(
        _mlstm_kernel,
        grid=(b, s // ML_TS),
        in_specs=[
            tile(0), tile(1), tile(2),
            pl.BlockSpec((None, ML_TS, G_WIDTH), lambda bb, t: (bb, t, 0)),
            pl.BlockSpec((None, 2 * nh, ML_TS), lambda bb, t: (bb, 0, t)),
            const2((CONV_WIDTH, MIX_B)), const2((1, MIX_B)),
            const3((nh, MLSTM_V_DIM, MLSTM_QK_DIM)), const3((nh, MLSTM_V_DIM, MLSTM_QK_DIM)),
            const2((1, 2 * nh)), const2((2 * nh, 1)), const2((1, MIX_B)),
        ],
        out_specs=pl.BlockSpec((None, ML_TS, MIX_B), lambda bb, t: (bb, t, 0)),
        out_shape=jax.ShapeDtypeStruct((b, s, MIX_B), BF16),
        scratch_shapes=[
            pltpu.VMEM((ML_PAD + ML_TS, MIX_B), F32),
            pltpu.VMEM((nh, ML_TS, MLSTM_QK_DIM), BF16),
            pltpu.VMEM((nh, ML_TS, MLSTM_QK_DIM), F32),
            pltpu.VMEM((nh, MLSTM_QK_DIM, MLSTM_V_DIM), F32),
            pltpu.VMEM((nh, 1, MLSTM_QK_DIM), F32),
            pltpu.VMEM((nh, 1, 1), F32),
        ],
        compiler_params=pltpu.CompilerParams(
            dimension_semantics=("parallel", "arbitrary"), vmem_limit_bytes=VMEM_LIMIT),
        name="mlstm",
    )(m3, m3, m3, g3, gt, conv_w, conv_b, wq, wk, bcol, brow, head_g)


MERGE_TM = 512


def _merge_kernel(x_ref, ya_ref, yb_ref, ga_ref, gb_ref, wa_ref, wb_ref, wo_ref, o_ref):
    merged = (jax.nn.sigmoid(ga_ref[...]) * _dot(ya_ref[...], wa_ref[...])
              + jax.nn.sigmoid(gb_ref[...]) * _dot(yb_ref[...], wb_ref[...]))
    o_ref[...] = x_ref[...] + _dot(merged.astype(BF16), wo_ref[...])


def _merge(x2d, ya, yb, gates, wa, wb, wo):
    m = x2d.shape[0]
    row = lambda i: (i, 0)
    const = lambda i: (0, 0)
    return pl.pallas_call(
        _merge_kernel,
        grid=(m // MERGE_TM,),
        in_specs=[
            pl.BlockSpec((MERGE_TM, D_MODEL), row),
            pl.BlockSpec((MERGE_TM, MIX_A), row),
            pl.BlockSpec((MERGE_TM, MIX_B), row),
            pl.BlockSpec((MERGE_TM, D_MODEL), lambda i: (i, 0)),
            pl.BlockSpec((MERGE_TM, D_MODEL), lambda i: (i, 1)),
            pl.BlockSpec((MIX_A, D_MODEL), const),
            pl.BlockSpec((MIX_B, D_MODEL), const),
            pl.BlockSpec((D_MODEL, D_MODEL), const),
        ],
        out_specs=pl.BlockSpec((MERGE_TM, D_MODEL), row),
        out_shape=jax.ShapeDtypeStruct((m, D_MODEL), F32),
        compiler_params=pltpu.CompilerParams(
            dimension_semantics=("parallel",), vmem_limit_bytes=VMEM_LIMIT),
        name="merge",
    )(x2d, ya, yb, gates, gates, wa, wb, wo)


def _block_diag_ones(n, blk):
    r = jnp.arange(n) // blk
    return (r[:, None] == r[None, :]).astype(BF16)


def _arrange_w_in(w_in):
    widths = (512, 128, 128, 512, 64, 8, 512, 512, 4, 4, 512, 1024, 1024)
    parts, off = [], 0
    for wd in widths:
        parts.append(w_in[:, off:off + wd])
        off += wd
    aq, ak, av, iq, ik, iw, mx, mv, mi, mf, mo, ga, gb = parts
    z = lambda n: jnp.zeros((w_in.shape[0], n), w_in.dtype)
    cols = [aq, iq, ak, ik, z(64), av,
            iw, mi, mf, z(G_WIDTH - 16),
            mx, mv, mo, ga, gb]
    return jnp.concatenate(cols, axis=1).astype(BF16)


def kernel(x, ffn1_norm, ffn1_w_gate, ffn1_w_up, ffn1_w_down, mix_norm, w_in, q_norm, k_norm,
           idx_k_norm, conv_w, conv_b, w_mq, w_mk, b_i, b_f, m_head_norm, w_proj_a, w_proj_b,
           w_out, ffn2_norm, ffn2_w_gate, ffn2_w_up, ffn2_w_down):
    b, s, _ = x.shape
    m = b * s
    nb = s // Q_BLOCK
    nh = MLSTM_HEADS
    x2d = x.reshape(m, D_MODEL)
    for l in range(ffn1_norm.shape[0]):
        x2d = _ffn(x2d, ffn1_norm[l], ffn1_w_gate[l].astype(BF16), ffn1_w_up[l].astype(BF16),
                   ffn1_w_down[l].astype(BF16))

        gq = (jnp.tile(q_norm[l], ATT_HEADS) * (ATT_HEAD_DIM ** -0.5 * LOG2_E)).reshape(1, 512)
        gk = jnp.concatenate([jnp.tile(k_norm[l], ATT_KV_HEADS), idx_k_norm[l],
                              jnp.zeros((64,), F32)]).reshape(1, 256)
        a2d, g2d, m2d, t2d = _proj(x2d, mix_norm[l], _arrange_w_in(w_in[l]),
                                   _block_diag_ones(512, ATT_HEAD_DIM),
                                   _block_diag_ones(256, ATT_HEAD_DIM), gq, gk)

        a3 = a2d.reshape(b, s, A_WIDTH)
        vt = _v_transposed(a3[:, :, COL_V:COL_V + ATT_KV_HEADS * ATT_HEAD_DIM])
        g3 = g2d.reshape(b, s, G_WIDTH)
        iwt = jnp.swapaxes(g3[:, :, 0:IDX_HEADS].reshape(b, nb, Q_BLOCK, IDX_HEADS), 2, 3)
        ya = _dsa(a3, vt, iwt)

        gt = jnp.swapaxes(g3[:, :, IDX_HEADS:IDX_HEADS + 2 * nh], 1, 2)
        bias = jnp.concatenate([b_i[l], b_f[l]])
        yb = _mlstm(m2d.reshape(b, s, M_WIDTH), g3, gt, conv_w[l], conv_b[l].reshape(1, MIX_B),
                    w_mq[l].astype(BF16), w_mk[l].astype(BF16),
                    bias.reshape(1, 2 * nh), bias.reshape(2 * nh, 1),
                    m_head_norm[l].reshape(1, MIX_B))

        x2d = _merge(x2d, ya.reshape(m, MIX_A), yb.reshape(m, MIX_B), t2d,
                     w_proj_a[l].astype(BF16), w_proj_b[l].astype(BF16), w_out[l].astype(BF16))

        x2d = _ffn(x2d, ffn2_norm[l], ffn2_w_gate[l].astype(BF16), ffn2_w_up[l].astype(BF16),
                   ffn2_w_down[l].astype(BF16))
    return x2d.reshape(b, s, D_MODEL)
```

```python
import functools

import jax
import jax.numpy as jnp
from jax import lax
from jax.experimental import pallas as pl
from jax.experimental.pallas import tpu as pltpu

F32 = jnp.float32
BF16 = jnp.bfloat16

D_MODEL = 1024
D_FF = 2816
EPS = 1e-6
ATT_HEADS = 8
ATT_KV_HEADS = 2
ATT_HEAD_DIM = 64
IDX_HEADS = 8
IDX_HEAD_DIM = 64
TOPK_MAX = 256
Q_BLOCK = 128
CHUNK = 64
MLSTM_HEADS = 4
MLSTM_QK_DIM = 64
MLSTM_V_DIM = 128
CONV_WIDTH = 4
MIX_A = ATT_HEADS * ATT_HEAD_DIM
MIX_B = MLSTM_HEADS * MLSTM_V_DIM

LANES = 128
VMEM_LIMIT = 56 * 1024 * 1024
INT_MIN = -2147483648
NEG_BIG = -1e30
LOG2_E = 1.4426950408889634

COL_Q, COL_IQ, COL_K, COL_IK, COL_V = 0, 512, 1024, 1152, 1280
A_WIDTH = 1408
G_WIDTH = 128
M_WIDTH = 3 * MIX_B
T_WIDTH = 2 * D_MODEL
W_TOTAL = A_WIDTH + G_WIDTH + M_WIDTH + T_WIDTH


def _rms(x, g):
    return x * lax.rsqrt(jnp.mean(x * x, axis=-1, keepdims=True) + EPS) * g


def _dot(a, b):
    return jnp.dot(a, b, preferred_element_type=F32)


def _dot_nt(a, b):
    return lax.dot_general(a, b, (((1,), (1,)), ((), ())), preferred_element_type=F32)


def _dot_tn(a, b):
    return lax.dot_general(a, b, (((0,), (0,)), ((), ())), preferred_element_type=F32)


FFN_TM = 512
FFN_TF = 1408


def _ffn_kernel(x_ref, g_ref, wg_ref, wu_ref, wd_ref, o_ref, h_ref, acc_ref):
    j = pl.program_id(1)

    @pl.when(j == 0)
    def _():
        h_ref[...] = _rms(x_ref[...], g_ref[...]).astype(BF16)
        acc_ref[...] = jnp.zeros_like(acc_ref)

    h = h_ref[...]
    a = _dot(h, wg_ref[...])
    u = _dot(h, wu_ref[...])
    act = (a * jax.nn.sigmoid(a) * u).astype(BF16)
    acc_ref[...] += _dot(act, wd_ref[...])

    @pl.when(j == pl.num_programs(1) - 1)
    def _():
        o_ref[...] = x_ref[...] + 0.5 * acc_ref[...]


def _ffn(x2d, g, wg, wu, wd):
    m = x2d.shape[0]
    return pl.pallas_call(
        _ffn_kernel,
        grid=(m // FFN_TM, D_FF // FFN_TF),
        in_specs=[
            pl.BlockSpec((FFN_TM, D_MODEL), lambda i, j: (i, 0)),
            pl.BlockSpec((1, D_MODEL), lambda i, j: (0, 0)),
            pl.BlockSpec((D_MODEL, FFN_TF), lambda i, j: (0, j)),
            pl.BlockSpec((D_MODEL, FFN_TF), lambda i, j: (0, j)),
            pl.BlockSpec((FFN_TF, D_MODEL), lambda i, j: (j, 0)),
        ],
        out_specs=pl.BlockSpec((FFN_TM, D_MODEL), lambda i, j: (i, 0)),
        out_shape=jax.ShapeDtypeStruct((m, D_MODEL), F32),
        scratch_shapes=[pltpu.VMEM((FFN_TM, D_MODEL), BF16), pltpu.VMEM((FFN_TM, D_MODEL), F32)],
        compiler_params=pltpu.CompilerParams(
            dimension_semantics=("parallel", "arbitrary"), vmem_limit_bytes=VMEM_LIMIT),
        name="ffn",
    )(x2d, g.reshape(1, D_MODEL), wg, wu, wd)


PROJ_TM = 256


def _proj_kernel(x_ref, g_ref, w_ref, bdq_ref, bdk_ref, gq_ref, gk_ref,
                 a_ref, g_out_ref, m_ref, t_ref):
    h = _rms(x_ref[...], g_ref[...]).astype(BF16)

    def head_norm(p, bd_ref, gain_ref):
        ss = _dot((p * p).astype(BF16), bd_ref[...]) * (1.0 / ATT_HEAD_DIM)
        return p * lax.rsqrt(ss + EPS) * gain_ref[...]

    pq = _dot(h, w_ref[:, COL_Q:COL_IQ])
    a_ref[:, COL_Q:COL_IQ] = head_norm(pq, bdq_ref, gq_ref).astype(BF16)
    a_ref[:, COL_IQ:COL_K] = _dot(h, w_ref[:, COL_IQ:COL_K]).astype(BF16)
    pk = _dot(h, w_ref[:, COL_K:COL_V])
    a_ref[:, COL_K:COL_V] = head_norm(pk, bdk_ref, gk_ref).astype(BF16)
    a_ref[:, COL_V:A_WIDTH] = _dot(h, w_ref[:, COL_V:A_WIDTH]).astype(BF16)
    o = A_WIDTH
    g_out_ref[...] = _dot(h, w_ref[:, o:o + G_WIDTH])
    o += G_WIDTH
    for c in range(3):
        m_ref[:, c * MIX_B:(c + 1) * MIX_B] = _dot(h, w_ref[:, o + c * MIX_B:o + (c + 1) * MIX_B])
    o += M_WIDTH
    for c in range(2):
        t_ref[:, c * D_MODEL:(c + 1) * D_MODEL] = _dot(h, w_ref[:, o + c * D_MODEL:o + (c + 1) * D_MODEL])


def _proj(x2d, g, w_all, bdq, bdk, gq, gk):
    m = x2d.shape[0]
    const = lambda i: (0, 0)
    row = lambda i: (i, 0)
    return pl.pallas_call(
        _proj_kernel,
        grid=(m // PROJ_TM,),
        in_specs=[
            pl.BlockSpec((PROJ_TM, D_MODEL), row),
            pl.BlockSpec((1, D_MODEL), const),
            pl.BlockSpec((D_MODEL, W_TOTAL), const),
            pl.BlockSpec((512, 512), const),
            pl.BlockSpec((256, 256), const),
            pl.BlockSpec((1, 512), const),
            pl.BlockSpec((1, 256), const),
        ],
        out_specs=[
            pl.BlockSpec((PROJ_TM, A_WIDTH), row),
            pl.BlockSpec((PROJ_TM, G_WIDTH), row),
            pl.BlockSpec((PROJ_TM, M_WIDTH), row),
            pl.BlockSpec((PROJ_TM, T_WIDTH), row),
        ],
        out_shape=[
            jax.ShapeDtypeStruct((m, A_WIDTH), BF16),
            jax.ShapeDtypeStruct((m, G_WIDTH), F32),
            jax.ShapeDtypeStruct((m, M_WIDTH), F32),
            jax.ShapeDtypeStruct((m, T_WIDTH), F32),
        ],
        compiler_params=pltpu.CompilerParams(
            dimension_semantics=("parallel",), vmem_limit_bytes=VMEM_LIMIT),
        name="proj",
    )(x2d, g.reshape(1, D_MODEL), w_all, bdq, bdk, gq, gk)


DSA_KB = 512
DSA_KA = 256
DSA_VROWS = 80

BIT_GROUP = 256
SEARCH_CHUNK = 1024


def _bit_transpose(words):
    a = list(words)
    j, m = 16, 0x0000FFFF
    while j:
        k = 0
        while k < 32:
            t = (a[k] ^ (a[k + j] >> j)) & m
            a[k] = a[k] ^ t
            a[k + j] = a[k + j] ^ (t << j)
            k = (k + j + 1) & ~j
        j >>= 1
        m = (m ^ (m << j)) & 0xFFFFFFFF
    return a


def _heads_to_lanes(x, n_heads, dh):
    xf = x.astype(F32)
    return jnp.concatenate([xf[:, h * dh:(h + 1) * dh].T for h in range(n_heads)], axis=1).astype(BF16)


def _dsa_kernel(q_ref, iq_ref, k_ref, ik_ref, vt_ref, iwt_ref, o_ref,
                keys_ref, planes_ref, alive_ref, acc_ref, sa_ref, sb_ref, *, seq, k_sel, idx_scale):
    kb_sz = DSA_KB
    bi = pl.program_id(1)
    nkb = (bi * Q_BLOCK + Q_BLOCK + kb_sz - 1) // kb_sz
    lane = lax.broadcasted_iota(jnp.int32, (1, LANES), 1)
    limit = bi * Q_BLOCK + jnp.where(lane < CHUNK, CHUNK, 2 * CHUNK)
    row_iota = lax.broadcasted_iota(jnp.int32, (kb_sz, LANES), 0)

    iq_t = _heads_to_lanes(iq_ref[...], IDX_HEADS, IDX_HEAD_DIM)
    w = iwt_ref[...]

    def score_block(kb, _):
        off = pl.multiple_of(kb * kb_sz, kb_sz)
        ikb = ik_ref[pl.ds(off, kb_sz), :][:, :IDX_HEAD_DIM]
        logits = _dot(ikb, iq_t)
        sc = jnp.zeros((kb_sz, LANES), F32)
        for h in range(IDX_HEADS):
            sc = sc + jnp.maximum(logits[:, h * LANES:(h + 1) * LANES], 0.0) * w[h:h + 1, :]
        sc = sc * idx_scale
        bits = pltpu.bitcast(sc, jnp.int32)
        key = bits ^ ((bits >> 31) & 0x7FFFFFFF)
        key = jnp.where(sc == 0.0, 0, key)
        key = jnp.where(off + row_iota < limit, key, INT_MIN)
        keys_ref[pl.ds(off, kb_sz), :] = key
        ukey = key ^ INT_MIN
        for grp in range(kb_sz // BIT_GROUP):
            base = grp * BIT_GROUP
            planes = _bit_transpose([ukey[base + 8 * i:base + 8 * i + 8, :] for i in range(32)])
            row = pl.multiple_of((off + base) // 32, 8)
            for p in range(32):
                planes_ref[p, pl.ds(row, 8), :] = planes[p]
        return 0

    lax.fori_loop(0, nkb, score_block, 0)

    @pl.when(nkb % 2 == 1)
    def _():
        row = pl.multiple_of(nkb * (kb_sz // 32), kb_sz // 32)
        for p in range(32):
            planes_ref[p, pl.ds(row, kb_sz // 32), :] = jnp.zeros((kb_sz // 32, LANES), jnp.int32)

    def count(pred):
        def body(kb, c):
            off = pl.multiple_of(kb * kb_sz, kb_sz)
            ind = jnp.where(pred(keys_ref[pl.ds(off, kb_sz), :], off + row_iota), 1, 0)
            return c + jnp.sum(ind.reshape(kb_sz // 8, 8, LANES), axis=0)
        c8 = lax.fori_loop(0, nkb, body, jnp.zeros((8, LANES), jnp.int32))
        return jnp.sum(c8, axis=0, keepdims=True)

    zeros = jnp.zeros((1, LANES), jnp.int32)
    ch_rows = SEARCH_CHUNK // 32
    n_ch = (nkb + 1) // 2

    zeros8 = jnp.zeros((8, LANES), jnp.int32)

    def ones_in(x):
        return jnp.sum(lax.population_count(x).reshape(ch_rows // 8, 8, LANES), axis=0)

    def over_chunks(body):
        return jnp.sum(lax.fori_loop(0, n_ch, lambda c, cnt: cnt + body(c), zeros8), axis=0, keepdims=True)

    def first_chunk(c):
        row = pl.multiple_of(c * ch_rows, ch_rows)
        alive_ref[pl.ds(row, ch_rows), :] = jnp.full((ch_rows, LANES), -1, jnp.int32)
        return ones_in(planes_ref[0, pl.ds(row, ch_rows), :])

    def decide(p, t_u, above, c1):
        take = above + c1 >= k_sel
        t_u = jnp.where(take, t_u | jnp.left_shift(jnp.int32(1), 31 - p), t_u)
        return t_u, jnp.where(take, above, above + c1), jnp.where(take, 0, -1)

    def radix_pass(p, carry):
        t_u, above, flip = carry

        def chunk(c):
            row = pl.multiple_of(c * ch_rows, ch_rows)
            alive = alive_ref[pl.ds(row, ch_rows), :] & (planes_ref[p - 1, pl.ds(row, ch_rows), :] ^ flip)
            alive_ref[pl.ds(row, ch_rows), :] = alive
            return ones_in(alive & planes_ref[p, pl.ds(row, ch_rows), :])

        return decide(p, t_u, above, over_chunks(chunk))

    carry = decide(0, zeros, zeros, over_chunks(first_chunk))
    t_u, above, flip = lax.fori_loop(1, 32, radix_pass, carry)

    def last_chunk(c):
        row = pl.multiple_of(c * ch_rows, ch_rows)
        return ones_in(alive_ref[pl.ds(row, ch_rows), :] & (planes_ref[31, pl.ds(row, ch_rows), :] ^ flip))

    c_eq = over_chunks(last_chunk)
    few = t_u == 0
    thr = jnp.maximum(t_u ^ INT_MIN, INT_MIN + 1)
    need = k_sel - above
    excess = jnp.where(few, 0, c_eq - need)
    pos_bits = seq.bit_length() - 1

    @pl.when(jnp.max(excess) > 0)
    def _():
        def jbit(i, cut):
            cand = cut | jnp.left_shift(jnp.int32(1), pos_bits - 1 - i)
            c = count(lambda kk, pos: (kk == thr) & (pos < cand))
            return jnp.where(c <= need, cand, cut)
        cut = lax.fori_loop(0, pos_bits, jbit, zeros)
        cut = jnp.where(excess > 0, cut, seq)

        def drop_block(kb, _):
            off = pl.multiple_of(kb * kb_sz, kb_sz)
            kk = keys_ref[pl.ds(off, kb_sz), :]
            keys_ref[pl.ds(off, kb_sz), :] = jnp.where(
                (kk == thr) & (off + row_iota >= cut), INT_MIN, kk)
            return 0

        lax.fori_loop(0, nkb, drop_block, 0)

    reps = ATT_HEADS // ATT_KV_HEADS
    width = reps * LANES
    q_t = _heads_to_lanes(q_ref[...], ATT_HEADS, ATT_HEAD_DIM)
    acc_ref[...] = jnp.zeros_like(acc_ref)

    ka = DSA_KA

    def masked_scores(blk):
        off = pl.multiple_of(blk * ka, ka)
        bias = jnp.where(keys_ref[pl.ds(off, ka), :] >= thr, 0.0, NEG_BIG)
        bias = jnp.concatenate([bias] * reps, axis=1)
        kblk = k_ref[pl.ds(off, ka), :]
        return jnp.concatenate(
            [_dot(kblk[:, g * ATT_HEAD_DIM:(g + 1) * ATT_HEAD_DIM], q_t[:, g * width:(g + 1) * width]) + bias
             for g in range(ATT_KV_HEADS)], axis=1)

    def consume(s_ref, blk, m_old):
        off = pl.multiple_of(blk * ka, ka)
        s = s_ref[...]
        m_new = jnp.maximum(m_old, jnp.max(s, axis=0, keepdims=True))
        alpha = jnp.exp2(m_old - m_new)
        p = jnp.exp2(s - m_new).astype(BF16)
        for g in range(ATT_KV_HEADS):
            vt = vt_ref[g, :, pl.ds(off, ka)]
            acc_ref[g] = (alpha[:, g * width:(g + 1) * width] * acc_ref[g]
                          + _dot(vt, p[:, g * width:(g + 1) * width]))
        return m_new

    n_sub = nkb * (kb_sz // ka)
    sa_ref[...] = masked_scores(0)

    def att_pair(j, m_run):
        sb_ref[...] = masked_scores(2 * j + 1)
        m_run = consume(sa_ref, 2 * j, m_run)
        sa_ref[...] = masked_scores(jnp.minimum(2 * j + 2, n_sub - 1))
        return consume(sb_ref, 2 * j + 1, m_run)

    lax.fori_loop(0, n_sub // 2, att_pair, jnp.full((1, ATT_HEADS * LANES), NEG_BIG, F32))

    pieces = []
    for g in range(ATT_KV_HEADS):
        a = acc_ref[g]
        o = a[0:ATT_HEAD_DIM, :] / a[ATT_HEAD_DIM:ATT_HEAD_DIM + 1, :]
        pieces += [o[:, r * LANES:(r + 1) * LANES] for r in range(reps)]
    o_ref[...] = jnp.concatenate(pieces, axis=0).T.astype(o_ref.dtype)


def _v_transposed(v):
    b, s, _ = v.shape
    vt = jnp.swapaxes(v.reshape(b, s, ATT_KV_HEADS, ATT_HEAD_DIM), 1, 3)
    vt = jnp.swapaxes(vt, 1, 2)
    ones = jnp.ones((b, ATT_KV_HEADS, 1, s), v.dtype)
    pad = jnp.zeros((b, ATT_KV_HEADS, DSA_VROWS - ATT_HEAD_DIM - 1, s), v.dtype)
    return jnp.concatenate([vt, ones, pad], axis=2)


def _dsa(a3, vt, iwt):
    b, s, _ = a3.shape
    nb = s // Q_BLOCK
    k_sel = min(TOPK_MAX, s // 4)
    idx_scale = (IDX_HEAD_DIM ** -0.5) * (IDX_HEADS ** -0.5)
    kern = functools.partial(_dsa_kernel, seq=s, k_sel=k_sel, idx_scale=idx_scale)
    return pl.pallas_call(
        kern,
        grid=(b, nb),
        in_specs=[
            pl.BlockSpec((None, Q_BLOCK, 512), lambda bb, i: (bb, i, COL_Q // 512)),
            pl.BlockSpec((None, Q_BLOCK, 512), lambda bb, i: (bb, i, COL_IQ // 512)),
            pl.BlockSpec((None, s, LANES), lambda bb, i: (bb, 0, COL_K // LANES)),
            pl.BlockSpec((None, s, LANES), lambda bb, i: (bb, 0, COL_IK // LANES)),
            pl.BlockSpec((None, ATT_KV_HEADS, DSA_VROWS, s), lambda bb, i: (bb, 0, 0, 0)),
            pl.BlockSpec((None, None, IDX_HEADS, LANES), lambda bb, i: (bb, i, 0, 0)),
        ],
        out_specs=pl.BlockSpec((None, Q_BLOCK, MIX_A), lambda bb, i: (bb, i, 0)),
        out_shape=jax.ShapeDtypeStruct((b, s, MIX_A), BF16),
        scratch_shapes=[
            pltpu.VMEM((s, LANES), jnp.int32),
            pltpu.VMEM((32, s // 32, LANES), jnp.int32),
            pltpu.VMEM((s // 32, LANES), jnp.int32),
            pltpu.VMEM((ATT_KV_HEADS, DSA_VROWS, (ATT_HEADS // ATT_KV_HEADS) * LANES), F32),
            pltpu.VMEM((DSA_KA, ATT_HEADS * LANES), F32),
            pltpu.VMEM((DSA_KA, ATT_HEADS * LANES), F32),
        ],
        compiler_params=pltpu.CompilerParams(
            dimension_semantics=("parallel", "arbitrary"), vmem_limit_bytes=VMEM_LIMIT),
        name="dsa",
    )(a3, a3, a3, a3, vt, iwt)


ML_L = 128
ML_TS = 512
ML_PAD = 8


def _mlstm_kernel(mx_ref, mv_ref, mo_ref, gc_ref, gr_ref, cw_ref, cb_ref, wq_ref, wk_ref,
                  bcol_ref, brow_ref, hg_ref, o_ref,
                  xbuf_ref, q_ref, k_ref, c_ref, n_ref, m_ref):
    t_idx = pl.program_id(1)
    nh, dk, dv, L = MLSTM_HEADS, MLSTM_QK_DIM, MLSTM_V_DIM, ML_L

    @pl.when(t_idx == 0)
    def _():
        xbuf_ref[0:ML_PAD, :] = jnp.zeros((ML_PAD, MIX_B), F32)
        c_ref[...] = jnp.zeros_like(c_ref)
        n_ref[...] = jnp.zeros_like(n_ref)
        m_ref[...] = jnp.zeros_like(m_ref)

    xbuf_ref[ML_PAD:ML_PAD + ML_TS, :] = mx_ref[...]
    xc = cb_ref[...] + jnp.zeros((ML_TS, MIX_B), F32)
    for j in range(CONV_WIDTH):
        s0 = ML_PAD - (CONV_WIDTH - 1) + j
        xc = xc + xbuf_ref[s0:s0 + ML_TS, :] * cw_ref[j:j + 1, :]
    xbuf_ref[0:ML_PAD, :] = mx_ref[ML_TS - ML_PAD:ML_TS, :]
    xc = (xc * jax.nn.sigmoid(xc)).astype(BF16)
    for h in range(nh):
        xh = xc[:, h * dv:(h + 1) * dv]
        q_ref[h] = (_dot(xh, wq_ref[h]) * (dk ** -0.5)).astype(BF16)
        k_ref[h] = _dot(xh, wk_ref[h])

    gc = gc_ref[...]
    li_c = gc[:, 8:8 + nh] + bcol_ref[0:1, 0:nh]
    lf_c = jax.nn.log_sigmoid(gc[:, 8 + nh:8 + 2 * nh] + bcol_ref[0:1, nh:2 * nh])
    gr = gr_ref[...]
    li_r = gr[0:nh, :] + brow_ref[0:nh, :]
    lf_r = jax.nn.log_sigmoid(gr[nh:2 * nh, :] + brow_ref[nh:2 * nh, :])

    ri = lax.broadcasted_iota(jnp.int32, (L, L), 0)
    ci = lax.broadcasted_iota(jnp.int32, (L, L), 1)
    causal = ri >= ci
    tril = jnp.where(causal, 1.0, 0.0).astype(F32)
    triu = jnp.where(ri <= ci, 1.0, 0.0).astype(F32)
    hi = lax.Precision.HIGHEST

    for c in range(ML_TS // L):
        r0 = c * L
        b_c = jnp.dot(tril, lf_c[r0:r0 + L, :], precision=hi, preferred_element_type=F32)
        b_r = jnp.dot(lf_r[:, r0:r0 + L], triu, precision=hi, preferred_element_type=F32)
        outs = []
        for h in range(nh):
            bc = b_c[:, h:h + 1]
            br = b_r[h:h + 1, :]
            ir = li_r[h:h + 1, r0:r0 + L]
            ic = li_c[r0:r0 + L, h:h + 1]
            m_prev = m_ref[h]
            d = jnp.where(causal, bc - br + ir, NEG_BIG)
            inter = bc + m_prev
            m_t = jnp.maximum(jnp.max(d, axis=-1, keepdims=True), inter)
            qh = q_ref[h, r0:r0 + L, :]
            kh = k_ref[h, r0:r0 + L, :]
            vh = mv_ref[r0:r0 + L, h * dv:(h + 1) * dv].astype(BF16)
            wt = _dot_nt(qh, kh.astype(BF16)) * jnp.exp(d - m_t)
            w_int = jnp.exp(inter - m_t)
            num = _dot(wt.astype(BF16), vh) + w_int * _dot(qh, c_ref[h].astype(BF16))
            qn = jnp.sum(qh.astype(F32) * n_ref[h], axis=-1, keepdims=True)
            den = jnp.sum(wt, axis=-1, keepdims=True) + w_int * qn
            outs.append(num / jnp.maximum(jnp.abs(den), jnp.exp(-m_t)))
            b_last = bc[L - 1:L, :]
            wk_r = b_last - br + ir
            m_new = jnp.maximum(b_last + m_prev, jnp.max(wk_r, axis=-1, keepdims=True))
            ws_c = jnp.exp(b_last - bc + ic - m_new)
            wc = jnp.exp(b_last + m_prev - m_new)
            kw = kh * ws_c
            c_ref[h] = wc * c_ref[h] + _dot_tn(kw.astype(BF16), vh)
            n_ref[h] = wc * n_ref[h] + jnp.sum(kw, axis=0, keepdims=True)
            m_ref[h] = m_new
        hh = jnp.concatenate(
            [_rms(outs[h], hg_ref[0:1, h * dv:(h + 1) * dv]) for h in range(nh)], axis=1)
        o_ref[r0:r0 + L, :] = (hh * jax.nn.sigmoid(mo_ref[r0:r0 + L, :])).astype(o_ref.dtype)


def _mlstm(m3, g3, gt, conv_w, conv_b, wq, wk, bcol, brow, head_g):
    b, s, _ = m3.shape
    nh = MLSTM_HEADS
    tile = lambda c: pl.BlockSpec((None, ML_TS, MIX_B), lambda bb, t: (bb, t, c))
    const2 = lambda shape: pl.BlockSpec(shape, lambda bb, t: (0, 0))
    const3 = lambda shape: pl.BlockSpec(shape, lambda bb, t: (0, 0, 0))
    return pl.pallas_call(
        _mlstm_kernel,
        grid=(b, s // ML_TS),
        in_specs=[
            tile(0), tile(1), tile(2),
            pl.BlockSpec((None, ML_TS, G_WIDTH), lambda bb, t: (bb, t, 0)),
            pl.BlockSpec((None, 2 * nh, ML_TS), lambda bb, t: (bb, 0, t)),
            const2((CONV_WIDTH, MIX_B)), const2((1, MIX_B)),
            const3((nh, MLSTM_V_DIM, MLSTM_QK_DIM)), const3((nh, MLSTM_V_DIM, MLSTM_QK_DIM)),
            const2((1, 2 * nh)), const2((2 * nh, 1)), const2((1, MIX_B)),
        ],
        out_specs=pl.BlockSpec((None, ML_TS, MIX_B), lambda bb, t: (bb, t, 0)),
        out_shape=jax.ShapeDtypeStruct((b, s, MIX_B), BF16),
        scratch_shapes=[
            pltpu.VMEM((ML_PAD + ML_TS, MIX_B), F32),
            pltpu.VMEM((nh, ML_TS, MLSTM_QK_DIM), BF16),
            pltpu.VMEM((nh, ML_TS, MLSTM_QK_DIM), F32),
            pltpu.VMEM((nh, MLSTM_QK_DIM, MLSTM_V_DIM), F32),
            pltpu.VMEM((nh, 1, MLSTM_QK_DIM), F32),
            pltpu.VMEM((nh, 1, 1), F32),
        ],
        compiler_params=pltpu.CompilerParams(
            dimension_semantics=("parallel", "arbitrary"), vmem_limit_bytes=VMEM_LIMIT),
        name="mlstm",
    )(m3, m3, m3, g3, gt, conv_w, conv_b, wq, wk, bcol, brow, head_g)


MERGE_TM = 512


def _merge_kernel(x_ref, ya_ref, yb_ref, ga_ref, gb_ref, wa_ref, wb_ref, wo_ref, o_ref):
    merged = (jax.nn.sigmoid(ga_ref[...]) * _dot(ya_ref[...], wa_ref[...])
              + jax.nn.sigmoid(gb_ref[...]) * _dot(yb_ref[...], wb_ref[...]))
    o_ref[...] = x_ref[...] + _dot(merged.astype(BF16), wo_ref[...])


def _merge(x2d, ya, yb, gates, wa, wb, wo):
    m = x2d.shape[0]
    row = lambda i: (i, 0)
    const = lambda i: (0, 0)
    return pl.pallas_call(
        _merge_kernel,
        grid=(m // MERGE_TM,),
        in_specs=[
            pl.BlockSpec((MERGE_TM, D_MODEL), row),
            pl.BlockSpec((MERGE_TM, MIX_A), row),
            pl.BlockSpec((MERGE_TM, MIX_B), row),
            pl.BlockSpec((MERGE_TM, D_MODEL), lambda i: (i, 0)),
            pl.BlockSpec((MERGE_TM, D_MODEL), lambda i: (i, 1)),
            pl.BlockSpec((MIX_A, D_MODEL), const),
            pl.BlockSpec((MIX_B, D_MODEL), const),
            pl.BlockSpec((D_MODEL, D_MODEL), const),
        ],
        out_specs=pl.BlockSpec((MERGE_TM, D_MODEL), row),
        out_shape=jax.ShapeDtypeStruct((m, D_MODEL), F32),
        compiler_params=pltpu.CompilerParams(
            dimension_semantics=("parallel",), vmem_limit_bytes=VMEM_LIMIT),
        name="merge",
    )(x2d, ya, yb, gates, gates, wa, wb, wo)


def _block_diag_ones(n, blk):
    r = jnp.arange(n) // blk
    return (r[:, None] == r[None, :]).astype(BF16)


def _arrange_w_in(w_in):
    widths = (512, 128, 128, 512, 64, 8, 512, 512, 4, 4, 512, 1024, 1024)
    parts, off = [], 0
    for wd in widths:
        parts.append(w_in[:, off:off + wd])
        off += wd
    aq, ak, av, iq, ik, iw, mx, mv, mi, mf, mo, ga, gb = parts
    z = lambda n: jnp.zeros((w_in.shape[0], n), w_in.dtype)
    cols = [aq, iq, ak, ik, z(64), av,
            iw, mi, mf, z(G_WIDTH - 16),
            mx, mv, mo, ga, gb]
    return jnp.concatenate(cols, axis=1).astype(BF16)


def kernel(x, ffn1_norm, ffn1_w_gate, ffn1_w_up, ffn1_w_down, mix_norm, w_in, q_norm, k_norm,
           idx_k_norm, conv_w, conv_b, w_mq, w_mk, b_i, b_f, m_head_norm, w_proj_a, w_proj_b,
           w_out, ffn2_norm, ffn2_w_gate, ffn2_w_up, ffn2_w_down):
    b, s, _ = x.shape
    m = b * s
    nb = s // Q_BLOCK
    nh = MLSTM_HEADS
    x2d = x.reshape(m, D_MODEL)
    for l in range(ffn1_norm.shape[0]):
        x2d = _ffn(x2d, ffn1_norm[l], ffn1_w_gate[l].astype(BF16), ffn1_w_up[l].astype(BF16),
                   ffn1_w_down[l].astype(BF16))

        gq = (jnp.tile(q_norm[l], ATT_HEADS) * (ATT_HEAD_DIM ** -0.5 * LOG2_E)).reshape(1, 512)
        gk = jnp.concatenate([jnp.tile(k_norm[l], ATT_KV_HEADS), idx_k_norm[l],
                              jnp.zeros((64,), F32)]).reshape(1, 256)
        a2d, g2d, m2d, t2d = _proj(x2d, mix_norm[l], _arrange_w_in(w_in[l]),
                                   _block_diag_ones(512, ATT_HEAD_DIM),
                                   _block_diag_ones(256, ATT_HEAD_DIM), gq, gk)

        a3 = a2d.reshape(b, s, A_WIDTH)
        vt = _v_transposed(a3[:, :, COL_V:COL_V + ATT_KV_HEADS * ATT_HEAD_DIM])
        g3 = g2d.reshape(b, s, G_WIDTH)
        iwt = jnp.swapaxes(g3[:, :, 0:IDX_HEADS].reshape(b, nb, Q_BLOCK, IDX_HEADS), 2, 3)
        ya = _dsa(a3, vt, iwt)

        gt = jnp.swapaxes(g3[:, :, IDX_HEADS:IDX_HEADS + 2 * nh], 1, 2)
        bias = jnp.concatenate([b_i[l], b_f[l]])
        yb = _mlstm(m2d.reshape(b, s, M_WIDTH), g3, gt, conv_w[l], conv_b[l].reshape(1, MIX_B),
                    w_mq[l].astype(BF16), w_mk[l].astype(BF16),
                    bias.reshape(1, 2 * nh), bias.reshape(2 * nh, 1),
                    m_head_norm[l].reshape(1, MIX_B))

        x2d = _merge(x2d, ya.reshape(m, MIX_A), yb.reshape(m, MIX_B), t2d,
                     w_proj_a[l].astype(BF16), w_proj_b[l].astype(BF16), w_out[l].astype(BF16))

        x2d = _ffn(x2d, ffn2_norm[l], ffn2_w_gate[l].astype(BF16), ffn2_w_up[l].astype(BF16),
                   ffn2_w_down[l].astype(BF16))
    return x2d.reshape(b, s, D_MODEL)
```

```python
import functools

import jax
import jax.numpy as jnp
from jax import lax
from jax.experimental import pallas as pl
from jax.experimental.pallas import tpu as pltpu

F32 = jnp.float32
BF16 = jnp.bfloat16

D_MODEL = 1024
D_FF = 2816
EPS = 1e-6
ATT_HEADS = 8
ATT_KV_HEADS = 2
ATT_HEAD_DIM = 64
IDX_HEADS = 8
IDX_HEAD_DIM = 64
TOPK_MAX = 256
Q_BLOCK = 128
CHUNK = 64
MLSTM_HEADS = 4
MLSTM_QK_DIM = 64
MLSTM_V_DIM = 128
CONV_WIDTH = 4
MIX_A = ATT_HEADS * ATT_HEAD_DIM
MIX_B = MLSTM_HEADS * MLSTM_V_DIM

LANES = 128
VMEM_LIMIT = 56 * 1024 * 1024
INT_MIN = -2147483648
NEG_BIG = -1e30
LOG2_E = 1.4426950408889634

COL_Q, COL_IQ, COL_K, COL_IK, COL_V = 0, 512, 1024, 1152, 1280
A_WIDTH = 1408
G_WIDTH = 128
M_WIDTH = 3 * MIX_B
T_WIDTH = 2 * D_MODEL
W_TOTAL = A_WIDTH + G_WIDTH + M_WIDTH + T_WIDTH


def _rms(x, g):
    return x * lax.rsqrt(jnp.mean(x * x, axis=-1, keepdims=True) + EPS) * g


def _dot(a, b):
    return jnp.dot(a, b, preferred_element_type=F32)


def _dot_nt(a, b):
    return lax.dot_general(a, b, (((1,), (1,)), ((), ())), preferred_element_type=F32)


def _dot_tn(a, b):
    return lax.dot_general(a, b, (((0,), (0,)), ((), ())), preferred_element_type=F32)


FFN_TM = 512
FFN_TF = 1408


def _ffn_kernel(x_ref, g_ref, wg_ref, wu_ref, wd_ref, o_ref):
    x = x_ref[...]
    h = _rms(x, g_ref[...]).astype(BF16)
    acc = None
    for c in range(D_FF // FFN_TF):
        cols = slice(c * FFN_TF, (c + 1) * FFN_TF)
        a = _dot(h, wg_ref[:, cols])
        u = _dot(h, wu_ref[:, cols])
        act = (a * jax.nn.sigmoid(a) * u).astype(BF16)
        part = _dot(act, wd_ref[cols, :])
        acc = part if acc is None else acc + part
    o_ref[...] = x + 0.5 * acc


def _ffn(x2d, g, wg, wu, wd):
    m = x2d.shape[0]
    const = lambda i: (0, 0)
    resident = lambda shape: pl.BlockSpec(shape, const, pipeline_mode=pl.Buffered(1))
    return pl.pallas_call(
        _ffn_kernel,
        grid=(m // FFN_TM,),
        in_specs=[
            pl.BlockSpec((FFN_TM, D_MODEL), lambda i: (i, 0)),
            pl.BlockSpec((1, D_MODEL), const),
            resident((D_MODEL, D_FF)),
            resident((D_MODEL, D_FF)),
            resident((D_FF, D_MODEL)),
        ],
        out_specs=pl.BlockSpec((FFN_TM, D_MODEL), lambda i: (i, 0)),
        out_shape=jax.ShapeDtypeStruct((m, D_MODEL), F32),
        compiler_params=pltpu.CompilerParams(
            dimension_semantics=("parallel",), vmem_limit_bytes=VMEM_LIMIT),
        name="ffn",
    )(x2d, g.reshape(1, D_MODEL), wg, wu, wd)


PROJ_TM = 512


def _proj_kernel(x_ref, g_ref, w_ref, bdq_ref, bdk_ref, gq_ref, gk_ref,
                 a_ref, g_out_ref, m_ref, t_ref):
    h = _rms(x_ref[...], g_ref[...]).astype(BF16)

    def head_norm(p, bd_ref, gain_ref):
        ss = _dot((p * p).astype(BF16), bd_ref[...]) * (1.0 / ATT_HEAD_DIM)
        return p * lax.rsqrt(ss + EPS) * gain_ref[...]

    pq = _dot(h, w_ref[:, COL_Q:COL_IQ])
    a_ref[:, COL_Q:COL_IQ] = head_norm(pq, bdq_ref, gq_ref).astype(BF16)
    a_ref[:, COL_IQ:COL_K] = _dot(h, w_ref[:, COL_IQ:COL_K]).astype(BF16)
    pk = _dot(h, w_ref[:, COL_K:COL_V])
    a_ref[:, COL_K:COL_V] = head_norm(pk, bdk_ref, gk_ref).astype(BF16)
    a_ref[:, COL_V:A_WIDTH] = _dot(h, w_ref[:, COL_V:A_WIDTH]).astype(BF16)
    o = A_WIDTH
    g_out_ref[...] = _dot(h, w_ref[:, o:o + G_WIDTH])
    o += G_WIDTH
    for c in range(3):
        m_ref[:, c * MIX_B:(c + 1) * MIX_B] = _dot(h, w_ref[:, o + c * MIX_B:o + (c + 1) * MIX_B])
    o += M_WIDTH
    for c in range(2):
        t_ref[:, c * D_MODEL:(c + 1) * D_MODEL] = _dot(h, w_ref[:, o + c * D_MODEL:o + (c + 1) * D_MODEL])


def _proj(x2d, g, w_all, bdq, bdk, gq, gk):
    m = x2d.shape[0]
    const = lambda i: (0, 0)
    row = lambda i: (i, 0)
    return pl.pallas_call(
        _proj_kernel,
        grid=(m // PROJ_TM,),
        in_specs=[
            pl.BlockSpec((PROJ_TM, D_MODEL), row),
            pl.BlockSpec((1, D_MODEL), const),
            pl.BlockSpec((D_MODEL, W_TOTAL), const, pipeline_mode=pl.Buffered(1)),
            pl.BlockSpec((512, 512), const),
            pl.BlockSpec((256, 256), const),
            pl.BlockSpec((1, 512), const),
            pl.BlockSpec((1, 256), const),
        ],
        out_specs=[
            pl.BlockSpec((PROJ_TM, A_WIDTH), row),
            pl.BlockSpec((PROJ_TM, G_WIDTH), row),
            pl.BlockSpec((PROJ_TM, M_WIDTH), row),
            pl.BlockSpec((PROJ_TM, T_WIDTH), row),
        ],
        out_shape=[
            jax.ShapeDtypeStruct((m, A_WIDTH), BF16),
            jax.ShapeDtypeStruct((m, G_WIDTH), F32),
            jax.ShapeDtypeStruct((m, M_WIDTH), F32),
            jax.ShapeDtypeStruct((m, T_WIDTH), F32),
        ],
        compiler_params=pltpu.CompilerParams(
            dimension_semantics=("parallel",), vmem_limit_bytes=VMEM_LIMIT),
        name="proj",
    )(x2d, g.reshape(1, D_MODEL), w_all, bdq, bdk, gq, gk)


DSA_KB = 512
DSA_KA = 256
DSA_VROWS = 80

BIT_GROUP = 256
SEARCH_CHUNK = 1024


def _bit_transpose(words):
    a = list(words)
    j, m = 16, 0x0000FFFF
    while j:
        k = 0
        while k < 32:
            t = (a[k] ^ (a[k + j] >> j)) & m
            a[k] = a[k] ^ t
            a[k + j] = a[k + j] ^ (t << j)
            k = (k + j + 1) & ~j
        j >>= 1
        m = (m ^ (m << j)) & 0xFFFFFFFF
    return a


def _heads_to_lanes(x, n_heads, dh):
    xf = x.astype(F32)
    return jnp.concatenate([xf[:, h * dh:(h + 1) * dh].T for h in range(n_heads)], axis=1).astype(BF16)


def _dsa_kernel(q_ref, iq_ref, k_ref, ik_ref, vt_ref, iwt_ref, o_ref,
                keys_ref, planes_ref, alive_ref, la_ref, lb_ref, acc_ref, sa_ref, sb_ref,
                *, seq, k_sel, idx_scale):
    kb_sz = DSA_KB
    bi = pl.program_id(1)
    nkb = (bi * Q_BLOCK + Q_BLOCK + kb_sz - 1) // kb_sz
    lane = lax.broadcasted_iota(jnp.int32, (1, LANES), 1)
    limit = bi * Q_BLOCK + jnp.where(lane < CHUNK, CHUNK, 2 * CHUNK)
    row_iota = lax.broadcasted_iota(jnp.int32, (kb_sz, LANES), 0)

    iq_t = _heads_to_lanes(iq_ref[...], IDX_HEADS, IDX_HEAD_DIM)
    w = iwt_ref[...]

    def logits_of(blk):
        off = pl.multiple_of(blk * kb_sz, kb_sz)
        return _dot(ik_ref[pl.ds(off, kb_sz), :][:, :IDX_HEAD_DIM], iq_t)

    def keys_from(l_ref, blk, masked):
        off = pl.multiple_of(blk * kb_sz, kb_sz)
        sc = jnp.zeros((kb_sz, LANES), F32)
        for h in range(IDX_HEADS):
            sc = sc + jnp.maximum(l_ref[:, h * LANES:(h + 1) * LANES], 0.0) * w[h:h + 1, :]
        sc = sc * idx_scale
        bits = pltpu.bitcast(sc, jnp.int32)
        key = bits ^ ((bits >> 31) & 0x7FFFFFFF)
        key = jnp.where(sc == 0.0, 0, key)
        if masked:
            key = jnp.where(off + row_iota < limit, key, INT_MIN)
        keys_ref[pl.ds(off, kb_sz), :] = key
        ukey = key ^ INT_MIN
        for grp in range(kb_sz // BIT_GROUP):
            base = grp * BIT_GROUP
            planes = _bit_transpose([ukey[base + 8 * i:base + 8 * i + 8, :] for i in range(32)])
            row = pl.multiple_of((off + base) // 32, 8)
            for p in range(32):
                planes_ref[p, pl.ds(row, 8), :] = planes[p]

    n_pairs = (nkb + 1) // 2

    def score_pair(j, masked):
        lb_ref[...] = logits_of(2 * j + 1)
        keys_from(la_ref, 2 * j, masked)
        la_ref[...] = logits_of(jnp.minimum(2 * j + 2, 2 * n_pairs - 1))
        keys_from(lb_ref, 2 * j + 1, masked)

    la_ref[...] = logits_of(0)
    lax.fori_loop(0, n_pairs - 1, lambda j, c: (score_pair(j, False), c)[1], 0)
    score_pair(n_pairs - 1, True)

    def count(pred):
        def body(kb, c):
            off = pl.multiple_of(kb * kb_sz, kb_sz)
            ind = jnp.where(pred(keys_ref[pl.ds(off, kb_sz), :], off + row_iota), 1, 0)
            return c + jnp.sum(ind.reshape(kb_sz // 8, 8, LANES), axis=0)
        c8 = lax.fori_loop(0, nkb, body, jnp.zeros((8, LANES), jnp.int32))
        return jnp.sum(c8, axis=0, keepdims=True)

    zeros = jnp.zeros((1, LANES), jnp.int32)
    ch_rows = SEARCH_CHUNK // 32
    n_ch = (nkb + 1) // 2

    zeros8 = jnp.zeros((8, LANES), jnp.int32)

    def ones_in(x):
        return jnp.sum(lax.population_count(x).reshape(ch_rows // 8, 8, LANES), axis=0)

    def over_chunks(body):
        return jnp.sum(lax.fori_loop(0, n_ch, lambda c, cnt: cnt + body(c), zeros8), axis=0, keepdims=True)

    def first_chunk(c):
        row = pl.multiple_of(c * ch_rows, ch_rows)
        alive_ref[pl.ds(row, ch_rows), :] = jnp.full((ch_rows, LANES), -1, jnp.int32)
        return ones_in(planes_ref[0, pl.ds(row, ch_rows), :])

    def decide(p, t_u, above, c1):
        take = above + c1 >= k_sel
        t_u = jnp.where(take, t_u | jnp.left_shift(jnp.int32(1), 31 - p), t_u)
        return t_u, jnp.where(take, above, above + c1), jnp.where(take, 0, -1)

    def radix_pass(p, carry):
        t_u, above, flip = carry

        def chunk(c):
            row = pl.multiple_of(c * ch_rows, ch_rows)
            alive = alive_ref[pl.ds(row, ch_rows), :] & (planes_ref[p - 1, pl.ds(row, ch_rows), :] ^ flip)
            alive_ref[pl.ds(row, ch_rows), :] = alive
            return ones_in(alive & planes_ref[p, pl.ds(row, ch_rows), :])

        return decide(p, t_u, above, over_chunks(chunk))

    carry = decide(0, zeros, zeros, over_chunks(first_chunk))
    t_u, above, flip = lax.fori_loop(1, 32, radix_pass, carry)

    def last_chunk(c):
        row = pl.multiple_of(c * ch_rows, ch_rows)
        return ones_in(alive_ref[pl.ds(row, ch_rows), :] & (planes_ref[31, pl.ds(row, ch_rows), :] ^ flip))

    c_eq = over_chunks(last_chunk)
    few = t_u == 0
    thr = jnp.maximum(t_u ^ INT_MIN, INT_MIN + 1)
    need = k_sel - above
    excess = jnp.where(few, 0, c_eq - need)
    pos_bits = seq.bit_length() - 1

    @pl.when(jnp.max(excess) > 0)
    def _():
        def jbit(i, cut):
            cand = cut | jnp.left_shift(jnp.int32(1), pos_bits - 1 - i)
            c = count(lambda kk, pos: (kk == thr) & (pos < cand))
            return jnp.where(c <= need, cand, cut)
        cut = lax.fori_loop(0, pos_bits, jbit, zeros)
        cut = jnp.where(excess > 0, cut, seq)

        def drop_block(kb, _):
            off = pl.multiple_of(kb * kb_sz, kb_sz)
            kk = keys_ref[pl.ds(off, kb_sz), :]
            keys_ref[pl.ds(off, kb_sz), :] = jnp.where(
                (kk == thr) & (off + row_iota >= cut), INT_MIN, kk)
            return 0

        lax.fori_loop(0, nkb, drop_block, 0)

    reps = ATT_HEADS // ATT_KV_HEADS
    width = reps * LANES
    q_t = _heads_to_lanes(q_ref[...], ATT_HEADS, ATT_HEAD_DIM)
    acc_ref[...] = jnp.zeros_like(acc_ref)

    ka = DSA_KA

    def masked_scores(blk):
        off = pl.multiple_of(blk * ka, ka)
        bias = jnp.where(keys_ref[pl.ds(off, ka), :] >= thr, 0.0, NEG_BIG)
        bias = jnp.concatenate([bias] * reps, axis=1)
        kblk = k_ref[pl.ds(off, ka), :]
        return jnp.concatenate(
            [_dot(kblk[:, g * ATT_HEAD_DIM:(g + 1) * ATT_HEAD_DIM], q_t[:, g * width:(g + 1) * width]) + bias
             for g in range(ATT_KV_HEADS)], axis=1)

    def consume(s_ref, blk, m_old):
        off = pl.multiple_of(blk * ka, ka)
        s = s_ref[...]
        m_new = jnp.maximum(m_old, jnp.max(s, axis=0, keepdims=True))
        alpha = jnp.exp2(m_old - m_new)
        p = jnp.exp2(s - m_new).astype(BF16)
        for g in range(ATT_KV_HEADS):
            vt = vt_ref[g, :, pl.ds(off, ka)]
            acc_ref[g] = (alpha[:, g * width:(g + 1) * width] * acc_ref[g]
                          + _dot(vt, p[:, g * width:(g + 1) * width]))
        return m_new

    n_sub = nkb * (kb_sz // ka)
    sa_ref[...] = masked_scores(0)

    def att_pair(j, m_run):
        sb_ref[...] = masked_scores(2 * j + 1)
        m_run = consume(sa_ref, 2 * j, m_run)
        sa_ref[...] = masked_scores(jnp.minimum(2 * j + 2, n_sub - 1))
        return consume(sb_ref, 2 * j + 1, m_run)

    lax.fori_loop(0, n_sub // 2, att_pair, jnp.full((1, ATT_HEADS * LANES), NEG_BIG, F32))

    pieces = []
    for g in range(ATT_KV_HEADS):
        a = acc_ref[g]
        o = a[0:ATT_HEAD_DIM, :] / a[ATT_HEAD_DIM:ATT_HEAD_DIM + 1, :]
        pieces += [o[:, r * LANES:(r + 1) * LANES] for r in range(reps)]
    o_ref[...] = jnp.concatenate(pieces, axis=0).T.astype(o_ref.dtype)


def _v_transposed(v):
    b, s, _ = v.shape
    vt = jnp.swapaxes(v.reshape(b, s, ATT_KV_HEADS, ATT_HEAD_DIM), 1, 3)
    vt = jnp.swapaxes(vt, 1, 2)
    ones = jnp.ones((b, ATT_KV_HEADS, 1, s), v.dtype)
    pad = jnp.zeros((b, ATT_KV_HEADS, DSA_VROWS - ATT_HEAD_DIM - 1, s), v.dtype)
    return jnp.concatenate([vt, ones, pad], axis=2)


def _dsa(a3, vt, iwt):
    b, s, _ = a3.shape
    nb = s // Q_BLOCK
    k_sel = min(TOPK_MAX, s // 4)
    idx_scale = (IDX_HEAD_DIM ** -0.5) * (IDX_HEADS ** -0.5)
    kern = functools.partial(_dsa_kernel, seq=s, k_sel=k_sel, idx_scale=idx_scale)
    return pl.pallas_call(
        kern,
        grid=(b, nb),
        in_specs=[
            pl.BlockSpec((None, Q_BLOCK, 512), lambda bb, i: (bb, i, COL_Q // 512)),
            pl.BlockSpec((None, Q_BLOCK, 512), lambda bb, i: (bb, i, COL_IQ // 512)),
            pl.BlockSpec((None, s, LANES), lambda bb, i: (bb, 0, COL_K // LANES)),
            pl.BlockSpec((None, s, LANES), lambda bb, i: (bb, 0, COL_IK // LANES)),
            pl.BlockSpec((None, ATT_KV_HEADS, DSA_VROWS, s), lambda bb, i: (bb, 0, 0, 0)),
            pl.BlockSpec((None, None, IDX_HEADS, LANES), lambda bb, i: (bb, i, 0, 0)),
        ],
        out_specs=pl.BlockSpec((None, Q_BLOCK, MIX_A), lambda bb, i: (bb, i, 0)),
        out_shape=jax.ShapeDtypeStruct((b, s, MIX_A), BF16),
        scratch_shapes=[
            pltpu.VMEM((s, LANES), jnp.int32),
            pltpu.VMEM((32, s // 32, LANES), jnp.int32),
            pltpu.VMEM((s // 32, LANES), jnp.int32),
            pltpu.VMEM((DSA_KB, IDX_HEADS * LANES), F32),
            pltpu.VMEM((DSA_KB, IDX_HEADS * LANES), F32),
            pltpu.VMEM((ATT_KV_HEADS, DSA_VROWS, (ATT_HEADS // ATT_KV_HEADS) * LANES), F32),
            pltpu.VMEM((DSA_KA, ATT_HEADS * LANES), F32),
            pltpu.VMEM((DSA_KA, ATT_HEADS * LANES), F32),
        ],
        compiler_params=pltpu.CompilerParams(
            dimension_semantics=("parallel", "arbitrary"), vmem_limit_bytes=VMEM_LIMIT),
        name="dsa",
    )(a3, a3, a3, a3, vt, iwt)


ML_L = 128
ML_TS = 512
ML_PAD = 8


def _mlstm_kernel(mx_ref, mv_ref, mo_ref, gc_ref, gr_ref, cw_ref, cb_ref, wq_ref, wk_ref,
                  bcol_ref, brow_ref, hg_ref, o_ref,
                  xbuf_ref, q_ref, k_ref, c_ref, n_ref, m_ref):
    t_idx = pl.program_id(1)
    nh, dk, dv, L = MLSTM_HEADS, MLSTM_QK_DIM, MLSTM_V_DIM, ML_L

    @pl.when(t_idx == 0)
    def _():
        xbuf_ref[0:ML_PAD, :] = jnp.zeros((ML_PAD, MIX_B), F32)
        c_ref[...] = jnp.zeros_like(c_ref)
        n_ref[...] = jnp.zeros_like(n_ref)
        m_ref[...] = jnp.zeros_like(m_ref)

    xbuf_ref[ML_PAD:ML_PAD + ML_TS, :] = mx_ref[...]
    xc = cb_ref[...] + jnp.zeros((ML_TS, MIX_B), F32)
    for j in range(CONV_WIDTH):
        s0 = ML_PAD - (CONV_WIDTH - 1) + j
        xc = xc + xbuf_ref[s0:s0 + ML_TS, :] * cw_ref[j:j + 1, :]
    xbuf_ref[0:ML_PAD, :] = mx_ref[ML_TS - ML_PAD:ML_TS, :]
    xc = (xc * jax.nn.sigmoid(xc)).astype(BF16)
    for h in range(nh):
        xh = xc[:, h * dv:(h + 1) * dv]
        q_ref[h] = (_dot(xh, wq_ref[h]) * (dk ** -0.5)).astype(BF16)
        k_ref[h] = _dot(xh, wk_ref[h])

    gc = gc_ref[...]
    li_c = gc[:, 8:8 + nh] + bcol_ref[0:1, 0:nh]
    lf_c = jax.nn.log_sigmoid(gc[:, 8 + nh:8 + 2 * nh] + bcol_ref[0:1, nh:2 * nh])
    gr = gr_ref[...]
    li_r = gr[0:nh, :] + brow_ref[0:nh, :]
    lf_r = jax.nn.log_sigmoid(gr[nh:2 * nh, :] + brow_ref[nh:2 * nh, :])

    ri = lax.broadcasted_iota(jnp.int32, (L, L), 0)
    ci = lax.broadcasted_iota(jnp.int32, (L, L), 1)
    causal = ri >= ci
    tril = jnp.where(causal, 1.0, 0.0).astype(F32)
    triu = jnp.where(ri <= ci, 1.0, 0.0).astype(F32)
    hi = lax.Precision.HIGHEST

    for c in range(ML_TS // L):
        r0 = c * L
        b_c = jnp.dot(tril, lf_c[r0:r0 + L, :], precision=hi, preferred_element_type=F32)
        b_r = jnp.dot(lf_r[:, r0:r0 + L], triu, precision=hi, preferred_element_type=F32)
        outs = []
        for h in range(nh):
            bc = b_c[:, h:h + 1]
            br = b_r[h:h + 1, :]
            ir = li_r[h:h + 1, r0:r0 + L]
            ic = li_c[r0:r0 + L, h:h + 1]
            m_prev = m_ref[h]
            d = jnp.where(causal, bc - br + ir, NEG_BIG)
            inter = bc + m_prev
            m_t = jnp.maximum(jnp.max(d, axis=-1, keepdims=True), inter)
            qh = q_ref[h, r0:r0 + L, :]
            kh = k_ref[h, r0:r0 + L, :]
            vh = mv_ref[r0:r0 + L, h * dv:(h + 1) * dv].astype(BF16)
            wt = _dot_nt(qh, kh.astype(BF16)) * jnp.exp(d - m_t)
            w_int = jnp.exp(inter - m_t)
            num = _dot(wt.astype(BF16), vh) + w_int * _dot(qh, c_ref[h].astype(BF16))
            qn = jnp.sum(qh.astype(F32) * n_ref[h], axis=-1, keepdims=True)
            den = jnp.sum(wt, axis=-1, keepdims=True) + w_int * qn
            outs.append(num / jnp.maximum(jnp.abs(den), jnp.exp(-m_t)))
            b_last = bc[L - 1:L, :]
            wk_r = b_last - br + ir
            m_new = jnp.maximum(b_last + m_prev, jnp.max(wk_r, axis=-1, keepdims=True))
            ws_c = jnp.exp(b_last - bc + ic - m_new)
            wc = jnp.exp(b_last + m_prev - m_new)
            kw = kh * ws_c
            c_ref[h] = wc * c_ref[h] + _dot_tn(kw.astype(BF16), vh)
            n_ref[h] = wc * n_ref[h] + jnp.sum(kw, axis=0, keepdims=True)
            m_ref[h] = m_new
        hh = jnp.concatenate(
            [_rms(outs[h], hg_ref[0:1, h * dv:(h + 1) * dv]) for h in range(nh)], axis=1)
        o_ref[r0:r0 + L, :] = (hh * jax.nn.sigmoid(mo_ref[r0:r0 + L, :])).astype(o_ref.dtype)


def _mlstm(m3, g3, gt, conv_w, conv_b, wq, wk, bcol, brow, head_g):
    b, s, _ = m3.shape
    nh = MLSTM_HEADS
    tile = lambda c: pl.BlockSpec((None, ML_TS, MIX_B), lambda bb, t: (bb, t, c))
    const2 = lambda shape: pl.BlockSpec(shape, lambda bb, t: (0, 0))
    const3 = lambda shape: pl.BlockSpec(shape, lambda bb, t: (0, 0, 0))
    return pl.pallas_call(
        _mlstm_kernel,
        grid=(b, s // ML_TS),
        in_specs=[
            tile(0), tile(1), tile(2),
            pl.BlockSpec((None, ML_TS, G_WIDTH), lambda bb, t: (bb, t, 0)),
            pl.BlockSpec((None, 2 * nh, ML_TS), lambda bb, t: (bb, 0, t)),
            const2((CONV_WIDTH, MIX_B)), const2((1, MIX_B)),
            const3((nh, MLSTM_V_DIM, MLSTM_QK_DIM)), const3((nh, MLSTM_V_DIM, MLSTM_QK_DIM)),
            const2((1, 2 * nh)), const2((2 * nh, 1)), const2((1, MIX_B)),
        ],
        out_specs=pl.BlockSpec((None, ML_TS, MIX_B), lambda bb, t: (bb, t, 0)),
        out_shape=jax.ShapeDtypeStruct((b, s, MIX_B), BF16),
        scratch_shapes=[
            pltpu.VMEM((ML_PAD + ML_TS, MIX_B), F32),
            pltpu.VMEM((nh, ML_TS, MLSTM_QK_DIM), BF16),
            pltpu.VMEM((nh, ML_TS, MLSTM_QK_DIM), F32),
            pltpu.VMEM((nh, MLSTM_QK_DIM, MLSTM_V_DIM), F32),
            pltpu.VMEM((nh, 1, MLSTM_QK_DIM), F32),
            pltpu.VMEM((nh, 1, 1), F32),
        ],
        compiler_params=pltpu.CompilerParams(
            dimension_semantics=("parallel", "arbitrary"), vmem_limit_bytes=VMEM_LIMIT),
        name="mlstm",
    )(m3, m3, m3, g3, gt, conv_w, conv_b, wq, wk, bcol, brow, head_g)


MERGE_TM = 512


def _merge_kernel(x_ref, ya_ref, yb_ref, ga_ref, gb_ref, wa_ref, wb_ref, wo_ref, o_ref):
    merged = (jax.nn.sigmoid(ga_ref[...]) * _dot(ya_ref[...], wa_ref[...])
              + jax.nn.sigmoid(gb_ref[...]) * _dot(yb_ref[...], wb_ref[...]))
    o_ref[...] = x_ref[...] + _dot(merged.astype(BF16), wo_ref[...])


def _merge(x2d, ya, yb, gates, wa, wb, wo):
    m = x2d.shape[0]
    row = lambda i: (i, 0)
    const = lambda i: (0, 0)
    return pl.pallas_call(
        _merge_kernel,
        grid=(m // MERGE_TM,),
        in_specs=[
            pl.BlockSpec((MERGE_TM, D_MODEL), row),
            pl.BlockSpec((MERGE_TM, MIX_A), row),
            pl.BlockSpec((MERGE_TM, MIX_B), row),
            pl.BlockSpec((MERGE_TM, D_MODEL), lambda i: (i, 0)),
            pl.BlockSpec((MERGE_TM, D_MODEL), lambda i: (i, 1)),
            pl.BlockSpec((MIX_A, D_MODEL), const),
            pl.BlockSpec((MIX_B, D_MODEL), const),
            pl.BlockSpec((D_MODEL, D_MODEL), const),
        ],
        out_specs=pl.BlockSpec((MERGE_TM, D_MODEL), row),
        out_shape=jax.ShapeDtypeStruct((m, D_MODEL), F32),
        compiler_params=pltpu.CompilerParams(
            dimension_semantics=("parallel",), vmem_limit_bytes=VMEM_LIMIT),
        name="merge",
    )(x2d, ya, yb, gates, gates, wa, wb, wo)


def _block_diag_ones(n, blk):
    r = jnp.arange(n) // blk
    return (r[:, None] == r[None, :]).astype(BF16)


def _arrange_w_in(w_in):
    widths = (512, 128, 128, 512, 64, 8, 512, 512, 4, 4, 512, 1024, 1024)
    parts, off = [], 0
    for wd in widths:
        parts.append(w_in[:, off:off + wd])
        off += wd
    aq, ak, av, iq, ik, iw, mx, mv, mi, mf, mo, ga, gb = parts
    z = lambda n: jnp.zeros((w_in.shape[0], n), w_in.dtype)
    cols = [aq, iq, ak, ik, z(64), av,
            iw, mi, mf, z(G_WIDTH - 16),
            mx, mv, mo, ga, gb]
    return jnp.concatenate(cols, axis=1).astype(BF16)


def kernel(x, ffn1_norm, ffn1_w_gate, ffn1_w_up, ffn1_w_down, mix_norm, w_in, q_norm, k_norm,
           idx_k_norm, conv_w, conv_b, w_mq, w_mk, b_i, b_f, m_head_norm, w_proj_a, w_proj_b,
           w_out, ffn2_norm, ffn2_w_gate, ffn2_w_up, ffn2_w_down):
    b, s, _ = x.shape
    m = b * s
    nb = s // Q_BLOCK
    nh = MLSTM_HEADS
    x2d = x.reshape(m, D_MODEL)
    for l in range(ffn1_norm.shape[0]):
        x2d = _ffn(x2d, ffn1_norm[l], ffn1_w_gate[l].astype(BF16), ffn1_w_up[l].astype(BF16),
                   ffn1_w_down[l].astype(BF16))

        gq = (jnp.tile(q_norm[l], ATT_HEADS) * (ATT_HEAD_DIM ** -0.5 * LOG2_E)).reshape(1, 512)
        gk = jnp.concatenate([jnp.tile(k_norm[l], ATT_KV_HEADS), idx_k_norm[l],
                              jnp.zeros((64,), F32)]).reshape(1, 256)
        a2d, g2d, m2d, t2d = _proj(x2d, mix_norm[l], _arrange_w_in(w_in[l]),
                                   _block_diag_ones(512, ATT_HEAD_DIM),
                                   _block_diag_ones(256, ATT_HEAD_DIM), gq, gk)

        a3 = a2d.reshape(b, s, A_WIDTH)
        vt = _v_transposed(a3[:, :, COL_V:COL_V + ATT_KV_HEADS * ATT_HEAD_DIM])
        g3 = g2d.reshape(b, s, G_WIDTH)
        iwt = jnp.swapaxes(g3[:, :, 0:IDX_HEADS].reshape(b, nb, Q_BLOCK, IDX_HEADS), 2, 3)
        ya = _dsa(a3, vt, iwt)

        gt = jnp.swapaxes(g3[:, :, IDX_HEADS:IDX_HEADS + 2 * nh], 1, 2)
        bias = jnp.concatenate([b_i[l], b_f[l]])
        yb = _mlstm(m2d.reshape(b, s, M_WIDTH), g3, gt, conv_w[l], conv_b[l].reshape(1, MIX_B),
                    w_mq[l].astype(BF16), w_mk[l].astype(BF16),
                    bias.reshape(1, 2 * nh), bias.reshape(2 * nh, 1),
                    m_head_norm[l].reshape(1, MIX_B))

        x2d = _merge(x2d, ya.reshape(m, MIX_A), yb.reshape(m, MIX_B), t2d,
                     w_proj_a[l].astype(BF16), w_proj_b[l].astype(BF16), w_out[l].astype(BF16))

        x2d = _ffn(x2d, ffn2_norm[l], ffn2_w_gate[l].astype(BF16), ffn2_w_up[l].astype(BF16),
                   ffn2_w_down[l].astype(BF16))
    return x2d.reshape(b, s, D_MODEL)
```

```python
import functools

import jax
import jax.numpy as jnp
from jax import lax
from jax.experimental import pallas as pl
from jax.experimental.pallas import tpu as pltpu

F32 = jnp.float32
BF16 = jnp.bfloat16

D_MODEL = 1024
D_FF = 2816
EPS = 1e-6
ATT_HEADS = 8
ATT_KV_HEADS = 2
ATT_HEAD_DIM = 64
IDX_HEADS = 8
IDX_HEAD_DIM = 64
TOPK_MAX = 256
Q_BLOCK = 128
CHUNK = 64
MLSTM_HEADS = 4
MLSTM_QK_DIM = 64
MLSTM_V_DIM = 128
CONV_WIDTH = 4
MIX_A = ATT_HEADS * ATT_HEAD_DIM
MIX_B = MLSTM_HEADS * MLSTM_V_DIM

LANES = 128
VMEM_LIMIT = 56 * 1024 * 1024
INT_MIN = -2147483648
NEG_BIG = -1e30
LOG2_E = 1.4426950408889634

COL_Q, COL_IQ, COL_K, COL_IK, COL_V = 0, 512, 1024, 1280, 1408
K_EXT = 128
A_WIDTH = 1536
G_WIDTH = 128
M_WIDTH = 3 * MIX_B
T_WIDTH = 2 * D_MODEL
W_TOTAL = A_WIDTH + G_WIDTH + M_WIDTH + T_WIDTH


def _rms(x, g):
    return x * lax.rsqrt(jnp.mean(x * x, axis=-1, keepdims=True) + EPS) * g


def _dot(a, b):
    return jnp.dot(a, b, preferred_element_type=F32)


def _dot_nt(a, b):
    return lax.dot_general(a, b, (((1,), (1,)), ((), ())), preferred_element_type=F32)


def _dot_tn(a, b):
    return lax.dot_general(a, b, (((0,), (0,)), ((), ())), preferred_element_type=F32)


FFN_TM = 512
FFN_TF = 1408


def _ffn_kernel(x_ref, g_ref, wg_ref, wu_ref, wd_ref, o_ref):
    x = x_ref[...]
    h = _rms(x, g_ref[...]).astype(BF16)
    acc = None
    for c in range(D_FF // FFN_TF):
        cols = slice(c * FFN_TF, (c + 1) * FFN_TF)
        a = _dot(h, wg_ref[:, cols])
        u = _dot(h, wu_ref[:, cols])
        act = (a * jax.nn.sigmoid(a) * u).astype(BF16)
        part = _dot(act, wd_ref[cols, :])
        acc = part if acc is None else acc + part
    o_ref[...] = x + 0.5 * acc


def _ffn(x2d, g, wg, wu, wd):
    m = x2d.shape[0]
    const = lambda i: (0, 0)
    resident = lambda shape: pl.BlockSpec(shape, const, pipeline_mode=pl.Buffered(1))
    return pl.pallas_call(
        _ffn_kernel,
        grid=(m // FFN_TM,),
        in_specs=[
            pl.BlockSpec((FFN_TM, D_MODEL), lambda i: (i, 0)),
            pl.BlockSpec((1, D_MODEL), const),
            resident((D_MODEL, D_FF)),
            resident((D_MODEL, D_FF)),
            resident((D_FF, D_MODEL)),
        ],
        out_specs=pl.BlockSpec((FFN_TM, D_MODEL), lambda i: (i, 0)),
        out_shape=jax.ShapeDtypeStruct((m, D_MODEL), F32),
        compiler_params=pltpu.CompilerParams(
            dimension_semantics=("parallel",), vmem_limit_bytes=VMEM_LIMIT),
        name="ffn",
    )(x2d, g.reshape(1, D_MODEL), wg, wu, wd)


PROJ_TM = 512


def _proj_kernel(x_ref, g_ref, w_ref, bdq_ref, bdk_ref, gq_ref, gk_ref, kone_ref,
                 a_ref, g_out_ref, m_ref, t_ref):
    h = _rms(x_ref[...], g_ref[...]).astype(BF16)

    def head_norm(p, bd_ref, gain_ref):
        ss = _dot((p * p).astype(BF16), bd_ref[...]) * (1.0 / ATT_HEAD_DIM)
        return p * lax.rsqrt(ss + EPS) * gain_ref[...]

    pq = _dot(h, w_ref[:, COL_Q:COL_IQ])
    a_ref[:, COL_Q:COL_IQ] = head_norm(pq, bdq_ref, gq_ref).astype(BF16)
    a_ref[:, COL_IQ:COL_K] = _dot(h, w_ref[:, COL_IQ:COL_K]).astype(BF16)
    pk = _dot(h, w_ref[:, COL_K:COL_V])
    a_ref[:, COL_K:COL_V] = (head_norm(pk, bdk_ref, gk_ref) + kone_ref[...]).astype(BF16)
    a_ref[:, COL_V:A_WIDTH] = _dot(h, w_ref[:, COL_V:A_WIDTH]).astype(BF16)
    o = A_WIDTH
    g_out_ref[...] = _dot(h, w_ref[:, o:o + G_WIDTH])
    o += G_WIDTH
    for c in range(3):
        m_ref[:, c * MIX_B:(c + 1) * MIX_B] = _dot(h, w_ref[:, o + c * MIX_B:o + (c + 1) * MIX_B])
    o += M_WIDTH
    for c in range(2):
        t_ref[:, c * D_MODEL:(c + 1) * D_MODEL] = _dot(h, w_ref[:, o + c * D_MODEL:o + (c + 1) * D_MODEL])


def _proj(x2d, g, w_all, bdq, bdk, gq, gk, kone):
    m = x2d.shape[0]
    const = lambda i: (0, 0)
    row = lambda i: (i, 0)
    return pl.pallas_call(
        _proj_kernel,
        grid=(m // PROJ_TM,),
        in_specs=[
            pl.BlockSpec((PROJ_TM, D_MODEL), row),
            pl.BlockSpec((1, D_MODEL), const),
            pl.BlockSpec((D_MODEL, W_TOTAL), const, pipeline_mode=pl.Buffered(1)),
            pl.BlockSpec((512, 512), const),
            pl.BlockSpec((COL_V - COL_K, COL_V - COL_K), const),
            pl.BlockSpec((1, 512), const),
            pl.BlockSpec((1, COL_V - COL_K), const),
            pl.BlockSpec((1, COL_V - COL_K), const),
        ],
        out_specs=[
            pl.BlockSpec((PROJ_TM, A_WIDTH), row),
            pl.BlockSpec((PROJ_TM, G_WIDTH), row),
            pl.BlockSpec((PROJ_TM, M_WIDTH), row),
            pl.BlockSpec((PROJ_TM, T_WIDTH), row),
        ],
        out_shape=[
            jax.ShapeDtypeStruct((m, A_WIDTH), BF16),
            jax.ShapeDtypeStruct((m, G_WIDTH), F32),
            jax.ShapeDtypeStruct((m, M_WIDTH), F32),
            jax.ShapeDtypeStruct((m, T_WIDTH), F32),
        ],
        compiler_params=pltpu.CompilerParams(
            dimension_semantics=("parallel",), vmem_limit_bytes=VMEM_LIMIT),
        name="proj",
    )(x2d, g.reshape(1, D_MODEL), w_all, bdq, bdk, gq, gk, kone)


DSA_KB = 512
DSA_KA = 256
BOUND_LIMIT = 60.0
BOUND_SLACK = 1.01
DSA_VROWS = 80

BIT_GROUP = 256
SEARCH_CHUNK = 1024


def _bit_transpose(words):
    a = list(words)
    j, m = 16, 0x0000FFFF
    while j:
        k = 0
        while k < 32:
            t = (a[k] ^ (a[k + j] >> j)) & m
            a[k] = a[k] ^ t
            a[k + j] = a[k + j] ^ (t << j)
            k = (k + j + 1) & ~j
        j >>= 1
        m = (m ^ (m << j)) & 0xFFFFFFFF
    return a


def _heads_to_lanes(x, n_heads, dh):
    xf = x.astype(F32)
    return jnp.concatenate([xf[:, h * dh:(h + 1) * dh].T for h in range(n_heads)], axis=1).astype(BF16)


def _dsa_kernel(qcap_ref, q_ref, iq_ref, k_ref, ik_ref, vt_ref, iwt_ref, o_ref,
                keys_ref, planes_ref, alive_ref, la_ref, lb_ref, acc_ref, sa_ref, sb_ref, stab_ref,
                *, seq, k_sel, idx_scale):
    kb_sz = DSA_KB
    bi = pl.program_id(1)
    nkb = (bi * Q_BLOCK + Q_BLOCK + kb_sz - 1) // kb_sz
    lane = lax.broadcasted_iota(jnp.int32, (1, LANES), 1)
    limit = bi * Q_BLOCK + jnp.where(lane < CHUNK, CHUNK, 2 * CHUNK)
    row_iota = lax.broadcasted_iota(jnp.int32, (kb_sz, LANES), 0)

    iq_t = _heads_to_lanes(iq_ref[...], IDX_HEADS, IDX_HEAD_DIM)
    w = iwt_ref[...]

    def logits_of(blk):
        off = pl.multiple_of(blk * kb_sz, kb_sz)
        return _dot(ik_ref[pl.ds(off, kb_sz), :][:, :IDX_HEAD_DIM], iq_t)

    def keys_from(l_ref, blk, masked):
        off = pl.multiple_of(blk * kb_sz, kb_sz)
        sc = jnp.zeros((kb_sz, LANES), F32)
        for h in range(IDX_HEADS):
            sc = sc + jnp.maximum(l_ref[:, h * LANES:(h + 1) * LANES], 0.0) * w[h:h + 1, :]
        sc = sc * idx_scale
        bits = pltpu.bitcast(sc, jnp.int32)
        key = bits ^ ((bits >> 31) & 0x7FFFFFFF)
        key = jnp.where(sc == 0.0, 0, key)
        if masked:
            key = jnp.where(off + row_iota < limit, key, INT_MIN)
        keys_ref[pl.ds(off, kb_sz), :] = key
        ukey = key ^ INT_MIN
        for grp in range(kb_sz // BIT_GROUP):
            base = grp * BIT_GROUP
            planes = _bit_transpose([ukey[base + 8 * i:base + 8 * i + 8, :] for i in range(32)])
            row = pl.multiple_of((off + base) // 32, 8)
            for p in range(32):
                planes_ref[p, pl.ds(row, 8), :] = planes[p]

    n_pairs = (nkb + 1) // 2

    def score_pair(j, masked):
        lb_ref[...] = logits_of(2 * j + 1)
        keys_from(la_ref, 2 * j, masked)
        la_ref[...] = logits_of(jnp.minimum(2 * j + 2, 2 * n_pairs - 1))
        keys_from(lb_ref, 2 * j + 1, masked)

    la_ref[...] = logits_of(0)
    lax.fori_loop(0, n_pairs - 1, lambda j, c: (score_pair(j, False), c)[1], 0)
    score_pair(n_pairs - 1, True)

    def count(pred):
        def body(kb, c):
            off = pl.multiple_of(kb * kb_sz, kb_sz)
            ind = jnp.where(pred(keys_ref[pl.ds(off, kb_sz), :], off + row_iota), 1, 0)
            return c + jnp.sum(ind.reshape(kb_sz // 8, 8, LANES), axis=0)
        c8 = lax.fori_loop(0, nkb, body, jnp.zeros((8, LANES), jnp.int32))
        return jnp.sum(c8, axis=0, keepdims=True)

    zeros = jnp.zeros((1, LANES), jnp.int32)
    ch_rows = SEARCH_CHUNK // 32
    n_ch = (nkb + 1) // 2

    zeros8 = jnp.zeros((8, LANES), jnp.int32)

    def ones_in(x):
        return jnp.sum(lax.population_count(x).reshape(ch_rows // 8, 8, LANES), axis=0)

    def over_chunks(body):
        return jnp.sum(lax.fori_loop(0, n_ch, lambda c, cnt: cnt + body(c), zeros8), axis=0, keepdims=True)

    def first_chunk(c):
        row = pl.multiple_of(c * ch_rows, ch_rows)
        alive_ref[pl.ds(row, ch_rows), :] = jnp.full((ch_rows, LANES), -1, jnp.int32)
        return ones_in(planes_ref[0, pl.ds(row, ch_rows), :])

    def decide(p, t_u, above, c1):
        take = above + c1 >= k_sel
        t_u = jnp.where(take, t_u | jnp.left_shift(jnp.int32(1), 31 - p), t_u)
        return t_u, jnp.where(take, above, above + c1), jnp.where(take, 0, -1)

    def radix_pass(p, carry):
        t_u, above, flip = carry

        def chunk(c):
            row = pl.multiple_of(c * ch_rows, ch_rows)
            alive = alive_ref[pl.ds(row, ch_rows), :] & (planes_ref[p - 1, pl.ds(row, ch_rows), :] ^ flip)
            alive_ref[pl.ds(row, ch_rows), :] = alive
            return ones_in(alive & planes_ref[p, pl.ds(row, ch_rows), :])

        return decide(p, t_u, above, over_chunks(chunk))

    carry = decide(0, zeros, zeros, over_chunks(first_chunk))
    t_u, above, flip = lax.fori_loop(1, 32, radix_pass, carry)

    def last_chunk(c):
        row = pl.multiple_of(c * ch_rows, ch_rows)
        return ones_in(alive_ref[pl.ds(row, ch_rows), :] & (planes_ref[31, pl.ds(row, ch_rows), :] ^ flip))

    c_eq = over_chunks(last_chunk)
    few = t_u == 0
    thr = jnp.maximum(t_u ^ INT_MIN, INT_MIN + 1)
    need = k_sel - above
    excess = jnp.where(few, 0, c_eq - need)
    pos_bits = seq.bit_length() - 1

    @pl.when(jnp.max(excess) > 0)
    def _():
        def jbit(i, cut):
            cand = cut | jnp.left_shift(jnp.int32(1), pos_bits - 1 - i)
            c = count(lambda kk, pos: (kk == thr) & (pos < cand))
            return jnp.where(c <= need, cand, cut)
        cut = lax.fori_loop(0, pos_bits, jbit, zeros)
        cut = jnp.where(excess > 0, cut, seq)

        def drop_block(kb, _):
            off = pl.multiple_of(kb * kb_sz, kb_sz)
            kk = keys_ref[pl.ds(off, kb_sz), :]
            keys_ref[pl.ds(off, kb_sz), :] = jnp.where(
                (kk == thr) & (off + row_iota >= cut), INT_MIN, kk)
            return 0

        lax.fori_loop(0, nkb, drop_block, 0)

    reps = ATT_HEADS // ATT_KV_HEADS
    width = reps * LANES
    q_t = _heads_to_lanes(q_ref[...], ATT_HEADS, ATT_HEAD_DIM)
    acc_ref[...] = jnp.zeros_like(acc_ref)

    @pl.when(bi == 0)
    def _():
        def kmax_block(kb, best):
            off = pl.multiple_of(kb * kb_sz, kb_sz)
            kf = k_ref[pl.ds(off, kb_sz), :].astype(F32)
            n2 = [jnp.max(jnp.sum(jnp.square(kf[:, g * K_EXT:g * K_EXT + ATT_HEAD_DIM]), axis=1, keepdims=True))
                  for g in range(ATT_KV_HEADS)]
            return tuple(jnp.maximum(b_, n_) for b_, n_ in zip(best, n2))
        best = lax.fori_loop(0, seq // kb_sz, kmax_block, (jnp.float32(0.0),) * ATT_KV_HEADS)
        worst = jnp.float32(0.0)
        for g in range(ATT_KV_HEADS):
            stab_ref[g] = jnp.sqrt(best[g]) * BOUND_SLACK
            worst = jnp.maximum(worst, stab_ref[g] * qcap_ref[0])
        stab_ref[ATT_KV_HEADS] = jnp.where(worst < BOUND_LIMIT, 1.0, 0.0)

    bounded = stab_ref[ATT_KV_HEADS] > 0.5
    qf = q_t.astype(F32)
    q_len = jnp.sqrt(jnp.sum(qf * qf, axis=0, keepdims=True))
    rhs = []
    for g in range(ATT_KV_HEADS):
        cols = slice(g * width, (g + 1) * width)
        neg_bound = jnp.where(bounded, -q_len[:, cols] * stab_ref[g], 0.0)
        rhs.append(jnp.concatenate(
            [qf[:, cols], neg_bound, jnp.zeros((K_EXT - ATT_HEAD_DIM - 1, width), F32)], axis=0).astype(BF16))

    ka = DSA_KA

    def masked_scores(blk):
        off = pl.multiple_of(blk * ka, ka)
        bias = jnp.where(keys_ref[pl.ds(off, ka), :] >= thr, 0.0, NEG_BIG)
        bias = jnp.concatenate([bias] * reps, axis=1)
        kblk = k_ref[pl.ds(off, ka), :]
        return jnp.concatenate(
            [_dot(kblk[:, g * K_EXT:(g + 1) * K_EXT], rhs[g]) + bias for g in range(ATT_KV_HEADS)],
            axis=1)

    def consume_bounded(s_ref, blk):
        off = pl.multiple_of(blk * ka, ka)
        p = jnp.exp2(s_ref[...]).astype(BF16)
        for g in range(ATT_KV_HEADS):
            acc_ref[g] += _dot(vt_ref[g, :, pl.ds(off, ka)], p[:, g * width:(g + 1) * width])

    def consume_online(s_ref, blk, m_old):
        off = pl.multiple_of(blk * ka, ka)
        s = s_ref[...]
        m_new = jnp.maximum(m_old, jnp.max(s, axis=0, keepdims=True))
        alpha = jnp.exp2(m_old - m_new)
        p = jnp.exp2(s - m_new).astype(BF16)
        for g in range(ATT_KV_HEADS):
            vt = vt_ref[g, :, pl.ds(off, ka)]
            acc_ref[g] = (alpha[:, g * width:(g + 1) * width] * acc_ref[g]
                          + _dot(vt, p[:, g * width:(g + 1) * width]))
        return m_new

    n_sub = nkb * (kb_sz // ka)
    sa_ref[...] = masked_scores(0)

    @pl.when(bounded)
    def _():
        def att_pair(j, c):
            sb_ref[...] = masked_scores(2 * j + 1)
            consume_bounded(sa_ref, 2 * j)
            sa_ref[...] = masked_scores(jnp.minimum(2 * j + 2, n_sub - 1))
            consume_bounded(sb_ref, 2 * j + 1)
            return c
        lax.fori_loop(0, n_sub // 2, att_pair, 0)

    @pl.when(jnp.logical_not(bounded))
    def _():
        def att_pair(j, m_run):
            sb_ref[...] = masked_scores(2 * j + 1)
            m_run = consume_online(sa_ref, 2 * j, m_run)
            sa_ref[...] = masked_scores(jnp.minimum(2 * j + 2, n_sub - 1))
            return consume_online(sb_ref, 2 * j + 1, m_run)
        lax.fori_loop(0, n_sub // 2, att_pair, jnp.full((1, ATT_HEADS * LANES), NEG_BIG, F32))

    pieces = []
    for g in range(ATT_KV_HEADS):
        a = acc_ref[g]
        o = a[0:ATT_HEAD_DIM, :] / a[ATT_HEAD_DIM:ATT_HEAD_DIM + 1, :]
        pieces += [o[:, r * LANES:(r + 1) * LANES] for r in range(reps)]
    o_ref[...] = jnp.concatenate(pieces, axis=0).T.astype(o_ref.dtype)


def _v_transposed(v):
    b, s, _ = v.shape
    vt = jnp.swapaxes(v.reshape(b, s, ATT_KV_HEADS, ATT_HEAD_DIM), 1, 3)
    vt = jnp.swapaxes(vt, 1, 2)
    ones = jnp.ones((b, ATT_KV_HEADS, 1, s), v.dtype)
    pad = jnp.zeros((b, ATT_KV_HEADS, DSA_VROWS - ATT_HEAD_DIM - 1, s), v.dtype)
    return jnp.concatenate([vt, ones, pad], axis=2)


def _dsa(a3, vt, iwt, qcap):
    b, s, _ = a3.shape
    nb = s // Q_BLOCK
    k_sel = min(TOPK_MAX, s // 4)
    idx_scale = (IDX_HEAD_DIM ** -0.5) * (IDX_HEADS ** -0.5)
    kern = functools.partial(_dsa_kernel, seq=s, k_sel=k_sel, idx_scale=idx_scale)
    return pl.pallas_call(
        kern,
        grid=(b, nb),
        in_specs=[
            pl.BlockSpec(memory_space=pltpu.SMEM),
            pl.BlockSpec((None, Q_BLOCK, 512), lambda bb, i: (bb, i, COL_Q // 512)),
            pl.BlockSpec((None, Q_BLOCK, 512), lambda bb, i: (bb, i, COL_IQ // 512)),
            pl.BlockSpec((None, s, ATT_KV_HEADS * K_EXT), lambda bb, i: (bb, 0, COL_K // (ATT_KV_HEADS * K_EXT))),
            pl.BlockSpec((None, s, LANES), lambda bb, i: (bb, 0, COL_IK // LANES)),
            pl.BlockSpec((None, ATT_KV_HEADS, DSA_VROWS, s), lambda bb, i: (bb, 0, 0, 0)),
            pl.BlockSpec((None, None, IDX_HEADS, LANES), lambda bb, i: (bb, i, 0, 0)),
        ],
        out_specs=pl.BlockSpec((None, Q_BLOCK, MIX_A), lambda bb, i: (bb, i, 0)),
        out_shape=jax.ShapeDtypeStruct((b, s, MIX_A), BF16),
        scratch_shapes=[
            pltpu.VMEM((s, LANES), jnp.int32),
            pltpu.VMEM((32, s // 32, LANES), jnp.int32),
            pltpu.VMEM((s // 32, LANES), jnp.int32),
            pltpu.VMEM((DSA_KB, IDX_HEADS * LANES), F32),
            pltpu.VMEM((DSA_KB, IDX_HEADS * LANES), F32),
            pltpu.VMEM((ATT_KV_HEADS, DSA_VROWS, (ATT_HEADS // ATT_KV_HEADS) * LANES), F32),
            pltpu.VMEM((DSA_KA, ATT_HEADS * LANES), F32),
            pltpu.VMEM((DSA_KA, ATT_HEADS * LANES), F32),
            pltpu.SMEM((ATT_KV_HEADS + 1,), F32),
        ],
        compiler_params=pltpu.CompilerParams(
            dimension_semantics=("parallel", "arbitrary"), vmem_limit_bytes=VMEM_LIMIT),
        name="dsa",
    )(qcap, a3, a3, a3, a3, vt, iwt)


ML_L = 128
ML_TS = 512
ML_PAD = 8


def _mlstm_kernel(mx_ref, mv_ref, mo_ref, gc_ref, gr_ref, cw_ref, cb_ref, wq_ref, wk_ref,
                  bcol_ref, brow_ref, hg_ref, o_ref,
                  xbuf_ref, q_ref, k_ref, c_ref, n_ref, m_ref):
    t_idx = pl.program_id(1)
    nh, dk, dv, L = MLSTM_HEADS, MLSTM_QK_DIM, MLSTM_V_DIM, ML_L

    @pl.when(t_idx == 0)
    def _():
        xbuf_ref[0:ML_PAD, :] = jnp.zeros((ML_PAD, MIX_B), F32)
        c_ref[...] = jnp.zeros_like(c_ref)
        n_ref[...] = jnp.zeros_like(n_ref)
        m_ref[...] = jnp.zeros_like(m_ref)

    xbuf_ref[ML_PAD:ML_PAD + ML_TS, :] = mx_ref[...]
    xc = cb_ref[...] + jnp.zeros((ML_TS, MIX_B), F32)
    for j in range(CONV_WIDTH):
        s0 = ML_PAD - (CONV_WIDTH - 1) + j
        xc = xc + xbuf_ref[s0:s0 + ML_TS, :] * cw_ref[j:j + 1, :]
    xbuf_ref[0:ML_PAD, :] = mx_ref[ML_TS - ML_PAD:ML_TS, :]
    xc = (xc * jax.nn.sigmoid(xc)).astype(BF16)
    for h in range(nh):
        xh = xc[:, h * dv:(h + 1) * dv]
        q_ref[h] = (_dot(xh, wq_ref[h]) * (dk ** -0.5)).astype(BF16)
        k_ref[h] = _dot(xh, wk_ref[h])

    gc = gc_ref[...]
    li_c = gc[:, 8:8 + nh] + bcol_ref[0:1, 0:nh]
    lf_c = jax.nn.log_sigmoid(gc[:, 8 + nh:8 + 2 * nh] + bcol_ref[0:1, nh:2 * nh])
    gr = gr_ref[...]
    li_r = gr[0:nh, :] + brow_ref[0:nh, :]
    lf_r = jax.nn.log_sigmoid(gr[nh:2 * nh, :] + brow_ref[nh:2 * nh, :])

    ri = lax.broadcasted_iota(jnp.int32, (L, L), 0)
    ci = lax.broadcasted_iota(jnp.int32, (L, L), 1)
    causal = ri >= ci
    tril = jnp.where(causal, 1.0, 0.0).astype(F32)
    triu = jnp.where(ri <= ci, 1.0, 0.0).astype(F32)
    hi = lax.Precision.HIGHEST

    for c in range(ML_TS // L):
        r0 = c * L
        b_c = jnp.dot(tril, lf_c[r0:r0 + L, :], precision=hi, preferred_element_type=F32)
        b_r = jnp.dot(lf_r[:, r0:r0 + L], triu, precision=hi, preferred_element_type=F32)
        outs = []
        for h in range(nh):
            bc = b_c[:, h:h + 1]
            br = b_r[h:h + 1, :]
            ir = li_r[h:h + 1, r0:r0 + L]
            ic = li_c[r0:r0 + L, h:h + 1]
            m_prev = m_ref[h]
            d = jnp.where(causal, bc - br + ir, NEG_BIG)
            inter = bc + m_prev
            m_t = jnp.maximum(jnp.max(d, axis=-1, keepdims=True), inter)
            qh = q_ref[h, r0:r0 + L, :]
            kh = k_ref[h, r0:r0 + L, :]
            vh = mv_ref[r0:r0 + L, h * dv:(h + 1) * dv].astype(BF16)
            wt = _dot_nt(qh, kh.astype(BF16)) * jnp.exp(d - m_t)
            w_int = jnp.exp(inter - m_t)
            num = _dot(wt.astype(BF16), vh) + w_int * _dot(qh, c_ref[h].astype(BF16))
            qn = jnp.sum(qh.astype(F32) * n_ref[h], axis=-1, keepdims=True)
            den = jnp.sum(wt, axis=-1, keepdims=True) + w_int * qn
            outs.append(num / jnp.maximum(jnp.abs(den), jnp.exp(-m_t)))
            b_last = bc[L - 1:L, :]
            wk_r = b_last - br + ir
            m_new = jnp.maximum(b_last + m_prev, jnp.max(wk_r, axis=-1, keepdims=True))
            ws_c = jnp.exp(b_last - bc + ic - m_new)
            wc = jnp.exp(b_last + m_prev - m_new)
            kw = kh * ws_c
            c_ref[h] = wc * c_ref[h] + _dot_tn(kw.astype(BF16), vh)
            n_ref[h] = wc * n_ref[h] + jnp.sum(kw, axis=0, keepdims=True)
            m_ref[h] = m_new
        hh = jnp.concatenate(
            [_rms(outs[h], hg_ref[0:1, h * dv:(h + 1) * dv]) for h in range(nh)], axis=1)
        o_ref[r0:r0 + L, :] = (hh * jax.nn.sigmoid(mo_ref[r0:r0 + L, :])).astype(o_ref.dtype)


def _mlstm(m3, g3, gt, conv_w, conv_b, wq, wk, bcol, brow, head_g):
    b, s, _ = m3.shape
    nh = MLSTM_HEADS
    tile = lambda c: pl.BlockSpec((None, ML_TS, MIX_B), lambda bb, t: (bb, t, c))
    const2 = lambda shape: pl.BlockSpec(shape, lambda bb, t: (0, 0))
    const3 = lambda shape: pl.BlockSpec(shape, lambda bb, t: (0, 0, 0))
    return pl.pallas_call(
        _mlstm_kernel,
        grid=(b, s // ML_TS),
        in_specs=[
            tile(0), tile(1), tile(2),
            pl.BlockSpec((None, ML_TS, G_WIDTH), lambda bb, t: (bb, t, 0)),
            pl.BlockSpec((None, 2 * nh, ML_TS), lambda bb, t: (bb, 0, t)),
            const2((CONV_WIDTH, MIX_B)), const2((1, MIX_B)),
            const3((nh, MLSTM_V_DIM, MLSTM_QK_DIM)), const3((nh, MLSTM_V_DIM, MLSTM_QK_DIM)),
            const2((1, 2 * nh)), const2((2 * nh, 1)), const2((1, MIX_B)),
        ],
        out_specs=pl.BlockSpec((None, ML_TS, MIX_B), lambda bb, t: (bb, t, 0)),
        out_shape=jax.ShapeDtypeStruct((b, s, MIX_B), BF16),
        scratch_shapes=[
            pltpu.VMEM((ML_PAD + ML_TS, MIX_B), F32),
            pltpu.VMEM((nh, ML_TS, MLSTM_QK_DIM), BF16),
            pltpu.VMEM((nh, ML_TS, MLSTM_QK_DIM), F32),
            pltpu.VMEM((nh, MLSTM_QK_DIM, MLSTM_V_DIM), F32),
            pltpu.VMEM((nh, 1, MLSTM_QK_DIM), F32),
            pltpu.VMEM((nh, 1, 1), F32),
        ],
        compiler_params=pltpu.CompilerParams(
            dimension_semantics=("parallel", "arbitrary"), vmem_limit_bytes=VMEM_LIMIT),
        name="mlstm",
    )(m3, m3, m3, g3, gt, conv_w, conv_b, wq, wk, bcol, brow, head_g)


MERGE_TM = 512


def _merge_kernel(x_ref, ya_ref, yb_ref, ga_ref, gb_ref, wa_ref, wb_ref, wo_ref, o_ref):
    merged = (jax.nn.sigmoid(ga_ref[...]) * _dot(ya_ref[...], wa_ref[...])
              + jax.nn.sigmoid(gb_ref[...]) * _dot(yb_ref[...], wb_ref[...]))
    o_ref[...] = x_ref[...] + _dot(merged.astype(BF16), wo_ref[...])


def _merge(x2d, ya, yb, gates, wa, wb, wo):
    m = x2d.shape[0]
    row = lambda i: (i, 0)
    const = lambda i: (0, 0)
    return pl.pallas_call(
        _merge_kernel,
        grid=(m // MERGE_TM,),
        in_specs=[
            pl.BlockSpec((MERGE_TM, D_MODEL), row),
            pl.BlockSpec((MERGE_TM, MIX_A), row),
            pl.BlockSpec((MERGE_TM, MIX_B), row),
            pl.BlockSpec((MERGE_TM, D_MODEL), lambda i: (i, 0)),
            pl.BlockSpec((MERGE_TM, D_MODEL), lambda i: (i, 1)),
            pl.BlockSpec((MIX_A, D_MODEL), const),
            pl.BlockSpec((MIX_B, D_MODEL), const),
            pl.BlockSpec((D_MODEL, D_MODEL), const),
        ],
        out_specs=pl.BlockSpec((MERGE_TM, D_MODEL), row),
        out_shape=jax.ShapeDtypeStruct((m, D_MODEL), F32),
        compiler_params=pltpu.CompilerParams(
            dimension_semantics=("parallel",), vmem_limit_bytes=VMEM_LIMIT),
        name="merge",
    )(x2d, ya, yb, gates, gates, wa, wb, wo)


def _block_diag_ones(n, blk):
    r = jnp.arange(n) // blk
    return (r[:, None] == r[None, :]).astype(BF16)


def _arrange_w_in(w_in):
    widths = (512, 128, 128, 512, 64, 8, 512, 512, 4, 4, 512, 1024, 1024)
    parts, off = [], 0
    for wd in widths:
        parts.append(w_in[:, off:off + wd])
        off += wd
    aq, ak, av, iq, ik, iw, mx, mv, mi, mf, mo, ga, gb = parts
    z = lambda n: jnp.zeros((w_in.shape[0], n), w_in.dtype)
    cols = [aq, iq, ak[:, :64], z(64), ak[:, 64:], z(64), ik, z(64), av,
            iw, mi, mf, z(G_WIDTH - 16),
            mx, mv, mo, ga, gb]
    return jnp.concatenate(cols, axis=1).astype(BF16)


def kernel(x, ffn1_norm, ffn1_w_gate, ffn1_w_up, ffn1_w_down, mix_norm, w_in, q_norm, k_norm,
           idx_k_norm, conv_w, conv_b, w_mq, w_mk, b_i, b_f, m_head_norm, w_proj_a, w_proj_b,
           w_out, ffn2_norm, ffn2_w_gate, ffn2_w_up, ffn2_w_down):
    b, s, _ = x.shape
    m = b * s
    nb = s // Q_BLOCK
    nh = MLSTM_HEADS
    x2d = x.reshape(m, D_MODEL)
    for l in range(ffn1_norm.shape[0]):
        x2d = _ffn(x2d, ffn1_norm[l], ffn1_w_gate[l].astype(BF16), ffn1_w_up[l].astype(BF16),
                   ffn1_w_down[l].astype(BF16))

        gq = (jnp.tile(q_norm[l], ATT_HEADS) * (ATT_HEAD_DIM ** -0.5 * LOG2_E)).reshape(1, 512)
        z64 = jnp.zeros((ATT_HEAD_DIM,), F32)
        one64 = z64.at[0].set(1.0)
        gk = jnp.concatenate([k_norm[l], z64] * ATT_KV_HEADS + [idx_k_norm[l], z64]).reshape(1, -1)
        kone = jnp.concatenate([z64, one64] * ATT_KV_HEADS + [z64, z64]).reshape(1, -1)
        a2d, g2d, m2d, t2d = _proj(x2d, mix_norm[l], _arrange_w_in(w_in[l]),
                                   _block_diag_ones(512, ATT_HEAD_DIM),
                                   _block_diag_ones(COL_V - COL_K, ATT_HEAD_DIM), gq, gk, kone)

        a3 = a2d.reshape(b, s, A_WIDTH)
        vt = _v_transposed(a3[:, :, COL_V:COL_V + ATT_KV_HEADS * ATT_HEAD_DIM])
        g3 = g2d.reshape(b, s, G_WIDTH)
        iwt = jnp.swapaxes(g3[:, :, 0:IDX_HEADS].reshape(b, nb, Q_BLOCK, IDX_HEADS), 2, 3)
        qcap = (jnp.max(jnp.abs(gq)) * (ATT_HEAD_DIM ** 0.5) * BOUND_SLACK).reshape(1)
        ya = _dsa(a3, vt, iwt, qcap)

        gt = jnp.swapaxes(g3[:, :, IDX_HEADS:IDX_HEADS + 2 * nh], 1, 2)
        bias = jnp.concatenate([b_i[l], b_f[l]])
        yb = _mlstm(m2d.reshape(b, s, M_WIDTH), g3, gt, conv_w[l], conv_b[l].reshape(1, MIX_B),
                    w_mq[l].astype(BF16), w_mk[l].astype(BF16),
                    bias.reshape(1, 2 * nh), bias.reshape(2 * nh, 1),
                    m_head_norm[l].reshape(1, MIX_B))

        x2d = _merge(x2d, ya.reshape(m, MIX_A), yb.reshape(m, MIX_B), t2d,
                     w_proj_a[l].astype(BF16), w_proj_b[l].astype(BF16), w_out[l].astype(BF16))

        x2d = _ffn(x2d, ffn2_norm[l], ffn2_w_gate[l].astype(BF16), ffn2_w_up[l].astype(BF16),
                   ffn2_w_down[l].astype(BF16))
    return x2d.reshape(b, s, D_MODEL)
```

```python
import functools

import jax
import jax.numpy as jnp
from jax import lax
from jax.experimental import pallas as pl
from jax.experimental.pallas import tpu as pltpu

F32 = jnp.float32
BF16 = jnp.bfloat16

D_MODEL = 1024
D_FF = 2816
EPS = 1e-6
ATT_HEADS = 8
ATT_KV_HEADS = 2
ATT_HEAD_DIM = 64
IDX_HEADS = 8
IDX_HEAD_DIM = 64
TOPK_MAX = 256
Q_BLOCK = 128
CHUNK = 64
MLSTM_HEADS = 4
MLSTM_QK_DIM = 64
MLSTM_V_DIM = 128
CONV_WIDTH = 4
MIX_A = ATT_HEADS * ATT_HEAD_DIM
MIX_B = MLSTM_HEADS * MLSTM_V_DIM

LANES = 128
VMEM_LIMIT = 56 * 1024 * 1024
INT_MIN = -2147483648
NEG_BIG = -1e30
LOG2_E = 1.4426950408889634

COL_Q, COL_IQ, COL_K, COL_IK, COL_V = 0, 512, 1024, 1280, 1408
K_EXT = 128
A_WIDTH = 1536
G_WIDTH = 128
M_WIDTH = 3 * MIX_B
T_WIDTH = 2 * D_MODEL
W_TOTAL = A_WIDTH + G_WIDTH + M_WIDTH + T_WIDTH


def _rms(x, g):
    return x * lax.rsqrt(jnp.mean(x * x, axis=-1, keepdims=True) + EPS) * g


def _dot(a, b):
    return jnp.dot(a, b, preferred_element_type=F32)


def _dot_nt(a, b):
    return lax.dot_general(a, b, (((1,), (1,)), ((), ())), preferred_element_type=F32)


def _dot_tn(a, b):
    return lax.dot_general(a, b, (((0,), (0,)), ((), ())), preferred_element_type=F32)


FFN_TM = 512
FFN_TF = 1408


def _ffn_kernel(x_ref, g_ref, wg_ref, wu_ref, wd_ref, o_ref):
    x = x_ref[...]
    h = _rms(x, g_ref[...]).astype(BF16)
    acc = None
    for c in range(D_FF // FFN_TF):
        cols = slice(c * FFN_TF, (c + 1) * FFN_TF)
        a = _dot(h, wg_ref[:, cols])
        u = _dot(h, wu_ref[:, cols])
        act = (a * jax.nn.sigmoid(a) * u).astype(BF16)
        part = _dot(act, wd_ref[cols, :])
        acc = part if acc is None else acc + part
    o_ref[...] = x + 0.5 * acc


def _ffn(x2d, g, wg, wu, wd):
    m = x2d.shape[0]
    const = lambda i: (0, 0)
    resident = lambda shape: pl.BlockSpec(shape, const, pipeline_mode=pl.Buffered(1))
    return pl.pallas_call(
        _ffn_kernel,
        grid=(m // FFN_TM,),
        in_specs=[
            pl.BlockSpec((FFN_TM, D_MODEL), lambda i: (i, 0)),
            pl.BlockSpec((1, D_MODEL), const),
            resident((D_MODEL, D_FF)),
            resident((D_MODEL, D_FF)),
            resident((D_FF, D_MODEL)),
        ],
        out_specs=pl.BlockSpec((FFN_TM, D_MODEL), lambda i: (i, 0)),
        out_shape=jax.ShapeDtypeStruct((m, D_MODEL), F32),
        compiler_params=pltpu.CompilerParams(
            dimension_semantics=("parallel",), vmem_limit_bytes=VMEM_LIMIT),
        name="ffn",
    )(x2d, g.reshape(1, D_MODEL), wg, wu, wd)


PROJ_TM = 512


def _proj_kernel(x_ref, g_ref, w_ref, bdq_ref, bdk_ref, gq_ref, gk_ref, kone_ref,
                 a_ref, g_out_ref, m_ref, t_ref):
    h = _rms(x_ref[...], g_ref[...]).astype(BF16)

    def head_norm(p, bd_ref, gain_ref):
        ss = _dot((p * p).astype(BF16), bd_ref[...]) * (1.0 / ATT_HEAD_DIM)
        return p * lax.rsqrt(ss + EPS) * gain_ref[...]

    pq = _dot(h, w_ref[:, COL_Q:COL_IQ])
    a_ref[:, COL_Q:COL_IQ] = head_norm(pq, bdq_ref, gq_ref).astype(BF16)
    a_ref[:, COL_IQ:COL_K] = _dot(h, w_ref[:, COL_IQ:COL_K]).astype(BF16)
    pk = _dot(h, w_ref[:, COL_K:COL_V])
    a_ref[:, COL_K:COL_V] = (head_norm(pk, bdk_ref, gk_ref) + kone_ref[...]).astype(BF16)
    a_ref[:, COL_V:A_WIDTH] = _dot(h, w_ref[:, COL_V:A_WIDTH]).astype(BF16)
    o = A_WIDTH
    g_out_ref[...] = _dot(h, w_ref[:, o:o + G_WIDTH])
    o += G_WIDTH
    for c in range(3):
        m_ref[:, c * MIX_B:(c + 1) * MIX_B] = _dot(h, w_ref[:, o + c * MIX_B:o + (c + 1) * MIX_B])
    o += M_WIDTH
    for c in range(2):
        t_ref[:, c * D_MODEL:(c + 1) * D_MODEL] = _dot(h, w_ref[:, o + c * D_MODEL:o + (c + 1) * D_MODEL])


def _proj(x2d, g, w_all, bdq, bdk, gq, gk, kone):
    m = x2d.shape[0]
    const = lambda i: (0, 0)
    row = lambda i: (i, 0)
    return pl.pallas_call(
        _proj_kernel,
        grid=(m // PROJ_TM,),
        in_specs=[
            pl.BlockSpec((PROJ_TM, D_MODEL), row),
            pl.BlockSpec((1, D_MODEL), const),
            pl.BlockSpec((D_MODEL, W_TOTAL), const, pipeline_mode=pl.Buffered(1)),
            pl.BlockSpec((512, 512), const),
            pl.BlockSpec((COL_V - COL_K, COL_V - COL_K), const),
            pl.BlockSpec((1, 512), const),
            pl.BlockSpec((1, COL_V - COL_K), const),
            pl.BlockSpec((1, COL_V - COL_K), const),
        ],
        out_specs=[
            pl.BlockSpec((PROJ_TM, A_WIDTH), row),
            pl.BlockSpec((PROJ_TM, G_WIDTH), row),
            pl.BlockSpec((PROJ_TM, M_WIDTH), row),
            pl.BlockSpec((PROJ_TM, T_WIDTH), row),
        ],
        out_shape=[
            jax.ShapeDtypeStruct((m, A_WIDTH), BF16),
            jax.ShapeDtypeStruct((m, G_WIDTH), F32),
            jax.ShapeDtypeStruct((m, M_WIDTH), F32),
            jax.ShapeDtypeStruct((m, T_WIDTH), F32),
        ],
        compiler_params=pltpu.CompilerParams(
            dimension_semantics=("parallel",), vmem_limit_bytes=VMEM_LIMIT),
        name="proj",
    )(x2d, g.reshape(1, D_MODEL), w_all, bdq, bdk, gq, gk, kone)


DSA_KB = 512
DSA_KA = 256
BOUND_LIMIT = 60.0
BOUND_SLACK = 1.01
DSA_VROWS = 80

BIT_GROUP = 256
SEARCH_CHUNK = 1024


def _bit_transpose(words):
    a = list(words)
    j, m = 16, 0x0000FFFF
    while j:
        k = 0
        while k < 32:
            t = (a[k] ^ (a[k + j] >> j)) & m
            a[k] = a[k] ^ t
            a[k + j] = a[k + j] ^ (t << j)
            k = (k + j + 1) & ~j
        j >>= 1
        m = (m ^ (m << j)) & 0xFFFFFFFF
    return a


def _heads_to_lanes(x, n_heads, dh):
    xf = x.astype(F32)
    return jnp.concatenate([xf[:, h * dh:(h + 1) * dh].T for h in range(n_heads)], axis=1).astype(BF16)


def _dsa_kernel(qcap_ref, q_ref, iq_ref, k_ref, ik_ref, vt_ref, iwt_ref, o_ref,
                keys_ref, planes_ref, alive_ref, la_ref, lb_ref, acc_ref, sa_ref, sb_ref, stab_ref,
                *, seq, k_sel, idx_scale):
    kb_sz = DSA_KB
    bi = pl.program_id(1)
    nkb = (bi * Q_BLOCK + Q_BLOCK + kb_sz - 1) // kb_sz
    lane = lax.broadcasted_iota(jnp.int32, (1, LANES), 1)
    limit = bi * Q_BLOCK + jnp.where(lane < CHUNK, CHUNK, 2 * CHUNK)
    row_iota = lax.broadcasted_iota(jnp.int32, (kb_sz, LANES), 0)

    iq_t = _heads_to_lanes(iq_ref[...], IDX_HEADS, IDX_HEAD_DIM)
    w = iwt_ref[...]

    def logits_of(blk):
        off = pl.multiple_of(blk * kb_sz, kb_sz)
        return _dot(ik_ref[pl.ds(off, kb_sz), :][:, :IDX_HEAD_DIM], iq_t)

    def keys_from(l_ref, blk, masked):
        off = pl.multiple_of(blk * kb_sz, kb_sz)
        sc = jnp.zeros((kb_sz, LANES), F32)
        for h in range(IDX_HEADS):
            sc = sc + jnp.maximum(l_ref[:, h * LANES:(h + 1) * LANES], 0.0) * w[h:h + 1, :]
        sc = sc * idx_scale
        bits = pltpu.bitcast(sc, jnp.int32)
        key = bits ^ ((bits >> 31) & 0x7FFFFFFF)
        key = jnp.where(sc == 0.0, 0, key)
        if masked:
            key = jnp.where(off + row_iota < limit, key, INT_MIN)
        keys_ref[pl.ds(off, kb_sz), :] = key
        ukey = key ^ INT_MIN
        for grp in range(kb_sz // BIT_GROUP):
            base = grp * BIT_GROUP
            planes = _bit_transpose([ukey[base + 8 * i:base + 8 * i + 8, :] for i in range(32)])
            row = pl.multiple_of((off + base) // 32, 8)
            for p in range(32):
                planes_ref[p, pl.ds(row, 8), :] = planes[p]

    n_pairs = (nkb + 1) // 2

    def score_pair(j, masked):
        lb_ref[...] = logits_of(2 * j + 1)
        keys_from(la_ref, 2 * j, masked)
        la_ref[...] = logits_of(jnp.minimum(2 * j + 2, 2 * n_pairs - 1))
        keys_from(lb_ref, 2 * j + 1, masked)

    la_ref[...] = logits_of(0)
    lax.fori_loop(0, n_pairs - 1, lambda j, c: (score_pair(j, False), c)[1], 0)
    score_pair(n_pairs - 1, True)

    def count(pred):
        def body(kb, c):
            off = pl.multiple_of(kb * kb_sz, kb_sz)
            ind = jnp.where(pred(keys_ref[pl.ds(off, kb_sz), :], off + row_iota), 1, 0)
            return c + jnp.sum(ind.reshape(kb_sz // 8, 8, LANES), axis=0)
        c8 = lax.fori_loop(0, nkb, body, jnp.zeros((8, LANES), jnp.int32))
        return jnp.sum(c8, axis=0, keepdims=True)

    zeros = jnp.zeros((1, LANES), jnp.int32)
    ch_rows = SEARCH_CHUNK // 32
    n_ch = (nkb + 1) // 2

    zeros8 = jnp.zeros((8, LANES), jnp.int32)

    def ones_in(x):
        return jnp.sum(lax.population_count(x).reshape(ch_rows // 8, 8, LANES), axis=0)

    def over_chunks(body):
        return jnp.sum(lax.fori_loop(0, n_ch, lambda c, cnt: cnt + body(c), zeros8), axis=0, keepdims=True)

    def first_chunk(c):
        row = pl.multiple_of(c * ch_rows, ch_rows)
        alive_ref[pl.ds(row, ch_rows), :] = jnp.full((ch_rows, LANES), -1, jnp.int32)
        return ones_in(planes_ref[0, pl.ds(row, ch_rows), :])

    def decide(p, t_u, above, c1):
        take = above + c1 >= k_sel
        t_u = jnp.where(take, t_u | jnp.left_shift(jnp.int32(1), 31 - p), t_u)
        return t_u, jnp.where(take, above, above + c1), jnp.where(take, 0, -1)

    def radix_pass(p, carry):
        t_u, above, flip = carry

        def chunk(c):
            row = pl.multiple_of(c * ch_rows, ch_rows)
            alive = alive_ref[pl.ds(row, ch_rows), :] & (planes_ref[p - 1, pl.ds(row, ch_rows), :] ^ flip)
            alive_ref[pl.ds(row, ch_rows), :] = alive
            return ones_in(alive & planes_ref[p, pl.ds(row, ch_rows), :])

        return decide(p, t_u, above, over_chunks(chunk))

    carry = decide(0, zeros, zeros, over_chunks(first_chunk))
    t_u, above, flip = lax.fori_loop(1, 32, radix_pass, carry)

    def last_chunk(c):
        row = pl.multiple_of(c * ch_rows, ch_rows)
        return ones_in(alive_ref[pl.ds(row, ch_rows), :] & (planes_ref[31, pl.ds(row, ch_rows), :] ^ flip))

    c_eq = over_chunks(last_chunk)
    few = t_u == 0
    thr = jnp.maximum(t_u ^ INT_MIN, INT_MIN + 1)
    need = k_sel - above
    excess = jnp.where(few, 0, c_eq - need)
    pos_bits = seq.bit_length() - 1

    @pl.when(jnp.max(excess) > 0)
    def _():
        def jbit(i, cut):
            cand = cut | jnp.left_shift(jnp.int32(1), pos_bits - 1 - i)
            c = count(lambda kk, pos: (kk == thr) & (pos < cand))
            return jnp.where(c <= need, cand, cut)
        cut = lax.fori_loop(0, pos_bits, jbit, zeros)
        cut = jnp.where(excess > 0, cut, seq)

        def drop_block(kb, _):
            off = pl.multiple_of(kb * kb_sz, kb_sz)
            kk = keys_ref[pl.ds(off, kb_sz), :]
            keys_ref[pl.ds(off, kb_sz), :] = jnp.where(
                (kk == thr) & (off + row_iota >= cut), INT_MIN, kk)
            return 0

        lax.fori_loop(0, nkb, drop_block, 0)

    reps = ATT_HEADS // ATT_KV_HEADS
    width = reps * LANES
    q_t = _heads_to_lanes(q_ref[...], ATT_HEADS, ATT_HEAD_DIM)
    acc_ref[...] = jnp.zeros_like(acc_ref)

    @pl.when(bi == 0)
    def _():
        def kmax_block(kb, best):
            off = pl.multiple_of(kb * kb_sz, kb_sz)
            kf = k_ref[pl.ds(off, kb_sz), :].astype(F32)
            n2 = [jnp.max(jnp.sum(jnp.square(kf[:, g * K_EXT:g * K_EXT + ATT_HEAD_DIM]), axis=1, keepdims=True))
                  for g in range(ATT_KV_HEADS)]
            return tuple(jnp.maximum(b_, n_) for b_, n_ in zip(best, n2))
        best = lax.fori_loop(0, seq // kb_sz, kmax_block, (jnp.float32(0.0),) * ATT_KV_HEADS)
        worst = jnp.float32(0.0)
        for g in range(ATT_KV_HEADS):
            stab_ref[g] = jnp.sqrt(best[g]) * BOUND_SLACK
            worst = jnp.maximum(worst, stab_ref[g] * qcap_ref[0])
        stab_ref[ATT_KV_HEADS] = jnp.where(worst < BOUND_LIMIT, 1.0, 0.0)

    bounded = stab_ref[ATT_KV_HEADS] > 0.5
    qf = q_t.astype(F32)
    q_len = jnp.sqrt(jnp.sum(qf * qf, axis=0, keepdims=True))
    rhs = []
    for g in range(ATT_KV_HEADS):
        cols = slice(g * width, (g + 1) * width)
        neg_bound = jnp.where(bounded, -q_len[:, cols] * stab_ref[g], 0.0)
        rhs.append(jnp.concatenate(
            [qf[:, cols], neg_bound, jnp.zeros((K_EXT - ATT_HEAD_DIM - 1, width), F32)], axis=0).astype(BF16))

    ka = DSA_KA

    def masked_scores(blk):
        off = pl.multiple_of(blk * ka, ka)
        bias = jnp.where(keys_ref[pl.ds(off, ka), :] >= thr, 0.0, NEG_BIG)
        bias = jnp.concatenate([bias] * reps, axis=1)
        kblk = k_ref[pl.ds(off, ka), :]
        return jnp.concatenate(
            [_dot(kblk[:, g * K_EXT:(g + 1) * K_EXT], rhs[g]) + bias for g in range(ATT_KV_HEADS)],
            axis=1)

    def consume_bounded(s_ref, blk):
        off = pl.multiple_of(blk * ka, ka)
        p = jnp.exp2(s_ref[...]).astype(BF16)
        for g in range(ATT_KV_HEADS):
            acc_ref[g] += _dot(vt_ref[g, :, pl.ds(off, ka)], p[:, g * width:(g + 1) * width])

    def consume_online(s_ref, blk, m_old):
        off = pl.multiple_of(blk * ka, ka)
        s = s_ref[...]
        m_new = jnp.maximum(m_old, jnp.max(s, axis=0, keepdims=True))
        alpha = jnp.exp2(m_old - m_new)
        p = jnp.exp2(s - m_new).astype(BF16)
        for g in range(ATT_KV_HEADS):
            vt = vt_ref[g, :, pl.ds(off, ka)]
            acc_ref[g] = (alpha[:, g * width:(g + 1) * width] * acc_ref[g]
                          + _dot(vt, p[:, g * width:(g + 1) * width]))
        return m_new

    n_sub = nkb * (kb_sz // ka)
    sa_ref[...] = masked_scores(0)

    @pl.when(bounded)
    def _():
        def att_pair(j, c):
            sb_ref[...] = masked_scores(2 * j + 1)
            consume_bounded(sa_ref, 2 * j)
            sa_ref[...] = masked_scores(jnp.minimum(2 * j + 2, n_sub - 1))
            consume_bounded(sb_ref, 2 * j + 1)
            return c
        lax.fori_loop(0, n_sub // 2, att_pair, 0)

    @pl.when(jnp.logical_not(bounded))
    def _():
        def att_pair(j, m_run):
            sb_ref[...] = masked_scores(2 * j + 1)
            m_run = consume_online(sa_ref, 2 * j, m_run)
            sa_ref[...] = masked_scores(jnp.minimum(2 * j + 2, n_sub - 1))
            return consume_online(sb_ref, 2 * j + 1, m_run)
        lax.fori_loop(0, n_sub // 2, att_pair, jnp.full((1, ATT_HEADS * LANES), NEG_BIG, F32))

    pieces = []
    for g in range(ATT_KV_HEADS):
        a = acc_ref[g]
        o = a[0:ATT_HEAD_DIM, :] / a[ATT_HEAD_DIM:ATT_HEAD_DIM + 1, :]
        pieces += [o[:, r * LANES:(r + 1) * LANES] for r in range(reps)]
    o_ref[...] = jnp.concatenate(pieces, axis=0).T.astype(o_ref.dtype)


def _v_transposed(v):
    b, s, _ = v.shape
    vt = jnp.swapaxes(v.reshape(b, s, ATT_KV_HEADS, ATT_HEAD_DIM), 1, 3)
    vt = jnp.swapaxes(vt, 1, 2)
    ones = jnp.ones((b, ATT_KV_HEADS, 1, s), v.dtype)
    pad = jnp.zeros((b, ATT_KV_HEADS, DSA_VROWS - ATT_HEAD_DIM - 1, s), v.dtype)
    return jnp.concatenate([vt, ones, pad], axis=2)


def _dsa(a3, vt, iwt, qcap):
    b, s, _ = a3.shape
    nb = s // Q_BLOCK
    k_sel = min(TOPK_MAX, s // 4)
    idx_scale = (IDX_HEAD_DIM ** -0.5) * (IDX_HEADS ** -0.5)
    kern = functools.partial(_dsa_kernel, seq=s, k_sel=k_sel, idx_scale=idx_scale)
    return pl.pallas_call(
        kern,
        grid=(b, nb),
        in_specs=[
            pl.BlockSpec(memory_space=pltpu.SMEM),
            pl.BlockSpec((None, Q_BLOCK, 512), lambda bb, i: (bb, i, COL_Q // 512)),
            pl.BlockSpec((None, Q_BLOCK, 512), lambda bb, i: (bb, i, COL_IQ // 512)),
            pl.BlockSpec((None, s, ATT_KV_HEADS * K_EXT), lambda bb, i: (bb, 0, COL_K // (ATT_KV_HEADS * K_EXT))),
            pl.BlockSpec((None, s, LANES), lambda bb, i: (bb, 0, COL_IK // LANES)),
            pl.BlockSpec((None, ATT_KV_HEADS, DSA_VROWS, s), lambda bb, i: (bb, 0, 0, 0)),
            pl.BlockSpec((None, None, IDX_HEADS, LANES), lambda bb, i: (bb, i, 0, 0)),
        ],
        out_specs=pl.BlockSpec((None, Q_BLOCK, MIX_A), lambda bb, i: (bb, i, 0)),
        out_shape=jax.ShapeDtypeStruct((b, s, MIX_A), BF16),
        scratch_shapes=[
            pltpu.VMEM((s, LANES), jnp.int32),
            pltpu.VMEM((32, s // 32, LANES), jnp.int32),
            pltpu.VMEM((s // 32, LANES), jnp.int32),
            pltpu.VMEM((DSA_KB, IDX_HEADS * LANES), F32),
            pltpu.VMEM((DSA_KB, IDX_HEADS * LANES), F32),
            pltpu.VMEM((ATT_KV_HEADS, DSA_VROWS, (ATT_HEADS // ATT_KV_HEADS) * LANES), F32),
            pltpu.VMEM((DSA_KA, ATT_HEADS * LANES), F32),
            pltpu.VMEM((DSA_KA, ATT_HEADS * LANES), F32),
            pltpu.SMEM((ATT_KV_HEADS + 1,), F32),
        ],
        compiler_params=pltpu.CompilerParams(
            dimension_semantics=("parallel", "arbitrary"), vmem_limit_bytes=VMEM_LIMIT),
        name="dsa",
    )(qcap, a3, a3, a3, a3, vt, iwt)


ML_L = 128
ML_TS = 512
ML_PAD = 8


def _mlstm_kernel(mx_ref, mv_ref, mo_ref, gr_ref, cw_ref, cb_ref, wq_ref, wkt_ref, brow_ref, hg_ref, o_ref,
                  xbuf_ref, q_ref, kt_ref, s_ref, m_ref):
    t_idx = pl.program_id(1)
    nh, dk, dv, L = MLSTM_HEADS, MLSTM_QK_DIM, MLSTM_V_DIM, ML_L

    @pl.when(t_idx == 0)
    def _():
        xbuf_ref[0:ML_PAD, :] = jnp.zeros((ML_PAD, MIX_B), F32)
        s_ref[...] = jnp.zeros_like(s_ref)
        m_ref[...] = jnp.zeros_like(m_ref)

    xbuf_ref[ML_PAD:ML_PAD + ML_TS, :] = mx_ref[...]
    xc = cb_ref[...] + jnp.zeros((ML_TS, MIX_B), F32)
    for j in range(CONV_WIDTH):
        s0 = ML_PAD - (CONV_WIDTH - 1) + j
        xc = xc + xbuf_ref[s0:s0 + ML_TS, :] * cw_ref[j:j + 1, :]
    xbuf_ref[0:ML_PAD, :] = mx_ref[ML_TS - ML_PAD:ML_TS, :]
    xc = (xc * jax.nn.sigmoid(xc)).astype(BF16)
    for h in range(nh):
        xh = xc[:, h * dv:(h + 1) * dv]
        q_ref[h] = (_dot(xh, wq_ref[h]) * (dk ** -0.5)).astype(BF16)
        kt_ref[h] = _dot_nt(wkt_ref[h], xh)

    gr = gr_ref[...]
    li_r = gr[0:nh, :] + brow_ref[0:nh, :]
    lf_r = jax.nn.log_sigmoid(gr[nh:2 * nh, :] + brow_ref[nh:2 * nh, :])

    ri = lax.broadcasted_iota(jnp.int32, (L, L), 0)
    ci = lax.broadcasted_iota(jnp.int32, (L, L), 1)
    causal = ri >= ci
    ones_v = jnp.ones((L, dv), BF16)
    ones_n = jnp.ones((dv, dv), BF16)

    tr = lax.broadcasted_iota(jnp.int32, (ML_TS, ML_TS), 0)
    tc = lax.broadcasted_iota(jnp.int32, (ML_TS, ML_TS), 1)
    chunk_triu = jnp.where((tr <= tc) & (tr // L == tc // L), 1.0, 0.0).astype(F32)
    b_all = jnp.dot(lf_r, chunk_triu, precision=lax.Precision.HIGHEST, preferred_element_type=F32)
    g_all = li_r - b_all
    in_chunk = lax.broadcasted_iota(jnp.int32, (nh, ML_TS), 1) % L
    run_all = g_all
    for sh in (1, 2, 4, 8, 16, 32, 64):
        run_all = jnp.maximum(run_all, jnp.where(in_chunk >= sh, pltpu.roll(run_all, sh, axis=1), NEG_BIG))

    def to_rows(row):
        return jnp.broadcast_to(row, (L, L)).T

    for c in range(ML_TS // L):
        r0 = c * L
        b_r, g_r, run = b_all[:, r0:r0 + L], g_all[:, r0:r0 + L], run_all[:, r0:r0 + L]
        run_rows = [to_rows(run[h:h + 1, :]) for h in range(nh)]
        b_rows = [to_rows(b_r[h:h + 1, :]) for h in range(nh)]
        m_prev = m_ref[...]
        b_last = b_r[:, L - 1:L]
        wk_r = b_last + g_r
        m_new = jnp.maximum(b_last + m_prev, jnp.max(wk_r, axis=1, keepdims=True))
        ws_r = jnp.exp(wk_r - m_new)
        wc_r = jnp.exp(b_last + m_prev - m_new)
        m_ref[...] = m_new
        outs = []
        for h in range(nh):
            big_m_t = jnp.maximum(run_rows[h], m_prev[h:h + 1, :])
            m_t = big_m_t + b_rows[h]
            decay = jnp.where(causal, jnp.exp(g_r[h:h + 1, :] - big_m_t), 0.0)
            w_int = jnp.exp(m_prev[h:h + 1, :] - big_m_t)
            qh = q_ref[h, r0:r0 + L, :]
            kt = kt_ref[h, :, r0:r0 + L]
            v_ext = jnp.concatenate(
                [mv_ref[r0:r0 + L, h * dv:(h + 1) * dv].astype(BF16), ones_v], axis=1)
            wt = _dot(qh, kt.astype(BF16)) * decay
            nd = (_dot(wt.astype(BF16), v_ext)
                  + jnp.concatenate([w_int, w_int], axis=1) * _dot(qh, s_ref[h].astype(BF16)))
            hv = nd[:, 0:dv] / jnp.maximum(jnp.abs(nd[:, dv:2 * dv]), jnp.exp(-m_t))
            ms = _dot((hv * hv).astype(BF16), ones_n) * (1.0 / dv)
            outs.append(hv * lax.rsqrt(ms + EPS) * hg_ref[0:1, h * dv:(h + 1) * dv])
            wc = jnp.broadcast_to(wc_r[h:h + 1, :], (dk, L))
            s_ref[h] = (jnp.concatenate([wc, wc], axis=1) * s_ref[h]
                        + _dot((kt * ws_r[h:h + 1, :]).astype(BF16), v_ext))
        hh = jnp.concatenate(outs, axis=1)
        o_ref[r0:r0 + L, :] = (hh * jax.nn.sigmoid(mo_ref[r0:r0 + L, :])).astype(o_ref.dtype)


def _mlstm(m3, gt, conv_w, conv_b, wq, wkt, brow, head_g):
    b, s, _ = m3.shape
    nh = MLSTM_HEADS
    tile = lambda c: pl.BlockSpec((None, ML_TS, MIX_B), lambda bb, t: (bb, t, c))
    const2 = lambda shape: pl.BlockSpec(shape, lambda bb, t: (0, 0))
    const3 = lambda shape: pl.BlockSpec(shape, lambda bb, t: (0, 0, 0))
    return pl.pallas_call(
        _mlstm_kernel,
        grid=(b, s // ML_TS),
        in_specs=[
            tile(0), tile(1), tile(2),
            pl.BlockSpec((None, 2 * nh, ML_TS), lambda bb, t: (bb, 0, t)),
            const2((CONV_WIDTH, MIX_B)), const2((1, MIX_B)),
            const3((nh, MLSTM_V_DIM, MLSTM_QK_DIM)), const3((nh, MLSTM_QK_DIM, MLSTM_V_DIM)),
            const2((2 * nh, 1)), const2((1, MIX_B)),
        ],
        out_specs=pl.BlockSpec((None, ML_TS, MIX_B), lambda bb, t: (bb, t, 0)),
        out_shape=jax.ShapeDtypeStruct((b, s, MIX_B), BF16),
        scratch_shapes=[
            pltpu.VMEM((ML_PAD + ML_TS, MIX_B), F32),
            pltpu.VMEM((nh, ML_TS, MLSTM_QK_DIM), BF16),
            pltpu.VMEM((nh, MLSTM_QK_DIM, ML_TS), F32),
            pltpu.VMEM((nh, MLSTM_QK_DIM, 2 * MLSTM_V_DIM), F32),
            pltpu.VMEM((nh, ML_L), F32),
        ],
        compiler_params=pltpu.CompilerParams(
            dimension_semantics=("parallel", "arbitrary"), vmem_limit_bytes=VMEM_LIMIT),
        name="mlstm",
    )(m3, m3, m3, gt, conv_w, conv_b, wq, wkt, brow, head_g)


MERGE_TM = 512


def _merge_kernel(x_ref, ya_ref, yb_ref, ga_ref, gb_ref, wa_ref, wb_ref, wo_ref, o_ref):
    merged = (jax.nn.sigmoid(ga_ref[...]) * _dot(ya_ref[...], wa_ref[...])
              + jax.nn.sigmoid(gb_ref[...]) * _dot(yb_ref[...], wb_ref[...]))
    o_ref[...] = x_ref[...] + _dot(merged.astype(BF16), wo_ref[...])


def _merge(x2d, ya, yb, gates, wa, wb, wo):
    m = x2d.shape[0]
    row = lambda i: (i, 0)
    const = lambda i: (0, 0)
    return pl.pallas_call(
        _merge_kernel,
        grid=(m // MERGE_TM,),
        in_specs=[
            pl.BlockSpec((MERGE_TM, D_MODEL), row),
            pl.BlockSpec((MERGE_TM, MIX_A), row),
            pl.BlockSpec((MERGE_TM, MIX_B), row),
            pl.BlockSpec((MERGE_TM, D_MODEL), lambda i: (i, 0)),
            pl.BlockSpec((MERGE_TM, D_MODEL), lambda i: (i, 1)),
            pl.BlockSpec((MIX_A, D_MODEL), const),
            pl.BlockSpec((MIX_B, D_MODEL), const),
            pl.BlockSpec((D_MODEL, D_MODEL), const),
        ],
        out_specs=pl.BlockSpec((MERGE_TM, D_MODEL), row),
        out_shape=jax.ShapeDtypeStruct((m, D_MODEL), F32),
        compiler_params=pltpu.CompilerParams(
            dimension_semantics=("parallel",), vmem_limit_bytes=VMEM_LIMIT),
        name="merge",
    )(x2d, ya, yb, gates, gates, wa, wb, wo)


def _block_diag_ones(n, blk):
    r = jnp.arange(n) // blk
    return (r[:, None] == r[None, :]).astype(BF16)


def _arrange_w_in(w_in):
    widths = (512, 128, 128, 512, 64, 8, 512, 512, 4, 4, 512, 1024, 1024)
    parts, off = [], 0
    for wd in widths:
        parts.append(w_in[:, off:off + wd])
        off += wd
    aq, ak, av, iq, ik, iw, mx, mv, mi, mf, mo, ga, gb = parts
    z = lambda n: jnp.zeros((w_in.shape[0], n), w_in.dtype)
    cols = [aq, iq, ak[:, :64], z(64), ak[:, 64:], z(64), ik, z(64), av,
            iw, mi, mf, z(G_WIDTH - 16),
            mx, mv, mo, ga, gb]
    return jnp.concatenate(cols, axis=1).astype(BF16)


def kernel(x, ffn1_norm, ffn1_w_gate, ffn1_w_up, ffn1_w_down, mix_norm, w_in, q_norm, k_norm,
           idx_k_norm, conv_w, conv_b, w_mq, w_mk, b_i, b_f, m_head_norm, w_proj_a, w_proj_b,
           w_out, ffn2_norm, ffn2_w_gate, ffn2_w_up, ffn2_w_down):
    b, s, _ = x.shape
    m = b * s
    nb = s // Q_BLOCK
    nh = MLSTM_HEADS
    x2d = x.reshape(m, D_MODEL)
    for l in range(ffn1_norm.shape[0]):
        x2d = _ffn(x2d, ffn1_norm[l], ffn1_w_gate[l].astype(BF16), ffn1_w_up[l].astype(BF16),
                   ffn1_w_down[l].astype(BF16))

        gq = (jnp.tile(q_norm[l], ATT_HEADS) * (ATT_HEAD_DIM ** -0.5 * LOG2_E)).reshape(1, 512)
        z64 = jnp.zeros((ATT_HEAD_DIM,), F32)
        one64 = z64.at[0].set(1.0)
        gk = jnp.concatenate([k_norm[l], z64] * ATT_KV_HEADS + [idx_k_norm[l], z64]).reshape(1, -1)
        kone = jnp.concatenate([z64, one64] * ATT_KV_HEADS + [z64, z64]).reshape(1, -1)
        a2d, g2d, m2d, t2d = _proj(x2d, mix_norm[l], _arrange_w_in(w_in[l]),
                                   _block_diag_ones(512, ATT_HEAD_DIM),
                                   _block_diag_ones(COL_V - COL_K, ATT_HEAD_DIM), gq, gk, kone)

        a3 = a2d.reshape(b, s, A_WIDTH)
        vt = _v_transposed(a3[:, :, COL_V:COL_V + ATT_KV_HEADS * ATT_HEAD_DIM])
        g3 = g2d.reshape(b, s, G_WIDTH)
        iwt = jnp.swapaxes(g3[:, :, 0:IDX_HEADS].reshape(b, nb, Q_BLOCK, IDX_HEADS), 2, 3)
        qcap = (jnp.max(jnp.abs(gq)) * (ATT_HEAD_DIM ** 0.5) * BOUND_SLACK).reshape(1)
        ya = _dsa(a3, vt, iwt, qcap)

        gt = jnp.swapaxes(g3[:, :, IDX_HEADS:IDX_HEADS + 2 * nh], 1, 2)
        bias = jnp.concatenate([b_i[l], b_f[l]])
        yb = _mlstm(m2d.reshape(b, s, M_WIDTH), gt, conv_w[l], conv_b[l].reshape(1, MIX_B),
                    w_mq[l].astype(BF16), jnp.swapaxes(w_mk[l], 1, 2).astype(BF16),
                    bias.reshape(2 * nh, 1), m_head_norm[l].reshape(1, MIX_B))

        x2d = _merge(x2d, ya.reshape(m, MIX_A), yb.reshape(m, MIX_B), t2d,
                     w_proj_a[l].astype(BF16), w_proj_b[l].astype(BF16), w_out[l].astype(BF16))

        x2d = _ffn(x2d, ffn2_norm[l], ffn2_w_gate[l].astype(BF16), ffn2_w_up[l].astype(BF16),
                   ffn2_w_down[l].astype(BF16))
    return x2d.reshape(b, s, D_MODEL)
```

```python
import functools

import jax
import jax.numpy as jnp
from jax import lax
from jax.experimental import pallas as pl
from jax.experimental.pallas import tpu as pltpu

F32 = jnp.float32
BF16 = jnp.bfloat16

D_MODEL = 1024
D_FF = 2816
EPS = 1e-6
ATT_HEADS = 8
ATT_KV_HEADS = 2
ATT_HEAD_DIM = 64
IDX_HEADS = 8
IDX_HEAD_DIM = 64
TOPK_MAX = 256
Q_BLOCK = 128
CHUNK = 64
MLSTM_HEADS = 4
MLSTM_QK_DIM = 64
MLSTM_V_DIM = 128
CONV_WIDTH = 4
MIX_A = ATT_HEADS * ATT_HEAD_DIM
MIX_B = MLSTM_HEADS * MLSTM_V_DIM

LANES = 128
VMEM_LIMIT = 56 * 1024 * 1024
INT_MIN = -2147483648
NEG_BIG = -1e30
LOG2_E = 1.4426950408889634

COL_Q, COL_IQ, COL_K, COL_IK, COL_V = 0, 512, 1024, 1280, 1408
K_EXT = 128
A_WIDTH = 1536
G_WIDTH = 128
M_WIDTH = 3 * MIX_B
T_WIDTH = 2 * D_MODEL
W_TOTAL = A_WIDTH + G_WIDTH + M_WIDTH + T_WIDTH


def _rms(x, g):
    return x * lax.rsqrt(jnp.mean(x * x, axis=-1, keepdims=True) + EPS) * g


def _dot(a, b):
    return jnp.dot(a, b, preferred_element_type=F32)


def _dot_nt(a, b):
    return lax.dot_general(a, b, (((1,), (1,)), ((), ())), preferred_element_type=F32)


def _dot_tn(a, b):
    return lax.dot_general(a, b, (((0,), (0,)), ((), ())), preferred_element_type=F32)


FFN_TM = 512
FFN_TF = 1408


def _ffn_kernel(x_ref, g_ref, wg_ref, wu_ref, wd_ref, o_ref):
    _half_ffn(x_ref[...], g_ref, wg_ref, wu_ref, wd_ref, o_ref)


def _half_ffn(x, g_ref, wg_ref, wu_ref, wd_ref, o_ref):
    h = _rms(x, g_ref[...]).astype(BF16)
    acc = None
    for c in range(D_FF // FFN_TF):
        cols = slice(c * FFN_TF, (c + 1) * FFN_TF)
        a = _dot(h, wg_ref[:, cols])
        u = _dot(h, wu_ref[:, cols])
        act = (a * jax.nn.sigmoid(a) * u).astype(BF16)
        part = _dot(act, wd_ref[cols, :])
        acc = part if acc is None else acc + part
    o_ref[...] = x + 0.5 * acc


def _ffn(x2d, g, wg, wu, wd):
    m = x2d.shape[0]
    const = lambda i: (0, 0)
    resident = lambda shape: pl.BlockSpec(shape, const, pipeline_mode=pl.Buffered(1))
    return pl.pallas_call(
        _ffn_kernel,
        grid=(m // FFN_TM,),
        in_specs=[
            pl.BlockSpec((FFN_TM, D_MODEL), lambda i: (i, 0)),
            pl.BlockSpec((1, D_MODEL), const),
            resident((D_MODEL, D_FF)),
            resident((D_MODEL, D_FF)),
            resident((D_FF, D_MODEL)),
        ],
        out_specs=pl.BlockSpec((FFN_TM, D_MODEL), lambda i: (i, 0)),
        out_shape=jax.ShapeDtypeStruct((m, D_MODEL), F32),
        compiler_params=pltpu.CompilerParams(
            dimension_semantics=("parallel",), vmem_limit_bytes=VMEM_LIMIT),
        name="ffn",
    )(x2d, g.reshape(1, D_MODEL), wg, wu, wd)


PROJ_TM = 512


def _proj_kernel(x_ref, g_ref, w_ref, bdq_ref, bdk_ref, gq_ref, gk_ref, kone_ref,
                 a_ref, g_out_ref, mx_ref, mvo_ref, t_ref):
    h = _rms(x_ref[...], g_ref[...]).astype(BF16)

    def head_norm(p, bd_ref, gain_ref):
        ss = _dot((p * p).astype(BF16), bd_ref[...]) * (1.0 / ATT_HEAD_DIM)
        return p * lax.rsqrt(ss + EPS) * gain_ref[...]

    pq = _dot(h, w_ref[:, COL_Q:COL_IQ])
    a_ref[:, COL_Q:COL_IQ] = head_norm(pq, bdq_ref, gq_ref).astype(BF16)
    a_ref[:, COL_IQ:COL_K] = _dot(h, w_ref[:, COL_IQ:COL_K]).astype(BF16)
    pk = _dot(h, w_ref[:, COL_K:COL_V])
    a_ref[:, COL_K:COL_V] = (head_norm(pk, bdk_ref, gk_ref) + kone_ref[...]).astype(BF16)
    a_ref[:, COL_V:A_WIDTH] = _dot(h, w_ref[:, COL_V:A_WIDTH]).astype(BF16)
    o = A_WIDTH
    g_out_ref[...] = _dot(h, w_ref[:, o:o + G_WIDTH])
    o += G_WIDTH
    mx_ref[...] = _dot(h, w_ref[:, o:o + MIX_B])
    o += MIX_B
    for c in range(2):
        mvo_ref[:, c * MIX_B:(c + 1) * MIX_B] = _dot(h, w_ref[:, o + c * MIX_B:o + (c + 1) * MIX_B]).astype(BF16)
    o += 2 * MIX_B
    for c in range(2):
        t_ref[:, c * D_MODEL:(c + 1) * D_MODEL] = _dot(
            h, w_ref[:, o + c * D_MODEL:o + (c + 1) * D_MODEL]).astype(BF16)


def _proj(x2d, g, w_all, bdq, bdk, gq, gk, kone):
    m = x2d.shape[0]
    const = lambda i: (0, 0)
    row = lambda i: (i, 0)
    return pl.pallas_call(
        _proj_kernel,
        grid=(m // PROJ_TM,),
        in_specs=[
            pl.BlockSpec((PROJ_TM, D_MODEL), row),
            pl.BlockSpec((1, D_MODEL), const),
            pl.BlockSpec((D_MODEL, W_TOTAL), const, pipeline_mode=pl.Buffered(1)),
            pl.BlockSpec((512, 512), const),
            pl.BlockSpec((COL_V - COL_K, COL_V - COL_K), const),
            pl.BlockSpec((1, 512), const),
            pl.BlockSpec((1, COL_V - COL_K), const),
            pl.BlockSpec((1, COL_V - COL_K), const),
        ],
        out_specs=[
            pl.BlockSpec((PROJ_TM, A_WIDTH), row),
            pl.BlockSpec((PROJ_TM, G_WIDTH), row),
            pl.BlockSpec((PROJ_TM, MIX_B), row),
            pl.BlockSpec((PROJ_TM, 2 * MIX_B), row),
            pl.BlockSpec((PROJ_TM, T_WIDTH), row),
        ],
        out_shape=[
            jax.ShapeDtypeStruct((m, A_WIDTH), BF16),
            jax.ShapeDtypeStruct((m, G_WIDTH), F32),
            jax.ShapeDtypeStruct((m, MIX_B), F32),
            jax.ShapeDtypeStruct((m, 2 * MIX_B), BF16),
            jax.ShapeDtypeStruct((m, T_WIDTH), BF16),
        ],
        compiler_params=pltpu.CompilerParams(
            dimension_semantics=("parallel",), vmem_limit_bytes=VMEM_LIMIT),
        name="proj",
    )(x2d, g.reshape(1, D_MODEL), w_all, bdq, bdk, gq, gk, kone)


DSA_KB = 512
DSA_KA = 256
BOUND_LIMIT = 60.0
BOUND_SLACK = 1.01
DSA_VROWS = 80

BIT_GROUP = 256
SEARCH_CHUNK = 1024


def _bit_transpose(words):
    a = list(words)
    j, m = 16, 0x0000FFFF
    while j:
        k = 0
        while k < 32:
            t = (a[k] ^ (a[k + j] >> j)) & m
            a[k] = a[k] ^ t
            a[k + j] = a[k + j] ^ (t << j)
            k = (k + j + 1) & ~j
        j >>= 1
        m = (m ^ (m << j)) & 0xFFFFFFFF
    return a


def _heads_to_lanes(x, n_heads, dh):
    xf = x.astype(F32)
    return jnp.concatenate([xf[:, h * dh:(h + 1) * dh].T for h in range(n_heads)], axis=1).astype(BF16)


def _dsa_kernel(qcap_ref, q_ref, iq_ref, k_ref, ik_ref, vt_ref, iwt_ref, o_ref,
                keys_ref, planes_ref, alive_ref, la_ref, lb_ref, acc_ref, sa_ref, sb_ref, stab_ref,
                *, seq, k_sel, idx_scale):
    kb_sz = DSA_KB
    bi = pl.program_id(1)
    nkb = (bi * Q_BLOCK + Q_BLOCK + kb_sz - 1) // kb_sz
    lane = lax.broadcasted_iota(jnp.int32, (1, LANES), 1)
    limit = bi * Q_BLOCK + jnp.where(lane < CHUNK, CHUNK, 2 * CHUNK)
    row_iota = lax.broadcasted_iota(jnp.int32, (kb_sz, LANES), 0)

    iq_t = _heads_to_lanes(iq_ref[...], IDX_HEADS, IDX_HEAD_DIM)
    w = iwt_ref[...]

    def logits_of(blk):
        off = pl.multiple_of(blk * kb_sz, kb_sz)
        return _dot(ik_ref[pl.ds(off, kb_sz), :][:, :IDX_HEAD_DIM], iq_t)

    def keys_from(l_ref, blk, masked):
        off = pl.multiple_of(blk * kb_sz, kb_sz)
        sc = jnp.zeros((kb_sz, LANES), F32)
        for h in range(IDX_HEADS):
            sc = sc + jnp.maximum(l_ref[:, h * LANES:(h + 1) * LANES], 0.0) * w[h:h + 1, :]
        sc = sc * idx_scale
        bits = pltpu.bitcast(sc, jnp.int32)
        key = bits ^ ((bits >> 31) & 0x7FFFFFFF)
        key = jnp.where(sc == 0.0, 0, key)
        if masked:
            key = jnp.where(off + row_iota < limit, key, INT_MIN)
        keys_ref[pl.ds(off, kb_sz), :] = key
        ukey = key ^ INT_MIN
        for grp in range(kb_sz // BIT_GROUP):
            base = grp * BIT_GROUP
            planes = _bit_transpose([ukey[base + 8 * i:base + 8 * i + 8, :] for i in range(32)])
            row = pl.multiple_of((off + base) // 32, 8)
            for p in range(32):
                planes_ref[p, pl.ds(row, 8), :] = planes[p]

    n_pairs = (nkb + 1) // 2

    def score_pair(j, masked):
        lb_ref[...] = logits_of(2 * j + 1)
        keys_from(la_ref, 2 * j, masked)
        la_ref[...] = logits_of(jnp.minimum(2 * j + 2, 2 * n_pairs - 1))
        keys_from(lb_ref, 2 * j + 1, masked)

    la_ref[...] = logits_of(0)
    lax.fori_loop(0, n_pairs - 1, lambda j, c: (score_pair(j, False), c)[1], 0)
    score_pair(n_pairs - 1, True)

    def count(pred):
        def body(kb, c):
            off = pl.multiple_of(kb * kb_sz, kb_sz)
            ind = jnp.where(pred(keys_ref[pl.ds(off, kb_sz), :], off + row_iota), 1, 0)
            return c + jnp.sum(ind.reshape(kb_sz // 8, 8, LANES), axis=0)
        c8 = lax.fori_loop(0, nkb, body, jnp.zeros((8, LANES), jnp.int32))
        return jnp.sum(c8, axis=0, keepdims=True)

    zeros = jnp.zeros((1, LANES), jnp.int32)
    ch_rows = SEARCH_CHUNK // 32
    n_ch = (nkb + 1) // 2

    zeros8 = jnp.zeros((8, LANES), jnp.int32)

    def ones_in(x):
        return jnp.sum(lax.population_count(x).reshape(ch_rows // 8, 8, LANES), axis=0)

    def over_chunks(body):
        return jnp.sum(lax.fori_loop(0, n_ch, lambda c, cnt: cnt + body(c), zeros8), axis=0, keepdims=True)

    def first_chunk(c):
        row = pl.multiple_of(c * ch_rows, ch_rows)
        alive_ref[pl.ds(row, ch_rows), :] = jnp.full((ch_rows, LANES), -1, jnp.int32)
        return ones_in(planes_ref[0, pl.ds(row, ch_rows), :])

    def decide(p, t_u, above, c1):
        take = above + c1 >= k_sel
        t_u = jnp.where(take, t_u | jnp.left_shift(jnp.int32(1), 31 - p), t_u)
        return t_u, jnp.where(take, above, above + c1), jnp.where(take, 0, -1)

    def radix_pass(p, carry):
        t_u, above, flip = carry

        def chunk(c):
            row = pl.multiple_of(c * ch_rows, ch_rows)
            alive = alive_ref[pl.ds(row, ch_rows), :] & (planes_ref[p - 1, pl.ds(row, ch_rows), :] ^ flip)
            alive_ref[pl.ds(row, ch_rows), :] = alive
            return ones_in(alive & planes_ref[p, pl.ds(row, ch_rows), :])

        return decide(p, t_u, above, over_chunks(chunk))

    carry = decide(0, zeros, zeros, over_chunks(first_chunk))
    t_u, above, flip = lax.fori_loop(1, 32, radix_pass, carry)

    def last_chunk(c):
        row = pl.multiple_of(c * ch_rows, ch_rows)
        return ones_in(alive_ref[pl.ds(row, ch_rows), :] & (planes_ref[31, pl.ds(row, ch_rows), :] ^ flip))

    c_eq = over_chunks(last_chunk)
    few = t_u == 0
    thr = jnp.maximum(t_u ^ INT_MIN, INT_MIN + 1)
    need = k_sel - above
    excess = jnp.where(few, 0, c_eq - need)
    pos_bits = seq.bit_length() - 1

    @pl.when(jnp.max(excess) > 0)
    def _():
        def jbit(i, cut):
            cand = cut | jnp.left_shift(jnp.int32(1), pos_bits - 1 - i)
            c = count(lambda kk, pos: (kk == thr) & (pos < cand))
            return jnp.where(c <= need, cand, cut)
        cut = lax.fori_loop(0, pos_bits, jbit, zeros)
        cut = jnp.where(excess > 0, cut, seq)

        def drop_block(kb, _):
            off = pl.multiple_of(kb * kb_sz, kb_sz)
            kk = keys_ref[pl.ds(off, kb_sz), :]
            keys_ref[pl.ds(off, kb_sz), :] = jnp.where(
                (kk == thr) & (off + row_iota >= cut), INT_MIN, kk)
            return 0

        lax.fori_loop(0, nkb, drop_block, 0)

    reps = ATT_HEADS // ATT_KV_HEADS
    width = reps * LANES
    q_t = _heads_to_lanes(q_ref[...], ATT_HEADS, ATT_HEAD_DIM)
    acc_ref[...] = jnp.zeros_like(acc_ref)

    @pl.when(bi == 0)
    def _():
        def kmax_block(kb, best):
            off = pl.multiple_of(kb * kb_sz, kb_sz)
            kf = k_ref[pl.ds(off, kb_sz), :].astype(F32)
            n2 = [jnp.max(jnp.sum(jnp.square(kf[:, g * K_EXT:g * K_EXT + ATT_HEAD_DIM]), axis=1, keepdims=True))
                  for g in range(ATT_KV_HEADS)]
            return tuple(jnp.maximum(b_, n_) for b_, n_ in zip(best, n2))
        best = lax.fori_loop(0, seq // kb_sz, kmax_block, (jnp.float32(0.0),) * ATT_KV_HEADS)
        worst = jnp.float32(0.0)
        for g in range(ATT_KV_HEADS):
            stab_ref[g] = jnp.sqrt(best[g]) * BOUND_SLACK
            worst = jnp.maximum(worst, stab_ref[g] * qcap_ref[0])
        stab_ref[ATT_KV_HEADS] = jnp.where(worst < BOUND_LIMIT, 1.0, 0.0)

    bounded = stab_ref[ATT_KV_HEADS] > 0.5
    qf = q_t.astype(F32)
    q_len = jnp.sqrt(jnp.sum(qf * qf, axis=0, keepdims=True))
    rhs = []
    for g in range(ATT_KV_HEADS):
        cols = slice(g * width, (g + 1) * width)
        neg_bound = jnp.where(bounded, -q_len[:, cols] * stab_ref[g], 0.0)
        rhs.append(jnp.concatenate(
            [qf[:, cols], neg_bound, jnp.zeros((K_EXT - ATT_HEAD_DIM - 1, width), F32)], axis=0).astype(BF16))

    ka = DSA_KA

    def masked_scores(blk):
        off = pl.multiple_of(blk * ka, ka)
        bias = jnp.where(keys_ref[pl.ds(off, ka), :] >= thr, 0.0, NEG_BIG)
        bias = jnp.concatenate([bias] * reps, axis=1)
        kblk = k_ref[pl.ds(off, ka), :]
        return jnp.concatenate(
            [_dot(kblk[:, g * K_EXT:(g + 1) * K_EXT], rhs[g]) + bias for g in range(ATT_KV_HEADS)],
            axis=1)

    def consume_bounded(s_ref, blk):
        off = pl.multiple_of(blk * ka, ka)
        p = jnp.exp2(s_ref[...]).astype(BF16)
        for g in range(ATT_KV_HEADS):
            acc_ref[g] += _dot(vt_ref[g, :, pl.ds(off, ka)], p[:, g * width:(g + 1) * width])

    def consume_online(s_ref, blk, m_old):
        off = pl.multiple_of(blk * ka, ka)
        s = s_ref[...]
        m_new = jnp.maximum(m_old, jnp.max(s, axis=0, keepdims=True))
        alpha = jnp.exp2(m_old - m_new)
        p = jnp.exp2(s - m_new).astype(BF16)
        for g in range(ATT_KV_HEADS):
            vt = vt_ref[g, :, pl.ds(off, ka)]
            acc_ref[g] = (alpha[:, g * width:(g + 1) * width] * acc_ref[g]
                          + _dot(vt, p[:, g * width:(g + 1) * width]))
        return m_new

    n_sub = nkb * (kb_sz // ka)
    sa_ref[...] = masked_scores(0)

    @pl.when(bounded)
    def _():
        def att_pair(j, c):
            sb_ref[...] = masked_scores(2 * j + 1)
            consume_bounded(sa_ref, 2 * j)
            sa_ref[...] = masked_scores(jnp.minimum(2 * j + 2, n_sub - 1))
            consume_bounded(sb_ref, 2 * j + 1)
            return c
        lax.fori_loop(0, n_sub // 2, att_pair, 0)

    @pl.when(jnp.logical_not(bounded))
    def _():
        def att_pair(j, m_run):
            sb_ref[...] = masked_scores(2 * j + 1)
            m_run = consume_online(sa_ref, 2 * j, m_run)
            sa_ref[...] = masked_scores(jnp.minimum(2 * j + 2, n_sub - 1))
            return consume_online(sb_ref, 2 * j + 1, m_run)
        lax.fori_loop(0, n_sub // 2, att_pair, jnp.full((1, ATT_HEADS * LANES), NEG_BIG, F32))

    pieces = []
    for g in range(ATT_KV_HEADS):
        a = acc_ref[g]
        o = a[0:ATT_HEAD_DIM, :] / a[ATT_HEAD_DIM:ATT_HEAD_DIM + 1, :]
        pieces += [o[:, r * LANES:(r + 1) * LANES] for r in range(reps)]
    o_ref[...] = jnp.concatenate(pieces, axis=0).T.astype(o_ref.dtype)


def _v_transposed(v):
    b, s, _ = v.shape
    vt = jnp.swapaxes(v.reshape(b, s, ATT_KV_HEADS, ATT_HEAD_DIM), 1, 3)
    vt = jnp.swapaxes(vt, 1, 2)
    ones = jnp.ones((b, ATT_KV_HEADS, 1, s), v.dtype)
    pad = jnp.zeros((b, ATT_KV_HEADS, DSA_VROWS - ATT_HEAD_DIM - 1, s), v.dtype)
    return jnp.concatenate([vt, ones, pad], axis=2)


def _dsa(a3, vt, iwt, qcap):
    b, s, _ = a3.shape
    nb = s // Q_BLOCK
    k_sel = min(TOPK_MAX, s // 4)
    idx_scale = (IDX_HEAD_DIM ** -0.5) * (IDX_HEADS ** -0.5)
    kern = functools.partial(_dsa_kernel, seq=s, k_sel=k_sel, idx_scale=idx_scale)
    return pl.pallas_call(
        kern,
        grid=(b, nb),
        in_specs=[
            pl.BlockSpec(memory_space=pltpu.SMEM),
            pl.BlockSpec((None, Q_BLOCK, 512), lambda bb, i: (bb, i, COL_Q // 512)),
            pl.BlockSpec((None, Q_BLOCK, 512), lambda bb, i: (bb, i, COL_IQ // 512)),
            pl.BlockSpec((None, s, ATT_KV_HEADS * K_EXT), lambda bb, i: (bb, 0, COL_K // (ATT_KV_HEADS * K_EXT))),
            pl.BlockSpec((None, s, LANES), lambda bb, i: (bb, 0, COL_IK // LANES)),
            pl.BlockSpec((None, ATT_KV_HEADS, DSA_VROWS, s), lambda bb, i: (bb, 0, 0, 0)),
            pl.BlockSpec((None, None, IDX_HEADS, LANES), lambda bb, i: (bb, i, 0, 0)),
        ],
        out_specs=pl.BlockSpec((None, Q_BLOCK, MIX_A), lambda bb, i: (bb, i, 0)),
        out_shape=jax.ShapeDtypeStruct((b, s, MIX_A), BF16),
        scratch_shapes=[
            pltpu.VMEM((s, LANES), jnp.int32),
            pltpu.VMEM((32, s // 32, LANES), jnp.int32),
            pltpu.VMEM((s // 32, LANES), jnp.int32),
            pltpu.VMEM((DSA_KB, IDX_HEADS * LANES), F32),
            pltpu.VMEM((DSA_KB, IDX_HEADS * LANES), F32),
            pltpu.VMEM((ATT_KV_HEADS, DSA_VROWS, (ATT_HEADS // ATT_KV_HEADS) * LANES), F32),
            pltpu.VMEM((DSA_KA, ATT_HEADS * LANES), F32),
            pltpu.VMEM((DSA_KA, ATT_HEADS * LANES), F32),
            pltpu.SMEM((ATT_KV_HEADS + 1,), F32),
        ],
        compiler_params=pltpu.CompilerParams(
            dimension_semantics=("parallel", "arbitrary"), vmem_limit_bytes=VMEM_LIMIT),
        name="dsa",
    )(qcap, a3, a3, a3, a3, vt, iwt)


ML_L = 128
ML_TS = 512
ML_PAD = 8


def _mlstm_kernel(mx_ref, mv_ref, mo_ref, gr_ref, cw_ref, cb_ref, wq_ref, wkt_ref, brow_ref, hg_ref, o_ref,
                  xbuf_ref, q_ref, kt_ref, s_ref, m_ref):
    t_idx = pl.program_id(1)
    nh, dk, dv, L = MLSTM_HEADS, MLSTM_QK_DIM, MLSTM_V_DIM, ML_L

    @pl.when(t_idx == 0)
    def _():
        xbuf_ref[0:ML_PAD, :] = jnp.zeros((ML_PAD, MIX_B), F32)
        s_ref[...] = jnp.zeros_like(s_ref)
        m_ref[...] = jnp.zeros_like(m_ref)

    xbuf_ref[ML_PAD:ML_PAD + ML_TS, :] = mx_ref[...]
    xc = cb_ref[...] + jnp.zeros((ML_TS, MIX_B), F32)
    for j in range(CONV_WIDTH):
        s0 = ML_PAD - (CONV_WIDTH - 1) + j
        xc = xc + xbuf_ref[s0:s0 + ML_TS, :] * cw_ref[j:j + 1, :]
    xbuf_ref[0:ML_PAD, :] = mx_ref[ML_TS - ML_PAD:ML_TS, :]
    xc = (xc * jax.nn.sigmoid(xc)).astype(BF16)
    for h in range(nh):
        xh = xc[:, h * dv:(h + 1) * dv]
        q_ref[h] = (_dot(xh, wq_ref[h]) * (dk ** -0.5)).astype(BF16)
        kt_ref[h] = _dot_nt(wkt_ref[h], xh)

    gr = gr_ref[...]
    li_r = gr[0:nh, :] + brow_ref[0:nh, :]
    lf_r = jax.nn.log_sigmoid(gr[nh:2 * nh, :] + brow_ref[nh:2 * nh, :])

    ri = lax.broadcasted_iota(jnp.int32, (L, L), 0)
    ci = lax.broadcasted_iota(jnp.int32, (L, L), 1)
    causal = ri >= ci
    ones_v = jnp.ones((L, dv), BF16)
    ones_n = jnp.ones((dv, dv), BF16)

    tr = lax.broadcasted_iota(jnp.int32, (ML_TS, ML_TS), 0)
    tc = lax.broadcasted_iota(jnp.int32, (ML_TS, ML_TS), 1)
    chunk_triu = jnp.where((tr <= tc) & (tr // L == tc // L), 1.0, 0.0).astype(F32)
    b_all = jnp.dot(lf_r, chunk_triu, precision=lax.Precision.HIGHEST, preferred_element_type=F32)
    g_all = li_r - b_all
    in_chunk = lax.broadcasted_iota(jnp.int32, (nh, ML_TS), 1) % L
    run_all = g_all
    for sh in (1, 2, 4, 8, 16, 32, 64):
        run_all = jnp.maximum(run_all, jnp.where(in_chunk >= sh, pltpu.roll(run_all, sh, axis=1), NEG_BIG))

    def to_rows(row):
        return jnp.broadcast_to(row, (L, L)).T

    for c in range(ML_TS // L):
        r0 = c * L
        b_r, g_r, run = b_all[:, r0:r0 + L], g_all[:, r0:r0 + L], run_all[:, r0:r0 + L]
        run_rows = [to_rows(run[h:h + 1, :]) for h in range(nh)]
        b_rows = [to_rows(b_r[h:h + 1, :]) for h in range(nh)]
        m_prev = m_ref[...]
        b_last = b_r[:, L - 1:L]
        wk_r = b_last + g_r
        m_new = jnp.maximum(b_last + m_prev, jnp.max(wk_r, axis=1, keepdims=True))
        ws_r = jnp.exp(wk_r - m_new)
        wc_r = jnp.exp(b_last + m_prev - m_new)
        m_ref[...] = m_new
        outs = []
        for h in range(nh):
            big_m_t = jnp.maximum(run_rows[h], m_prev[h:h + 1, :])
            m_t = big_m_t + b_rows[h]
            decay = jnp.where(causal, jnp.exp(g_r[h:h + 1, :] - big_m_t), 0.0)
            w_int = jnp.exp(m_prev[h:h + 1, :] - big_m_t)
            qh = q_ref[h, r0:r0 + L, :]
            kt = kt_ref[h, :, r0:r0 + L]
            v_ext = jnp.concatenate(
                [mv_ref[r0:r0 + L, h * dv:(h + 1) * dv], ones_v], axis=1)
            wt = _dot(qh, kt.astype(BF16)) * decay
            nd = (_dot(wt.astype(BF16), v_ext)
                  + jnp.concatenate([w_int, w_int], axis=1) * _dot(qh, s_ref[h].astype(BF16)))
            hv = nd[:, 0:dv] / jnp.maximum(jnp.abs(nd[:, dv:2 * dv]), jnp.exp(-m_t))
            ms = _dot((hv * hv).astype(BF16), ones_n) * (1.0 / dv)
            outs.append(hv * lax.rsqrt(ms + EPS) * hg_ref[0:1, h * dv:(h + 1) * dv])
            wc = jnp.broadcast_to(wc_r[h:h + 1, :], (dk, L))
            s_ref[h] = (jnp.concatenate([wc, wc], axis=1) * s_ref[h]
                        + _dot((kt * ws_r[h:h + 1, :]).astype(BF16), v_ext))
        hh = jnp.concatenate(outs, axis=1)
        o_ref[r0:r0 + L, :] = (hh * jax.nn.sigmoid(mo_ref[r0:r0 + L, :].astype(F32))).astype(o_ref.dtype)


def _mlstm(m3, mvo3, gt, conv_w, conv_b, wq, wkt, brow, head_g):
    b, s, _ = m3.shape
    nh = MLSTM_HEADS
    tile = lambda c: pl.BlockSpec((None, ML_TS, MIX_B), lambda bb, t: (bb, t, c))
    const2 = lambda shape: pl.BlockSpec(shape, lambda bb, t: (0, 0))
    const3 = lambda shape: pl.BlockSpec(shape, lambda bb, t: (0, 0, 0))
    return pl.pallas_call(
        _mlstm_kernel,
        grid=(b, s // ML_TS),
        in_specs=[
            tile(0), tile(0), tile(1),
            pl.BlockSpec((None, 2 * nh, ML_TS), lambda bb, t: (bb, 0, t)),
            const2((CONV_WIDTH, MIX_B)), const2((1, MIX_B)),
            const3((nh, MLSTM_V_DIM, MLSTM_QK_DIM)), const3((nh, MLSTM_QK_DIM, MLSTM_V_DIM)),
            const2((2 * nh, 1)), const2((1, MIX_B)),
        ],
        out_specs=pl.BlockSpec((None, ML_TS, MIX_B), lambda bb, t: (bb, t, 0)),
        out_shape=jax.ShapeDtypeStruct((b, s, MIX_B), BF16),
        scratch_shapes=[
            pltpu.VMEM((ML_PAD + ML_TS, MIX_B), F32),
            pltpu.VMEM((nh, ML_TS, MLSTM_QK_DIM), BF16),
            pltpu.VMEM((nh, MLSTM_QK_DIM, ML_TS), F32),
            pltpu.VMEM((nh, MLSTM_QK_DIM, 2 * MLSTM_V_DIM), F32),
            pltpu.VMEM((nh, ML_L), F32),
        ],
        compiler_params=pltpu.CompilerParams(
            dimension_semantics=("parallel", "arbitrary"), vmem_limit_bytes=VMEM_LIMIT),
        name="mlstm",
    )(m3, mvo3, mvo3, gt, conv_w, conv_b, wq, wkt, brow, head_g)


MERGE_TM = 512


def _merge_ffn_kernel(x_ref, ya_ref, yb_ref, ga_ref, gb_ref, wa_ref, wb_ref, wo_ref,
                      g_ref, wg_ref, wu_ref, wd_ref, o_ref):
    merged = (jax.nn.sigmoid(ga_ref[...].astype(F32)) * _dot(ya_ref[...], wa_ref[...])
              + jax.nn.sigmoid(gb_ref[...].astype(F32)) * _dot(yb_ref[...], wb_ref[...]))
    x = x_ref[...] + _dot(merged.astype(BF16), wo_ref[...])
    _half_ffn(x, g_ref, wg_ref, wu_ref, wd_ref, o_ref)


def _merge_ffn(x2d, ya, yb, gates, wa, wb, wo, g, wg, wu, wd):
    m = x2d.shape[0]
    row = lambda i: (i, 0)
    const = lambda i: (0, 0)
    resident = lambda shape: pl.BlockSpec(shape, const, pipeline_mode=pl.Buffered(1))
    return pl.pallas_call(
        _merge_ffn_kernel,
        grid=(m // MERGE_TM,),
        in_specs=[
            pl.BlockSpec((MERGE_TM, D_MODEL), row),
            pl.BlockSpec((MERGE_TM, MIX_A), row),
            pl.BlockSpec((MERGE_TM, MIX_B), row),
            pl.BlockSpec((MERGE_TM, D_MODEL), lambda i: (i, 0)),
            pl.BlockSpec((MERGE_TM, D_MODEL), lambda i: (i, 1)),
            resident((MIX_A, D_MODEL)),
            resident((MIX_B, D_MODEL)),
            resident((D_MODEL, D_MODEL)),
            pl.BlockSpec((1, D_MODEL), const),
            resident((D_MODEL, D_FF)),
            resident((D_MODEL, D_FF)),
            resident((D_FF, D_MODEL)),
        ],
        out_specs=pl.BlockSpec((MERGE_TM, D_MODEL), row),
        out_shape=jax.ShapeDtypeStruct((m, D_MODEL), F32),
        compiler_params=pltpu.CompilerParams(
            dimension_semantics=("parallel",), vmem_limit_bytes=VMEM_LIMIT),
        name="merge_ffn",
    )(x2d, ya, yb, gates, gates, wa, wb, wo, g.reshape(1, D_MODEL), wg, wu, wd)


def _block_diag_ones(n, blk):
    r = jnp.arange(n) // blk
    return (r[:, None] == r[None, :]).astype(BF16)


def _arrange_w_in(w_in):
    widths = (512, 128, 128, 512, 64, 8, 512, 512, 4, 4, 512, 1024, 1024)
    parts, off = [], 0
    for wd in widths:
        parts.append(w_in[:, off:off + wd])
        off += wd
    aq, ak, av, iq, ik, iw, mx, mv, mi, mf, mo, ga, gb = parts
    z = lambda n: jnp.zeros((w_in.shape[0], n), w_in.dtype)
    cols = [aq, iq, ak[:, :64], z(64), ak[:, 64:], z(64), ik, z(64), av,
            iw, mi, mf, z(G_WIDTH - 16),
            mx, mv, mo, ga, gb]
    return jnp.concatenate(cols, axis=1).astype(BF16)


def kernel(x, ffn1_norm, ffn1_w_gate, ffn1_w_up, ffn1_w_down, mix_norm, w_in, q_norm, k_norm,
           idx_k_norm, conv_w, conv_b, w_mq, w_mk, b_i, b_f, m_head_norm, w_proj_a, w_proj_b,
           w_out, ffn2_norm, ffn2_w_gate, ffn2_w_up, ffn2_w_down):
    b, s, _ = x.shape
    m = b * s
    nb = s // Q_BLOCK
    nh = MLSTM_HEADS
    x2d = x.reshape(m, D_MODEL)
    for l in range(ffn1_norm.shape[0]):
        x2d = _ffn(x2d, ffn1_norm[l], ffn1_w_gate[l].astype(BF16), ffn1_w_up[l].astype(BF16),
                   ffn1_w_down[l].astype(BF16))

        gq = (jnp.tile(q_norm[l], ATT_HEADS) * (ATT_HEAD_DIM ** -0.5 * LOG2_E)).reshape(1, 512)
        z64 = jnp.zeros((ATT_HEAD_DIM,), F32)
        one64 = z64.at[0].set(1.0)
        gk = jnp.concatenate([k_norm[l], z64] * ATT_KV_HEADS + [idx_k_norm[l], z64]).reshape(1, -1)
        kone = jnp.concatenate([z64, one64] * ATT_KV_HEADS + [z64, z64]).reshape(1, -1)
        a2d, g2d, mx2d, mvo2d, t2d = _proj(x2d, mix_norm[l], _arrange_w_in(w_in[l]),
                                   _block_diag_ones(512, ATT_HEAD_DIM),
                                   _block_diag_ones(COL_V - COL_K, ATT_HEAD_DIM), gq, gk, kone)

        a3 = a2d.reshape(b, s, A_WIDTH)
        vt = _v_transposed(a3[:, :, COL_V:COL_V + ATT_KV_HEADS * ATT_HEAD_DIM])
        g3 = g2d.reshape(b, s, G_WIDTH)
        iwt = jnp.swapaxes(g3[:, :, 0:IDX_HEADS].reshape(b, nb, Q_BLOCK, IDX_HEADS), 2, 3)
        qcap = (jnp.max(jnp.abs(gq)) * (ATT_HEAD_DIM ** 0.5) * BOUND_SLACK).reshape(1)
        ya = _dsa(a3, vt, iwt, qcap)

        gt = jnp.swapaxes(g3[:, :, IDX_HEADS:IDX_HEADS + 2 * nh], 1, 2)
        bias = jnp.concatenate([b_i[l], b_f[l]])
        yb = _mlstm(mx2d.reshape(b, s, MIX_B), mvo2d.reshape(b, s, 2 * MIX_B), gt,
                    conv_w[l], conv_b[l].reshape(1, MIX_B),
                    w_mq[l].astype(BF16), jnp.swapaxes(w_mk[l], 1, 2).astype(BF16),
                    bias.reshape(2 * nh, 1), m_head_norm[l].reshape(1, MIX_B))

        x2d = _merge_ffn(x2d, ya.reshape(m, MIX_A), yb.reshape(m, MIX_B), t2d,
                         w_proj_a[l].astype(BF16), w_proj_b[l].astype(BF16), w_out[l].astype(BF16),
                         ffn2_norm[l], ffn2_w_gate[l].astype(BF16), ffn2_w_up[l].astype(BF16),
                         ffn2_w_down[l].astype(BF16))
    return x2d.reshape(b, s, D_MODEL)
```

```python
import functools

import jax
import jax.numpy as jnp
from jax import lax
from jax.experimental import pallas as pl
from jax.experimental.pallas import tpu as pltpu

F32 = jnp.float32
BF16 = jnp.bfloat16

D_MODEL = 1024
D_FF = 2816
EPS = 1e-6
ATT_HEADS = 8
ATT_KV_HEADS = 2
ATT_HEAD_DIM = 64
IDX_HEADS = 8
IDX_HEAD_DIM = 64
TOPK_MAX = 256
Q_BLOCK = 128
CHUNK = 64
MLSTM_HEADS = 4
MLSTM_QK_DIM = 64
MLSTM_V_DIM = 128
CONV_WIDTH = 4
MIX_A = ATT_HEADS * ATT_HEAD_DIM
MIX_B = MLSTM_HEADS * MLSTM_V_DIM

LANES = 128
VMEM_LIMIT = 56 * 1024 * 1024
INT_MIN = -2147483648
NEG_BIG = -1e30
LOG2_E = 1.4426950408889634

COL_Q, COL_IQ, COL_K, COL_IK, COL_V = 0, 512, 1024, 1280, 1408
K_EXT = 128
A_WIDTH = 1536
G_WIDTH = 128
M_WIDTH = 3 * MIX_B
T_WIDTH = 2 * D_MODEL
W_TOTAL = A_WIDTH + G_WIDTH + M_WIDTH + T_WIDTH


def _rms(x, g):
    return x * lax.rsqrt(jnp.mean(x * x, axis=-1, keepdims=True) + EPS) * g


def _dot(a, b):
    return jnp.dot(a, b, preferred_element_type=F32)


def _dot_nt(a, b):
    return lax.dot_general(a, b, (((1,), (1,)), ((), ())), preferred_element_type=F32)


def _dot_tn(a, b):
    return lax.dot_general(a, b, (((0,), (0,)), ((), ())), preferred_element_type=F32)


FFN_TM = 512
FFN_TF = 1408


def _ffn_kernel(x_ref, g_ref, wg_ref, wu_ref, wd_ref, o_ref):
    _half_ffn(x_ref[...], g_ref, wg_ref, wu_ref, wd_ref, o_ref)


def _half_ffn(x, g_ref, wg_ref, wu_ref, wd_ref, o_ref):
    h = _rms(x, g_ref[...]).astype(BF16)
    acc = None
    for c in range(D_FF // FFN_TF):
        cols = slice(c * FFN_TF, (c + 1) * FFN_TF)
        a = _dot(h, wg_ref[:, cols])
        u = _dot(h, wu_ref[:, cols])
        act = (a * jax.nn.sigmoid(a) * u).astype(BF16)
        part = _dot(act, wd_ref[cols, :])
        acc = part if acc is None else acc + part
    o_ref[...] = x + 0.5 * acc


def _ffn(x2d, g, wg, wu, wd):
    m = x2d.shape[0]
    const = lambda i: (0, 0)
    resident = lambda shape: pl.BlockSpec(shape, const, pipeline_mode=pl.Buffered(1))
    return pl.pallas_call(
        _ffn_kernel,
        grid=(m // FFN_TM,),
        in_specs=[
            pl.BlockSpec((FFN_TM, D_MODEL), lambda i: (i, 0)),
            pl.BlockSpec((1, D_MODEL), const),
            resident((D_MODEL, D_FF)),
            resident((D_MODEL, D_FF)),
            resident((D_FF, D_MODEL)),
        ],
        out_specs=pl.BlockSpec((FFN_TM, D_MODEL), lambda i: (i, 0)),
        out_shape=jax.ShapeDtypeStruct((m, D_MODEL), F32),
        compiler_params=pltpu.CompilerParams(
            dimension_semantics=("parallel",), vmem_limit_bytes=VMEM_LIMIT),
        name="ffn",
    )(x2d, g.reshape(1, D_MODEL), wg, wu, wd)


PROJ_TM = 512


def _proj_kernel(x_ref, g_ref, w_ref, bdq_ref, bdk_ref, gq_ref, gk_ref, kone_ref,
                 a_ref, g_out_ref, mx_ref, mvo_ref, t_ref):
    h = _rms(x_ref[...], g_ref[...]).astype(BF16)

    def head_norm(p, bd_ref, gain_ref):
        ss = _dot((p * p).astype(BF16), bd_ref[...]) * (1.0 / ATT_HEAD_DIM)
        return p * lax.rsqrt(ss + EPS) * gain_ref[...]

    pq = _dot(h, w_ref[:, COL_Q:COL_IQ])
    a_ref[:, COL_Q:COL_IQ] = head_norm(pq, bdq_ref, gq_ref).astype(BF16)
    a_ref[:, COL_IQ:COL_K] = _dot(h, w_ref[:, COL_IQ:COL_K]).astype(BF16)
    pk = _dot(h, w_ref[:, COL_K:COL_V])
    a_ref[:, COL_K:COL_V] = (head_norm(pk, bdk_ref, gk_ref) + kone_ref[...]).astype(BF16)
    a_ref[:, COL_V:A_WIDTH] = _dot(h, w_ref[:, COL_V:A_WIDTH]).astype(BF16)
    o = A_WIDTH
    g_out_ref[...] = _dot(h, w_ref[:, o:o + G_WIDTH])
    o += G_WIDTH
    mx_ref[...] = _dot(h, w_ref[:, o:o + MIX_B])
    o += MIX_B
    for c in range(2):
        mvo_ref[:, c * MIX_B:(c + 1) * MIX_B] = _dot(h, w_ref[:, o + c * MIX_B:o + (c + 1) * MIX_B]).astype(BF16)
    o += 2 * MIX_B
    for c in range(2):
        t_ref[:, c * D_MODEL:(c + 1) * D_MODEL] = _dot(
            h, w_ref[:, o + c * D_MODEL:o + (c + 1) * D_MODEL]).astype(BF16)


def _proj(x2d, g, w_all, bdq, bdk, gq, gk, kone):
    m = x2d.shape[0]
    const = lambda i: (0, 0)
    row = lambda i: (i, 0)
    return pl.pallas_call(
        _proj_kernel,
        grid=(m // PROJ_TM,),
        in_specs=[
            pl.BlockSpec((PROJ_TM, D_MODEL), row),
            pl.BlockSpec((1, D_MODEL), const),
            pl.BlockSpec((D_MODEL, W_TOTAL), const, pipeline_mode=pl.Buffered(1)),
            pl.BlockSpec((512, 512), const),
            pl.BlockSpec((COL_V - COL_K, COL_V - COL_K), const),
            pl.BlockSpec((1, 512), const),
            pl.BlockSpec((1, COL_V - COL_K), const),
            pl.BlockSpec((1, COL_V - COL_K), const),
        ],
        out_specs=[
            pl.BlockSpec((PROJ_TM, A_WIDTH), row),
            pl.BlockSpec((PROJ_TM, G_WIDTH), row),
            pl.BlockSpec((PROJ_TM, MIX_B), row),
            pl.BlockSpec((PROJ_TM, 2 * MIX_B), row),
            pl.BlockSpec((PROJ_TM, T_WIDTH), row),
        ],
        out_shape=[
            jax.ShapeDtypeStruct((m, A_WIDTH), BF16),
            jax.ShapeDtypeStruct((m, G_WIDTH), F32),
            jax.ShapeDtypeStruct((m, MIX_B), F32),
            jax.ShapeDtypeStruct((m, 2 * MIX_B), BF16),
            jax.ShapeDtypeStruct((m, T_WIDTH), BF16),
        ],
        compiler_params=pltpu.CompilerParams(
            dimension_semantics=("parallel",), vmem_limit_bytes=VMEM_LIMIT),
        name="proj",
    )(x2d, g.reshape(1, D_MODEL), w_all, bdq, bdk, gq, gk, kone)


DSA_KB = 512
DSA_KA = 256
BOUND_LIMIT = 60.0
BOUND_SLACK = 1.01
DSA_VROWS = 80

BIT_GROUP = 256
SEARCH_CHUNK = 1024


def _bit_transpose(words):
    a = list(words)
    j, m = 16, 0x0000FFFF
    while j:
        k = 0
        while k < 32:
            t = (a[k] ^ (a[k + j] >> j)) & m
            a[k] = a[k] ^ t
            a[k + j] = a[k + j] ^ (t << j)
            k = (k + j + 1) & ~j
        j >>= 1
        m = (m ^ (m << j)) & 0xFFFFFFFF
    return a


def _heads_to_lanes(x, n_heads, dh):
    xf = x.astype(F32)
    return jnp.concatenate([xf[:, h * dh:(h + 1) * dh].T for h in range(n_heads)], axis=1).astype(BF16)


def _dsa_kernel(qcap_ref, q_ref, iq_ref, k_ref, ik_ref, vt_ref, iwt_ref, o_ref,
                keys_ref, planes_ref, alive_ref, la_ref, lb_ref, acc_ref, sa_ref, sb_ref, thr_ref, stab_ref,
                *, seq, k_sel, idx_scale):
    kb_sz = DSA_KB
    n_blocks = seq // Q_BLOCK
    step = pl.program_id(1)
    has_cur = step < n_blocks
    has_prev = step >= 1
    cur = jnp.minimum(step, n_blocks - 1)
    prev = jnp.maximum(step - 1, 0)
    cur_buf = cur % 2
    prev_buf = prev % 2
    nkb = (cur * Q_BLOCK + Q_BLOCK + kb_sz - 1) // kb_sz
    nkb_prev = (prev * Q_BLOCK + Q_BLOCK + kb_sz - 1) // kb_sz
    lane = lax.broadcasted_iota(jnp.int32, (1, LANES), 1)
    limit = cur * Q_BLOCK + jnp.where(lane < CHUNK, CHUNK, 2 * CHUNK)
    row_iota = lax.broadcasted_iota(jnp.int32, (kb_sz, LANES), 0)

    iq_t = _heads_to_lanes(iq_ref[...], IDX_HEADS, IDX_HEAD_DIM)
    w = iwt_ref[...]

    def logits_of(blk):
        off = pl.multiple_of(blk * kb_sz, kb_sz)
        return _dot(ik_ref[pl.ds(off, kb_sz), :][:, :IDX_HEAD_DIM], iq_t)

    def keys_from(l_ref, blk, masked):
        off = pl.multiple_of(blk * kb_sz, kb_sz)
        sc = jnp.zeros((kb_sz, LANES), F32)
        for h in range(IDX_HEADS):
            sc = sc + jnp.maximum(l_ref[:, h * LANES:(h + 1) * LANES], 0.0) * w[h:h + 1, :]
        sc = sc * idx_scale
        bits = pltpu.bitcast(sc, jnp.int32)
        key = bits ^ ((bits >> 31) & 0x7FFFFFFF)
        key = jnp.where(sc == 0.0, 0, key)
        if masked:
            key = jnp.where(off + row_iota < limit, key, INT_MIN)
        keys_ref[cur_buf, pl.ds(off, kb_sz), :] = key
        ukey = key ^ INT_MIN
        for grp in range(kb_sz // BIT_GROUP):
            base = grp * BIT_GROUP
            planes = _bit_transpose([ukey[base + 8 * i:base + 8 * i + 8, :] for i in range(32)])
            row = pl.multiple_of((off + base) // 32, 8)
            for p in range(32):
                planes_ref[p, pl.ds(row, 8), :] = planes[p]

    n_pairs = (nkb + 1) // 2

    def score_pair(j, masked):
        lb_ref[...] = logits_of(2 * j + 1)
        keys_from(la_ref, 2 * j, masked)
        la_ref[...] = logits_of(jnp.minimum(2 * j + 2, 2 * n_pairs - 1))
        keys_from(lb_ref, 2 * j + 1, masked)

    def count(pred):
        def body(kb, c):
            off = pl.multiple_of(kb * kb_sz, kb_sz)
            ind = jnp.where(pred(keys_ref[cur_buf, pl.ds(off, kb_sz), :], off + row_iota), 1, 0)
            return c + jnp.sum(ind.reshape(kb_sz // 8, 8, LANES), axis=0)
        c8 = lax.fori_loop(0, nkb, body, jnp.zeros((8, LANES), jnp.int32))
        return jnp.sum(c8, axis=0, keepdims=True)

    def select_threshold():
        zeros = jnp.zeros((1, LANES), jnp.int32)
        ch_rows = SEARCH_CHUNK // 32
        n_ch = (nkb + 1) // 2
        zeros8 = jnp.zeros((8, LANES), jnp.int32)

        def ones_in(x):
            return jnp.sum(lax.population_count(x).reshape(ch_rows // 8, 8, LANES), axis=0)

        def over_chunks(body):
            return jnp.sum(lax.fori_loop(0, n_ch, lambda c, cnt: cnt + body(c), zeros8), axis=0, keepdims=True)

        def first_chunk(c):
            row = pl.multiple_of(c * ch_rows, ch_rows)
            alive_ref[pl.ds(row, ch_rows), :] = jnp.full((ch_rows, LANES), -1, jnp.int32)
            return ones_in(planes_ref[0, pl.ds(row, ch_rows), :])

        def decide(p, t_u, above, c1):
            take = above + c1 >= k_sel
            t_u = jnp.where(take, t_u | jnp.left_shift(jnp.int32(1), 31 - p), t_u)
            return t_u, jnp.where(take, above, above + c1), jnp.where(take, 0, -1)

        def radix_pass(p, carry):
            t_u, above, flip = carry

            def chunk(c):
                row = pl.multiple_of(c * ch_rows, ch_rows)
                alive = alive_ref[pl.ds(row, ch_rows), :] & (planes_ref[p - 1, pl.ds(row, ch_rows), :] ^ flip)
                alive_ref[pl.ds(row, ch_rows), :] = alive
                return ones_in(alive & planes_ref[p, pl.ds(row, ch_rows), :])

            return decide(p, t_u, above, over_chunks(chunk))

        carry = decide(0, zeros, zeros, over_chunks(first_chunk))
        t_u, above, flip = lax.fori_loop(1, 32, radix_pass, carry)

        def last_chunk(c):
            row = pl.multiple_of(c * ch_rows, ch_rows)
            return ones_in(alive_ref[pl.ds(row, ch_rows), :] & (planes_ref[31, pl.ds(row, ch_rows), :] ^ flip))

        c_eq = over_chunks(last_chunk)
        few = t_u == 0
        thr = jnp.maximum(t_u ^ INT_MIN, INT_MIN + 1)
        thr_ref[cur_buf] = thr
        need = k_sel - above
        excess = jnp.where(few, 0, c_eq - need)
        pos_bits = seq.bit_length() - 1

        @pl.when(jnp.max(excess) > 0)
        def _():
            def jbit(i, cut):
                cand = cut | jnp.left_shift(jnp.int32(1), pos_bits - 1 - i)
                c = count(lambda kk, pos: (kk == thr) & (pos < cand))
                return jnp.where(c <= need, cand, cut)
            cut = lax.fori_loop(0, pos_bits, jbit, zeros)
            cut = jnp.where(excess > 0, cut, seq)

            def drop_block(kb, _):
                off = pl.multiple_of(kb * kb_sz, kb_sz)
                kk = keys_ref[cur_buf, pl.ds(off, kb_sz), :]
                keys_ref[cur_buf, pl.ds(off, kb_sz), :] = jnp.where(
                    (kk == thr) & (off + row_iota >= cut), INT_MIN, kk)
                return 0

            lax.fori_loop(0, nkb, drop_block, 0)

    reps = ATT_HEADS // ATT_KV_HEADS
    width = reps * LANES
    q_t = _heads_to_lanes(q_ref[...], ATT_HEADS, ATT_HEAD_DIM)

    @pl.when(step == 0)
    def _():
        def kmax_block(kb, best):
            off = pl.multiple_of(kb * kb_sz, kb_sz)
            kf = k_ref[pl.ds(off, kb_sz), :].astype(F32)
            n2 = [jnp.max(jnp.sum(jnp.square(kf[:, g * K_EXT:g * K_EXT + ATT_HEAD_DIM]), axis=1, keepdims=True))
                  for g in range(ATT_KV_HEADS)]
            return tuple(jnp.maximum(b_, n_) for b_, n_ in zip(best, n2))
        best = lax.fori_loop(0, seq // kb_sz, kmax_block, (jnp.float32(0.0),) * ATT_KV_HEADS)
        worst = jnp.float32(0.0)
        for g in range(ATT_KV_HEADS):
            stab_ref[g] = jnp.sqrt(best[g]) * BOUND_SLACK
            worst = jnp.maximum(worst, stab_ref[g] * qcap_ref[0])
        stab_ref[ATT_KV_HEADS] = jnp.where(worst < BOUND_LIMIT, 1.0, 0.0)

    bounded = stab_ref[ATT_KV_HEADS] > 0.5
    qf = q_t.astype(F32)
    q_len = jnp.sqrt(jnp.sum(qf * qf, axis=0, keepdims=True))
    rhs = []
    for g in range(ATT_KV_HEADS):
        cols = slice(g * width, (g + 1) * width)
        neg_bound = jnp.where(bounded, -q_len[:, cols] * stab_ref[g], 0.0)
        rhs.append(jnp.concatenate(
            [qf[:, cols], neg_bound, jnp.zeros((K_EXT - ATT_HEAD_DIM - 1, width), F32)], axis=0).astype(BF16))

    ka = DSA_KA

    def masked_scores(blk):
        off = pl.multiple_of(blk * ka, ka)
        bias = jnp.where(keys_ref[prev_buf, pl.ds(off, ka), :] >= thr_prev, 0.0, NEG_BIG)
        bias = jnp.concatenate([bias] * reps, axis=1)
        kblk = k_ref[pl.ds(off, ka), :]
        return jnp.concatenate(
            [_dot(kblk[:, g * K_EXT:(g + 1) * K_EXT], rhs[g]) + bias for g in range(ATT_KV_HEADS)],
            axis=1)

    def consume_bounded(s_ref, blk):
        off = pl.multiple_of(blk * ka, ka)
        p = jnp.exp2(s_ref[...]).astype(BF16)
        for g in range(ATT_KV_HEADS):
            acc_ref[g] += _dot(vt_ref[g, :, pl.ds(off, ka)], p[:, g * width:(g + 1) * width])

    def consume_online(s_ref, blk, m_old):
        off = pl.multiple_of(blk * ka, ka)
        s = s_ref[...]
        m_new = jnp.maximum(m_old, jnp.max(s, axis=0, keepdims=True))
        alpha = jnp.exp2(m_old - m_new)
        p = jnp.exp2(s - m_new).astype(BF16)
        for g in range(ATT_KV_HEADS):
            vt = vt_ref[g, :, pl.ds(off, ka)]
            acc_ref[g] = (alpha[:, g * width:(g + 1) * width] * acc_ref[g]
                          + _dot(vt, p[:, g * width:(g + 1) * width]))
        return m_new

    n_sub = nkb_prev * (kb_sz // ka)
    thr_prev = thr_ref[prev_buf]

    def att_pair_bounded(j):
        sb_ref[...] = masked_scores(2 * j + 1)
        consume_bounded(sa_ref, 2 * j)
        sa_ref[...] = masked_scores(jnp.minimum(2 * j + 2, n_sub - 1))
        consume_bounded(sb_ref, 2 * j + 1)

    @pl.when(has_cur)
    def _():
        la_ref[...] = logits_of(0)

    @pl.when(has_prev)
    def _():
        acc_ref[...] = jnp.zeros_like(acc_ref)
        sa_ref[...] = masked_scores(0)

    n_fused = jnp.where(has_cur & has_prev & bounded, n_pairs - 1, 0)

    def fused(j, c):
        score_pair(j, False)
        att_pair_bounded(2 * j)
        att_pair_bounded(2 * j + 1)
        return c

    lax.fori_loop(0, n_fused, fused, 0)

    @pl.when(has_cur)
    def _():
        lax.fori_loop(n_fused, n_pairs - 1, lambda j, c: (score_pair(j, False), c)[1], 0)
        score_pair(n_pairs - 1, True)
        select_threshold()

    @pl.when(has_prev & bounded)
    def _():
        lax.fori_loop(2 * n_fused, n_sub // 2, lambda j, c: (att_pair_bounded(j), c)[1], 0)

    @pl.when(has_prev & jnp.logical_not(bounded))
    def _():
        def att_pair(j, m_run):
            sb_ref[...] = masked_scores(2 * j + 1)
            m_run = consume_online(sa_ref, 2 * j, m_run)
            sa_ref[...] = masked_scores(jnp.minimum(2 * j + 2, n_sub - 1))
            return consume_online(sb_ref, 2 * j + 1, m_run)
        lax.fori_loop(0, n_sub // 2, att_pair, jnp.full((1, ATT_HEADS * LANES), NEG_BIG, F32))

    @pl.when(has_prev)
    def _():
        pieces = []
        for g in range(ATT_KV_HEADS):
            a = acc_ref[g]
            o = a[0:ATT_HEAD_DIM, :] / a[ATT_HEAD_DIM:ATT_HEAD_DIM + 1, :]
            pieces += [o[:, r * LANES:(r + 1) * LANES] for r in range(reps)]
        o_ref[...] = jnp.concatenate(pieces, axis=0).T.astype(o_ref.dtype)


def _v_transposed(v):
    b, s, _ = v.shape
    vt = jnp.swapaxes(v.reshape(b, s, ATT_KV_HEADS, ATT_HEAD_DIM), 1, 3)
    vt = jnp.swapaxes(vt, 1, 2)
    ones = jnp.ones((b, ATT_KV_HEADS, 1, s), v.dtype)
    pad = jnp.zeros((b, ATT_KV_HEADS, DSA_VROWS - ATT_HEAD_DIM - 1, s), v.dtype)
    return jnp.concatenate([vt, ones, pad], axis=2)


def _dsa(a3, vt, iwt, qcap):
    b, s, _ = a3.shape
    nb = s // Q_BLOCK
    k_sel = min(TOPK_MAX, s // 4)
    idx_scale = (IDX_HEAD_DIM ** -0.5) * (IDX_HEADS ** -0.5)
    kern = functools.partial(_dsa_kernel, seq=s, k_sel=k_sel, idx_scale=idx_scale)
    return pl.pallas_call(
        kern,
        grid=(b, nb + 1),
        in_specs=[
            pl.BlockSpec(memory_space=pltpu.SMEM),
            pl.BlockSpec((None, Q_BLOCK, 512), lambda bb, i: (bb, jnp.maximum(i - 1, 0), COL_Q // 512)),
            pl.BlockSpec((None, Q_BLOCK, 512), lambda bb, i: (bb, jnp.minimum(i, nb - 1), COL_IQ // 512)),
            pl.BlockSpec((None, s, ATT_KV_HEADS * K_EXT), lambda bb, i: (bb, 0, COL_K // (ATT_KV_HEADS * K_EXT))),
            pl.BlockSpec((None, s, LANES), lambda bb, i: (bb, 0, COL_IK // LANES)),
            pl.BlockSpec((None, ATT_KV_HEADS, DSA_VROWS, s), lambda bb, i: (bb, 0, 0, 0)),
            pl.BlockSpec((None, None, IDX_HEADS, LANES), lambda bb, i: (bb, jnp.minimum(i, nb - 1), 0, 0)),
        ],
        out_specs=pl.BlockSpec((None, Q_BLOCK, MIX_A), lambda bb, i: (bb, jnp.maximum(i - 1, 0), 0)),
        out_shape=jax.ShapeDtypeStruct((b, s, MIX_A), BF16),
        scratch_shapes=[
            pltpu.VMEM((2, s, LANES), jnp.int32),
            pltpu.VMEM((32, s // 32, LANES), jnp.int32),
            pltpu.VMEM((s // 32, LANES), jnp.int32),
            pltpu.VMEM((DSA_KB, IDX_HEADS * LANES), F32),
            pltpu.VMEM((DSA_KB, IDX_HEADS * LANES), F32),
            pltpu.VMEM((ATT_KV_HEADS, DSA_VROWS, (ATT_HEADS // ATT_KV_HEADS) * LANES), F32),
            pltpu.VMEM((DSA_KA, ATT_HEADS * LANES), F32),
            pltpu.VMEM((DSA_KA, ATT_HEADS * LANES), F32),
            pltpu.VMEM((2, 1, LANES), jnp.int32),
            pltpu.SMEM((ATT_KV_HEADS + 1,), F32),
        ],
        compiler_params=pltpu.CompilerParams(
            dimension_semantics=("parallel", "arbitrary"), vmem_limit_bytes=VMEM_LIMIT),
        name="dsa",
    )(qcap, a3, a3, a3, a3, vt, iwt)


ML_L = 128
ML_TS = 512
ML_PAD = 8


def _mlstm_kernel(mx_ref, mv_ref, mo_ref, gr_ref, cw_ref, cb_ref, wq_ref, wkt_ref, brow_ref, hg_ref, o_ref,
                  xbuf_ref, q_ref, kt_ref, s_ref, m_ref):
    t_idx = pl.program_id(1)
    nh, dk, dv, L = MLSTM_HEADS, MLSTM_QK_DIM, MLSTM_V_DIM, ML_L

    @pl.when(t_idx == 0)
    def _():
        xbuf_ref[0:ML_PAD, :] = jnp.zeros((ML_PAD, MIX_B), F32)
        s_ref[...] = jnp.zeros_like(s_ref)
        m_ref[...] = jnp.zeros_like(m_ref)

    xbuf_ref[ML_PAD:ML_PAD + ML_TS, :] = mx_ref[...]
    xc = cb_ref[...] + jnp.zeros((ML_TS, MIX_B), F32)
    for j in range(CONV_WIDTH):
        s0 = ML_PAD - (CONV_WIDTH - 1) + j
        xc = xc + xbuf_ref[s0:s0 + ML_TS, :] * cw_ref[j:j + 1, :]
    xbuf_ref[0:ML_PAD, :] = mx_ref[ML_TS - ML_PAD:ML_TS, :]
    xc = (xc * jax.nn.sigmoid(xc)).astype(BF16)
    for h in range(nh):
        xh = xc[:, h * dv:(h + 1) * dv]
        q_ref[h] = (_dot(xh, wq_ref[h]) * (dk ** -0.5)).astype(BF16)
        kt_ref[h] = _dot_nt(wkt_ref[h], xh)

    gr = gr_ref[...]
    li_r = gr[0:nh, :] + brow_ref[0:nh, :]
    lf_r = jax.nn.log_sigmoid(gr[nh:2 * nh, :] + brow_ref[nh:2 * nh, :])

    ri = lax.broadcasted_iota(jnp.int32, (L, L), 0)
    ci = lax.broadcasted_iota(jnp.int32, (L, L), 1)
    causal = ri >= ci
    ones_v = jnp.ones((L, dv), BF16)
    ones_n = jnp.ones((dv, dv), BF16)

    tr = lax.broadcasted_iota(jnp.int32, (ML_TS, ML_TS), 0)
    tc = lax.broadcasted_iota(jnp.int32, (ML_TS, ML_TS), 1)
    chunk_triu = jnp.where((tr <= tc) & (tr // L == tc // L), 1.0, 0.0).astype(F32)
    b_all = jnp.dot(lf_r, chunk_triu, precision=lax.Precision.HIGHEST, preferred_element_type=F32)
    g_all = li_r - b_all
    in_chunk = lax.broadcasted_iota(jnp.int32, (nh, ML_TS), 1) % L
    run_all = g_all
    for sh in (1, 2, 4, 8, 16, 32, 64):
        run_all = jnp.maximum(run_all, jnp.where(in_chunk >= sh, pltpu.roll(run_all, sh, axis=1), NEG_BIG))

    def to_rows(row):
        return jnp.broadcast_to(row, (L, L)).T

    for c in range(ML_TS // L):
        r0 = c * L
        b_r, g_r, run = b_all[:, r0:r0 + L], g_all[:, r0:r0 + L], run_all[:, r0:r0 + L]
        run_rows = [to_rows(run[h:h + 1, :]) for h in range(nh)]
        b_rows = [to_rows(b_r[h:h + 1, :]) for h in range(nh)]
        m_prev = m_ref[...]
        b_last = b_r[:, L - 1:L]
        wk_r = b_last + g_r
        m_new = jnp.maximum(b_last + m_prev, jnp.max(wk_r, axis=1, keepdims=True))
        ws_r = jnp.exp(wk_r - m_new)
        wc_r = jnp.exp(b_last + m_prev - m_new)
        m_ref[...] = m_new
        outs = []
        for h in range(nh):
            big_m_t = jnp.maximum(run_rows[h], m_prev[h:h + 1, :])
            m_t = big_m_t + b_rows[h]
            decay = jnp.where(causal, jnp.exp(g_r[h:h + 1, :] - big_m_t), 0.0)
            w_int = jnp.exp(m_prev[h:h + 1, :] - big_m_t)
            qh = q_ref[h, r0:r0 + L, :]
            kt = kt_ref[h, :, r0:r0 + L]
            v_ext = jnp.concatenate(
                [mv_ref[r0:r0 + L, h * dv:(h + 1) * dv], ones_v], axis=1)
            wt = _dot(qh, kt.astype(BF16)) * decay
            nd = (_dot(wt.astype(BF16), v_ext)
                  + jnp.concatenate([w_int, w_int], axis=1) * _dot(qh, s_ref[h].astype(BF16)))
            hv = nd[:, 0:dv] / jnp.maximum(jnp.abs(nd[:, dv:2 * dv]), jnp.exp(-m_t))
            ms = _dot((hv * hv).astype(BF16), ones_n) * (1.0 / dv)
            outs.append(hv * lax.rsqrt(ms + EPS) * hg_ref[0:1, h * dv:(h + 1) * dv])
            wc = jnp.broadcast_to(wc_r[h:h + 1, :], (dk, L))
            s_ref[h] = (jnp.concatenate([wc, wc], axis=1) * s_ref[h]
                        + _dot((kt * ws_r[h:h + 1, :]).astype(BF16), v_ext))
        hh = jnp.concatenate(outs, axis=1)
        o_ref[r0:r0 + L, :] = (hh * jax.nn.sigmoid(mo_ref[r0:r0 + L, :].astype(F32))).astype(o_ref.dtype)


def _mlstm(m3, mvo3, gt, conv_w, conv_b, wq, wkt, brow, head_g):
    b, s, _ = m3.shape
    nh = MLSTM_HEADS
    tile = lambda c: pl.BlockSpec((None, ML_TS, MIX_B), lambda bb, t: (bb, t, c))
    const2 = lambda shape: pl.BlockSpec(shape, lambda bb, t: (0, 0))
    const3 = lambda shape: pl.BlockSpec(shape, lambda bb, t: (0, 0, 0))
    return pl.pallas_call(
        _mlstm_kernel,
        grid=(b, s // ML_TS),
        in_specs=[
            tile(0), tile(0), tile(1),
            pl.BlockSpec((None, 2 * nh, ML_TS), lambda bb, t: (bb, 0, t)),
            const2((CONV_WIDTH, MIX_B)), const2((1, MIX_B)),
            const3((nh, MLSTM_V_DIM, MLSTM_QK_DIM)), const3((nh, MLSTM_QK_DIM, MLSTM_V_DIM)),
            const2((2 * nh, 1)), const2((1, MIX_B)),
        ],
        out_specs=pl.BlockSpec((None, ML_TS, MIX_B), lambda bb, t: (bb, t, 0)),
        out_shape=jax.ShapeDtypeStruct((b, s, MIX_B), BF16),
        scratch_shapes=[
            pltpu.VMEM((ML_PAD + ML_TS, MIX_B), F32),
            pltpu.VMEM((nh, ML_TS, MLSTM_QK_DIM), BF16),
            pltpu.VMEM((nh, MLSTM_QK_DIM, ML_TS), F32),
            pltpu.VMEM((nh, MLSTM_QK_DIM, 2 * MLSTM_V_DIM), F32),
            pltpu.VMEM((nh, ML_L), F32),
        ],
        compiler_params=pltpu.CompilerParams(
            dimension_semantics=("parallel", "arbitrary"), vmem_limit_bytes=VMEM_LIMIT),
        name="mlstm",
    )(m3, mvo3, mvo3, gt, conv_w, conv_b, wq, wkt, brow, head_g)


MERGE_TM = 512


def _merge_ffn_kernel(x_ref, ya_ref, yb_ref, ga_ref, gb_ref, wa_ref, wb_ref, wo_ref,
                      g_ref, wg_ref, wu_ref, wd_ref, o_ref):
    merged = (jax.nn.sigmoid(ga_ref[...].astype(F32)) * _dot(ya_ref[...], wa_ref[...])
              + jax.nn.sigmoid(gb_ref[...].astype(F32)) * _dot(yb_ref[...], wb_ref[...]))
    x = x_ref[...] + _dot(merged.astype(BF16), wo_ref[...])
    _half_ffn(x, g_ref, wg_ref, wu_ref, wd_ref, o_ref)


def _merge_ffn(x2d, ya, yb, gates, wa, wb, wo, g, wg, wu, wd):
    m = x2d.shape[0]
    row = lambda i: (i, 0)
    const = lambda i: (0, 0)
    resident = lambda shape: pl.BlockSpec(shape, const, pipeline_mode=pl.Buffered(1))
    return pl.pallas_call(
        _merge_ffn_kernel,
        grid=(m // MERGE_TM,),
        in_specs=[
            pl.BlockSpec((MERGE_TM, D_MODEL), row),
            pl.BlockSpec((MERGE_TM, MIX_A), row),
            pl.BlockSpec((MERGE_TM, MIX_B), row),
            pl.BlockSpec((MERGE_TM, D_MODEL), lambda i: (i, 0)),
            pl.BlockSpec((MERGE_TM, D_MODEL), lambda i: (i, 1)),
            resident((MIX_A, D_MODEL)),
            resident((MIX_B, D_MODEL)),
            resident((D_MODEL, D_MODEL)),
            pl.BlockSpec((1, D_MODEL), const),
            resident((D_MODEL, D_FF)),
            resident((D_MODEL, D_FF)),
            resident((D_FF, D_MODEL)),
        ],
        out_specs=pl.BlockSpec((MERGE_TM, D_MODEL), row),
        out_shape=jax.ShapeDtypeStruct((m, D_MODEL), F32),
        compiler_params=pltpu.CompilerParams(
            dimension_semantics=("parallel",), vmem_limit_bytes=VMEM_LIMIT),
        name="merge_ffn",
    )(x2d, ya, yb, gates, gates, wa, wb, wo, g.reshape(1, D_MODEL), wg, wu, wd)


def _block_diag_ones(n, blk):
    r = jnp.arange(n) // blk
    return (r[:, None] == r[None, :]).astype(BF16)


def _arrange_w_in(w_in):
    widths = (512, 128, 128, 512, 64, 8, 512, 512, 4, 4, 512, 1024, 1024)
    parts, off = [], 0
    for wd in widths:
        parts.append(w_in[:, off:off + wd])
        off += wd
    aq, ak, av, iq, ik, iw, mx, mv, mi, mf, mo, ga, gb = parts
    z = lambda n: jnp.zeros((w_in.shape[0], n), w_in.dtype)
    cols = [aq, iq, ak[:, :64], z(64), ak[:, 64:], z(64), ik, z(64), av,
            iw, mi, mf, z(G_WIDTH - 16),
            mx, mv, mo, ga, gb]
    return jnp.concatenate(cols, axis=1).astype(BF16)


def kernel(x, ffn1_norm, ffn1_w_gate, ffn1_w_up, ffn1_w_down, mix_norm, w_in, q_norm, k_norm,
           idx_k_norm, conv_w, conv_b, w_mq, w_mk, b_i, b_f, m_head_norm, w_proj_a, w_proj_b,
           w_out, ffn2_norm, ffn2_w_gate, ffn2_w_up, ffn2_w_down):
    b, s, _ = x.shape
    m = b * s
    nb = s // Q_BLOCK
    nh = MLSTM_HEADS
    x2d = x.reshape(m, D_MODEL)
    for l in range(ffn1_norm.shape[0]):
        x2d = _ffn(x2d, ffn1_norm[l], ffn1_w_gate[l].astype(BF16), ffn1_w_up[l].astype(BF16),
                   ffn1_w_down[l].astype(BF16))

        gq = (jnp.tile(q_norm[l], ATT_HEADS) * (ATT_HEAD_DIM ** -0.5 * LOG2_E)).reshape(1, 512)
        z64 = jnp.zeros((ATT_HEAD_DIM,), F32)
        one64 = z64.at[0].set(1.0)
        gk = jnp.concatenate([k_norm[l], z64] * ATT_KV_HEADS + [idx_k_norm[l], z64]).reshape(1, -1)
        kone = jnp.concatenate([z64, one64] * ATT_KV_HEADS + [z64, z64]).reshape(1, -1)
        a2d, g2d, mx2d, mvo2d, t2d = _proj(x2d, mix_norm[l], _arrange_w_in(w_in[l]),
                                   _block_diag_ones(512, ATT_HEAD_DIM),
                                   _block_diag_ones(COL_V - COL_K, ATT_HEAD_DIM), gq, gk, kone)

        a3 = a2d.reshape(b, s, A_WIDTH)
        vt = _v_transposed(a3[:, :, COL_V:COL_V + ATT_KV_HEADS * ATT_HEAD_DIM])
        g3 = g2d.reshape(b, s, G_WIDTH)
        iwt = jnp.swapaxes(g3[:, :, 0:IDX_HEADS].reshape(b, nb, Q_BLOCK, IDX_HEADS), 2, 3)
        qcap = (jnp.max(jnp.abs(gq)) * (ATT_HEAD_DIM ** 0.5) * BOUND_SLACK).reshape(1)
        ya = _dsa(a3, vt, iwt, qcap)

        gt = jnp.swapaxes(g3[:, :, IDX_HEADS:IDX_HEADS + 2 * nh], 1, 2)
        bias = jnp.concatenate([b_i[l], b_f[l]])
        yb = _mlstm(mx2d.reshape(b, s, MIX_B), mvo2d.reshape(b, s, 2 * MIX_B), gt,
                    conv_w[l], conv_b[l].reshape(1, MIX_B),
                    w_mq[l].astype(BF16), jnp.swapaxes(w_mk[l], 1, 2).astype(BF16),
                    bias.reshape(2 * nh, 1), m_head_norm[l].reshape(1, MIX_B))

        x2d = _merge_ffn(x2d, ya.reshape(m, MIX_A), yb.reshape(m, MIX_B), t2d,
                         w_proj_a[l].astype(BF16), w_proj_b[l].astype(BF16), w_out[l].astype(BF16),
                         ffn2_norm[l], ffn2_w_gate[l].astype(BF16), ffn2_w_up[l].astype(BF16),
                         ffn2_w_down[l].astype(BF16))
    return x2d.reshape(b, s, D_MODEL)
```

```python
import functools

import jax
import jax.numpy as jnp
from jax import lax
from jax.experimental import pallas as pl
from jax.experimental.pallas import tpu as pltpu

F32 = jnp.float32
BF16 = jnp.bfloat16

D_MODEL = 1024
D_FF = 2816
EPS = 1e-6
ATT_HEADS = 8
ATT_KV_HEADS = 2
ATT_HEAD_DIM = 64
IDX_HEADS = 8
IDX_HEAD_DIM = 64
TOPK_MAX = 256
Q_BLOCK = 128
CHUNK = 64
MLSTM_HEADS = 4
MLSTM_QK_DIM = 64
MLSTM_V_DIM = 128
CONV_WIDTH = 4
MIX_A = ATT_HEADS * ATT_HEAD_DIM
MIX_B = MLSTM_HEADS * MLSTM_V_DIM

LANES = 128
VMEM_LIMIT = 56 * 1024 * 1024
INT_MIN = -2147483648
NEG_BIG = -1e30
LOG2_E = 1.4426950408889634

COL_Q, COL_IQ, COL_K, COL_IK, COL_V = 0, 512, 1024, 1280, 1408
K_EXT = 128
A_WIDTH = 1536
G_WIDTH = 128
M_WIDTH = 3 * MIX_B
T_WIDTH = 2 * D_MODEL
W_TOTAL = A_WIDTH + G_WIDTH + M_WIDTH + T_WIDTH


def _rms(x, g):
    return x * lax.rsqrt(jnp.mean(x * x, axis=-1, keepdims=True) + EPS) * g


def _dot(a, b):
    return jnp.dot(a, b, preferred_element_type=F32)


def _dot_nt(a, b):
    return lax.dot_general(a, b, (((1,), (1,)), ((), ())), preferred_element_type=F32)


def _dot_tn(a, b):
    return lax.dot_general(a, b, (((0,), (0,)), ((), ())), preferred_element_type=F32)


FFN_TM = 512
FFN_TF = 1408


def _ffn_kernel(x_ref, g_ref, wg_ref, wu_ref, wd_ref, o_ref):
    _half_ffn(x_ref[...], g_ref, wg_ref, wu_ref, wd_ref, o_ref)


def _half_ffn(x, g_ref, wg_ref, wu_ref, wd_ref, o_ref):
    h = _rms(x, g_ref[...]).astype(BF16)
    acc = None
    for c in range(D_FF // FFN_TF):
        cols = slice(c * FFN_TF, (c + 1) * FFN_TF)
        a = _dot(h, wg_ref[:, cols])
        u = _dot(h, wu_ref[:, cols])
        act = (a * jax.nn.sigmoid(a) * u).astype(BF16)
        part = _dot(act, wd_ref[cols, :])
        acc = part if acc is None else acc + part
    o_ref[...] = x + 0.5 * acc


def _ffn(x2d, g, wg, wu, wd):
    m = x2d.shape[0]
    const = lambda i: (0, 0)
    resident = lambda shape: pl.BlockSpec(shape, const, pipeline_mode=pl.Buffered(1))
    return pl.pallas_call(
        _ffn_kernel,
        grid=(m // FFN_TM,),
        in_specs=[
            pl.BlockSpec((FFN_TM, D_MODEL), lambda i: (i, 0)),
            pl.BlockSpec((1, D_MODEL), const),
            resident((D_MODEL, D_FF)),
            resident((D_MODEL, D_FF)),
            resident((D_FF, D_MODEL)),
        ],
        out_specs=pl.BlockSpec((FFN_TM, D_MODEL), lambda i: (i, 0)),
        out_shape=jax.ShapeDtypeStruct((m, D_MODEL), F32),
        compiler_params=pltpu.CompilerParams(
            dimension_semantics=("parallel",), vmem_limit_bytes=VMEM_LIMIT),
        name="ffn",
    )(x2d, g.reshape(1, D_MODEL), wg, wu, wd)


PROJ_TM = 512


def _proj_kernel(x_ref, g_ref, w_ref, bdq_ref, bdk_ref, gq_ref, gk_ref, kone_ref,
                 a_ref, g_out_ref, mx_ref, mvo_ref, t_ref):
    h = _rms(x_ref[...], g_ref[...]).astype(BF16)

    def head_norm(p, bd_ref, gain_ref):
        ss = _dot((p * p).astype(BF16), bd_ref[...]) * (1.0 / ATT_HEAD_DIM)
        return p * lax.rsqrt(ss + EPS) * gain_ref[...]

    pq = _dot(h, w_ref[:, COL_Q:COL_IQ])
    a_ref[:, COL_Q:COL_IQ] = head_norm(pq, bdq_ref, gq_ref).astype(BF16)
    a_ref[:, COL_IQ:COL_K] = _dot(h, w_ref[:, COL_IQ:COL_K]).astype(BF16)
    pk = _dot(h, w_ref[:, COL_K:COL_V])
    a_ref[:, COL_K:COL_V] = (head_norm(pk, bdk_ref, gk_ref) + kone_ref[...]).astype(BF16)
    a_ref[:, COL_V:A_WIDTH] = _dot(h, w_ref[:, COL_V:A_WIDTH]).astype(BF16)
    o = A_WIDTH
    g_out_ref[...] = _dot(h, w_ref[:, o:o + G_WIDTH])
    o += G_WIDTH
    mx_ref[...] = _dot(h, w_ref[:, o:o + MIX_B])
    o += MIX_B
    for c in range(2):
        mvo_ref[:, c * MIX_B:(c + 1) * MIX_B] = _dot(h, w_ref[:, o + c * MIX_B:o + (c + 1) * MIX_B]).astype(BF16)
    o += 2 * MIX_B
    for c in range(2):
        t_ref[:, c * D_MODEL:(c + 1) * D_MODEL] = _dot(
            h, w_ref[:, o + c * D_MODEL:o + (c + 1) * D_MODEL]).astype(BF16)


def _proj(x2d, g, w_all, bdq, bdk, gq, gk, kone):
    m = x2d.shape[0]
    const = lambda i: (0, 0)
    row = lambda i: (i, 0)
    return pl.pallas_call(
        _proj_kernel,
        grid=(m // PROJ_TM,),
        in_specs=[
            pl.BlockSpec((PROJ_TM, D_MODEL), row),
            pl.BlockSpec((1, D_MODEL), const),
            pl.BlockSpec((D_MODEL, W_TOTAL), const, pipeline_mode=pl.Buffered(1)),
            pl.BlockSpec((512, 512), const),
            pl.BlockSpec((COL_V - COL_K, COL_V - COL_K), const),
            pl.BlockSpec((1, 512), const),
            pl.BlockSpec((1, COL_V - COL_K), const),
            pl.BlockSpec((1, COL_V - COL_K), const),
        ],
        out_specs=[
            pl.BlockSpec((PROJ_TM, A_WIDTH), row),
            pl.BlockSpec((PROJ_TM, G_WIDTH), row),
            pl.BlockSpec((PROJ_TM, MIX_B), row),
            pl.BlockSpec((PROJ_TM, 2 * MIX_B), row),
            pl.BlockSpec((PROJ_TM, T_WIDTH), row),
        ],
        out_shape=[
            jax.ShapeDtypeStruct((m, A_WIDTH), BF16),
            jax.ShapeDtypeStruct((m, G_WIDTH), F32),
            jax.ShapeDtypeStruct((m, MIX_B), F32),
            jax.ShapeDtypeStruct((m, 2 * MIX_B), BF16),
            jax.ShapeDtypeStruct((m, T_WIDTH), BF16),
        ],
        compiler_params=pltpu.CompilerParams(
            dimension_semantics=("parallel",), vmem_limit_bytes=VMEM_LIMIT),
        name="proj",
    )(x2d, g.reshape(1, D_MODEL), w_all, bdq, bdk, gq, gk, kone)


DSA_KB = 512
DSA_KA = 256
BOUND_LIMIT = 60.0
BOUND_SLACK = 1.01
DSA_VROWS = 80

BIT_GROUP = 256
SEARCH_CHUNK = 1024


def _bit_transpose(words):
    a = list(words)
    j, m = 16, 0x0000FFFF
    while j:
        k = 0
        while k < 32:
            t = (a[k] ^ (a[k + j] >> j)) & m
            a[k] = a[k] ^ t
            a[k + j] = a[k + j] ^ (t << j)
            k = (k + j + 1) & ~j
        j >>= 1
        m = (m ^ (m << j)) & 0xFFFFFFFF
    return a


def _heads_to_lanes(x, n_heads, dh):
    xf = x.astype(F32)
    return jnp.concatenate([xf[:, h * dh:(h + 1) * dh].T for h in range(n_heads)], axis=1).astype(BF16)


def _dsa_kernel(qcap_ref, q_ref, iq_ref, k_ref, ik_ref, vt_ref, iwt_ref, o_ref,
                keys_ref, planes_ref, alive_ref, la_ref, lb_ref, acc_ref, sa_ref, sb_ref, thr_ref, stab_ref,
                *, seq, k_sel, idx_scale):
    kb_sz = DSA_KB
    n_blocks = seq // Q_BLOCK
    step = pl.program_id(1)
    has_cur = step < n_blocks
    has_prev = step >= 1
    cur = jnp.minimum(step, n_blocks - 1)
    prev = jnp.maximum(step - 1, 0)
    cur_buf = cur % 2
    prev_buf = prev % 2
    nkb = (cur * Q_BLOCK + Q_BLOCK + kb_sz - 1) // kb_sz
    nkb_prev = (prev * Q_BLOCK + Q_BLOCK + kb_sz - 1) // kb_sz
    lane = lax.broadcasted_iota(jnp.int32, (1, LANES), 1)
    limit = cur * Q_BLOCK + jnp.where(lane < CHUNK, CHUNK, 2 * CHUNK)
    row_iota = lax.broadcasted_iota(jnp.int32, (kb_sz, LANES), 0)

    iq_t = _heads_to_lanes(iq_ref[...], IDX_HEADS, IDX_HEAD_DIM)
    w = iwt_ref[...]

    def logits_of(blk):
        off = pl.multiple_of(blk * kb_sz, kb_sz)
        return _dot(ik_ref[pl.ds(off, kb_sz), :][:, :IDX_HEAD_DIM], iq_t)

    def keys_from(l_ref, blk, masked):
        off = pl.multiple_of(blk * kb_sz, kb_sz)
        sc = jnp.zeros((kb_sz, LANES), F32)
        for h in range(IDX_HEADS):
            sc = sc + jnp.maximum(l_ref[:, h * LANES:(h + 1) * LANES], 0.0) * w[h:h + 1, :]
        sc = sc * idx_scale
        bits = pltpu.bitcast(sc, jnp.int32)
        key = bits ^ ((bits >> 31) & 0x7FFFFFFF)
        key = jnp.where(sc == 0.0, 0, key)
        if masked:
            key = jnp.where(off + row_iota < limit, key, INT_MIN)
        keys_ref[cur_buf, pl.ds(off, kb_sz), :] = key
        ukey = key ^ INT_MIN
        for grp in range(kb_sz // BIT_GROUP):
            base = grp * BIT_GROUP
            planes = _bit_transpose([ukey[base + 8 * i:base + 8 * i + 8, :] for i in range(32)])
            row = pl.multiple_of((off + base) // 32, 8)
            for p in range(32):
                planes_ref[p, pl.ds(row, 8), :] = planes[p]

    n_pairs = (nkb + 1) // 2

    def score_pair(j, masked):
        lb_ref[...] = logits_of(2 * j + 1)
        keys_from(la_ref, 2 * j, masked)
        la_ref[...] = logits_of(jnp.minimum(2 * j + 2, 2 * n_pairs - 1))
        keys_from(lb_ref, 2 * j + 1, masked)

    def count(pred):
        def body(kb, c):
            off = pl.multiple_of(kb * kb_sz, kb_sz)
            ind = jnp.where(pred(keys_ref[cur_buf, pl.ds(off, kb_sz), :], off + row_iota), 1, 0)
            return c + jnp.sum(ind.reshape(kb_sz // 8, 8, LANES), axis=0)
        c8 = lax.fori_loop(0, nkb, body, jnp.zeros((8, LANES), jnp.int32))
        return jnp.sum(c8, axis=0, keepdims=True)

    def select_threshold():
        zeros = jnp.zeros((1, LANES), jnp.int32)
        ch_rows = SEARCH_CHUNK // 32
        n_ch = (nkb + 1) // 2
        zeros8 = jnp.zeros((8, LANES), jnp.int32)

        def ones_in(x):
            return jnp.sum(lax.population_count(x).reshape(ch_rows // 8, 8, LANES), axis=0)

        def over_chunks(body, n_out):
            tot = lax.fori_loop(0, n_ch, lambda c, cnt: tuple(a + b_ for a, b_ in zip(cnt, body(c))),
                                (zeros8,) * n_out)
            return [jnp.sum(t, axis=0, keepdims=True) for t in tot]

        def plane(p, row):
            return planes_ref[p, pl.ds(row, ch_rows), :]

        def digit_counts(alive, p_hi, p_lo):
            x1 = alive & p_hi
            return ones_in(x1), ones_in(x1 & p_lo), ones_in((alive ^ x1) & p_lo)

        def decide(q, t_u, above, counts):
            c1, c11, c01 = counts
            hi = above + c1 >= k_sel
            lo = jnp.where(hi, above + c11, above + c1 + c01) >= k_sel
            above = jnp.where(hi, jnp.where(lo, above, above + c11),
                              jnp.where(lo, above + c1, above + c1 + c01))
            t_u = (t_u | jnp.where(hi, jnp.left_shift(jnp.int32(1), 31 - 2 * q), 0)
                   | jnp.where(lo, jnp.left_shift(jnp.int32(1), 30 - 2 * q), 0))
            return t_u, above, jnp.where(hi, 0, -1), jnp.where(lo, 0, -1)

        def first_chunk(c):
            row = pl.multiple_of(c * ch_rows, ch_rows)
            everything = jnp.full((ch_rows, LANES), -1, jnp.int32)
            alive_ref[pl.ds(row, ch_rows), :] = everything
            return digit_counts(everything, plane(0, row), plane(1, row))

        def radix_pass(q, carry):
            t_u, above, flip_hi, flip_lo = carry

            def chunk(c):
                row = pl.multiple_of(c * ch_rows, ch_rows)
                alive = (alive_ref[pl.ds(row, ch_rows), :]
                         & (plane(2 * q - 2, row) ^ flip_hi) & (plane(2 * q - 1, row) ^ flip_lo))
                alive_ref[pl.ds(row, ch_rows), :] = alive
                return digit_counts(alive, plane(2 * q, row), plane(2 * q + 1, row))

            return decide(q, t_u, above, over_chunks(chunk, 3))

        carry = decide(0, zeros, zeros, over_chunks(first_chunk, 3))
        t_u, above, flip_hi, flip_lo = lax.fori_loop(1, 16, radix_pass, carry)

        def last_chunk(c):
            row = pl.multiple_of(c * ch_rows, ch_rows)
            return (ones_in(alive_ref[pl.ds(row, ch_rows), :]
                            & (plane(30, row) ^ flip_hi) & (plane(31, row) ^ flip_lo)),)

        c_eq = over_chunks(last_chunk, 1)[0]
        few = t_u == 0
        thr = jnp.maximum(t_u ^ INT_MIN, INT_MIN + 1)
        thr_ref[cur_buf] = thr
        need = k_sel - above
        excess = jnp.where(few, 0, c_eq - need)
        pos_bits = seq.bit_length() - 1

        @pl.when(jnp.max(excess) > 0)
        def _():
            def jbit(i, cut):
                cand = cut | jnp.left_shift(jnp.int32(1), pos_bits - 1 - i)
                c = count(lambda kk, pos: (kk == thr) & (pos < cand))
                return jnp.where(c <= need, cand, cut)
            cut = lax.fori_loop(0, pos_bits, jbit, zeros)
            cut = jnp.where(excess > 0, cut, seq)

            def drop_block(kb, _):
                off = pl.multiple_of(kb * kb_sz, kb_sz)
                kk = keys_ref[cur_buf, pl.ds(off, kb_sz), :]
                keys_ref[cur_buf, pl.ds(off, kb_sz), :] = jnp.where(
                    (kk == thr) & (off + row_iota >= cut), INT_MIN, kk)
                return 0

            lax.fori_loop(0, nkb, drop_block, 0)

    reps = ATT_HEADS // ATT_KV_HEADS
    width = reps * LANES
    q_t = _heads_to_lanes(q_ref[...], ATT_HEADS, ATT_HEAD_DIM)

    @pl.when(step == 0)
    def _():
        def kmax_block(kb, best):
            off = pl.multiple_of(kb * kb_sz, kb_sz)
            kf = k_ref[pl.ds(off, kb_sz), :].astype(F32)
            n2 = [jnp.max(jnp.sum(jnp.square(kf[:, g * K_EXT:g * K_EXT + ATT_HEAD_DIM]), axis=1, keepdims=True))
                  for g in range(ATT_KV_HEADS)]
            return tuple(jnp.maximum(b_, n_) for b_, n_ in zip(best, n2))
        best = lax.fori_loop(0, seq // kb_sz, kmax_block, (jnp.float32(0.0),) * ATT_KV_HEADS)
        worst = jnp.float32(0.0)
        for g in range(ATT_KV_HEADS):
            stab_ref[g] = jnp.sqrt(best[g]) * BOUND_SLACK
            worst = jnp.maximum(worst, stab_ref[g] * qcap_ref[0])
        stab_ref[ATT_KV_HEADS] = jnp.where(worst < BOUND_LIMIT, 1.0, 0.0)

    bounded = stab_ref[ATT_KV_HEADS] > 0.5
    qf = q_t.astype(F32)
    q_len = jnp.sqrt(jnp.sum(qf * qf, axis=0, keepdims=True))
    rhs = []
    for g in range(ATT_KV_HEADS):
        cols = slice(g * width, (g + 1) * width)
        neg_bound = jnp.where(bounded, -q_len[:, cols] * stab_ref[g], 0.0)
        rhs.append(jnp.concatenate(
            [qf[:, cols], neg_bound, jnp.zeros((K_EXT - ATT_HEAD_DIM - 1, width), F32)], axis=0).astype(BF16))

    ka = DSA_KA

    def masked_scores(blk):
        off = pl.multiple_of(blk * ka, ka)
        bias = jnp.where(keys_ref[prev_buf, pl.ds(off, ka), :] >= thr_prev, 0.0, NEG_BIG)
        bias = jnp.concatenate([bias] * reps, axis=1)
        kblk = k_ref[pl.ds(off, ka), :]
        return jnp.concatenate(
            [_dot(kblk[:, g * K_EXT:(g + 1) * K_EXT], rhs[g]) + bias for g in range(ATT_KV_HEADS)],
            axis=1)

    def consume_bounded(s_ref, blk):
        off = pl.multiple_of(blk * ka, ka)
        p = jnp.exp2(s_ref[...]).astype(BF16)
        for g in range(ATT_KV_HEADS):
            acc_ref[g] += _dot(vt_ref[g, :, pl.ds(off, ka)], p[:, g * width:(g + 1) * width])

    def consume_online(s_ref, blk, m_old):
        off = pl.multiple_of(blk * ka, ka)
        s = s_ref[...]
        m_new = jnp.maximum(m_old, jnp.max(s, axis=0, keepdims=True))
        alpha = jnp.exp2(m_old - m_new)
        p = jnp.exp2(s - m_new).astype(BF16)
        for g in range(ATT_KV_HEADS):
            vt = vt_ref[g, :, pl.ds(off, ka)]
            acc_ref[g] = (alpha[:, g * width:(g + 1) * width] * acc_ref[g]
                          + _dot(vt, p[:, g * width:(g + 1) * width]))
        return m_new

    n_sub = nkb_prev * (kb_sz // ka)
    thr_prev = thr_ref[prev_buf]

    def att_pair_bounded(j):
        sb_ref[...] = masked_scores(2 * j + 1)
        consume_bounded(sa_ref, 2 * j)
        sa_ref[...] = masked_scores(jnp.minimum(2 * j + 2, n_sub - 1))
        consume_bounded(sb_ref, 2 * j + 1)

    @pl.when(has_cur)
    def _():
        la_ref[...] = logits_of(0)

    @pl.when(has_prev)
    def _():
        acc_ref[...] = jnp.zeros_like(acc_ref)
        sa_ref[...] = masked_scores(0)

    n_fused = jnp.where(has_cur & has_prev & bounded, n_pairs - 1, 0)

    def fused(j, c):
        score_pair(j, False)
        att_pair_bounded(2 * j)
        att_pair_bounded(2 * j + 1)
        return c

    lax.fori_loop(0, n_fused, fused, 0)

    @pl.when(has_cur)
    def _():
        lax.fori_loop(n_fused, n_pairs - 1, lambda j, c: (score_pair(j, False), c)[1], 0)
        score_pair(n_pairs - 1, True)
        select_threshold()

    @pl.when(has_prev & bounded)
    def _():
        lax.fori_loop(2 * n_fused, n_sub // 2, lambda j, c: (att_pair_bounded(j), c)[1], 0)

    @pl.when(has_prev & jnp.logical_not(bounded))
    def _():
        def att_pair(j, m_run):
            sb_ref[...] = masked_scores(2 * j + 1)
            m_run = consume_online(sa_ref, 2 * j, m_run)
            sa_ref[...] = masked_scores(jnp.minimum(2 * j + 2, n_sub - 1))
            return consume_online(sb_ref, 2 * j + 1, m_run)
        lax.fori_loop(0, n_sub // 2, att_pair, jnp.full((1, ATT_HEADS * LANES), NEG_BIG, F32))

    @pl.when(has_prev)
    def _():
        pieces = []
        for g in range(ATT_KV_HEADS):
            a = acc_ref[g]
            o = a[0:ATT_HEAD_DIM, :] / a[ATT_HEAD_DIM:ATT_HEAD_DIM + 1, :]
            pieces += [o[:, r * LANES:(r + 1) * LANES] for r in range(reps)]
        o_ref[...] = jnp.concatenate(pieces, axis=0).T.astype(o_ref.dtype)


def _v_transposed(v):
    b, s, _ = v.shape
    vt = jnp.swapaxes(v.reshape(b, s, ATT_KV_HEADS, ATT_HEAD_DIM), 1, 3)
    vt = jnp.swapaxes(vt, 1, 2)
    ones = jnp.ones((b, ATT_KV_HEADS, 1, s), v.dtype)
    pad = jnp.zeros((b, ATT_KV_HEADS, DSA_VROWS - ATT_HEAD_DIM - 1, s), v.dtype)
    return jnp.concatenate([vt, ones, pad], axis=2)


def _dsa(a3, vt, iwt, qcap):
    b, s, _ = a3.shape
    nb = s // Q_BLOCK
    k_sel = min(TOPK_MAX, s // 4)
    idx_scale = (IDX_HEAD_DIM ** -0.5) * (IDX_HEADS ** -0.5)
    kern = functools.partial(_dsa_kernel, seq=s, k_sel=k_sel, idx_scale=idx_scale)
    return pl.pallas_call(
        kern,
        grid=(b, nb + 1),
        in_specs=[
            pl.BlockSpec(memory_space=pltpu.SMEM),
            pl.BlockSpec((None, Q_BLOCK, 512), lambda bb, i: (bb, jnp.maximum(i - 1, 0), COL_Q // 512)),
            pl.BlockSpec((None, Q_BLOCK, 512), lambda bb, i: (bb, jnp.minimum(i, nb - 1), COL_IQ // 512)),
            pl.BlockSpec((None, s, ATT_KV_HEADS * K_EXT), lambda bb, i: (bb, 0, COL_K // (ATT_KV_HEADS * K_EXT))),
            pl.BlockSpec((None, s, LANES), lambda bb, i: (bb, 0, COL_IK // LANES)),
            pl.BlockSpec((None, ATT_KV_HEADS, DSA_VROWS, s), lambda bb, i: (bb, 0, 0, 0)),
            pl.BlockSpec((None, None, IDX_HEADS, LANES), lambda bb, i: (bb, jnp.minimum(i, nb - 1), 0, 0)),
        ],
        out_specs=pl.BlockSpec((None, Q_BLOCK, MIX_A), lambda bb, i: (bb, jnp.maximum(i - 1, 0), 0)),
        out_shape=jax.ShapeDtypeStruct((b, s, MIX_A), BF16),
        scratch_shapes=[
            pltpu.VMEM((2, s, LANES), jnp.int32),
            pltpu.VMEM((32, s // 32, LANES), jnp.int32),
            pltpu.VMEM((s // 32, LANES), jnp.int32),
            pltpu.VMEM((DSA_KB, IDX_HEADS * LANES), F32),
            pltpu.VMEM((DSA_KB, IDX_HEADS * LANES), F32),
            pltpu.VMEM((ATT_KV_HEADS, DSA_VROWS, (ATT_HEADS // ATT_KV_HEADS) * LANES), F32),
            pltpu.VMEM((DSA_KA, ATT_HEADS * LANES), F32),
            pltpu.VMEM((DSA_KA, ATT_HEADS * LANES), F32),
            pltpu.VMEM((2, 1, LANES), jnp.int32),
            pltpu.SMEM((ATT_KV_HEADS + 1,), F32),
        ],
        compiler_params=pltpu.CompilerParams(
            dimension_semantics=("parallel", "arbitrary"), vmem_limit_bytes=VMEM_LIMIT),
        name="dsa",
    )(qcap, a3, a3, a3, a3, vt, iwt)


ML_L = 128
ML_TS = 512
ML_PAD = 8


def _mlstm_kernel(mx_ref, mv_ref, mo_ref, gr_ref, cw_ref, cb_ref, wq_ref, wkt_ref, brow_ref, hg_ref, o_ref,
                  xbuf_ref, q_ref, kt_ref, s_ref, m_ref):
    t_idx = pl.program_id(1)
    nh, dk, dv, L = MLSTM_HEADS, MLSTM_QK_DIM, MLSTM_V_DIM, ML_L

    @pl.when(t_idx == 0)
    def _():
        xbuf_ref[0:ML_PAD, :] = jnp.zeros((ML_PAD, MIX_B), F32)
        s_ref[...] = jnp.zeros_like(s_ref)
        m_ref[...] = jnp.zeros_like(m_ref)

    xbuf_ref[ML_PAD:ML_PAD + ML_TS, :] = mx_ref[...]
    xc = cb_ref[...] + jnp.zeros((ML_TS, MIX_B), F32)
    for j in range(CONV_WIDTH):
        s0 = ML_PAD - (CONV_WIDTH - 1) + j
        xc = xc + xbuf_ref[s0:s0 + ML_TS, :] * cw_ref[j:j + 1, :]
    xbuf_ref[0:ML_PAD, :] = mx_ref[ML_TS - ML_PAD:ML_TS, :]
    xc = (xc * jax.nn.sigmoid(xc)).astype(BF16)
    for h in range(nh):
        xh = xc[:, h * dv:(h + 1) * dv]
        q_ref[h] = (_dot(xh, wq_ref[h]) * (dk ** -0.5)).astype(BF16)
        kt_ref[h] = _dot_nt(wkt_ref[h], xh)

    gr = gr_ref[...]
    li_r = gr[0:nh, :] + brow_ref[0:nh, :]
    lf_r = jax.nn.log_sigmoid(gr[nh:2 * nh, :] + brow_ref[nh:2 * nh, :])

    ri = lax.broadcasted_iota(jnp.int32, (L, L), 0)
    ci = lax.broadcasted_iota(jnp.int32, (L, L), 1)
    causal = ri >= ci
    ones_v = jnp.ones((L, dv), BF16)
    ones_n = jnp.ones((dv, dv), BF16)

    tr = lax.broadcasted_iota(jnp.int32, (ML_TS, ML_TS), 0)
    tc = lax.broadcasted_iota(jnp.int32, (ML_TS, ML_TS), 1)
    chunk_triu = jnp.where((tr <= tc) & (tr // L == tc // L), 1.0, 0.0).astype(F32)
    b_all = jnp.dot(lf_r, chunk_triu, precision=lax.Precision.HIGHEST, preferred_element_type=F32)
    g_all = li_r - b_all
    in_chunk = lax.broadcasted_iota(jnp.int32, (nh, ML_TS), 1) % L
    run_all = g_all
    for sh in (1, 2, 4, 8, 16, 32, 64):
        run_all = jnp.maximum(run_all, jnp.where(in_chunk >= sh, pltpu.roll(run_all, sh, axis=1), NEG_BIG))

    def to_rows(row):
        return jnp.broadcast_to(row, (L, L)).T

    for c in range(ML_TS // L):
        r0 = c * L
        b_r, g_r, run = b_all[:, r0:r0 + L], g_all[:, r0:r0 + L], run_all[:, r0:r0 + L]
        run_rows = [to_rows(run[h:h + 1, :]) for h in range(nh)]
        b_rows = [to_rows(b_r[h:h + 1, :]) for h in range(nh)]
        m_prev = m_ref[...]
        b_last = b_r[:, L - 1:L]
        wk_r = b_last + g_r
        m_new = jnp.maximum(b_last + m_prev, jnp.max(wk_r, axis=1, keepdims=True))
        ws_r = jnp.exp(wk_r - m_new)
        wc_r = jnp.exp(b_last + m_prev - m_new)
        m_ref[...] = m_new
        outs = []
        for h in range(nh):
            big_m_t = jnp.maximum(run_rows[h], m_prev[h:h + 1, :])
            m_t = big_m_t + b_rows[h]
            decay = jnp.where(causal, jnp.exp(g_r[h:h + 1, :] - big_m_t), 0.0)
            w_int = jnp.exp(m_prev[h:h + 1, :] - big_m_t)
            qh = q_ref[h, r0:r0 + L, :]
            kt = kt_ref[h, :, r0:r0 + L]
            v_ext = jnp.concatenate(
                [mv_ref[r0:r0 + L, h * dv:(h + 1) * dv], ones_v], axis=1)
            wt = _dot(qh, kt.astype(BF16)) * decay
            nd = (_dot(wt.astype(BF16), v_ext)
                  + jnp.concatenate([w_int, w_int], axis=1) * _dot(qh, s_ref[h].astype(BF16)))
            hv = nd[:, 0:dv] / jnp.maximum(jnp.abs(nd[:, dv:2 * dv]), jnp.exp(-m_t))
            ms = _dot((hv * hv).astype(BF16), ones_n) * (1.0 / dv)
            outs.append(hv * lax.rsqrt(ms + EPS) * hg_ref[0:1, h * dv:(h + 1) * dv])
            wc = jnp.broadcast_to(wc_r[h:h + 1, :], (dk, L))
            s_ref[h] = (jnp.concatenate([wc, wc], axis=1) * s_ref[h]
                        + _dot((kt * ws_r[h:h + 1, :]).astype(BF16), v_ext))
        hh = jnp.concatenate(outs, axis=1)
        o_ref[r0:r0 + L, :] = (hh * jax.nn.sigmoid(mo_ref[r0:r0 + L, :].astype(F32))).astype(o_ref.dtype)


def _mlstm(m3, mvo3, gt, conv_w, conv_b, wq, wkt, brow, head_g):
    b, s, _ = m3.shape
    nh = MLSTM_HEADS
    tile = lambda c: pl.BlockSpec((None, ML_TS, MIX_B), lambda bb, t: (bb, t, c))
    const2 = lambda shape: pl.BlockSpec(shape, lambda bb, t: (0, 0))
    const3 = lambda shape: pl.BlockSpec(shape, lambda bb, t: (0, 0, 0))
    return pl.pallas_call(
        _mlstm_kernel,
        grid=(b, s // ML_TS),
        in_specs=[
            tile(0), tile(0), tile(1),
            pl.BlockSpec((None, 2 * nh, ML_TS), lambda bb, t: (bb, 0, t)),
            const2((CONV_WIDTH, MIX_B)), const2((1, MIX_B)),
            const3((nh, MLSTM_V_DIM, MLSTM_QK_DIM)), const3((nh, MLSTM_QK_DIM, MLSTM_V_DIM)),
            const2((2 * nh, 1)), const2((1, MIX_B)),
        ],
        out_specs=pl.BlockSpec((None, ML_TS, MIX_B), lambda bb, t: (bb, t, 0)),
        out_shape=jax.ShapeDtypeStruct((b, s, MIX_B), BF16),
        scratch_shapes=[
            pltpu.VMEM((ML_PAD + ML_TS, MIX_B), F32),
            pltpu.VMEM((nh, ML_TS, MLSTM_QK_DIM), BF16),
            pltpu.VMEM((nh, MLSTM_QK_DIM, ML_TS), F32),
            pltpu.VMEM((nh, MLSTM_QK_DIM, 2 * MLSTM_V_DIM), F32),
            pltpu.VMEM((nh, ML_L), F32),
        ],
        compiler_params=pltpu.CompilerParams(
            dimension_semantics=("parallel", "arbitrary"), vmem_limit_bytes=VMEM_LIMIT),
        name="mlstm",
    )(m3, mvo3, mvo3, gt, conv_w, conv_b, wq, wkt, brow, head_g)


MERGE_TM = 512


def _merge_ffn_kernel(x_ref, ya_ref, yb_ref, ga_ref, gb_ref, wa_ref, wb_ref, wo_ref,
                      g_ref, wg_ref, wu_ref, wd_ref, o_ref):
    merged = (jax.nn.sigmoid(ga_ref[...].astype(F32)) * _dot(ya_ref[...], wa_ref[...])
              + jax.nn.sigmoid(gb_ref[...].astype(F32)) * _dot(yb_ref[...], wb_ref[...]))
    x = x_ref[...] + _dot(merged.astype(BF16), wo_ref[...])
    _half_ffn(x, g_ref, wg_ref, wu_ref, wd_ref, o_ref)


def _merge_ffn(x2d, ya, yb, gates, wa, wb, wo, g, wg, wu, wd):
    m = x2d.shape[0]
    row = lambda i: (i, 0)
    const = lambda i: (0, 0)
    resident = lambda shape: pl.BlockSpec(shape, const, pipeline_mode=pl.Buffered(1))
    return pl.pallas_call(
        _merge_ffn_kernel,
        grid=(m // MERGE_TM,),
        in_specs=[
            pl.BlockSpec((MERGE_TM, D_MODEL), row),
            pl.BlockSpec((MERGE_TM, MIX_A), row),
            pl.BlockSpec((MERGE_TM, MIX_B), row),
            pl.BlockSpec((MERGE_TM, D_MODEL), lambda i: (i, 0)),
            pl.BlockSpec((MERGE_TM, D_MODEL), lambda i: (i, 1)),
            resident((MIX_A, D_MODEL)),
            resident((MIX_B, D_MODEL)),
            resident((D_MODEL, D_MODEL)),
            pl.BlockSpec((1, D_MODEL), const),
            resident((D_MODEL, D_FF)),
            resident((D_MODEL, D_FF)),
            resident((D_FF, D_MODEL)),
        ],
        out_specs=pl.BlockSpec((MERGE_TM, D_MODEL), row),
        out_shape=jax.ShapeDtypeStruct((m, D_MODEL), F32),
        compiler_params=pltpu.CompilerParams(
            dimension_semantics=("parallel",), vmem_limit_bytes=VMEM_LIMIT),
        name="merge_ffn",
    )(x2d, ya, yb, gates, gates, wa, wb, wo, g.reshape(1, D_MODEL), wg, wu, wd)


def _block_diag_ones(n, blk):
    r = jnp.arange(n) // blk
    return (r[:, None] == r[None, :]).astype(BF16)


def _arrange_w_in(w_in):
    widths = (512, 128, 128, 512, 64, 8, 512, 512, 4, 4, 512, 1024, 1024)
    parts, off = [], 0
    for wd in widths:
        parts.append(w_in[:, off:off + wd])
        off += wd
    aq, ak, av, iq, ik, iw, mx, mv, mi, mf, mo, ga, gb = parts
    z = lambda n: jnp.zeros((w_in.shape[0], n), w_in.dtype)
    cols = [aq, iq, ak[:, :64], z(64), ak[:, 64:], z(64), ik, z(64), av,
            iw, mi, mf, z(G_WIDTH - 16),
            mx, mv, mo, ga, gb]
    return jnp.concatenate(cols, axis=1).astype(BF16)


def kernel(x, ffn1_norm, ffn1_w_gate, ffn1_w_up, ffn1_w_down, mix_norm, w_in, q_norm, k_norm,
           idx_k_norm, conv_w, conv_b, w_mq, w_mk, b_i, b_f, m_head_norm, w_proj_a, w_proj_b,
           w_out, ffn2_norm, ffn2_w_gate, ffn2_w_up, ffn2_w_down):
    b, s, _ = x.shape
    m = b * s
    nb = s // Q_BLOCK
    nh = MLSTM_HEADS
    x2d = x.reshape(m, D_MODEL)
    for l in range(ffn1_norm.shape[0]):
        x2d = _ffn(x2d, ffn1_norm[l], ffn1_w_gate[l].astype(BF16), ffn1_w_up[l].astype(BF16),
                   ffn1_w_down[l].astype(BF16))

        gq = (jnp.tile(q_norm[l], ATT_HEADS) * (ATT_HEAD_DIM ** -0.5 * LOG2_E)).reshape(1, 512)
        z64 = jnp.zeros((ATT_HEAD_DIM,), F32)
        one64 = z64.at[0].set(1.0)
        gk = jnp.concatenate([k_norm[l], z64] * ATT_KV_HEADS + [idx_k_norm[l], z64]).reshape(1, -1)
        kone = jnp.concatenate([z64, one64] * ATT_KV_HEADS + [z64, z64]).reshape(1, -1)
        a2d, g2d, mx2d, mvo2d, t2d = _proj(x2d, mix_norm[l], _arrange_w_in(w_in[l]),
                                   _block_diag_ones(512, ATT_HEAD_DIM),
                                   _block_diag_ones(COL_V - COL_K, ATT_HEAD_DIM), gq, gk, kone)

        a3 = a2d.reshape(b, s, A_WIDTH)
        vt = _v_transposed(a3[:, :, COL_V:COL_V + ATT_KV_HEADS * ATT_HEAD_DIM])
        g3 = g2d.reshape(b, s, G_WIDTH)
        iwt = jnp.swapaxes(g3[:, :, 0:IDX_HEADS].reshape(b, nb, Q_BLOCK, IDX_HEADS), 2, 3)
        qcap = (jnp.max(jnp.abs(gq)) * (ATT_HEAD_DIM ** 0.5) * BOUND_SLACK).reshape(1)
        ya = _dsa(a3, vt, iwt, qcap)

        gt = jnp.swapaxes(g3[:, :, IDX_HEADS:IDX_HEADS + 2 * nh], 1, 2)
        bias = jnp.concatenate([b_i[l], b_f[l]])
        yb = _mlstm(mx2d.reshape(b, s, MIX_B), mvo2d.reshape(b, s, 2 * MIX_B), gt,
                    conv_w[l], conv_b[l].reshape(1, MIX_B),
                    w_mq[l].astype(BF16), jnp.swapaxes(w_mk[l], 1, 2).astype(BF16),
                    bias.reshape(2 * nh, 1), m_head_norm[l].reshape(1, MIX_B))

        x2d = _merge_ffn(x2d, ya.reshape(m, MIX_A), yb.reshape(m, MIX_B), t2d,
                         w_proj_a[l].astype(BF16), w_proj_b[l].astype(BF16), w_out[l].astype(BF16),
                         ffn2_norm[l], ffn2_w_gate[l].astype(BF16), ffn2_w_up[l].astype(BF16),
                         ffn2_w_down[l].astype(BF16))
    return x2d.reshape(b, s, D_MODEL)
```

```python
import functools

import jax
import jax.numpy as jnp
from jax import lax
from jax.experimental import pallas as pl
from jax.experimental.pallas import tpu as pltpu

F32 = jnp.float32
BF16 = jnp.bfloat16

D_MODEL = 1024
D_FF = 2816
EPS = 1e-6
ATT_HEADS = 8
ATT_KV_HEADS = 2
ATT_HEAD_DIM = 64
IDX_HEADS = 8
IDX_HEAD_DIM = 64
TOPK_MAX = 256
Q_BLOCK = 128
CHUNK = 64
MLSTM_HEADS = 4
MLSTM_QK_DIM = 64
MLSTM_V_DIM = 128
CONV_WIDTH = 4
MIX_A = ATT_HEADS * ATT_HEAD_DIM
MIX_B = MLSTM_HEADS * MLSTM_V_DIM

LANES = 128
VMEM_LIMIT = 56 * 1024 * 1024
INT_MIN = -2147483648
NEG_BIG = -1e30
LOG2_E = 1.4426950408889634

COL_Q, COL_IQ, COL_K, COL_IK, COL_V = 0, 512, 1024, 1280, 1408
K_EXT = 128
A_WIDTH = 1536
G_WIDTH = 128
M_WIDTH = 3 * MIX_B
T_WIDTH = 2 * D_MODEL
W_TOTAL = A_WIDTH + G_WIDTH + M_WIDTH + T_WIDTH


def _rms(x, g):
    return x * lax.rsqrt(jnp.mean(x * x, axis=-1, keepdims=True) + EPS) * g


def _dot(a, b):
    return jnp.dot(a, b, preferred_element_type=F32)


def _dot_nt(a, b):
    return lax.dot_general(a, b, (((1,), (1,)), ((), ())), preferred_element_type=F32)


def _dot_tn(a, b):
    return lax.dot_general(a, b, (((0,), (0,)), ((), ())), preferred_element_type=F32)


FFN_TM = 512
FFN_TF = 1408


def _ffn_kernel(x_ref, g_ref, wg_ref, wu_ref, wd_ref, o_ref):
    _half_ffn(x_ref[...], g_ref, wg_ref, wu_ref, wd_ref, o_ref)


def _half_ffn(x, g_ref, wg_ref, wu_ref, wd_ref, o_ref):
    h = _rms(x, g_ref[...]).astype(BF16)
    acc = None
    for c in range(D_FF // FFN_TF):
        cols = slice(c * FFN_TF, (c + 1) * FFN_TF)
        a = _dot(h, wg_ref[:, cols])
        u = _dot(h, wu_ref[:, cols])
        act = (a * jax.nn.sigmoid(a) * u).astype(BF16)
        part = _dot(act, wd_ref[cols, :])
        acc = part if acc is None else acc + part
    o_ref[...] = x + 0.5 * acc


def _ffn(x2d, g, wg, wu, wd):
    m = x2d.shape[0]
    const = lambda i: (0, 0)
    resident = lambda shape: pl.BlockSpec(shape, const, pipeline_mode=pl.Buffered(1))
    return pl.pallas_call(
        _ffn_kernel,
        grid=(m // FFN_TM,),
        in_specs=[
            pl.BlockSpec((FFN_TM, D_MODEL), lambda i: (i, 0)),
            pl.BlockSpec((1, D_MODEL), const),
            resident((D_MODEL, D_FF)),
            resident((D_MODEL, D_FF)),
            resident((D_FF, D_MODEL)),
        ],
        out_specs=pl.BlockSpec((FFN_TM, D_MODEL), lambda i: (i, 0)),
        out_shape=jax.ShapeDtypeStruct((m, D_MODEL), F32),
        compiler_params=pltpu.CompilerParams(
            dimension_semantics=("parallel",), vmem_limit_bytes=VMEM_LIMIT),
        name="ffn",
    )(x2d, g.reshape(1, D_MODEL), wg, wu, wd)


PROJ_TM = 512


def _proj_kernel(x_ref, g_ref, w_ref, bdq_ref, bdk_ref, gq_ref, gk_ref, kone_ref,
                 a_ref, g_out_ref, mx_ref, mvo_ref, t_ref):
    h = _rms(x_ref[...], g_ref[...]).astype(BF16)

    def head_norm(p, bd_ref, gain_ref):
        ss = _dot((p * p).astype(BF16), bd_ref[...]) * (1.0 / ATT_HEAD_DIM)
        return p * lax.rsqrt(ss + EPS) * gain_ref[...]

    pq = _dot(h, w_ref[:, COL_Q:COL_IQ])
    a_ref[:, COL_Q:COL_IQ] = head_norm(pq, bdq_ref, gq_ref).astype(BF16)
    a_ref[:, COL_IQ:COL_K] = _dot(h, w_ref[:, COL_IQ:COL_K]).astype(BF16)
    pk = _dot(h, w_ref[:, COL_K:COL_V])
    a_ref[:, COL_K:COL_V] = (head_norm(pk, bdk_ref, gk_ref) + kone_ref[...]).astype(BF16)
    a_ref[:, COL_V:A_WIDTH] = _dot(h, w_ref[:, COL_V:A_WIDTH]).astype(BF16)
    o = A_WIDTH
    g_out_ref[...] = _dot(h, w_ref[:, o:o + G_WIDTH])
    o += G_WIDTH
    mx_ref[...] = _dot(h, w_ref[:, o:o + MIX_B])
    o += MIX_B
    for c in range(2):
        mvo_ref[:, c * MIX_B:(c + 1) * MIX_B] = _dot(h, w_ref[:, o + c * MIX_B:o + (c + 1) * MIX_B]).astype(BF16)
    o += 2 * MIX_B
    for c in range(2):
        t_ref[:, c * D_MODEL:(c + 1) * D_MODEL] = _dot(
            h, w_ref[:, o + c * D_MODEL:o + (c + 1) * D_MODEL]).astype(BF16)


def _proj(x2d, g, w_all, bdq, bdk, gq, gk, kone):
    m = x2d.shape[0]
    const = lambda i: (0, 0)
    row = lambda i: (i, 0)
    return pl.pallas_call(
        _proj_kernel,
        grid=(m // PROJ_TM,),
        in_specs=[
            pl.BlockSpec((PROJ_TM, D_MODEL), row),
            pl.BlockSpec((1, D_MODEL), const),
            pl.BlockSpec((D_MODEL, W_TOTAL), const, pipeline_mode=pl.Buffered(1)),
            pl.BlockSpec((512, 512), const),
            pl.BlockSpec((COL_V - COL_K, COL_V - COL_K), const),
            pl.BlockSpec((1, 512), const),
            pl.BlockSpec((1, COL_V - COL_K), const),
            pl.BlockSpec((1, COL_V - COL_K), const),
        ],
        out_specs=[
            pl.BlockSpec((PROJ_TM, A_WIDTH), row),
            pl.BlockSpec((PROJ_TM, G_WIDTH), row),
            pl.BlockSpec((PROJ_TM, MIX_B), row),
            pl.BlockSpec((PROJ_TM, 2 * MIX_B), row),
            pl.BlockSpec((PROJ_TM, T_WIDTH), row),
        ],
        out_shape=[
            jax.ShapeDtypeStruct((m, A_WIDTH), BF16),
            jax.ShapeDtypeStruct((m, G_WIDTH), F32),
            jax.ShapeDtypeStruct((m, MIX_B), F32),
            jax.ShapeDtypeStruct((m, 2 * MIX_B), BF16),
            jax.ShapeDtypeStruct((m, T_WIDTH), BF16),
        ],
        compiler_params=pltpu.CompilerParams(
            dimension_semantics=("parallel",), vmem_limit_bytes=VMEM_LIMIT),
        name="proj",
    )(x2d, g.reshape(1, D_MODEL), w_all, bdq, bdk, gq, gk, kone)


DSA_KB = 512
DSA_KA = 256
BOUND_LIMIT = 60.0
BOUND_SLACK = 1.01
DSA_VROWS = 80

BIT_GROUP = 256
SEARCH_CHUNK = 1024


def _bit_transpose(words):
    a = list(words)
    j, m = 16, 0x0000FFFF
    while j:
        k = 0
        while k < 32:
            t = (a[k] ^ (a[k + j] >> j)) & m
            a[k] = a[k] ^ t
            a[k + j] = a[k + j] ^ (t << j)
            k = (k + j + 1) & ~j
        j >>= 1
        m = (m ^ (m << j)) & 0xFFFFFFFF
    return a


def _heads_to_lanes(x, n_heads, dh):
    xf = x.astype(F32)
    return jnp.concatenate([xf[:, h * dh:(h + 1) * dh].T for h in range(n_heads)], axis=1).astype(BF16)


def _dsa_kernel(qcap_ref, q_ref, iq_ref, k_ref, ik_ref, vt_ref, iwt_ref, o_ref,
                keys_ref, planes_ref, alive_ref, la_ref, lb_ref, acc_ref, sa_ref, sb_ref, thr_ref, stab_ref,
                *, seq, k_sel, idx_scale):
    kb_sz = DSA_KB
    n_blocks = seq // Q_BLOCK
    step = pl.program_id(1)
    has_cur = step < n_blocks
    has_prev = step >= 1
    cur = jnp.minimum(step, n_blocks - 1)
    prev = jnp.maximum(step - 1, 0)
    cur_buf = cur % 2
    prev_buf = prev % 2
    nkb = (cur * Q_BLOCK + Q_BLOCK + kb_sz - 1) // kb_sz
    nkb_prev = (prev * Q_BLOCK + Q_BLOCK + kb_sz - 1) // kb_sz
    lane = lax.broadcasted_iota(jnp.int32, (1, LANES), 1)
    limit = cur * Q_BLOCK + jnp.where(lane < CHUNK, CHUNK, 2 * CHUNK)
    row_iota = lax.broadcasted_iota(jnp.int32, (kb_sz, LANES), 0)

    iq_t = _heads_to_lanes(iq_ref[...], IDX_HEADS, IDX_HEAD_DIM)
    w = iwt_ref[...]

    def logits_of(blk):
        off = pl.multiple_of(blk * kb_sz, kb_sz)
        return _dot(ik_ref[pl.ds(off, kb_sz), :][:, :IDX_HEAD_DIM], iq_t)

    def keys_from(l_ref, blk, masked):
        off = pl.multiple_of(blk * kb_sz, kb_sz)
        sc = jnp.zeros((kb_sz, LANES), F32)
        for h in range(IDX_HEADS):
            sc = sc + jnp.maximum(l_ref[:, h * LANES:(h + 1) * LANES], 0.0) * w[h:h + 1, :]
        sc = sc * idx_scale
        bits = pltpu.bitcast(sc, jnp.int32)
        key = bits ^ ((bits >> 31) & 0x7FFFFFFF)
        key = jnp.where(sc == 0.0, 0, key)
        if masked:
            key = jnp.where(off + row_iota < limit, key, INT_MIN)
        keys_ref[cur_buf, pl.ds(off, kb_sz), :] = key
        ukey = key ^ INT_MIN
        for grp in range(kb_sz // BIT_GROUP):
            base = grp * BIT_GROUP
            planes = _bit_transpose([ukey[base + 8 * i:base + 8 * i + 8, :] for i in range(32)])
            row = pl.multiple_of((off + base) // 32, 8)
            for p in range(32):
                planes_ref[p, pl.ds(row, 8), :] = planes[p]

    n_pairs = (nkb + 1) // 2

    def score_pair(j, masked):
        lb_ref[...] = logits_of(2 * j + 1)
        keys_from(la_ref, 2 * j, masked)
        la_ref[...] = logits_of(jnp.minimum(2 * j + 2, 2 * n_pairs - 1))
        keys_from(lb_ref, 2 * j + 1, masked)

    def count(pred):
        def body(kb, c):
            off = pl.multiple_of(kb * kb_sz, kb_sz)
            ind = jnp.where(pred(keys_ref[cur_buf, pl.ds(off, kb_sz), :], off + row_iota), 1, 0)
            return c + jnp.sum(ind.reshape(kb_sz // 8, 8, LANES), axis=0)
        c8 = lax.fori_loop(0, nkb, body, jnp.zeros((8, LANES), jnp.int32))
        return jnp.sum(c8, axis=0, keepdims=True)

    def select_threshold():
        zeros = jnp.zeros((1, LANES), jnp.int32)
        rows = seq // 32
        written = lax.broadcasted_iota(jnp.int32, (rows, LANES), 0) < n_pairs * (SEARCH_CHUNK // 32)
        everything = jnp.where(written, -1, 0)

        def ones_in(x):
            cnt = jnp.sum(lax.population_count(x).reshape(rows // 8, 8, LANES), axis=0)
            return jnp.sum(cnt, axis=0, keepdims=True)

        def digit_counts(alive, p_hi, p_lo):
            x1 = alive & p_hi
            return ones_in(x1), ones_in(x1 & p_lo), ones_in((alive ^ x1) & p_lo)

        def decide(q, t_u, above, counts):
            c1, c11, c01 = counts
            hi = above + c1 >= k_sel
            lo = jnp.where(hi, above + c11, above + c1 + c01) >= k_sel
            above = jnp.where(hi, jnp.where(lo, above, above + c11),
                              jnp.where(lo, above + c1, above + c1 + c01))
            t_u = (t_u | jnp.where(hi, jnp.left_shift(jnp.int32(1), 31 - 2 * q), 0)
                   | jnp.where(lo, jnp.left_shift(jnp.int32(1), 30 - 2 * q), 0))
            return t_u, above, jnp.where(hi, 0, -1), jnp.where(lo, 0, -1)

        def radix_pass(q, carry):
            t_u, above, flip_hi, flip_lo = carry
            alive = alive_ref[...] & (planes_ref[2 * q - 2] ^ flip_hi) & (planes_ref[2 * q - 1] ^ flip_lo)
            alive_ref[...] = alive
            return decide(q, t_u, above, digit_counts(alive, planes_ref[2 * q], planes_ref[2 * q + 1]))

        alive_ref[...] = everything
        carry = decide(0, zeros, zeros, digit_counts(everything, planes_ref[0], planes_ref[1]))
        t_u, above, flip_hi, flip_lo = lax.fori_loop(1, 16, radix_pass, carry)
        c_eq = ones_in(alive_ref[...] & (planes_ref[30] ^ flip_hi) & (planes_ref[31] ^ flip_lo))
        few = t_u == 0
        thr = jnp.maximum(t_u ^ INT_MIN, INT_MIN + 1)
        thr_ref[cur_buf] = thr
        need = k_sel - above
        excess = jnp.where(few, 0, c_eq - need)
        pos_bits = seq.bit_length() - 1

        @pl.when(jnp.max(excess) > 0)
        def _():
            def jbit(i, cut):
                cand = cut | jnp.left_shift(jnp.int32(1), pos_bits - 1 - i)
                c = count(lambda kk, pos: (kk == thr) & (pos < cand))
                return jnp.where(c <= need, cand, cut)
            cut = lax.fori_loop(0, pos_bits, jbit, zeros)
            cut = jnp.where(excess > 0, cut, seq)

            def drop_block(kb, _):
                off = pl.multiple_of(kb * kb_sz, kb_sz)
                kk = keys_ref[cur_buf, pl.ds(off, kb_sz), :]
                keys_ref[cur_buf, pl.ds(off, kb_sz), :] = jnp.where(
                    (kk == thr) & (off + row_iota >= cut), INT_MIN, kk)
                return 0

            lax.fori_loop(0, nkb, drop_block, 0)

    reps = ATT_HEADS // ATT_KV_HEADS
    width = reps * LANES
    q_t = _heads_to_lanes(q_ref[...], ATT_HEADS, ATT_HEAD_DIM)

    @pl.when(step == 0)
    def _():
        planes_ref[...] = jnp.zeros_like(planes_ref)

        def kmax_block(kb, best):
            off = pl.multiple_of(kb * kb_sz, kb_sz)
            kf = k_ref[pl.ds(off, kb_sz), :].astype(F32)
            n2 = [jnp.max(jnp.sum(jnp.square(kf[:, g * K_EXT:g * K_EXT + ATT_HEAD_DIM]), axis=1, keepdims=True))
                  for g in range(ATT_KV_HEADS)]
            return tuple(jnp.maximum(b_, n_) for b_, n_ in zip(best, n2))
        best = lax.fori_loop(0, seq // kb_sz, kmax_block, (jnp.float32(0.0),) * ATT_KV_HEADS)
        worst = jnp.float32(0.0)
        for g in range(ATT_KV_HEADS):
            stab_ref[g] = jnp.sqrt(best[g]) * BOUND_SLACK
            worst = jnp.maximum(worst, stab_ref[g] * qcap_ref[0])
        stab_ref[ATT_KV_HEADS] = jnp.where(worst < BOUND_LIMIT, 1.0, 0.0)

    bounded = stab_ref[ATT_KV_HEADS] > 0.5
    qf = q_t.astype(F32)
    q_len = jnp.sqrt(jnp.sum(qf * qf, axis=0, keepdims=True))
    rhs = []
    for g in range(ATT_KV_HEADS):
        cols = slice(g * width, (g + 1) * width)
        neg_bound = jnp.where(bounded, -q_len[:, cols] * stab_ref[g], 0.0)
        rhs.append(jnp.concatenate(
            [qf[:, cols], neg_bound, jnp.zeros((K_EXT - ATT_HEAD_DIM - 1, width), F32)], axis=0).astype(BF16))

    ka = DSA_KA

    def masked_scores(blk):
        off = pl.multiple_of(blk * ka, ka)
        bias = jnp.where(keys_ref[prev_buf, pl.ds(off, ka), :] >= thr_prev, 0.0, NEG_BIG)
        bias = jnp.concatenate([bias] * reps, axis=1)
        kblk = k_ref[pl.ds(off, ka), :]
        return jnp.concatenate(
            [_dot(kblk[:, g * K_EXT:(g + 1) * K_EXT], rhs[g]) + bias for g in range(ATT_KV_HEADS)],
            axis=1)

    def consume_bounded(s_ref, blk):
        off = pl.multiple_of(blk * ka, ka)
        p = jnp.exp2(s_ref[...]).astype(BF16)
        for g in range(ATT_KV_HEADS):
            acc_ref[g] += _dot(vt_ref[g, :, pl.ds(off, ka)], p[:, g * width:(g + 1) * width])

    def consume_online(s_ref, blk, m_old):
        off = pl.multiple_of(blk * ka, ka)
        s = s_ref[...]
        m_new = jnp.maximum(m_old, jnp.max(s, axis=0, keepdims=True))
        alpha = jnp.exp2(m_old - m_new)
        p = jnp.exp2(s - m_new).astype(BF16)
        for g in range(ATT_KV_HEADS):
            vt = vt_ref[g, :, pl.ds(off, ka)]
            acc_ref[g] = (alpha[:, g * width:(g + 1) * width] * acc_ref[g]
                          + _dot(vt, p[:, g * width:(g + 1) * width]))
        return m_new

    n_sub = nkb_prev * (kb_sz // ka)
    thr_prev = thr_ref[prev_buf]

    def att_pair_bounded(j):
        sb_ref[...] = masked_scores(2 * j + 1)
        consume_bounded(sa_ref, 2 * j)
        sa_ref[...] = masked_scores(jnp.minimum(2 * j + 2, n_sub - 1))
        consume_bounded(sb_ref, 2 * j + 1)

    @pl.when(has_cur)
    def _():
        la_ref[...] = logits_of(0)

    @pl.when(has_prev)
    def _():
        acc_ref[...] = jnp.zeros_like(acc_ref)
        sa_ref[...] = masked_scores(0)

    n_fused = jnp.where(has_cur & has_prev & bounded, n_pairs - 1, 0)

    def fused(j, c):
        score_pair(j, False)
        att_pair_bounded(2 * j)
        att_pair_bounded(2 * j + 1)
        return c

    lax.fori_loop(0, n_fused, fused, 0)

    @pl.when(has_cur)
    def _():
        lax.fori_loop(n_fused, n_pairs - 1, lambda j, c: (score_pair(j, False), c)[1], 0)
        score_pair(n_pairs - 1, True)
        select_threshold()

    @pl.when(has_prev & bounded)
    def _():
        lax.fori_loop(2 * n_fused, n_sub // 2, lambda j, c: (att_pair_bounded(j), c)[1], 0)

    @pl.when(has_prev & jnp.logical_not(bounded))
    def _():
        def att_pair(j, m_run):
            sb_ref[...] = masked_scores(2 * j + 1)
            m_run = consume_online(sa_ref, 2 * j, m_run)
            sa_ref[...] = masked_scores(jnp.minimum(2 * j + 2, n_sub - 1))
            return consume_online(sb_ref, 2 * j + 1, m_run)
        lax.fori_loop(0, n_sub // 2, att_pair, jnp.full((1, ATT_HEADS * LANES), NEG_BIG, F32))

    @pl.when(has_prev)
    def _():
        pieces = []
        for g in range(ATT_KV_HEADS):
            a = acc_ref[g]
            o = a[0:ATT_HEAD_DIM, :] / a[ATT_HEAD_DIM:ATT_HEAD_DIM + 1, :]
            pieces += [o[:, r * LANES:(r + 1) * LANES] for r in range(reps)]
        o_ref[...] = jnp.concatenate(pieces, axis=0).T.astype(o_ref.dtype)


def _v_transposed(v):
    b, s, _ = v.shape
    vt = jnp.swapaxes(v.reshape(b, s, ATT_KV_HEADS, ATT_HEAD_DIM), 1, 3)
    vt = jnp.swapaxes(vt, 1, 2)
    ones = jnp.ones((b, ATT_KV_HEADS, 1, s), v.dtype)
    pad = jnp.zeros((b, ATT_KV_HEADS, DSA_VROWS - ATT_HEAD_DIM - 1, s), v.dtype)
    return jnp.concatenate([vt, ones, pad], axis=2)


def _dsa(a3, vt, iwt, qcap):
    b, s, _ = a3.shape
    nb = s // Q_BLOCK
    k_sel = min(TOPK_MAX, s // 4)
    idx_scale = (IDX_HEAD_DIM ** -0.5) * (IDX_HEADS ** -0.5)
    kern = functools.partial(_dsa_kernel, seq=s, k_sel=k_sel, idx_scale=idx_scale)
    return pl.pallas_call(
        kern,
        grid=(b, nb + 1),
        in_specs=[
            pl.BlockSpec(memory_space=pltpu.SMEM),
            pl.BlockSpec((None, Q_BLOCK, 512), lambda bb, i: (bb, jnp.maximum(i - 1, 0), COL_Q // 512)),
            pl.BlockSpec((None, Q_BLOCK, 512), lambda bb, i: (bb, jnp.minimum(i, nb - 1), COL_IQ // 512)),
            pl.BlockSpec((None, s, ATT_KV_HEADS * K_EXT), lambda bb, i: (bb, 0, COL_K // (ATT_KV_HEADS * K_EXT))),
            pl.BlockSpec((None, s, LANES), lambda bb, i: (bb, 0, COL_IK // LANES)),
            pl.BlockSpec((None, ATT_KV_HEADS, DSA_VROWS, s), lambda bb, i: (bb, 0, 0, 0)),
            pl.BlockSpec((None, None, IDX_HEADS, LANES), lambda bb, i: (bb, jnp.minimum(i, nb - 1), 0, 0)),
        ],
        out_specs=pl.BlockSpec((None, Q_BLOCK, MIX_A), lambda bb, i: (bb, jnp.maximum(i - 1, 0), 0)),
        out_shape=jax.ShapeDtypeStruct((b, s, MIX_A), BF16),
        scratch_shapes=[
            pltpu.VMEM((2, s, LANES), jnp.int32),
            pltpu.VMEM((32, s // 32, LANES), jnp.int32),
            pltpu.VMEM((s // 32, LANES), jnp.int32),
            pltpu.VMEM((DSA_KB, IDX_HEADS * LANES), F32),
            pltpu.VMEM((DSA_KB, IDX_HEADS * LANES), F32),
            pltpu.VMEM((ATT_KV_HEADS, DSA_VROWS, (ATT_HEADS // ATT_KV_HEADS) * LANES), F32),
            pltpu.VMEM((DSA_KA, ATT_HEADS * LANES), F32),
            pltpu.VMEM((DSA_KA, ATT_HEADS * LANES), F32),
            pltpu.VMEM((2, 1, LANES), jnp.int32),
            pltpu.SMEM((ATT_KV_HEADS + 1,), F32),
        ],
        compiler_params=pltpu.CompilerParams(
            dimension_semantics=("parallel", "arbitrary"), vmem_limit_bytes=VMEM_LIMIT),
        name="dsa",
    )(qcap, a3, a3, a3, a3, vt, iwt)


ML_L = 128
ML_TS = 512
ML_PAD = 8


def _mlstm_kernel(mx_ref, mv_ref, mo_ref, gr_ref, cw_ref, cb_ref, wq_ref, wkt_ref, brow_ref, hg_ref, o_ref,
                  xbuf_ref, q_ref, kt_ref, s_ref, m_ref):
    t_idx = pl.program_id(1)
    nh, dk, dv, L = MLSTM_HEADS, MLSTM_QK_DIM, MLSTM_V_DIM, ML_L

    @pl.when(t_idx == 0)
    def _():
        xbuf_ref[0:ML_PAD, :] = jnp.zeros((ML_PAD, MIX_B), F32)
        s_ref[...] = jnp.zeros_like(s_ref)
        m_ref[...] = jnp.zeros_like(m_ref)

    xbuf_ref[ML_PAD:ML_PAD + ML_TS, :] = mx_ref[...]
    xc = cb_ref[...] + jnp.zeros((ML_TS, MIX_B), F32)
    for j in range(CONV_WIDTH):
        s0 = ML_PAD - (CONV_WIDTH - 1) + j
        xc = xc + xbuf_ref[s0:s0 + ML_TS, :] * cw_ref[j:j + 1, :]
    xbuf_ref[0:ML_PAD, :] = mx_ref[ML_TS - ML_PAD:ML_TS, :]
    xc = (xc * jax.nn.sigmoid(xc)).astype(BF16)
    for h in range(nh):
        xh = xc[:, h * dv:(h + 1) * dv]
        q_ref[h] = (_dot(xh, wq_ref[h]) * (dk ** -0.5)).astype(BF16)
        kt_ref[h] = _dot_nt(wkt_ref[h], xh)

    gr = gr_ref[...]
    li_r = gr[0:nh, :] + brow_ref[0:nh, :]
    lf_r = jax.nn.log_sigmoid(gr[nh:2 * nh, :] + brow_ref[nh:2 * nh, :])

    ri = lax.broadcasted_iota(jnp.int32, (L, L), 0)
    ci = lax.broadcasted_iota(jnp.int32, (L, L), 1)
    causal = ri >= ci
    ones_v = jnp.ones((L, dv), BF16)
    ones_n = jnp.ones((dv, dv), BF16)

    tr = lax.broadcasted_iota(jnp.int32, (ML_TS, ML_TS), 0)
    tc = lax.broadcasted_iota(jnp.int32, (ML_TS, ML_TS), 1)
    chunk_triu = jnp.where((tr <= tc) & (tr // L == tc // L), 1.0, 0.0).astype(F32)
    b_all = jnp.dot(lf_r, chunk_triu, precision=lax.Precision.HIGHEST, preferred_element_type=F32)
    g_all = li_r - b_all
    in_chunk = lax.broadcasted_iota(jnp.int32, (nh, ML_TS), 1) % L
    run_all = g_all
    for sh in (1, 2, 4, 8, 16, 32, 64):
        run_all = jnp.maximum(run_all, jnp.where(in_chunk >= sh, pltpu.roll(run_all, sh, axis=1), NEG_BIG))

    def to_rows(row):
        return jnp.broadcast_to(row, (L, L)).T

    for c in range(ML_TS // L):
        r0 = c * L
        b_r, g_r, run = b_all[:, r0:r0 + L], g_all[:, r0:r0 + L], run_all[:, r0:r0 + L]
        run_rows = [to_rows(run[h:h + 1, :]) for h in range(nh)]
        b_rows = [to_rows(b_r[h:h + 1, :]) for h in range(nh)]
        m_prev = m_ref[...]
        b_last = b_r[:, L - 1:L]
        wk_r = b_last + g_r
        m_new = jnp.maximum(b_last + m_prev, jnp.max(wk_r, axis=1, keepdims=True))
        ws_r = jnp.exp(wk_r - m_new)
        wc_r = jnp.exp(b_last + m_prev - m_new)
        m_ref[...] = m_new
        outs = []
        for h in range(nh):
            big_m_t = jnp.maximum(run_rows[h], m_prev[h:h + 1, :])
            m_t = big_m_t + b_rows[h]
            decay = jnp.where(causal, jnp.exp(g_r[h:h + 1, :] - big_m_t), 0.0)
            w_int = jnp.exp(m_prev[h:h + 1, :] - big_m_t)
            qh = q_ref[h, r0:r0 + L, :]
            kt = kt_ref[h, :, r0:r0 + L]
            v_ext = jnp.concatenate(
                [mv_ref[r0:r0 + L, h * dv:(h + 1) * dv], ones_v], axis=1)
            wt = _dot(qh, kt.astype(BF16)) * decay
            nd = (_dot(wt.astype(BF16), v_ext)
                  + jnp.concatenate([w_int, w_int], axis=1) * _dot(qh, s_ref[h].astype(BF16)))
            hv = nd[:, 0:dv] / jnp.maximum(jnp.abs(nd[:, dv:2 * dv]), jnp.exp(-m_t))
            ms = _dot((hv * hv).astype(BF16), ones_n) * (1.0 / dv)
            outs.append(hv * lax.rsqrt(ms + EPS) * hg_ref[0:1, h * dv:(h + 1) * dv])
            wc = jnp.broadcast_to(wc_r[h:h + 1, :], (dk, L))
            s_ref[h] = (jnp.concatenate([wc, wc], axis=1) * s_ref[h]
                        + _dot((kt * ws_r[h:h + 1, :]).astype(BF16), v_ext))
        hh = jnp.concatenate(outs, axis=1)
        o_ref[r0:r0 + L, :] = (hh * jax.nn.sigmoid(mo_ref[r0:r0 + L, :].astype(F32))).astype(o_ref.dtype)


def _mlstm(m3, mvo3, gt, conv_w, conv_b, wq, wkt, brow, head_g):
    b, s, _ = m3.shape
    nh = MLSTM_HEADS
    tile = lambda c: pl.BlockSpec((None, ML_TS, MIX_B), lambda bb, t: (bb, t, c))
    const2 = lambda shape: pl.BlockSpec(shape, lambda bb, t: (0, 0))
    const3 = lambda shape: pl.BlockSpec(shape, lambda bb, t: (0, 0, 0))
    return pl.pallas_call(
        _mlstm_kernel,
        grid=(b, s // ML_TS),
        in_specs=[
            tile(0), tile(0), tile(1),
            pl.BlockSpec((None, 2 * nh, ML_TS), lambda bb, t: (bb, 0, t)),
            const2((CONV_WIDTH, MIX_B)), const2((1, MIX_B)),
            const3((nh, MLSTM_V_DIM, MLSTM_QK_DIM)), const3((nh, MLSTM_QK_DIM, MLSTM_V_DIM)),
            const2((2 * nh, 1)), const2((1, MIX_B)),
        ],
        out_specs=pl.BlockSpec((None, ML_TS, MIX_B), lambda bb, t: (bb, t, 0)),
        out_shape=jax.ShapeDtypeStruct((b, s, MIX_B), BF16),
        scratch_shapes=[
            pltpu.VMEM((ML_PAD + ML_TS, MIX_B), F32),
            pltpu.VMEM((nh, ML_TS, MLSTM_QK_DIM), BF16),
            pltpu.VMEM((nh, MLSTM_QK_DIM, ML_TS), F32),
            pltpu.VMEM((nh, MLSTM_QK_DIM, 2 * MLSTM_V_DIM), F32),
            pltpu.VMEM((nh, ML_L), F32),
        ],
        compiler_params=pltpu.CompilerParams(
            dimension_semantics=("parallel", "arbitrary"), vmem_limit_bytes=VMEM_LIMIT),
        name="mlstm",
    )(m3, mvo3, mvo3, gt, conv_w, conv_b, wq, wkt, brow, head_g)


MERGE_TM = 512


def _merge_ffn_kernel(x_ref, ya_ref, yb_ref, ga_ref, gb_ref, wa_ref, wb_ref, wo_ref,
                      g_ref, wg_ref, wu_ref, wd_ref, o_ref):
    merged = (jax.nn.sigmoid(ga_ref[...].astype(F32)) * _dot(ya_ref[...], wa_ref[...])
              + jax.nn.sigmoid(gb_ref[...].astype(F32)) * _dot(yb_ref[...], wb_ref[...]))
    x = x_ref[...] + _dot(merged.astype(BF16), wo_ref[...])
    _half_ffn(x, g_ref, wg_ref, wu_ref, wd_ref, o_ref)


def _merge_ffn(x2d, ya, yb, gates, wa, wb, wo, g, wg, wu, wd):
    m = x2d.shape[0]
    row = lambda i: (i, 0)
    const = lambda i: (0, 0)
    resident = lambda shape: pl.BlockSpec(shape, const, pipeline_mode=pl.Buffered(1))
    return pl.pallas_call(
        _merge_ffn_kernel,
        grid=(m // MERGE_TM,),
        in_specs=[
            pl.BlockSpec((MERGE_TM, D_MODEL), row),
            pl.BlockSpec((MERGE_TM, MIX_A), row),
            pl.BlockSpec((MERGE_TM, MIX_B), row),
            pl.BlockSpec((MERGE_TM, D_MODEL), lambda i: (i, 0)),
            pl.BlockSpec((MERGE_TM, D_MODEL), lambda i: (i, 1)),
            resident((MIX_A, D_MODEL)),
            resident((MIX_B, D_MODEL)),
            resident((D_MODEL, D_MODEL)),
            pl.BlockSpec((1, D_MODEL), const),
            resident((D_MODEL, D_FF)),
            resident((D_MODEL, D_FF)),
            resident((D_FF, D_MODEL)),
        ],
        out_specs=pl.BlockSpec((MERGE_TM, D_MODEL), row),
        out_shape=jax.ShapeDtypeStruct((m, D_MODEL), F32),
        compiler_params=pltpu.CompilerParams(
            dimension_semantics=("parallel",), vmem_limit_bytes=VMEM_LIMIT),
        name="merge_ffn",
    )(x2d, ya, yb, gates, gates, wa, wb, wo, g.reshape(1, D_MODEL), wg, wu, wd)


def _block_diag_ones(n, blk):
    r = jnp.arange(n) // blk
    return (r[:, None] == r[None, :]).astype(BF16)


def _arrange_w_in(w_in):
    widths = (512, 128, 128, 512, 64, 8, 512, 512, 4, 4, 512, 1024, 1024)
    parts, off = [], 0
    for wd in widths:
        parts.append(w_in[:, off:off + wd])
        off += wd
    aq, ak, av, iq, ik, iw, mx, mv, mi, mf, mo, ga, gb = parts
    z = lambda n: jnp.zeros((w_in.shape[0], n), w_in.dtype)
    cols = [aq, iq, ak[:, :64], z(64), ak[:, 64:], z(64), ik, z(64), av,
            iw, mi, mf, z(G_WIDTH - 16),
            mx, mv, mo, ga, gb]
    return jnp.concatenate(cols, axis=1).astype(BF16)


def kernel(x, ffn1_norm, ffn1_w_gate, ffn1_w_up, ffn1_w_down, mix_norm, w_in, q_norm, k_norm,
           idx_k_norm, conv_w, conv_b, w_mq, w_mk, b_i, b_f, m_head_norm, w_proj_a, w_proj_b,
           w_out, ffn2_norm, ffn2_w_gate, ffn2_w_up, ffn2_w_down):
    b, s, _ = x.shape
    m = b * s
    nb = s // Q_BLOCK
    nh = MLSTM_HEADS
    x2d = x.reshape(m, D_MODEL)
    for l in range(ffn1_norm.shape[0]):
        x2d = _ffn(x2d, ffn1_norm[l], ffn1_w_gate[l].astype(BF16), ffn1_w_up[l].astype(BF16),
                   ffn1_w_down[l].astype(BF16))

        gq = (jnp.tile(q_norm[l], ATT_HEADS) * (ATT_HEAD_DIM ** -0.5 * LOG2_E)).reshape(1, 512)
        z64 = jnp.zeros((ATT_HEAD_DIM,), F32)
        one64 = z64.at[0].set(1.0)
        gk = jnp.concatenate([k_norm[l], z64] * ATT_KV_HEADS + [idx_k_norm[l], z64]).reshape(1, -1)
        kone = jnp.concatenate([z64, one64] * ATT_KV_HEADS + [z64, z64]).reshape(1, -1)
        a2d, g2d, mx2d, mvo2d, t2d = _proj(x2d, mix_norm[l], _arrange_w_in(w_in[l]),
                                   _block_diag_ones(512, ATT_HEAD_DIM),
                                   _block_diag_ones(COL_V - COL_K, ATT_HEAD_DIM), gq, gk, kone)

        a3 = a2d.reshape(b, s, A_WIDTH)
        vt = _v_transposed(a3[:, :, COL_V:COL_V + ATT_KV_HEADS * ATT_HEAD_DIM])
        g3 = g2d.reshape(b, s, G_WIDTH)
        iwt = jnp.swapaxes(g3[:, :, 0:IDX_HEADS].reshape(b, nb, Q_BLOCK, IDX_HEADS), 2, 3)
        qcap = (jnp.max(jnp.abs(gq)) * (ATT_HEAD_DIM ** 0.5) * BOUND_SLACK).reshape(1)
        ya = _dsa(a3, vt, iwt, qcap)

        gt = jnp.swapaxes(g3[:, :, IDX_HEADS:IDX_HEADS + 2 * nh], 1, 2)
        bias = jnp.concatenate([b_i[l], b_f[l]])
        yb = _mlstm(mx2d.reshape(b, s, MIX_B), mvo2d.reshape(b, s, 2 * MIX_B), gt,
                    conv_w[l], conv_b[l].reshape(1, MIX_B),
                    w_mq[l].astype(BF16), jnp.swapaxes(w_mk[l], 1, 2).astype(BF16),
                    bias.reshape(2 * nh, 1), m_head_norm[l].reshape(1, MIX_B))

        x2d = _merge_ffn(x2d, ya.reshape(m, MIX_A), yb.reshape(m, MIX_B), t2d,
                         w_proj_a[l].astype(BF16), w_proj_b[l].astype(BF16), w_out[l].astype(BF16),
                         ffn2_norm[l], ffn2_w_gate[l].astype(BF16), ffn2_w_up[l].astype(BF16),
                         ffn2_w_down[l].astype(BF16))
    return x2d.reshape(b, s, D_MODEL)
```

```python
import functools

import jax
import jax.numpy as jnp
from jax import lax
from jax.experimental import pallas as pl
from jax.experimental.pallas import tpu as pltpu

F32 = jnp.float32
BF16 = jnp.bfloat16

D_MODEL = 1024
D_FF = 2816
EPS = 1e-6
ATT_HEADS = 8
ATT_KV_HEADS = 2
ATT_HEAD_DIM = 64
IDX_HEADS = 8
IDX_HEAD_DIM = 64
TOPK_MAX = 256
Q_BLOCK = 128
CHUNK = 64
MLSTM_HEADS = 4
MLSTM_QK_DIM = 64
MLSTM_V_DIM = 128
CONV_WIDTH = 4
MIX_A = ATT_HEADS * ATT_HEAD_DIM
MIX_B = MLSTM_HEADS * MLSTM_V_DIM

LANES = 128
VMEM_LIMIT = 56 * 1024 * 1024
INT_MIN = -2147483648
NEG_BIG = -1e30
LOG2_E = 1.4426950408889634

COL_Q, COL_IQ, COL_K, COL_IK, COL_V = 0, 512, 1024, 1280, 1408
K_EXT = 128
A_WIDTH = 1536
G_WIDTH = 128
M_WIDTH = 3 * MIX_B
T_WIDTH = 2 * D_MODEL
W_TOTAL = A_WIDTH + G_WIDTH + M_WIDTH + T_WIDTH


def _rms(x, g):
    return x * lax.rsqrt(jnp.mean(x * x, axis=-1, keepdims=True) + EPS) * g


def _dot(a, b):
    return jnp.dot(a, b, preferred_element_type=F32)


def _dot_nt(a, b):
    return lax.dot_general(a, b, (((1,), (1,)), ((), ())), preferred_element_type=F32)


def _dot_tn(a, b):
    return lax.dot_general(a, b, (((0,), (0,)), ((), ())), preferred_element_type=F32)


FFN_TM = 512
FFN_TF = 1408


def _ffn_kernel(x_ref, g_ref, wg_ref, wu_ref, wd_ref, o_ref):
    _half_ffn(x_ref[...], g_ref, wg_ref, wu_ref, wd_ref, o_ref)


def _half_ffn(x, g_ref, wg_ref, wu_ref, wd_ref, o_ref):
    h = _rms(x, g_ref[...]).astype(BF16)
    acc = None
    for c in range(D_FF // FFN_TF):
        cols = slice(c * FFN_TF, (c + 1) * FFN_TF)
        a = _dot(h, wg_ref[:, cols])
        u = _dot(h, wu_ref[:, cols])
        act = (a * jax.nn.sigmoid(a) * u).astype(BF16)
        part = _dot(act, wd_ref[cols, :])
        acc = part if acc is None else acc + part
    o_ref[...] = x + 0.5 * acc


def _ffn(x2d, g, wg, wu, wd):
    m = x2d.shape[0]
    const = lambda i: (0, 0)
    resident = lambda shape: pl.BlockSpec(shape, const, pipeline_mode=pl.Buffered(1))
    return pl.pallas_call(
        _ffn_kernel,
        grid=(m // FFN_TM,),
        in_specs=[
            pl.BlockSpec((FFN_TM, D_MODEL), lambda i: (i, 0)),
            pl.BlockSpec((1, D_MODEL), const),
            resident((D_MODEL, D_FF)),
            resident((D_MODEL, D_FF)),
            resident((D_FF, D_MODEL)),
        ],
        out_specs=pl.BlockSpec((FFN_TM, D_MODEL), lambda i: (i, 0)),
        out_shape=jax.ShapeDtypeStruct((m, D_MODEL), F32),
        compiler_params=pltpu.CompilerParams(
            dimension_semantics=("parallel",), vmem_limit_bytes=VMEM_LIMIT),
        name="ffn",
    )(x2d, g.reshape(1, D_MODEL), wg, wu, wd)


PROJ_TM = 512


def _proj_kernel(x_ref, g_ref, w_ref, bdq_ref, bdk_ref, gq_ref, gk_ref, kone_ref,
                 a_ref, g_out_ref, mx_ref, mvo_ref, t_ref):
    h = _rms(x_ref[...], g_ref[...]).astype(BF16)

    def head_norm(p, bd_ref, gain_ref):
        ss = _dot((p * p).astype(BF16), bd_ref[...]) * (1.0 / ATT_HEAD_DIM)
        return p * lax.rsqrt(ss + EPS) * gain_ref[...]

    pq = _dot(h, w_ref[:, COL_Q:COL_IQ])
    a_ref[:, COL_Q:COL_IQ] = head_norm(pq, bdq_ref, gq_ref).astype(BF16)
    a_ref[:, COL_IQ:COL_K] = _dot(h, w_ref[:, COL_IQ:COL_K]).astype(BF16)
    pk = _dot(h, w_ref[:, COL_K:COL_V])
    a_ref[:, COL_K:COL_V] = (head_norm(pk, bdk_ref, gk_ref) + kone_ref[...]).astype(BF16)
    a_ref[:, COL_V:A_WIDTH] = _dot(h, w_ref[:, COL_V:A_WIDTH]).astype(BF16)
    o = A_WIDTH
    g_out_ref[...] = _dot(h, w_ref[:, o:o + G_WIDTH])
    o += G_WIDTH
    mx_ref[...] = _dot(h, w_ref[:, o:o + MIX_B])
    o += MIX_B
    for c in range(2):
        mvo_ref[:, c * MIX_B:(c + 1) * MIX_B] = _dot(h, w_ref[:, o + c * MIX_B:o + (c + 1) * MIX_B]).astype(BF16)
    o += 2 * MIX_B
    for c in range(2):
        t_ref[:, c * D_MODEL:(c + 1) * D_MODEL] = _dot(
            h, w_ref[:, o + c * D_MODEL:o + (c + 1) * D_MODEL]).astype(BF16)


def _proj(x2d, g, w_all, bdq, bdk, gq, gk, kone):
    m = x2d.shape[0]
    const = lambda i: (0, 0)
    row = lambda i: (i, 0)
    return pl.pallas_call(
        _proj_kernel,
        grid=(m // PROJ_TM,),
        in_specs=[
            pl.BlockSpec((PROJ_TM, D_MODEL), row),
            pl.BlockSpec((1, D_MODEL), const),
            pl.BlockSpec((D_MODEL, W_TOTAL), const, pipeline_mode=pl.Buffered(1)),
            pl.BlockSpec((512, 512), const),
            pl.BlockSpec((COL_V - COL_K, COL_V - COL_K), const),
            pl.BlockSpec((1, 512), const),
            pl.BlockSpec((1, COL_V - COL_K), const),
            pl.BlockSpec((1, COL_V - COL_K), const),
        ],
        out_specs=[
            pl.BlockSpec((PROJ_TM, A_WIDTH), row),
            pl.BlockSpec((PROJ_TM, G_WIDTH), row),
            pl.BlockSpec((PROJ_TM, MIX_B), row),
            pl.BlockSpec((PROJ_TM, 2 * MIX_B), row),
            pl.BlockSpec((PROJ_TM, T_WIDTH), row),
        ],
        out_shape=[
            jax.ShapeDtypeStruct((m, A_WIDTH), BF16),
            jax.ShapeDtypeStruct((m, G_WIDTH), F32),
            jax.ShapeDtypeStruct((m, MIX_B), F32),
            jax.ShapeDtypeStruct((m, 2 * MIX_B), BF16),
            jax.ShapeDtypeStruct((m, T_WIDTH), BF16),
        ],
        compiler_params=pltpu.CompilerParams(
            dimension_semantics=("parallel",), vmem_limit_bytes=VMEM_LIMIT),
        name="proj",
    )(x2d, g.reshape(1, D_MODEL), w_all, bdq, bdk, gq, gk, kone)


DSA_KB = 512
DSA_KA = 256
BOUND_LIMIT = 60.0
BOUND_SLACK = 1.01
DSA_VROWS = 80

BIT_GROUP = 256
SEARCH_CHUNK = 1024
PLANE_PAD_ROWS = 8


def _bit_transpose(words):
    a = list(words)
    j, m = 16, 0x0000FFFF
    while j:
        k = 0
        while k < 32:
            t = (a[k] ^ (a[k + j] >> j)) & m
            a[k] = a[k] ^ t
            a[k + j] = a[k + j] ^ (t << j)
            k = (k + j + 1) & ~j
        j >>= 1
        m = (m ^ (m << j)) & 0xFFFFFFFF
    return a


def _heads_to_lanes(x, n_heads, dh):
    xf = x.astype(F32)
    return jnp.concatenate([xf[:, h * dh:(h + 1) * dh].T for h in range(n_heads)], axis=1).astype(BF16)


def _dsa_kernel(qcap_ref, q_ref, iq_ref, k_ref, ik_ref, vt_ref, iwt_ref, o_ref,
                keys_ref, planes_ref, alive_ref, la_ref, lb_ref, acc_ref, sa_ref, sb_ref, thr_ref, stab_ref,
                *, seq, k_sel, idx_scale):
    kb_sz = DSA_KB
    n_blocks = seq // Q_BLOCK
    step = pl.program_id(1)
    has_cur = step < n_blocks
    has_prev = step >= 1
    cur = jnp.minimum(step, n_blocks - 1)
    prev = jnp.maximum(step - 1, 0)
    cur_buf = cur % 2
    prev_buf = prev % 2
    nkb = (cur * Q_BLOCK + Q_BLOCK + kb_sz - 1) // kb_sz
    nkb_prev = (prev * Q_BLOCK + Q_BLOCK + kb_sz - 1) // kb_sz
    lane = lax.broadcasted_iota(jnp.int32, (1, LANES), 1)
    limit = cur * Q_BLOCK + jnp.where(lane < CHUNK, CHUNK, 2 * CHUNK)
    row_iota = lax.broadcasted_iota(jnp.int32, (kb_sz, LANES), 0)

    iq_t = _heads_to_lanes(iq_ref[...], IDX_HEADS, IDX_HEAD_DIM)
    w = iwt_ref[...]

    def logits_of(blk):
        off = pl.multiple_of(blk * kb_sz, kb_sz)
        return _dot(ik_ref[pl.ds(off, kb_sz), :][:, :IDX_HEAD_DIM], iq_t)

    def keys_from(l_ref, blk, masked):
        off = pl.multiple_of(blk * kb_sz, kb_sz)
        sc = jnp.zeros((kb_sz, LANES), F32)
        for h in range(IDX_HEADS):
            sc = sc + jnp.maximum(l_ref[:, h * LANES:(h + 1) * LANES], 0.0) * w[h:h + 1, :]
        sc = sc * idx_scale
        bits = pltpu.bitcast(sc, jnp.int32)
        key = bits ^ ((bits >> 31) & 0x7FFFFFFF)
        key = jnp.where(sc == 0.0, 0, key)
        if masked:
            key = jnp.where(off + row_iota < limit, key, INT_MIN)
        keys_ref[cur_buf, pl.ds(off, kb_sz), :] = key
        ukey = key ^ INT_MIN
        for grp in range(kb_sz // BIT_GROUP):
            base = grp * BIT_GROUP
            planes = _bit_transpose([ukey[base + 8 * i:base + 8 * i + 8, :] for i in range(32)])
            row = pl.multiple_of((off + base) // 32, 8)
            for p in range(32):
                planes_ref[p, pl.ds(row, 8), :] = planes[p]

    n_pairs = (nkb + 1) // 2

    def score_pair(j, masked):
        lb_ref[...] = logits_of(2 * j + 1)
        keys_from(la_ref, 2 * j, masked)
        la_ref[...] = logits_of(jnp.minimum(2 * j + 2, 2 * n_pairs - 1))
        keys_from(lb_ref, 2 * j + 1, masked)

    def count(pred):
        def body(kb, c):
            off = pl.multiple_of(kb * kb_sz, kb_sz)
            ind = jnp.where(pred(keys_ref[cur_buf, pl.ds(off, kb_sz), :], off + row_iota), 1, 0)
            return c + jnp.sum(ind.reshape(kb_sz // 8, 8, LANES), axis=0)
        c8 = lax.fori_loop(0, nkb, body, jnp.zeros((8, LANES), jnp.int32))
        return jnp.sum(c8, axis=0, keepdims=True)

    def select_threshold():
        zeros = jnp.zeros((1, LANES), jnp.int32)
        ch_rows = SEARCH_CHUNK // 32
        n_ch = (nkb + 1) // 2
        zeros8 = jnp.zeros((8, LANES), jnp.int32)

        def ones_in(x):
            return jnp.sum(lax.population_count(x).reshape(ch_rows // 8, 8, LANES), axis=0)

        def over_chunks(body, n_out):
            tot = lax.fori_loop(0, n_ch, lambda c, cnt: tuple(a + b_ for a, b_ in zip(cnt, body(c))),
                                (zeros8,) * n_out)
            return [jnp.sum(t, axis=0, keepdims=True) for t in tot]

        def plane(p, row):
            return planes_ref[p, pl.ds(row, ch_rows), :]

        def digit_counts(alive, p_hi, p_lo):
            x1 = alive & p_hi
            return ones_in(x1), ones_in(x1 & p_lo), ones_in((alive ^ x1) & p_lo)

        def decide(q, t_u, above, counts):
            c1, c11, c01 = counts
            hi = above + c1 >= k_sel
            lo = jnp.where(hi, above + c11, above + c1 + c01) >= k_sel
            above = jnp.where(hi, jnp.where(lo, above, above + c11),
                              jnp.where(lo, above + c1, above + c1 + c01))
            t_u = (t_u | jnp.where(hi, jnp.left_shift(jnp.int32(1), 31 - 2 * q), 0)
                   | jnp.where(lo, jnp.left_shift(jnp.int32(1), 30 - 2 * q), 0))
            return t_u, above, jnp.where(hi, 0, -1), jnp.where(lo, 0, -1)

        def first_chunk(c):
            row = pl.multiple_of(c * ch_rows, ch_rows)
            everything = jnp.full((ch_rows, LANES), -1, jnp.int32)
            alive_ref[pl.ds(row, ch_rows), :] = everything
            return digit_counts(everything, plane(0, row), plane(1, row))

        def radix_pass(q, carry):
            t_u, above, flip_hi, flip_lo = carry

            def chunk(c):
                row = pl.multiple_of(c * ch_rows, ch_rows)
                alive = (alive_ref[pl.ds(row, ch_rows), :]
                         & (plane(2 * q - 2, row) ^ flip_hi) & (plane(2 * q - 1, row) ^ flip_lo))
                alive_ref[pl.ds(row, ch_rows), :] = alive
                return digit_counts(alive, plane(2 * q, row), plane(2 * q + 1, row))

            return decide(q, t_u, above, over_chunks(chunk, 3))

        carry = decide(0, zeros, zeros, over_chunks(first_chunk, 3))
        t_u, above, flip_hi, flip_lo = lax.fori_loop(1, 16, radix_pass, carry)

        def last_chunk(c):
            row = pl.multiple_of(c * ch_rows, ch_rows)
            return (ones_in(alive_ref[pl.ds(row, ch_rows), :]
                            & (plane(30, row) ^ flip_hi) & (plane(31, row) ^ flip_lo)),)

        c_eq = over_chunks(last_chunk, 1)[0]
        few = t_u == 0
        thr = jnp.maximum(t_u ^ INT_MIN, INT_MIN + 1)
        thr_ref[cur_buf] = thr
        need = k_sel - above
        excess = jnp.where(few, 0, c_eq - need)
        pos_bits = seq.bit_length() - 1

        @pl.when(jnp.max(excess) > 0)
        def _():
            def jbit(i, cut):
                cand = cut | jnp.left_shift(jnp.int32(1), pos_bits - 1 - i)
                c = count(lambda kk, pos: (kk == thr) & (pos < cand))
                return jnp.where(c <= need, cand, cut)
            cut = lax.fori_loop(0, pos_bits, jbit, zeros)
            cut = jnp.where(excess > 0, cut, seq)

            def drop_block(kb, _):
                off = pl.multiple_of(kb * kb_sz, kb_sz)
                kk = keys_ref[cur_buf, pl.ds(off, kb_sz), :]
                keys_ref[cur_buf, pl.ds(off, kb_sz), :] = jnp.where(
                    (kk == thr) & (off + row_iota >= cut), INT_MIN, kk)
                return 0

            lax.fori_loop(0, nkb, drop_block, 0)

    reps = ATT_HEADS // ATT_KV_HEADS
    width = reps * LANES
    q_t = _heads_to_lanes(q_ref[...], ATT_HEADS, ATT_HEAD_DIM)

    @pl.when(step == 0)
    def _():
        def kmax_block(kb, best):
            off = pl.multiple_of(kb * kb_sz, kb_sz)
            kf = k_ref[pl.ds(off, kb_sz), :].astype(F32)
            n2 = [jnp.max(jnp.sum(jnp.square(kf[:, g * K_EXT:g * K_EXT + ATT_HEAD_DIM]), axis=1, keepdims=True))
                  for g in range(ATT_KV_HEADS)]
            return tuple(jnp.maximum(b_, n_) for b_, n_ in zip(best, n2))
        best = lax.fori_loop(0, seq // kb_sz, kmax_block, (jnp.float32(0.0),) * ATT_KV_HEADS)
        worst = jnp.float32(0.0)
        for g in range(ATT_KV_HEADS):
            stab_ref[g] = jnp.sqrt(best[g]) * BOUND_SLACK
            worst = jnp.maximum(worst, stab_ref[g] * qcap_ref[0])
        stab_ref[ATT_KV_HEADS] = jnp.where(worst < BOUND_LIMIT, 1.0, 0.0)

    bounded = stab_ref[ATT_KV_HEADS] > 0.5
    qf = q_t.astype(F32)
    q_len = jnp.sqrt(jnp.sum(qf * qf, axis=0, keepdims=True))
    rhs = []
    for g in range(ATT_KV_HEADS):
        cols = slice(g * width, (g + 1) * width)
        neg_bound = jnp.where(bounded, -q_len[:, cols] * stab_ref[g], 0.0)
        rhs.append(jnp.concatenate(
            [qf[:, cols], neg_bound, jnp.zeros((K_EXT - ATT_HEAD_DIM - 1, width), F32)], axis=0).astype(BF16))

    ka = DSA_KA

    def masked_scores(blk):
        off = pl.multiple_of(blk * ka, ka)
        bias = jnp.where(keys_ref[prev_buf, pl.ds(off, ka), :] >= thr_prev, 0.0, NEG_BIG)
        bias = jnp.concatenate([bias] * reps, axis=1)
        kblk = k_ref[pl.ds(off, ka), :]
        return jnp.concatenate(
            [_dot(kblk[:, g * K_EXT:(g + 1) * K_EXT], rhs[g]) + bias for g in range(ATT_KV_HEADS)],
            axis=1)

    def consume_bounded(s_ref, blk):
        off = pl.multiple_of(blk * ka, ka)
        p = jnp.exp2(s_ref[...]).astype(BF16)
        for g in range(ATT_KV_HEADS):
            acc_ref[g] += _dot(vt_ref[g, :, pl.ds(off, ka)], p[:, g * width:(g + 1) * width])

    def consume_online(s_ref, blk, m_old):
        off = pl.multiple_of(blk * ka, ka)
        s = s_ref[...]
        m_new = jnp.maximum(m_old, jnp.max(s, axis=0, keepdims=True))
        alpha = jnp.exp2(m_old - m_new)
        p = jnp.exp2(s - m_new).astype(BF16)
        for g in range(ATT_KV_HEADS):
            vt = vt_ref[g, :, pl.ds(off, ka)]
            acc_ref[g] = (alpha[:, g * width:(g + 1) * width] * acc_ref[g]
                          + _dot(vt, p[:, g * width:(g + 1) * width]))
        return m_new

    n_sub = nkb_prev * (kb_sz // ka)
    thr_prev = thr_ref[prev_buf]

    def att_pair_bounded(j):
        sb_ref[...] = masked_scores(2 * j + 1)
        consume_bounded(sa_ref, 2 * j)
        sa_ref[...] = masked_scores(jnp.minimum(2 * j + 2, n_sub - 1))
        consume_bounded(sb_ref, 2 * j + 1)

    @pl.when(has_cur)
    def _():
        la_ref[...] = logits_of(0)

    @pl.when(has_prev)
    def _():
        acc_ref[...] = jnp.zeros_like(acc_ref)
        sa_ref[...] = masked_scores(0)

    n_fused = jnp.where(has_cur & has_prev & bounded, n_pairs - 1, 0)

    def fused(j, c):
        score_pair(j, False)
        att_pair_bounded(2 * j)
        att_pair_bounded(2 * j + 1)
        return c

    lax.fori_loop(0, n_fused, fused, 0)

    @pl.when(has_cur)
    def _():
        lax.fori_loop(n_fused, n_pairs - 1, lambda j, c: (score_pair(j, False), c)[1], 0)
        score_pair(n_pairs - 1, True)
        select_threshold()

    @pl.when(has_prev & bounded)
    def _():
        lax.fori_loop(2 * n_fused, n_sub // 2, lambda j, c: (att_pair_bounded(j), c)[1], 0)

    @pl.when(has_prev & jnp.logical_not(bounded))
    def _():
        def att_pair(j, m_run):
            sb_ref[...] = masked_scores(2 * j + 1)
            m_run = consume_online(sa_ref, 2 * j, m_run)
            sa_ref[...] = masked_scores(jnp.minimum(2 * j + 2, n_sub - 1))
            return consume_online(sb_ref, 2 * j + 1, m_run)
        lax.fori_loop(0, n_sub // 2, att_pair, jnp.full((1, ATT_HEADS * LANES), NEG_BIG, F32))

    @pl.when(has_prev)
    def _():
        pieces = []
        for g in range(ATT_KV_HEADS):
            a = acc_ref[g]
            o = a[0:ATT_HEAD_DIM, :] / a[ATT_HEAD_DIM:ATT_HEAD_DIM + 1, :]
            pieces += [o[:, r * LANES:(r + 1) * LANES] for r in range(reps)]
        o_ref[...] = jnp.concatenate(pieces, axis=0).T.astype(o_ref.dtype)


def _v_transposed(v):
    b, s, _ = v.shape
    vt = jnp.swapaxes(v.reshape(b, s, ATT_KV_HEADS, ATT_HEAD_DIM), 1, 3)
    vt = jnp.swapaxes(vt, 1, 2)
    ones = jnp.ones((b, ATT_KV_HEADS, 1, s), v.dtype)
    pad = jnp.zeros((b, ATT_KV_HEADS, DSA_VROWS - ATT_HEAD_DIM - 1, s), v.dtype)
    return jnp.concatenate([vt, ones, pad], axis=2)


def _dsa(a3, vt, iwt, qcap):
    b, s, _ = a3.shape
    nb = s // Q_BLOCK
    k_sel = min(TOPK_MAX, s // 4)
    idx_scale = (IDX_HEAD_DIM ** -0.5) * (IDX_HEADS ** -0.5)
    kern = functools.partial(_dsa_kernel, seq=s, k_sel=k_sel, idx_scale=idx_scale)
    return pl.pallas_call(
        kern,
        grid=(b, nb + 1),
        in_specs=[
            pl.BlockSpec(memory_space=pltpu.SMEM),
            pl.BlockSpec((None, Q_BLOCK, 512), lambda bb, i: (bb, jnp.maximum(i - 1, 0), COL_Q // 512)),
            pl.BlockSpec((None, Q_BLOCK, 512), lambda bb, i: (bb, jnp.minimum(i, nb - 1), COL_IQ // 512)),
            pl.BlockSpec((None, s, ATT_KV_HEADS * K_EXT), lambda bb, i: (bb, 0, COL_K // (ATT_KV_HEADS * K_EXT))),
            pl.BlockSpec((None, s, LANES), lambda bb, i: (bb, 0, COL_IK // LANES)),
            pl.BlockSpec((None, ATT_KV_HEADS, DSA_VROWS, s), lambda bb, i: (bb, 0, 0, 0)),
            pl.BlockSpec((None, None, IDX_HEADS, LANES), lambda bb, i: (bb, jnp.minimum(i, nb - 1), 0, 0)),
        ],
        out_specs=pl.BlockSpec((None, Q_BLOCK, MIX_A), lambda bb, i: (bb, jnp.maximum(i - 1, 0), 0)),
        out_shape=jax.ShapeDtypeStruct((b, s, MIX_A), BF16),
        scratch_shapes=[
            pltpu.VMEM((2, s, LANES), jnp.int32),
            pltpu.VMEM((32, s // 32 + PLANE_PAD_ROWS, LANES), jnp.int32),
            pltpu.VMEM((s // 32, LANES), jnp.int32),
            pltpu.VMEM((DSA_KB, IDX_HEADS * LANES), F32),
            pltpu.VMEM((DSA_KB, IDX_HEADS * LANES), F32),
            pltpu.VMEM((ATT_KV_HEADS, DSA_VROWS, (ATT_HEADS // ATT_KV_HEADS) * LANES), F32),
            pltpu.VMEM((DSA_KA, ATT_HEADS * LANES), F32),
            pltpu.VMEM((DSA_KA, ATT_HEADS * LANES), F32),
            pltpu.VMEM((2, 1, LANES), jnp.int32),
            pltpu.SMEM((ATT_KV_HEADS + 1,), F32),
        ],
        compiler_params=pltpu.CompilerParams(
            dimension_semantics=("parallel", "arbitrary"), vmem_limit_bytes=VMEM_LIMIT),
        name="dsa",
    )(qcap, a3, a3, a3, a3, vt, iwt)


ML_L = 128
ML_TS = 512
ML_PAD = 8


def _mlstm_kernel(mx_ref, mv_ref, mo_ref, gr_ref, cw_ref, cb_ref, wq_ref, wkt_ref, brow_ref, hg_ref, o_ref,
                  xbuf_ref, q_ref, kt_ref, s_ref, m_ref):
    t_idx = pl.program_id(1)
    nh, dk, dv, L = MLSTM_HEADS, MLSTM_QK_DIM, MLSTM_V_DIM, ML_L

    @pl.when(t_idx == 0)
    def _():
        xbuf_ref[0:ML_PAD, :] = jnp.zeros((ML_PAD, MIX_B), F32)
        s_ref[...] = jnp.zeros_like(s_ref)
        m_ref[...] = jnp.zeros_like(m_ref)

    xbuf_ref[ML_PAD:ML_PAD + ML_TS, :] = mx_ref[...]
    xc = cb_ref[...] + jnp.zeros((ML_TS, MIX_B), F32)
    for j in range(CONV_WIDTH):
        s0 = ML_PAD - (CONV_WIDTH - 1) + j
        xc = xc + xbuf_ref[s0:s0 + ML_TS, :] * cw_ref[j:j + 1, :]
    xbuf_ref[0:ML_PAD, :] = mx_ref[ML_TS - ML_PAD:ML_TS, :]
    xc = (xc * jax.nn.sigmoid(xc)).astype(BF16)
    for h in range(nh):
        xh = xc[:, h * dv:(h + 1) * dv]
        q_ref[h] = (_dot(xh, wq_ref[h]) * (dk ** -0.5)).astype(BF16)
        kt_ref[h] = _dot_nt(wkt_ref[h], xh)

    gr = gr_ref[...]
    li_r = gr[0:nh, :] + brow_ref[0:nh, :]
    lf_r = jax.nn.log_sigmoid(gr[nh:2 * nh, :] + brow_ref[nh:2 * nh, :])

    ri = lax.broadcasted_iota(jnp.int32, (L, L), 0)
    ci = lax.broadcasted_iota(jnp.int32, (L, L), 1)
    causal = ri >= ci
    ones_v = jnp.ones((L, dv), BF16)
    ones_n = jnp.ones((dv, dv), BF16)

    tr = lax.broadcasted_iota(jnp.int32, (ML_TS, ML_TS), 0)
    tc = lax.broadcasted_iota(jnp.int32, (ML_TS, ML_TS), 1)
    chunk_triu = jnp.where((tr <= tc) & (tr // L == tc // L), 1.0, 0.0).astype(F32)
    b_all = jnp.dot(lf_r, chunk_triu, precision=lax.Precision.HIGHEST, preferred_element_type=F32)
    g_all = li_r - b_all
    in_chunk = lax.broadcasted_iota(jnp.int32, (nh, ML_TS), 1) % L
    run_all = g_all
    for sh in (1, 2, 4, 8, 16, 32, 64):
        run_all = jnp.maximum(run_all, jnp.where(in_chunk >= sh, pltpu.roll(run_all, sh, axis=1), NEG_BIG))

    def to_rows(row):
        return jnp.broadcast_to(row, (L, L)).T

    for c in range(ML_TS // L):
        r0 = c * L
        b_r, g_r, run = b_all[:, r0:r0 + L], g_all[:, r0:r0 + L], run_all[:, r0:r0 + L]
        run_rows = [to_rows(run[h:h + 1, :]) for h in range(nh)]
        b_rows = [to_rows(b_r[h:h + 1, :]) for h in range(nh)]
        m_prev = m_ref[...]
        b_last = b_r[:, L - 1:L]
        wk_r = b_last + g_r
        m_new = jnp.maximum(b_last + m_prev, jnp.max(wk_r, axis=1, keepdims=True))
        ws_r = jnp.exp(wk_r - m_new)
        wc_r = jnp.exp(b_last + m_prev - m_new)
        m_ref[...] = m_new
        outs = []
        for h in range(nh):
            big_m_t = jnp.maximum(run_rows[h], m_prev[h:h + 1, :])
            m_t = big_m_t + b_rows[h]
            decay = jnp.where(causal, jnp.exp(g_r[h:h + 1, :] - big_m_t), 0.0)
            w_int = jnp.exp(m_prev[h:h + 1, :] - big_m_t)
            qh = q_ref[h, r0:r0 + L, :]
            kt = kt_ref[h, :, r0:r0 + L]
            v_ext = jnp.concatenate(
                [mv_ref[r0:r0 + L, h * dv:(h + 1) * dv], ones_v], axis=1)
            wt = _dot(qh, kt.astype(BF16)) * decay
            nd = (_dot(wt.astype(BF16), v_ext)
                  + jnp.concatenate([w_int, w_int], axis=1) * _dot(qh, s_ref[h].astype(BF16)))
            hv = nd[:, 0:dv] / jnp.maximum(jnp.abs(nd[:, dv:2 * dv]), jnp.exp(-m_t))
            ms = _dot((hv * hv).astype(BF16), ones_n) * (1.0 / dv)
            outs.append(hv * lax.rsqrt(ms + EPS) * hg_ref[0:1, h * dv:(h + 1) * dv])
            wc = jnp.broadcast_to(wc_r[h:h + 1, :], (dk, L))
            s_ref[h] = (jnp.concatenate([wc, wc], axis=1) * s_ref[h]
                        + _dot((kt * ws_r[h:h + 1, :]).astype(BF16), v_ext))
        hh = jnp.concatenate(outs, axis=1)
        o_ref[r0:r0 + L, :] = (hh * jax.nn.sigmoid(mo_ref[r0:r0 + L, :].astype(F32))).astype(o_ref.dtype)


def _mlstm(m3, mvo3, gt, conv_w, conv_b, wq, wkt, brow, head_g):
    b, s, _ = m3.shape
    nh = MLSTM_HEADS
    tile = lambda c: pl.BlockSpec((None, ML_TS, MIX_B), lambda bb, t: (bb, t, c))
    const2 = lambda shape: pl.BlockSpec(shape, lambda bb, t: (0, 0))
    const3 = lambda shape: pl.BlockSpec(shape, lambda bb, t: (0, 0, 0))
    return pl.pallas_call(
        _mlstm_kernel,
        grid=(b, s // ML_TS),
        in_specs=[
            tile(0), tile(0), tile(1),
            pl.BlockSpec((None, 2 * nh, ML_TS), lambda bb, t: (bb, 0, t)),
            const2((CONV_WIDTH, MIX_B)), const2((1, MIX_B)),
            const3((nh, MLSTM_V_DIM, MLSTM_QK_DIM)), const3((nh, MLSTM_QK_DIM, MLSTM_V_DIM)),
            const2((2 * nh, 1)), const2((1, MIX_B)),
        ],
        out_specs=pl.BlockSpec((None, ML_TS, MIX_B), lambda bb, t: (bb, t, 0)),
        out_shape=jax.ShapeDtypeStruct((b, s, MIX_B), BF16),
        scratch_shapes=[
            pltpu.VMEM((ML_PAD + ML_TS, MIX_B), F32),
            pltpu.VMEM((nh, ML_TS, MLSTM_QK_DIM), BF16),
            pltpu.VMEM((nh, MLSTM_QK_DIM, ML_TS), F32),
            pltpu.VMEM((nh, MLSTM_QK_DIM, 2 * MLSTM_V_DIM), F32),
            pltpu.VMEM((nh, ML_L), F32),
        ],
        compiler_params=pltpu.CompilerParams(
            dimension_semantics=("parallel", "arbitrary"), vmem_limit_bytes=VMEM_LIMIT),
        name="mlstm",
    )(m3, mvo3, mvo3, gt, conv_w, conv_b, wq, wkt, brow, head_g)


MERGE_TM = 512


def _merge_ffn_kernel(x_ref, ya_ref, yb_ref, ga_ref, gb_ref, wa_ref, wb_ref, wo_ref,
                      g_ref, wg_ref, wu_ref, wd_ref, o_ref):
    merged = (jax.nn.sigmoid(ga_ref[...].astype(F32)) * _dot(ya_ref[...], wa_ref[...])
              + jax.nn.sigmoid(gb_ref[...].astype(F32)) * _dot(yb_ref[...], wb_ref[...]))
    x = x_ref[...] + _dot(merged.astype(BF16), wo_ref[...])
    _half_ffn(x, g_ref, wg_ref, wu_ref, wd_ref, o_ref)


def _merge_ffn(x2d, ya, yb, gates, wa, wb, wo, g, wg, wu, wd):
    m = x2d.shape[0]
    row = lambda i: (i, 0)
    const = lambda i: (0, 0)
    resident = lambda shape: pl.BlockSpec(shape, const, pipeline_mode=pl.Buffered(1))
    return pl.pallas_call(
        _merge_ffn_kernel,
        grid=(m // MERGE_TM,),
        in_specs=[
            pl.BlockSpec((MERGE_TM, D_MODEL), row),
            pl.BlockSpec((MERGE_TM, MIX_A), row),
            pl.BlockSpec((MERGE_TM, MIX_B), row),
            pl.BlockSpec((MERGE_TM, D_MODEL), lambda i: (i, 0)),
            pl.BlockSpec((MERGE_TM, D_MODEL), lambda i: (i, 1)),
            resident((MIX_A, D_MODEL)),
            resident((MIX_B, D_MODEL)),
            resident((D_MODEL, D_MODEL)),
            pl.BlockSpec((1, D_MODEL), const),
            resident((D_MODEL, D_FF)),
            resident((D_MODEL, D_FF)),
            resident((D_FF, D_MODEL)),
        ],
        out_specs=pl.BlockSpec((MERGE_TM, D_MODEL), row),
        out_shape=jax.ShapeDtypeStruct((m, D_MODEL), F32),
        compiler_params=pltpu.CompilerParams(
            dimension_semantics=("parallel",), vmem_limit_bytes=VMEM_LIMIT),
        name="merge_ffn",
    )(x2d, ya, yb, gates, gates, wa, wb, wo, g.reshape(1, D_MODEL), wg, wu, wd)


def _block_diag_ones(n, blk):
    r = jnp.arange(n) // blk
    return (r[:, None] == r[None, :]).astype(BF16)


def _arrange_w_in(w_in):
    widths = (512, 128, 128, 512, 64, 8, 512, 512, 4, 4, 512, 1024, 1024)
    parts, off = [], 0
    for wd in widths:
        parts.append(w_in[:, off:off + wd])
        off += wd
    aq, ak, av, iq, ik, iw, mx, mv, mi, mf, mo, ga, gb = parts
    z = lambda n: jnp.zeros((w_in.shape[0], n), w_in.dtype)
    cols = [aq, iq, ak[:, :64], z(64), ak[:, 64:], z(64), ik, z(64), av,
            iw, mi, mf, z(G_WIDTH - 16),
            mx, mv, mo, ga, gb]
    return jnp.concatenate(cols, axis=1).astype(BF16)


def kernel(x, ffn1_norm, ffn1_w_gate, ffn1_w_up, ffn1_w_down, mix_norm, w_in, q_norm, k_norm,
           idx_k_norm, conv_w, conv_b, w_mq, w_mk, b_i, b_f, m_head_norm, w_proj_a, w_proj_b,
           w_out, ffn2_norm, ffn2_w_gate, ffn2_w_up, ffn2_w_down):
    b, s, _ = x.shape
    m = b * s
    nb = s // Q_BLOCK
    nh = MLSTM_HEADS
    x2d = x.reshape(m, D_MODEL)
    for l in range(ffn1_norm.shape[0]):
        x2d = _ffn(x2d, ffn1_norm[l], ffn1_w_gate[l].astype(BF16), ffn1_w_up[l].astype(BF16),
                   ffn1_w_down[l].astype(BF16))

        gq = (jnp.tile(q_norm[l], ATT_HEADS) * (ATT_HEAD_DIM ** -0.5 * LOG2_E)).reshape(1, 512)
        z64 = jnp.zeros((ATT_HEAD_DIM,), F32)
        one64 = z64.at[0].set(1.0)
        gk = jnp.concatenate([k_norm[l], z64] * ATT_KV_HEADS + [idx_k_norm[l], z64]).reshape(1, -1)
        kone = jnp.concatenate([z64, one64] * ATT_KV_HEADS + [z64, z64]).reshape(1, -1)
        a2d, g2d, mx2d, mvo2d, t2d = _proj(x2d, mix_norm[l], _arrange_w_in(w_in[l]),
                                   _block_diag_ones(512, ATT_HEAD_DIM),
                                   _block_diag_ones(COL_V - COL_K, ATT_HEAD_DIM), gq, gk, kone)

        a3 = a2d.reshape(b, s, A_WIDTH)
        vt = _v_transposed(a3[:, :, COL_V:COL_V + ATT_KV_HEADS * ATT_HEAD_DIM])
        g3 = g2d.reshape(b, s, G_WIDTH)
        iwt = jnp.swapaxes(g3[:, :, 0:IDX_HEADS].reshape(b, nb, Q_BLOCK, IDX_HEADS), 2, 3)
        qcap = (jnp.max(jnp.abs(gq)) * (ATT_HEAD_DIM ** 0.5) * BOUND_SLACK).reshape(1)
        ya = _dsa(a3, vt, iwt, qcap)

        gt = jnp.swapaxes(g3[:, :, IDX_HEADS:IDX_HEADS + 2 * nh], 1, 2)
        bias = jnp.concatenate([b_i[l], b_f[l]])
        yb = _mlstm(mx2d.reshape(b, s, MIX_B), mvo2d.reshape(b, s, 2 * MIX_B), gt,
                    conv_w[l], conv_b[l].reshape(1, MIX_B),
                    w_mq[l].astype(BF16), jnp.swapaxes(w_mk[l], 1, 2).astype(BF16),
                    bias.reshape(2 * nh, 1), m_head_norm[l].reshape(1, MIX_B))

        x2d = _merge_ffn(x2d, ya.reshape(m, MIX_A), yb.reshape(m, MIX_B), t2d,
                         w_proj_a[l].astype(BF16), w_proj_b[l].astype(BF16), w_out[l].astype(BF16),
                         ffn2_norm[l], ffn2_w_gate[l].astype(BF16), ffn2_w_up[l].astype(BF16),
                         ffn2_w_down[l].astype(BF16))
    return x2d.reshape(b, s, D_MODEL)
```

```python
import functools

import jax
import jax.numpy as jnp
from jax import lax
from jax.experimental import pallas as pl
from jax.experimental.pallas import tpu as pltpu

F32 = jnp.float32
BF16 = jnp.bfloat16

D_MODEL = 1024
D_FF = 2816
EPS = 1e-6
ATT_HEADS = 8
ATT_KV_HEADS = 2
ATT_HEAD_DIM = 64
IDX_HEADS = 8
IDX_HEAD_DIM = 64
TOPK_MAX = 256
Q_BLOCK = 128
CHUNK = 64
MLSTM_HEADS = 4
MLSTM_QK_DIM = 64
MLSTM_V_DIM = 128
CONV_WIDTH = 4
MIX_A = ATT_HEADS * ATT_HEAD_DIM
MIX_B = MLSTM_HEADS * MLSTM_V_DIM

LANES = 128
VMEM_LIMIT = 56 * 1024 * 1024
INT_MIN = -2147483648
NEG_BIG = -1e30
LOG2_E = 1.4426950408889634

COL_Q, COL_IQ, COL_K, COL_IK, COL_V = 0, 512, 1024, 1280, 1408
K_EXT = 128
A_WIDTH = 1536
G_WIDTH = 128
M_WIDTH = 3 * MIX_B
T_WIDTH = 2 * D_MODEL
W_TOTAL = A_WIDTH + G_WIDTH + M_WIDTH + T_WIDTH


def _rms(x, g):
    return x * lax.rsqrt(jnp.mean(x * x, axis=-1, keepdims=True) + EPS) * g


def _dot(a, b):
    return jnp.dot(a, b, preferred_element_type=F32)


def _dot_nt(a, b):
    return lax.dot_general(a, b, (((1,), (1,)), ((), ())), preferred_element_type=F32)


def _dot_tn(a, b):
    return lax.dot_general(a, b, (((0,), (0,)), ((), ())), preferred_element_type=F32)


FFN_TM = 512
FFN_TF = 2816


def _ffn_kernel(x_ref, g_ref, wg_ref, wu_ref, wd_ref, o_ref):
    _half_ffn(x_ref[...], g_ref, wg_ref, wu_ref, wd_ref, o_ref)


def _half_ffn(x, g_ref, wg_ref, wu_ref, wd_ref, o_ref):
    h = _rms(x, g_ref[...]).astype(BF16)
    acc = None
    for c in range(D_FF // FFN_TF):
        cols = slice(c * FFN_TF, (c + 1) * FFN_TF)
        a = _dot(h, wg_ref[:, cols])
        u = _dot(h, wu_ref[:, cols])
        act = (a * jax.nn.sigmoid(a) * u).astype(BF16)
        part = _dot(act, wd_ref[cols, :])
        acc = part if acc is None else acc + part
    o_ref[...] = x + 0.5 * acc


def _ffn(x2d, g, wg, wu, wd):
    m = x2d.shape[0]
    const = lambda i: (0, 0)
    resident = lambda shape: pl.BlockSpec(shape, const, pipeline_mode=pl.Buffered(1))
    return pl.pallas_call(
        _ffn_kernel,
        grid=(m // FFN_TM,),
        in_specs=[
            pl.BlockSpec((FFN_TM, D_MODEL), lambda i: (i, 0)),
            pl.BlockSpec((1, D_MODEL), const),
            resident((D_MODEL, D_FF)),
            resident((D_MODEL, D_FF)),
            resident((D_FF, D_MODEL)),
        ],
        out_specs=pl.BlockSpec((FFN_TM, D_MODEL), lambda i: (i, 0)),
        out_shape=jax.ShapeDtypeStruct((m, D_MODEL), F32),
        compiler_params=pltpu.CompilerParams(
            dimension_semantics=("parallel",), vmem_limit_bytes=VMEM_LIMIT),
        name="ffn",
    )(x2d, g.reshape(1, D_MODEL), wg, wu, wd)


PROJ_TM = 512


def _proj_kernel(x_ref, g_ref, w_ref, bdq_ref, bdk_ref, gq_ref, gk_ref, kone_ref,
                 a_ref, g_out_ref, mx_ref, mvo_ref, t_ref):
    h = _rms(x_ref[...], g_ref[...]).astype(BF16)

    def head_norm(p, bd_ref, gain_ref):
        ss = _dot((p * p).astype(BF16), bd_ref[...]) * (1.0 / ATT_HEAD_DIM)
        return p * lax.rsqrt(ss + EPS) * gain_ref[...]

    pq = _dot(h, w_ref[:, COL_Q:COL_IQ])
    a_ref[:, COL_Q:COL_IQ] = head_norm(pq, bdq_ref, gq_ref).astype(BF16)
    a_ref[:, COL_IQ:COL_K] = _dot(h, w_ref[:, COL_IQ:COL_K]).astype(BF16)
    pk = _dot(h, w_ref[:, COL_K:COL_V])
    a_ref[:, COL_K:COL_V] = (head_norm(pk, bdk_ref, gk_ref) + kone_ref[...]).astype(BF16)
    a_ref[:, COL_V:A_WIDTH] = _dot(h, w_ref[:, COL_V:A_WIDTH]).astype(BF16)
    o = A_WIDTH
    g_out_ref[...] = _dot(h, w_ref[:, o:o + G_WIDTH])
    o += G_WIDTH
    mx_ref[...] = _dot(h, w_ref[:, o:o + MIX_B])
    o += MIX_B
    for c in range(2):
        mvo_ref[:, c * MIX_B:(c + 1) * MIX_B] = _dot(h, w_ref[:, o + c * MIX_B:o + (c + 1) * MIX_B]).astype(BF16)
    o += 2 * MIX_B
    for c in range(2):
        t_ref[:, c * D_MODEL:(c + 1) * D_MODEL] = _dot(
            h, w_ref[:, o + c * D_MODEL:o + (c + 1) * D_MODEL]).astype(BF16)


def _proj(x2d, g, w_all, bdq, bdk, gq, gk, kone):
    m = x2d.shape[0]
    const = lambda i: (0, 0)
    row = lambda i: (i, 0)
    return pl.pallas_call(
        _proj_kernel,
        grid=(m // PROJ_TM,),
        in_specs=[
            pl.BlockSpec((PROJ_TM, D_MODEL), row),
            pl.BlockSpec((1, D_MODEL), const),
            pl.BlockSpec((D_MODEL, W_TOTAL), const, pipeline_mode=pl.Buffered(1)),
            pl.BlockSpec((512, 512), const),
            pl.BlockSpec((COL_V - COL_K, COL_V - COL_K), const),
            pl.BlockSpec((1, 512), const),
            pl.BlockSpec((1, COL_V - COL_K), const),
            pl.BlockSpec((1, COL_V - COL_K), const),
        ],
        out_specs=[
            pl.BlockSpec((PROJ_TM, A_WIDTH), row),
            pl.BlockSpec((PROJ_TM, G_WIDTH), row),
            pl.BlockSpec((PROJ_TM, MIX_B), row),
            pl.BlockSpec((PROJ_TM, 2 * MIX_B), row),
            pl.BlockSpec((PROJ_TM, T_WIDTH), row),
        ],
        out_shape=[
            jax.ShapeDtypeStruct((m, A_WIDTH), BF16),
            jax.ShapeDtypeStruct((m, G_WIDTH), F32),
            jax.ShapeDtypeStruct((m, MIX_B), F32),
            jax.ShapeDtypeStruct((m, 2 * MIX_B), BF16),
            jax.ShapeDtypeStruct((m, T_WIDTH), BF16),
        ],
        compiler_params=pltpu.CompilerParams(
            dimension_semantics=("parallel",), vmem_limit_bytes=VMEM_LIMIT),
        name="proj",
    )(x2d, g.reshape(1, D_MODEL), w_all, bdq, bdk, gq, gk, kone)


DSA_KB = 512
DSA_KA = 256
BOUND_LIMIT = 60.0
BOUND_SLACK = 1.01
DSA_VROWS = 80

BIT_GROUP = 256
SEARCH_CHUNK = 1024
PLANE_PAD_ROWS = 8


def _bit_transpose(words):
    a = list(words)
    j, m = 16, 0x0000FFFF
    while j:
        k = 0
        while k < 32:
            t = (a[k] ^ (a[k + j] >> j)) & m
            a[k] = a[k] ^ t
            a[k + j] = a[k + j] ^ (t << j)
            k = (k + j + 1) & ~j
        j >>= 1
        m = (m ^ (m << j)) & 0xFFFFFFFF
    return a


def _heads_to_lanes(x, n_heads, dh):
    xf = x.astype(F32)
    return jnp.concatenate([xf[:, h * dh:(h + 1) * dh].T for h in range(n_heads)], axis=1).astype(BF16)


def _dsa_kernel(qcap_ref, q_ref, iq_ref, k_ref, ik_ref, vt_ref, iwt_ref, o_ref,
                keys_ref, planes_ref, alive_ref, la_ref, lb_ref, acc_ref, sa_ref, sb_ref, thr_ref, stab_ref,
                *, seq, k_sel, idx_scale):
    kb_sz = DSA_KB
    n_blocks = seq // Q_BLOCK
    step = pl.program_id(1)
    has_cur = step < n_blocks
    has_prev = step >= 1
    cur = jnp.minimum(step, n_blocks - 1)
    prev = jnp.maximum(step - 1, 0)
    cur_buf = cur % 2
    prev_buf = prev % 2
    nkb = (cur * Q_BLOCK + Q_BLOCK + kb_sz - 1) // kb_sz
    nkb_prev = (prev * Q_BLOCK + Q_BLOCK + kb_sz - 1) // kb_sz
    lane = lax.broadcasted_iota(jnp.int32, (1, LANES), 1)
    limit = cur * Q_BLOCK + jnp.where(lane < CHUNK, CHUNK, 2 * CHUNK)
    row_iota = lax.broadcasted_iota(jnp.int32, (kb_sz, LANES), 0)

    iq_t = _heads_to_lanes(iq_ref[...], IDX_HEADS, IDX_HEAD_DIM)
    w = iwt_ref[...]

    def logits_of(blk):
        off = pl.multiple_of(blk * kb_sz, kb_sz)
        return _dot(ik_ref[pl.ds(off, kb_sz), :][:, :IDX_HEAD_DIM], iq_t)

    def keys_from(l_ref, blk, masked):
        off = pl.multiple_of(blk * kb_sz, kb_sz)
        sc = jnp.zeros((kb_sz, LANES), F32)
        for h in range(IDX_HEADS):
            sc = sc + jnp.maximum(l_ref[:, h * LANES:(h + 1) * LANES], 0.0) * w[h:h + 1, :]
        sc = sc * idx_scale
        bits = pltpu.bitcast(sc, jnp.int32)
        key = bits ^ ((bits >> 31) & 0x7FFFFFFF)
        key = jnp.where(sc == 0.0, 0, key)
        if masked:
            key = jnp.where(off + row_iota < limit, key, INT_MIN)
        keys_ref[cur_buf, pl.ds(off, kb_sz), :] = key
        ukey = key ^ INT_MIN
        for grp in range(kb_sz // BIT_GROUP):
            base = grp * BIT_GROUP
            planes = _bit_transpose([ukey[base + 8 * i:base + 8 * i + 8, :] for i in range(32)])
            row = pl.multiple_of((off + base) // 32, 8)
            for p in range(32):
                planes_ref[p, pl.ds(row, 8), :] = planes[p]

    n_pairs = (nkb + 1) // 2

    def score_pair(j, masked):
        lb_ref[...] = logits_of(2 * j + 1)
        keys_from(la_ref, 2 * j, masked)
        la_ref[...] = logits_of(jnp.minimum(2 * j + 2, 2 * n_pairs - 1))
        keys_from(lb_ref, 2 * j + 1, masked)

    def count(pred):
        def body(kb, c):
            off = pl.multiple_of(kb * kb_sz, kb_sz)
            ind = jnp.where(pred(keys_ref[cur_buf, pl.ds(off, kb_sz), :], off + row_iota), 1, 0)
            return c + jnp.sum(ind.reshape(kb_sz // 8, 8, LANES), axis=0)
        c8 = lax.fori_loop(0, nkb, body, jnp.zeros((8, LANES), jnp.int32))
        return jnp.sum(c8, axis=0, keepdims=True)

    def select_threshold():
        zeros = jnp.zeros((1, LANES), jnp.int32)
        ch_rows = SEARCH_CHUNK // 32
        n_ch = (nkb + 1) // 2
        zeros8 = jnp.zeros((8, LANES), jnp.int32)

        def ones_in(x):
            return jnp.sum(lax.population_count(x).reshape(ch_rows // 8, 8, LANES), axis=0)

        def over_chunks(body, n_out):
            tot = lax.fori_loop(0, n_ch, lambda c, cnt: tuple(a + b_ for a, b_ in zip(cnt, body(c))),
                                (zeros8,) * n_out)
            return [jnp.sum(t, axis=0, keepdims=True) for t in tot]

        def plane(p, row):
            return planes_ref[p, pl.ds(row, ch_rows), :]

        def digit_counts(alive, p_hi, p_lo):
            x1 = alive & p_hi
            return ones_in(x1), ones_in(x1 & p_lo), ones_in((alive ^ x1) & p_lo)

        def decide(q, t_u, above, counts):
            c1, c11, c01 = counts
            hi = above + c1 >= k_sel
            lo = jnp.where(hi, above + c11, above + c1 + c01) >= k_sel
            above = jnp.where(hi, jnp.where(lo, above, above + c11),
                              jnp.where(lo, above + c1, above + c1 + c01))
            t_u = (t_u | jnp.where(hi, jnp.left_shift(jnp.int32(1), 31 - 2 * q), 0)
                   | jnp.where(lo, jnp.left_shift(jnp.int32(1), 30 - 2 * q), 0))
            return t_u, above, jnp.where(hi, 0, -1), jnp.where(lo, 0, -1)

        def first_chunk(c):
            row = pl.multiple_of(c * ch_rows, ch_rows)
            everything = jnp.full((ch_rows, LANES), -1, jnp.int32)
            alive_ref[pl.ds(row, ch_rows), :] = everything
            return digit_counts(everything, plane(0, row), plane(1, row))

        def radix_pass(q, carry):
            t_u, above, flip_hi, flip_lo = carry

            def chunk(c):
                row = pl.multiple_of(c * ch_rows, ch_rows)
                alive = (alive_ref[pl.ds(row, ch_rows), :]
                         & (plane(2 * q - 2, row) ^ flip_hi) & (plane(2 * q - 1, row) ^ flip_lo))
                alive_ref[pl.ds(row, ch_rows), :] = alive
                return digit_counts(alive, plane(2 * q, row), plane(2 * q + 1, row))

            return decide(q, t_u, above, over_chunks(chunk, 3))

        carry = decide(0, zeros, zeros, over_chunks(first_chunk, 3))
        t_u, above, flip_hi, flip_lo = lax.fori_loop(1, 16, radix_pass, carry)

        def last_chunk(c):
            row = pl.multiple_of(c * ch_rows, ch_rows)
            return (ones_in(alive_ref[pl.ds(row, ch_rows), :]
                            & (plane(30, row) ^ flip_hi) & (plane(31, row) ^ flip_lo)),)

        c_eq = over_chunks(last_chunk, 1)[0]
        few = t_u == 0
        thr = jnp.maximum(t_u ^ INT_MIN, INT_MIN + 1)
        thr_ref[cur_buf] = thr
        need = k_sel - above
        excess = jnp.where(few, 0, c_eq - need)
        pos_bits = seq.bit_length() - 1

        @pl.when(jnp.max(excess) > 0)
        def _():
            def jbit(i, cut):
                cand = cut | jnp.left_shift(jnp.int32(1), pos_bits - 1 - i)
                c = count(lambda kk, pos: (kk == thr) & (pos < cand))
                return jnp.where(c <= need, cand, cut)
            cut = lax.fori_loop(0, pos_bits, jbit, zeros)
            cut = jnp.where(excess > 0, cut, seq)

            def drop_block(kb, _):
                off = pl.multiple_of(kb * kb_sz, kb_sz)
                kk = keys_ref[cur_buf, pl.ds(off, kb_sz), :]
                keys_ref[cur_buf, pl.ds(off, kb_sz), :] = jnp.where(
                    (kk == thr) & (off + row_iota >= cut), INT_MIN, kk)
                return 0

            lax.fori_loop(0, nkb, drop_block, 0)

    reps = ATT_HEADS // ATT_KV_HEADS
    width = reps * LANES
    q_t = _heads_to_lanes(q_ref[...], ATT_HEADS, ATT_HEAD_DIM)

    @pl.when(step == 0)
    def _():
        def kmax_block(kb, best):
            off = pl.multiple_of(kb * kb_sz, kb_sz)
            kf = k_ref[pl.ds(off, kb_sz), :].astype(F32)
            n2 = [jnp.max(jnp.sum(jnp.square(kf[:, g * K_EXT:g * K_EXT + ATT_HEAD_DIM]), axis=1, keepdims=True))
                  for g in range(ATT_KV_HEADS)]
            return tuple(jnp.maximum(b_, n_) for b_, n_ in zip(best, n2))
        best = lax.fori_loop(0, seq // kb_sz, kmax_block, (jnp.float32(0.0),) * ATT_KV_HEADS)
        worst = jnp.float32(0.0)
        for g in range(ATT_KV_HEADS):
            stab_ref[g] = jnp.sqrt(best[g]) * BOUND_SLACK
            worst = jnp.maximum(worst, stab_ref[g] * qcap_ref[0])
        stab_ref[ATT_KV_HEADS] = jnp.where(worst < BOUND_LIMIT, 1.0, 0.0)

    bounded = stab_ref[ATT_KV_HEADS] > 0.5
    qf = q_t.astype(F32)
    q_len = jnp.sqrt(jnp.sum(qf * qf, axis=0, keepdims=True))
    rhs = []
    for g in range(ATT_KV_HEADS):
        cols = slice(g * width, (g + 1) * width)
        neg_bound = jnp.where(bounded, -q_len[:, cols] * stab_ref[g], 0.0)
        rhs.append(jnp.concatenate(
            [qf[:, cols], neg_bound, jnp.zeros((K_EXT - ATT_HEAD_DIM - 1, width), F32)], axis=0).astype(BF16))

    ka = DSA_KA

    def masked_scores(blk):
        off = pl.multiple_of(blk * ka, ka)
        bias = jnp.where(keys_ref[prev_buf, pl.ds(off, ka), :] >= thr_prev, 0.0, NEG_BIG)
        bias = jnp.concatenate([bias] * reps, axis=1)
        kblk = k_ref[pl.ds(off, ka), :]
        return jnp.concatenate(
            [_dot(kblk[:, g * K_EXT:(g + 1) * K_EXT], rhs[g]) + bias for g in range(ATT_KV_HEADS)],
            axis=1)

    def consume_bounded(s_ref, blk):
        off = pl.multiple_of(blk * ka, ka)
        p = jnp.exp2(s_ref[...]).astype(BF16)
        for g in range(ATT_KV_HEADS):
            acc_ref[g] += _dot(vt_ref[g, :, pl.ds(off, ka)], p[:, g * width:(g + 1) * width])

    def consume_online(s_ref, blk, m_old):
        off = pl.multiple_of(blk * ka, ka)
        s = s_ref[...]
        m_new = jnp.maximum(m_old, jnp.max(s, axis=0, keepdims=True))
        alpha = jnp.exp2(m_old - m_new)
        p = jnp.exp2(s - m_new).astype(BF16)
        for g in range(ATT_KV_HEADS):
            vt = vt_ref[g, :, pl.ds(off, ka)]
            acc_ref[g] = (alpha[:, g * width:(g + 1) * width] * acc_ref[g]
                          + _dot(vt, p[:, g * width:(g + 1) * width]))
        return m_new

    n_sub = nkb_prev * (kb_sz // ka)
    thr_prev = thr_ref[prev_buf]

    def att_pair_bounded(j):
        sb_ref[...] = masked_scores(2 * j + 1)
        consume_bounded(sa_ref, 2 * j)
        sa_ref[...] = masked_scores(jnp.minimum(2 * j + 2, n_sub - 1))
        consume_bounded(sb_ref, 2 * j + 1)

    @pl.when(has_cur)
    def _():
        la_ref[...] = logits_of(0)

    @pl.when(has_prev)
    def _():
        acc_ref[...] = jnp.zeros_like(acc_ref)
        sa_ref[...] = masked_scores(0)

    n_fused = jnp.where(has_cur & has_prev & bounded, n_pairs - 1, 0)

    def fused(j, c):
        score_pair(j, False)
        att_pair_bounded(2 * j)
        att_pair_bounded(2 * j + 1)
        return c

    lax.fori_loop(0, n_fused, fused, 0)

    @pl.when(has_cur)
    def _():
        lax.fori_loop(n_fused, n_pairs - 1, lambda j, c: (score_pair(j, False), c)[1], 0)
        score_pair(n_pairs - 1, True)
        select_threshold()

    @pl.when(has_prev & bounded)
    def _():
        lax.fori_loop(2 * n_fused, n_sub // 2, lambda j, c: (att_pair_bounded(j), c)[1], 0)

    @pl.when(has_prev & jnp.logical_not(bounded))
    def _():
        def att_pair(j, m_run):
            sb_ref[...] = masked_scores(2 * j + 1)
            m_run = consume_online(sa_ref, 2 * j, m_run)
            sa_ref[...] = masked_scores(jnp.minimum(2 * j + 2, n_sub - 1))
            return consume_online(sb_ref, 2 * j + 1, m_run)
        lax.fori_loop(0, n_sub // 2, att_pair, jnp.full((1, ATT_HEADS * LANES), NEG_BIG, F32))

    @pl.when(has_prev)
    def _():
        pieces = []
        for g in range(ATT_KV_HEADS):
            a = acc_ref[g]
            o = a[0:ATT_HEAD_DIM, :] / a[ATT_HEAD_DIM:ATT_HEAD_DIM + 1, :]
            pieces += [o[:, r * LANES:(r + 1) * LANES] for r in range(reps)]
        o_ref[...] = jnp.concatenate(pieces, axis=0).T.astype(o_ref.dtype)


def _v_transposed(v):
    b, s, _ = v.shape
    vt = jnp.swapaxes(v.reshape(b, s, ATT_KV_HEADS, ATT_HEAD_DIM), 1, 3)
    vt = jnp.swapaxes(vt, 1, 2)
    ones = jnp.ones((b, ATT_KV_HEADS, 1, s), v.dtype)
    pad = jnp.zeros((b, ATT_KV_HEADS, DSA_VROWS - ATT_HEAD_DIM - 1, s), v.dtype)
    return jnp.concatenate([vt, ones, pad], axis=2)


def _dsa(a3, vt, iwt, qcap):
    b, s, _ = a3.shape
    nb = s // Q_BLOCK
    k_sel = min(TOPK_MAX, s // 4)
    idx_scale = (IDX_HEAD_DIM ** -0.5) * (IDX_HEADS ** -0.5)
    kern = functools.partial(_dsa_kernel, seq=s, k_sel=k_sel, idx_scale=idx_scale)
    return pl.pallas_call(
        kern,
        grid=(b, nb + 1),
        in_specs=[
            pl.BlockSpec(memory_space=pltpu.SMEM),
            pl.BlockSpec((None, Q_BLOCK, 512), lambda bb, i: (bb, jnp.maximum(i - 1, 0), COL_Q // 512)),
            pl.BlockSpec((None, Q_BLOCK, 512), lambda bb, i: (bb, jnp.minimum(i, nb - 1), COL_IQ // 512)),
            pl.BlockSpec((None, s, ATT_KV_HEADS * K_EXT), lambda bb, i: (bb, 0, COL_K // (ATT_KV_HEADS * K_EXT))),
            pl.BlockSpec((None, s, LANES), lambda bb, i: (bb, 0, COL_IK // LANES)),
            pl.BlockSpec((None, ATT_KV_HEADS, DSA_VROWS, s), lambda bb, i: (bb, 0, 0, 0)),
            pl.BlockSpec((None, None, IDX_HEADS, LANES), lambda bb, i: (bb, jnp.minimum(i, nb - 1), 0, 0)),
        ],
        out_specs=pl.BlockSpec((None, Q_BLOCK, MIX_A), lambda bb, i: (bb, jnp.maximum(i - 1, 0), 0)),
        out_shape=jax.ShapeDtypeStruct((b, s, MIX_A), BF16),
        scratch_shapes=[
            pltpu.VMEM((2, s, LANES), jnp.int32),
            pltpu.VMEM((32, s // 32 + PLANE_PAD_ROWS, LANES), jnp.int32),
            pltpu.VMEM((s // 32, LANES), jnp.int32),
            pltpu.VMEM((DSA_KB, IDX_HEADS * LANES), F32),
            pltpu.VMEM((DSA_KB, IDX_HEADS * LANES), F32),
            pltpu.VMEM((ATT_KV_HEADS, DSA_VROWS, (ATT_HEADS // ATT_KV_HEADS) * LANES), F32),
            pltpu.VMEM((DSA_KA, ATT_HEADS * LANES), F32),
            pltpu.VMEM((DSA_KA, ATT_HEADS * LANES), F32),
            pltpu.VMEM((2, 1, LANES), jnp.int32),
            pltpu.SMEM((ATT_KV_HEADS + 1,), F32),
        ],
        compiler_params=pltpu.CompilerParams(
            dimension_semantics=("parallel", "arbitrary"), vmem_limit_bytes=VMEM_LIMIT),
        name="dsa",
    )(qcap, a3, a3, a3, a3, vt, iwt)


ML_L = 128
ML_TS = 512
ML_PAD = 8


def _mlstm_kernel(mx_ref, mv_ref, mo_ref, gr_ref, cw_ref, cb_ref, wq_ref, wkt_ref, brow_ref, hg_ref, o_ref,
                  xbuf_ref, q_ref, kt_ref, s_ref, m_ref):
    t_idx = pl.program_id(1)
    nh, dk, dv, L = MLSTM_HEADS, MLSTM_QK_DIM, MLSTM_V_DIM, ML_L

    @pl.when(t_idx == 0)
    def _():
        xbuf_ref[0:ML_PAD, :] = jnp.zeros((ML_PAD, MIX_B), F32)
        s_ref[...] = jnp.zeros_like(s_ref)
        m_ref[...] = jnp.zeros_like(m_ref)

    xbuf_ref[ML_PAD:ML_PAD + ML_TS, :] = mx_ref[...]
    xc = cb_ref[...] + jnp.zeros((ML_TS, MIX_B), F32)
    for j in range(CONV_WIDTH):
        s0 = ML_PAD - (CONV_WIDTH - 1) + j
        xc = xc + xbuf_ref[s0:s0 + ML_TS, :] * cw_ref[j:j + 1, :]
    xbuf_ref[0:ML_PAD, :] = mx_ref[ML_TS - ML_PAD:ML_TS, :]
    xc = (xc * jax.nn.sigmoid(xc)).astype(BF16)
    for h in range(nh):
        xh = xc[:, h * dv:(h + 1) * dv]
        q_ref[h] = (_dot(xh, wq_ref[h]) * (dk ** -0.5)).astype(BF16)
        kt_ref[h] = _dot_nt(wkt_ref[h], xh)

    gr = gr_ref[...]
    li_r = gr[0:nh, :] + brow_ref[0:nh, :]
    lf_r = jax.nn.log_sigmoid(gr[nh:2 * nh, :] + brow_ref[nh:2 * nh, :])

    ri = lax.broadcasted_iota(jnp.int32, (L, L), 0)
    ci = lax.broadcasted_iota(jnp.int32, (L, L), 1)
    causal = ri >= ci
    ones_v = jnp.ones((L, dv), BF16)
    ones_n = jnp.ones((dv, dv), BF16)

    tr = lax.broadcasted_iota(jnp.int32, (ML_TS, ML_TS), 0)
    tc = lax.broadcasted_iota(jnp.int32, (ML_TS, ML_TS), 1)
    chunk_triu = jnp.where((tr <= tc) & (tr // L == tc // L), 1.0, 0.0).astype(F32)
    b_all = jnp.dot(lf_r, chunk_triu, precision=lax.Precision.HIGHEST, preferred_element_type=F32)
    g_all = li_r - b_all
    in_chunk = lax.broadcasted_iota(jnp.int32, (nh, ML_TS), 1) % L
    run_all = g_all
    for sh in (1, 2, 4, 8, 16, 32, 64):
        run_all = jnp.maximum(run_all, jnp.where(in_chunk >= sh, pltpu.roll(run_all, sh, axis=1), NEG_BIG))

    def to_rows(row):
        return jnp.broadcast_to(row, (L, L)).T

    for c in range(ML_TS // L):
        r0 = c * L
        b_r, g_r, run = b_all[:, r0:r0 + L], g_all[:, r0:r0 + L], run_all[:, r0:r0 + L]
        run_rows = [to_rows(run[h:h + 1, :]) for h in range(nh)]
        b_rows = [to_rows(b_r[h:h + 1, :]) for h in range(nh)]
        m_prev = m_ref[...]
        b_last = b_r[:, L - 1:L]
        wk_r = b_last + g_r
        m_new = jnp.maximum(b_last + m_prev, jnp.max(wk_r, axis=1, keepdims=True))
        ws_r = jnp.exp(wk_r - m_new)
        wc_r = jnp.exp(b_last + m_prev - m_new)
        m_ref[...] = m_new
        outs = []
        for h in range(nh):
            big_m_t = jnp.maximum(run_rows[h], m_prev[h:h + 1, :])
            m_t = big_m_t + b_rows[h]
            decay = jnp.where(causal, jnp.exp(g_r[h:h + 1, :] - big_m_t), 0.0)
            w_int = jnp.exp(m_prev[h:h + 1, :] - big_m_t)
            qh = q_ref[h, r0:r0 + L, :]
            kt = kt_ref[h, :, r0:r0 + L]
            v_ext = jnp.concatenate(
                [mv_ref[r0:r0 + L, h * dv:(h + 1) * dv], ones_v], axis=1)
            wt = _dot(qh, kt.astype(BF16)) * decay
            nd = (_dot(wt.astype(BF16), v_ext)
                  + jnp.concatenate([w_int, w_int], axis=1) * _dot(qh, s_ref[h].astype(BF16)))
            hv = nd[:, 0:dv] / jnp.maximum(jnp.abs(nd[:, dv:2 * dv]), jnp.exp(-m_t))
            ms = _dot((hv * hv).astype(BF16), ones_n) * (1.0 / dv)
            outs.append(hv * lax.rsqrt(ms + EPS) * hg_ref[0:1, h * dv:(h + 1) * dv])
            wc = jnp.broadcast_to(wc_r[h:h + 1, :], (dk, L))
            s_ref[h] = (jnp.concatenate([wc, wc], axis=1) * s_ref[h]
                        + _dot((kt * ws_r[h:h + 1, :]).astype(BF16), v_ext))
        hh = jnp.concatenate(outs, axis=1)
        o_ref[r0:r0 + L, :] = (hh * jax.nn.sigmoid(mo_ref[r0:r0 + L, :].astype(F32))).astype(o_ref.dtype)


def _mlstm(m3, mvo3, gt, conv_w, conv_b, wq, wkt, brow, head_g):
    b, s, _ = m3.shape
    nh = MLSTM_HEADS
    tile = lambda c: pl.BlockSpec((None, ML_TS, MIX_B), lambda bb, t: (bb, t, c))
    const2 = lambda shape: pl.BlockSpec(shape, lambda bb, t: (0, 0))
    const3 = lambda shape: pl.BlockSpec(shape, lambda bb, t: (0, 0, 0))
    return pl.pallas_call(
        _mlstm_kernel,
        grid=(b, s // ML_TS),
        in_specs=[
            tile(0), tile(0), tile(1),
            pl.BlockSpec((None, 2 * nh, ML_TS), lambda bb, t: (bb, 0, t)),
            const2((CONV_WIDTH, MIX_B)), const2((1, MIX_B)),
            const3((nh, MLSTM_V_DIM, MLSTM_QK_DIM)), const3((nh, MLSTM_QK_DIM, MLSTM_V_DIM)),
            const2((2 * nh, 1)), const2((1, MIX_B)),
        ],
        out_specs=pl.BlockSpec((None, ML_TS, MIX_B), lambda bb, t: (bb, t, 0)),
        out_shape=jax.ShapeDtypeStruct((b, s, MIX_B), BF16),
        scratch_shapes=[
            pltpu.VMEM((ML_PAD + ML_TS, MIX_B), F32),
            pltpu.VMEM((nh, ML_TS, MLSTM_QK_DIM), BF16),
            pltpu.VMEM((nh, MLSTM_QK_DIM, ML_TS), F32),
            pltpu.VMEM((nh, MLSTM_QK_DIM, 2 * MLSTM_V_DIM), F32),
            pltpu.VMEM((nh, ML_L), F32),
        ],
        compiler_params=pltpu.CompilerParams(
            dimension_semantics=("parallel", "arbitrary"), vmem_limit_bytes=VMEM_LIMIT),
        name="mlstm",
    )(m3, mvo3, mvo3, gt, conv_w, conv_b, wq, wkt, brow, head_g)


MERGE_TM = 512


def _merge_ffn_kernel(x_ref, ya_ref, yb_ref, ga_ref, gb_ref, wa_ref, wb_ref, wo_ref,
                      g_ref, wg_ref, wu_ref, wd_ref, o_ref):
    merged = (jax.nn.sigmoid(ga_ref[...].astype(F32)) * _dot(ya_ref[...], wa_ref[...])
              + jax.nn.sigmoid(gb_ref[...].astype(F32)) * _dot(yb_ref[...], wb_ref[...]))
    x = x_ref[...] + _dot(merged.astype(BF16), wo_ref[...])
    _half_ffn(x, g_ref, wg_ref, wu_ref, wd_ref, o_ref)


def _merge_ffn(x2d, ya, yb, gates, wa, wb, wo, g, wg, wu, wd):
    m = x2d.shape[0]
    row = lambda i: (i, 0)
    const = lambda i: (0, 0)
    resident = lambda shape: pl.BlockSpec(shape, const, pipeline_mode=pl.Buffered(1))
    return pl.pallas_call(
        _merge_ffn_kernel,
        grid=(m // MERGE_TM,),
        in_specs=[
            pl.BlockSpec((MERGE_TM, D_MODEL), row),
            pl.BlockSpec((MERGE_TM, MIX_A), row),
            pl.BlockSpec((MERGE_TM, MIX_B), row),
            pl.BlockSpec((MERGE_TM, D_MODEL), lambda i: (i, 0)),
            pl.BlockSpec((MERGE_TM, D_MODEL), lambda i: (i, 1)),
            resident((MIX_A, D_MODEL)),
            resident((MIX_B, D_MODEL)),
            resident((D_MODEL, D_MODEL)),
            pl.BlockSpec((1, D_MODEL), const),
            resident((D_MODEL, D_FF)),
            resident((D_MODEL, D_FF)),
            resident((D_FF, D_MODEL)),
        ],
        out_specs=pl.BlockSpec((MERGE_TM, D_MODEL), row),
        out_shape=jax.ShapeDtypeStruct((m, D_MODEL), F32),
        compiler_params=pltpu.CompilerParams(
            dimension_semantics=("parallel",), vmem_limit_bytes=VMEM_LIMIT),
        name="merge_ffn",
    )(x2d, ya, yb, gates, gates, wa, wb, wo, g.reshape(1, D_MODEL), wg, wu, wd)


def _block_diag_ones(n, blk):
    r = jnp.arange(n) // blk
    return (r[:, None] == r[None, :]).astype(BF16)


def _arrange_w_in(w_in):
    widths = (512, 128, 128, 512, 64, 8, 512, 512, 4, 4, 512, 1024, 1024)
    parts, off = [], 0
    for wd in widths:
        parts.append(w_in[:, off:off + wd])
        off += wd
    aq, ak, av, iq, ik, iw, mx, mv, mi, mf, mo, ga, gb = parts
    z = lambda n: jnp.zeros((w_in.shape[0], n), w_in.dtype)
    cols = [aq, iq, ak[:, :64], z(64), ak[:, 64:], z(64), ik, z(64), av,
            iw, mi, mf, z(G_WIDTH - 16),
            mx, mv, mo, ga, gb]
    return jnp.concatenate(cols, axis=1).astype(BF16)


def kernel(x, ffn1_norm, ffn1_w_gate, ffn1_w_up, ffn1_w_down, mix_norm, w_in, q_norm, k_norm,
           idx_k_norm, conv_w, conv_b, w_mq, w_mk, b_i, b_f, m_head_norm, w_proj_a, w_proj_b,
           w_out, ffn2_norm, ffn2_w_gate, ffn2_w_up, ffn2_w_down):
    b, s, _ = x.shape
    m = b * s
    nb = s // Q_BLOCK
    nh = MLSTM_HEADS
    x2d = x.reshape(m, D_MODEL)
    for l in range(ffn1_norm.shape[0]):
        x2d = _ffn(x2d, ffn1_norm[l], ffn1_w_gate[l].astype(BF16), ffn1_w_up[l].astype(BF16),
                   ffn1_w_down[l].astype(BF16))

        gq = (jnp.tile(q_norm[l], ATT_HEADS) * (ATT_HEAD_DIM ** -0.5 * LOG2_E)).reshape(1, 512)
        z64 = jnp.zeros((ATT_HEAD_DIM,), F32)
        one64 = z64.at[0].set(1.0)
        gk = jnp.concatenate([k_norm[l], z64] * ATT_KV_HEADS + [idx_k_norm[l], z64]).reshape(1, -1)
        kone = jnp.concatenate([z64, one64] * ATT_KV_HEADS + [z64, z64]).reshape(1, -1)
        a2d, g2d, mx2d, mvo2d, t2d = _proj(x2d, mix_norm[l], _arrange_w_in(w_in[l]),
                                   _block_diag_ones(512, ATT_HEAD_DIM),
                                   _block_diag_ones(COL_V - COL_K, ATT_HEAD_DIM), gq, gk, kone)

        a3 = a2d.reshape(b, s, A_WIDTH)
        vt = _v_transposed(a3[:, :, COL_V:COL_V + ATT_KV_HEADS * ATT_HEAD_DIM])
        g3 = g2d.reshape(b, s, G_WIDTH)
        iwt = jnp.swapaxes(g3[:, :, 0:IDX_HEADS].reshape(b, nb, Q_BLOCK, IDX_HEADS), 2, 3)
        qcap = (jnp.max(jnp.abs(gq)) * (ATT_HEAD_DIM ** 0.5) * BOUND_SLACK).reshape(1)
        ya = _dsa(a3, vt, iwt, qcap)

        gt = jnp.swapaxes(g3[:, :, IDX_HEADS:IDX_HEADS + 2 * nh], 1, 2)
        bias = jnp.concatenate([b_i[l], b_f[l]])
        yb = _mlstm(mx2d.reshape(b, s, MIX_B), mvo2d.reshape(b, s, 2 * MIX_B), gt,
                    conv_w[l], conv_b[l].reshape(1, MIX_B),
                    w_mq[l].astype(BF16), jnp.swapaxes(w_mk[l], 1, 2).astype(BF16),
                    bias.reshape(2 * nh, 1), m_head_norm[l].reshape(1, MIX_B))

        x2d = _merge_ffn(x2d, ya.reshape(m, MIX_A), yb.reshape(m, MIX_B), t2d,
                         w_proj_a[l].astype(BF16), w_proj_b[l].astype(BF16), w_out[l].astype(BF16),
                         ffn2_norm[l], ffn2_w_gate[l].astype(BF16), ffn2_w_up[l].astype(BF16),
                         ffn2_w_down[l].astype(BF16))
    return x2d.reshape(b, s, D_MODEL)
```

```python
import functools

import jax
import jax.numpy as jnp
from jax import lax
from jax.experimental import pallas as pl
from jax.experimental.pallas import tpu as pltpu

F32 = jnp.float32
BF16 = jnp.bfloat16

D_MODEL = 1024
D_FF = 2816
EPS = 1e-6
ATT_HEADS = 8
ATT_KV_HEADS = 2
ATT_HEAD_DIM = 64
IDX_HEADS = 8
IDX_HEAD_DIM = 64
TOPK_MAX = 256
Q_BLOCK = 128
CHUNK = 64
MLSTM_HEADS = 4
MLSTM_QK_DIM = 64
MLSTM_V_DIM = 128
CONV_WIDTH = 4
MIX_A = ATT_HEADS * ATT_HEAD_DIM
MIX_B = MLSTM_HEADS * MLSTM_V_DIM

LANES = 128
VMEM_LIMIT = 56 * 1024 * 1024
INT_MIN = -2147483648
NEG_BIG = -1e30
LOG2_E = 1.4426950408889634

COL_Q, COL_IQ, COL_K, COL_IK, COL_V = 0, 512, 1024, 1280, 1408
K_EXT = 128
A_WIDTH = 1536
G_WIDTH = 128
M_WIDTH = 3 * MIX_B
T_WIDTH = 2 * D_MODEL
W_TOTAL = A_WIDTH + G_WIDTH + M_WIDTH + T_WIDTH


def _rms(x, g):
    return x * lax.rsqrt(jnp.mean(x * x, axis=-1, keepdims=True) + EPS) * g


def _dot(a, b):
    return jnp.dot(a, b, preferred_element_type=F32)


def _dot_nt(a, b):
    return lax.dot_general(a, b, (((1,), (1,)), ((), ())), preferred_element_type=F32)


def _dot_tn(a, b):
    return lax.dot_general(a, b, (((0,), (0,)), ((), ())), preferred_element_type=F32)


FFN_TM = 512
FFN_TF = 2816


def _ffn_kernel(x_ref, g_ref, wg_ref, wu_ref, wd_ref, o_ref):
    _half_ffn(x_ref[...], g_ref, wg_ref, wu_ref, wd_ref, o_ref)


def _half_ffn(x, g_ref, wg_ref, wu_ref, wd_ref, o_ref):
    h = _rms(x, g_ref[...]).astype(BF16)
    acc = None
    for c in range(D_FF // FFN_TF):
        cols = slice(c * FFN_TF, (c + 1) * FFN_TF)
        a = _dot(h, wg_ref[:, cols])
        u = _dot(h, wu_ref[:, cols])
        act = (a * jax.nn.sigmoid(a) * u).astype(BF16)
        part = _dot(act, wd_ref[cols, :])
        acc = part if acc is None else acc + part
    o_ref[...] = x + 0.5 * acc


def _ffn(x2d, g, wg, wu, wd):
    m = x2d.shape[0]
    const = lambda i: (0, 0)
    resident = lambda shape: pl.BlockSpec(shape, const, pipeline_mode=pl.Buffered(1))
    return pl.pallas_call(
        _ffn_kernel,
        grid=(m // FFN_TM,),
        in_specs=[
            pl.BlockSpec((FFN_TM, D_MODEL), lambda i: (i, 0)),
            pl.BlockSpec((1, D_MODEL), const),
            resident((D_MODEL, D_FF)),
            resident((D_MODEL, D_FF)),
            resident((D_FF, D_MODEL)),
        ],
        out_specs=pl.BlockSpec((FFN_TM, D_MODEL), lambda i: (i, 0)),
        out_shape=jax.ShapeDtypeStruct((m, D_MODEL), F32),
        compiler_params=pltpu.CompilerParams(
            dimension_semantics=("parallel",), vmem_limit_bytes=VMEM_LIMIT),
        name="ffn",
    )(x2d, g.reshape(1, D_MODEL), wg, wu, wd)


PROJ_TM = 512


def _proj_kernel(x_ref, g_ref, w_ref, bdq_ref, bdk_ref, gq_ref, gk_ref, kone_ref,
                 a_ref, g_out_ref, mx_ref, mvo_ref, t_ref):
    h = _rms(x_ref[...], g_ref[...]).astype(BF16)

    def head_norm(p, bd_ref, gain_ref):
        ss = _dot((p * p).astype(BF16), bd_ref[...]) * (1.0 / ATT_HEAD_DIM)
        return p * lax.rsqrt(ss + EPS) * gain_ref[...]

    pq = _dot(h, w_ref[:, COL_Q:COL_IQ])
    a_ref[:, COL_Q:COL_IQ] = head_norm(pq, bdq_ref, gq_ref).astype(BF16)
    a_ref[:, COL_IQ:COL_K] = _dot(h, w_ref[:, COL_IQ:COL_K]).astype(BF16)
    pk = _dot(h, w_ref[:, COL_K:COL_V])
    a_ref[:, COL_K:COL_V] = (head_norm(pk, bdk_ref, gk_ref) + kone_ref[...]).astype(BF16)
    a_ref[:, COL_V:A_WIDTH] = _dot(h, w_ref[:, COL_V:A_WIDTH]).astype(BF16)
    o = A_WIDTH
    g_out_ref[...] = _dot(h, w_ref[:, o:o + G_WIDTH])
    o += G_WIDTH
    mx_ref[...] = _dot(h, w_ref[:, o:o + MIX_B])
    o += MIX_B
    for c in range(2):
        mvo_ref[:, c * MIX_B:(c + 1) * MIX_B] = _dot(h, w_ref[:, o + c * MIX_B:o + (c + 1) * MIX_B]).astype(BF16)
    o += 2 * MIX_B
    for c in range(2):
        t_ref[:, c * D_MODEL:(c + 1) * D_MODEL] = _dot(
            h, w_ref[:, o + c * D_MODEL:o + (c + 1) * D_MODEL]).astype(BF16)


def _proj(x2d, g, w_all, bdq, bdk, gq, gk, kone):
    m = x2d.shape[0]
    const = lambda i: (0, 0)
    row = lambda i: (i, 0)
    return pl.pallas_call(
        _proj_kernel,
        grid=(m // PROJ_TM,),
        in_specs=[
            pl.BlockSpec((PROJ_TM, D_MODEL), row),
            pl.BlockSpec((1, D_MODEL), const),
            pl.BlockSpec((D_MODEL, W_TOTAL), const, pipeline_mode=pl.Buffered(1)),
            pl.BlockSpec((512, 512), const),
            pl.BlockSpec((COL_V - COL_K, COL_V - COL_K), const),
            pl.BlockSpec((1, 512), const),
            pl.BlockSpec((1, COL_V - COL_K), const),
            pl.BlockSpec((1, COL_V - COL_K), const),
        ],
        out_specs=[
            pl.BlockSpec((PROJ_TM, A_WIDTH), row),
            pl.BlockSpec((PROJ_TM, G_WIDTH), row),
            pl.BlockSpec((PROJ_TM, MIX_B), row),
            pl.BlockSpec((PROJ_TM, 2 * MIX_B), row),
            pl.BlockSpec((PROJ_TM, T_WIDTH), row),
        ],
        out_shape=[
            jax.ShapeDtypeStruct((m, A_WIDTH), BF16),
            jax.ShapeDtypeStruct((m, G_WIDTH), F32),
            jax.ShapeDtypeStruct((m, MIX_B), F32),
            jax.ShapeDtypeStruct((m, 2 * MIX_B), BF16),
            jax.ShapeDtypeStruct((m, T_WIDTH), BF16),
        ],
        compiler_params=pltpu.CompilerParams(
            dimension_semantics=("parallel",), vmem_limit_bytes=VMEM_LIMIT),
        name="proj",
    )(x2d, g.reshape(1, D_MODEL), w_all, bdq, bdk, gq, gk, kone)


DSA_KB = 512
DSA_KA = 256
BOUND_LIMIT = 60.0
BOUND_SLACK = 1.01
DSA_VROWS = 80

BIT_GROUP = 256
SEARCH_CHUNK = 1024
PLANE_PAD_ROWS = 8


def _bit_transpose(words):
    a = list(words)
    j, m = 16, 0x0000FFFF
    while j:
        k = 0
        while k < 32:
            t = (a[k] ^ (a[k + j] >> j)) & m
            a[k] = a[k] ^ t
            a[k + j] = a[k + j] ^ (t << j)
            k = (k + j + 1) & ~j
        j >>= 1
        m = (m ^ (m << j)) & 0xFFFFFFFF
    return a


def _heads_to_lanes(x, n_heads, dh):
    xf = x.astype(F32)
    return jnp.concatenate([xf[:, h * dh:(h + 1) * dh].T for h in range(n_heads)], axis=1).astype(BF16)


def _dsa_kernel(qcap_ref, q_ref, iq_ref, k_ref, ik_ref, vt_ref, iwt_ref, o_ref,
                keys_ref, planes_ref, alive_ref, la_ref, lb_ref, acc_ref, sa_ref, sb_ref, thr_ref, stab_ref,
                *, seq, k_sel, idx_scale):
    kb_sz = DSA_KB
    n_blocks = seq // Q_BLOCK
    step = pl.program_id(1)
    has_cur = step < n_blocks
    has_prev = step >= 1
    cur = jnp.minimum(step, n_blocks - 1)
    prev = jnp.maximum(step - 1, 0)
    cur_buf = cur % 2
    prev_buf = prev % 2
    nkb = (cur * Q_BLOCK + Q_BLOCK + kb_sz - 1) // kb_sz
    nkb_prev = (prev * Q_BLOCK + Q_BLOCK + kb_sz - 1) // kb_sz
    lane = lax.broadcasted_iota(jnp.int32, (1, LANES), 1)
    limit = cur * Q_BLOCK + jnp.where(lane < CHUNK, CHUNK, 2 * CHUNK)
    row_iota = lax.broadcasted_iota(jnp.int32, (kb_sz, LANES), 0)

    iq_t = _heads_to_lanes(iq_ref[...], IDX_HEADS, IDX_HEAD_DIM)
    w = iwt_ref[...]

    def logits_of(blk):
        off = pl.multiple_of(blk * kb_sz, kb_sz)
        return _dot(ik_ref[pl.ds(off, kb_sz), :][:, :IDX_HEAD_DIM], iq_t)

    def keys_from(l_ref, blk, masked):
        off = pl.multiple_of(blk * kb_sz, kb_sz)
        sc = jnp.zeros((kb_sz, LANES), F32)
        for h in range(IDX_HEADS):
            sc = sc + jnp.maximum(l_ref[:, h * LANES:(h + 1) * LANES], 0.0) * w[h:h + 1, :]
        sc = sc * idx_scale
        bits = pltpu.bitcast(sc, jnp.int32)
        key = bits ^ ((bits >> 31) & 0x7FFFFFFF)
        key = jnp.where(sc == 0.0, 0, key)
        if masked:
            key = jnp.where(off + row_iota < limit, key, INT_MIN)
        keys_ref[cur_buf, pl.ds(off, kb_sz), :] = key
        ukey = key ^ INT_MIN
        for grp in range(kb_sz // BIT_GROUP):
            base = grp * BIT_GROUP
            planes = _bit_transpose([ukey[base + 8 * i:base + 8 * i + 8, :] for i in range(32)])
            row = pl.multiple_of((off + base) // 32, 8)
            for p in range(32):
                planes_ref[p, pl.ds(row, 8), :] = planes[p]

    n_pairs = (nkb + 1) // 2

    def score_pair(j, masked):
        lb_ref[...] = logits_of(2 * j + 1)
        keys_from(la_ref, 2 * j, masked)
        la_ref[...] = logits_of(jnp.minimum(2 * j + 2, 2 * n_pairs - 1))
        keys_from(lb_ref, 2 * j + 1, masked)

    def last_pair():
        j = n_pairs - 1

        @pl.when(nkb % 2 == 0)
        def _():
            score_pair(j, True)

        @pl.when(nkb % 2 == 1)
        def _():
            keys_from(la_ref, 2 * j, True)
            off = pl.multiple_of((2 * j + 1) * kb_sz, kb_sz)
            keys_ref[cur_buf, pl.ds(off, kb_sz), :] = jnp.full((kb_sz, LANES), INT_MIN, jnp.int32)
            row = pl.multiple_of(off // 32, kb_sz // 32)
            for p in range(32):
                planes_ref[p, pl.ds(row, kb_sz // 32), :] = jnp.zeros((kb_sz // 32, LANES), jnp.int32)

    def count(pred):
        def body(kb, c):
            off = pl.multiple_of(kb * kb_sz, kb_sz)
            ind = jnp.where(pred(keys_ref[cur_buf, pl.ds(off, kb_sz), :], off + row_iota), 1, 0)
            return c + jnp.sum(ind.reshape(kb_sz // 8, 8, LANES), axis=0)
        c8 = lax.fori_loop(0, nkb, body, jnp.zeros((8, LANES), jnp.int32))
        return jnp.sum(c8, axis=0, keepdims=True)

    def select_threshold():
        zeros = jnp.zeros((1, LANES), jnp.int32)
        ch_rows = SEARCH_CHUNK // 32
        n_ch = (nkb + 1) // 2
        zeros8 = jnp.zeros((8, LANES), jnp.int32)

        def ones_in(x):
            return jnp.sum(lax.population_count(x).reshape(ch_rows // 8, 8, LANES), axis=0)

        def over_chunks(body, n_out):
            tot = lax.fori_loop(0, n_ch, lambda c, cnt: tuple(a + b_ for a, b_ in zip(cnt, body(c))),
                                (zeros8,) * n_out)
            return [jnp.sum(t, axis=0, keepdims=True) for t in tot]

        def plane(p, row):
            return planes_ref[p, pl.ds(row, ch_rows), :]

        def digit_counts(alive, p_hi, p_lo):
            x1 = alive & p_hi
            return ones_in(x1), ones_in(x1 & p_lo), ones_in((alive ^ x1) & p_lo)

        def decide(q, t_u, above, counts):
            c1, c11, c01 = counts
            hi = above + c1 >= k_sel
            lo = jnp.where(hi, above + c11, above + c1 + c01) >= k_sel
            above = jnp.where(hi, jnp.where(lo, above, above + c11),
                              jnp.where(lo, above + c1, above + c1 + c01))
            t_u = (t_u | jnp.where(hi, jnp.left_shift(jnp.int32(1), 31 - 2 * q), 0)
                   | jnp.where(lo, jnp.left_shift(jnp.int32(1), 30 - 2 * q), 0))
            return t_u, above, jnp.where(hi, 0, -1), jnp.where(lo, 0, -1)

        def first_chunk(c):
            row = pl.multiple_of(c * ch_rows, ch_rows)
            everything = jnp.full((ch_rows, LANES), -1, jnp.int32)
            alive_ref[pl.ds(row, ch_rows), :] = everything
            return digit_counts(everything, plane(0, row), plane(1, row))

        def radix_pass(q, carry):
            t_u, above, flip_hi, flip_lo = carry

            def chunk(c):
                row = pl.multiple_of(c * ch_rows, ch_rows)
                alive = (alive_ref[pl.ds(row, ch_rows), :]
                         & (plane(2 * q - 2, row) ^ flip_hi) & (plane(2 * q - 1, row) ^ flip_lo))
                alive_ref[pl.ds(row, ch_rows), :] = alive
                return digit_counts(alive, plane(2 * q, row), plane(2 * q + 1, row))

            return decide(q, t_u, above, over_chunks(chunk, 3))

        carry = decide(0, zeros, zeros, over_chunks(first_chunk, 3))
        t_u, above, flip_hi, flip_lo = lax.fori_loop(1, 16, radix_pass, carry)

        def last_chunk(c):
            row = pl.multiple_of(c * ch_rows, ch_rows)
            return (ones_in(alive_ref[pl.ds(row, ch_rows), :]
                            & (plane(30, row) ^ flip_hi) & (plane(31, row) ^ flip_lo)),)

        c_eq = over_chunks(last_chunk, 1)[0]
        few = t_u == 0
        thr = jnp.maximum(t_u ^ INT_MIN, INT_MIN + 1)
        thr_ref[cur_buf] = thr
        need = k_sel - above
        excess = jnp.where(few, 0, c_eq - need)
        pos_bits = seq.bit_length() - 1

        @pl.when(jnp.max(excess) > 0)
        def _():
            def jbit(i, cut):
                cand = cut | jnp.left_shift(jnp.int32(1), pos_bits - 1 - i)
                c = count(lambda kk, pos: (kk == thr) & (pos < cand))
                return jnp.where(c <= need, cand, cut)
            cut = lax.fori_loop(0, pos_bits, jbit, zeros)
            cut = jnp.where(excess > 0, cut, seq)

            def drop_block(kb, _):
                off = pl.multiple_of(kb * kb_sz, kb_sz)
                kk = keys_ref[cur_buf, pl.ds(off, kb_sz), :]
                keys_ref[cur_buf, pl.ds(off, kb_sz), :] = jnp.where(
                    (kk == thr) & (off + row_iota >= cut), INT_MIN, kk)
                return 0

            lax.fori_loop(0, nkb, drop_block, 0)

    reps = ATT_HEADS // ATT_KV_HEADS
    width = reps * LANES
    q_t = _heads_to_lanes(q_ref[...], ATT_HEADS, ATT_HEAD_DIM)

    @pl.when(step == 0)
    def _():
        def kmax_block(kb, best):
            off = pl.multiple_of(kb * kb_sz, kb_sz)
            kf = k_ref[pl.ds(off, kb_sz), :].astype(F32)
            n2 = [jnp.max(jnp.sum(jnp.square(kf[:, g * K_EXT:g * K_EXT + ATT_HEAD_DIM]), axis=1, keepdims=True))
                  for g in range(ATT_KV_HEADS)]
            return tuple(jnp.maximum(b_, n_) for b_, n_ in zip(best, n2))
        best = lax.fori_loop(0, seq // kb_sz, kmax_block, (jnp.float32(0.0),) * ATT_KV_HEADS)
        worst = jnp.float32(0.0)
        for g in range(ATT_KV_HEADS):
            stab_ref[g] = jnp.sqrt(best[g]) * BOUND_SLACK
            worst = jnp.maximum(worst, stab_ref[g] * qcap_ref[0])
        stab_ref[ATT_KV_HEADS] = jnp.where(worst < BOUND_LIMIT, 1.0, 0.0)

    bounded = stab_ref[ATT_KV_HEADS] > 0.5
    qf = q_t.astype(F32)
    q_len = jnp.sqrt(jnp.sum(qf * qf, axis=0, keepdims=True))
    rhs = []
    for g in range(ATT_KV_HEADS):
        cols = slice(g * width, (g + 1) * width)
        neg_bound = jnp.where(bounded, -q_len[:, cols] * stab_ref[g], 0.0)
        rhs.append(jnp.concatenate(
            [qf[:, cols], neg_bound, jnp.zeros((K_EXT - ATT_HEAD_DIM - 1, width), F32)], axis=0).astype(BF16))

    ka = DSA_KA

    def masked_scores(blk):
        off = pl.multiple_of(blk * ka, ka)
        bias = jnp.where(keys_ref[prev_buf, pl.ds(off, ka), :] >= thr_prev, 0.0, NEG_BIG)
        bias = jnp.concatenate([bias] * reps, axis=1)
        kblk = k_ref[pl.ds(off, ka), :]
        return jnp.concatenate(
            [_dot(kblk[:, g * K_EXT:(g + 1) * K_EXT], rhs[g]) + bias for g in range(ATT_KV_HEADS)],
            axis=1)

    def consume_bounded(s_ref, blk):
        off = pl.multiple_of(blk * ka, ka)
        p = jnp.exp2(s_ref[...]).astype(BF16)
        for g in range(ATT_KV_HEADS):
            acc_ref[g] += _dot(vt_ref[g, :, pl.ds(off, ka)], p[:, g * width:(g + 1) * width])

    def consume_online(s_ref, blk, m_old):
        off = pl.multiple_of(blk * ka, ka)
        s = s_ref[...]
        m_new = jnp.maximum(m_old, jnp.max(s, axis=0, keepdims=True))
        alpha = jnp.exp2(m_old - m_new)
        p = jnp.exp2(s - m_new).astype(BF16)
        for g in range(ATT_KV_HEADS):
            vt = vt_ref[g, :, pl.ds(off, ka)]
            acc_ref[g] = (alpha[:, g * width:(g + 1) * width] * acc_ref[g]
                          + _dot(vt, p[:, g * width:(g + 1) * width]))
        return m_new

    n_sub = nkb_prev * (kb_sz // ka)
    thr_prev = thr_ref[prev_buf]

    def att_pair_bounded(j):
        sb_ref[...] = masked_scores(2 * j + 1)
        consume_bounded(sa_ref, 2 * j)
        sa_ref[...] = masked_scores(jnp.minimum(2 * j + 2, n_sub - 1))
        consume_bounded(sb_ref, 2 * j + 1)

    @pl.when(has_cur)
    def _():
        la_ref[...] = logits_of(0)

    @pl.when(has_prev)
    def _():
        acc_ref[...] = jnp.zeros_like(acc_ref)
        sa_ref[...] = masked_scores(0)

    n_fused = jnp.where(has_cur & has_prev & bounded, n_pairs - 1, 0)

    def fused(j, c):
        score_pair(j, False)
        att_pair_bounded(2 * j)
        att_pair_bounded(2 * j + 1)
        return c

    lax.fori_loop(0, n_fused, fused, 0)

    @pl.when(has_cur)
    def _():
        lax.fori_loop(n_fused, n_pairs - 1, lambda j, c: (score_pair(j, False), c)[1], 0)
        last_pair()
        select_threshold()

    @pl.when(has_prev & bounded)
    def _():
        lax.fori_loop(2 * n_fused, n_sub // 2, lambda j, c: (att_pair_bounded(j), c)[1], 0)

    @pl.when(has_prev & jnp.logical_not(bounded))
    def _():
        def att_pair(j, m_run):
            sb_ref[...] = masked_scores(2 * j + 1)
            m_run = consume_online(sa_ref, 2 * j, m_run)
            sa_ref[...] = masked_scores(jnp.minimum(2 * j + 2, n_sub - 1))
            return consume_online(sb_ref, 2 * j + 1, m_run)
        lax.fori_loop(0, n_sub // 2, att_pair, jnp.full((1, ATT_HEADS * LANES), NEG_BIG, F32))

    @pl.when(has_prev)
    def _():
        pieces = []
        for g in range(ATT_KV_HEADS):
            a = acc_ref[g]
            o = a[0:ATT_HEAD_DIM, :] / a[ATT_HEAD_DIM:ATT_HEAD_DIM + 1, :]
            pieces += [o[:, r * LANES:(r + 1) * LANES] for r in range(reps)]
        o_ref[...] = jnp.concatenate(pieces, axis=0).T.astype(o_ref.dtype)


def _v_transposed(v):
    b, s, _ = v.shape
    vt = jnp.swapaxes(v.reshape(b, s, ATT_KV_HEADS, ATT_HEAD_DIM), 1, 3)
    vt = jnp.swapaxes(vt, 1, 2)
    ones = jnp.ones((b, ATT_KV_HEADS, 1, s), v.dtype)
    pad = jnp.zeros((b, ATT_KV_HEADS, DSA_VROWS - ATT_HEAD_DIM - 1, s), v.dtype)
    return jnp.concatenate([vt, ones, pad], axis=2)


def _dsa(a3, vt, iwt, qcap):
    b, s, _ = a3.shape
    nb = s // Q_BLOCK
    k_sel = min(TOPK_MAX, s // 4)
    idx_scale = (IDX_HEAD_DIM ** -0.5) * (IDX_HEADS ** -0.5)
    kern = functools.partial(_dsa_kernel, seq=s, k_sel=k_sel, idx_scale=idx_scale)
    return pl.pallas_call(
        kern,
        grid=(b, nb + 1),
        in_specs=[
            pl.BlockSpec(memory_space=pltpu.SMEM),
            pl.BlockSpec((None, Q_BLOCK, 512), lambda bb, i: (bb, jnp.maximum(i - 1, 0), COL_Q // 512)),
            pl.BlockSpec((None, Q_BLOCK, 512), lambda bb, i: (bb, jnp.minimum(i, nb - 1), COL_IQ // 512)),
            pl.BlockSpec((None, s, ATT_KV_HEADS * K_EXT), lambda bb, i: (bb, 0, COL_K // (ATT_KV_HEADS * K_EXT))),
            pl.BlockSpec((None, s, LANES), lambda bb, i: (bb, 0, COL_IK // LANES)),
            pl.BlockSpec((None, ATT_KV_HEADS, DSA_VROWS, s), lambda bb, i: (bb, 0, 0, 0)),
            pl.BlockSpec((None, None, IDX_HEADS, LANES), lambda bb, i: (bb, jnp.minimum(i, nb - 1), 0, 0)),
        ],
        out_specs=pl.BlockSpec((None, Q_BLOCK, MIX_A), lambda bb, i: (bb, jnp.maximum(i - 1, 0), 0)),
        out_shape=jax.ShapeDtypeStruct((b, s, MIX_A), BF16),
        scratch_shapes=[
            pltpu.VMEM((2, s, LANES), jnp.int32),
            pltpu.VMEM((32, s // 32 + PLANE_PAD_ROWS, LANES), jnp.int32),
            pltpu.VMEM((s // 32, LANES), jnp.int32),
            pltpu.VMEM((DSA_KB, IDX_HEADS * LANES), F32),
            pltpu.VMEM((DSA_KB, IDX_HEADS * LANES), F32),
            pltpu.VMEM((ATT_KV_HEADS, DSA_VROWS, (ATT_HEADS // ATT_KV_HEADS) * LANES), F32),
            pltpu.VMEM((DSA_KA, ATT_HEADS * LANES), F32),
            pltpu.VMEM((DSA_KA, ATT_HEADS * LANES), F32),
            pltpu.VMEM((2, 1, LANES), jnp.int32),
            pltpu.SMEM((ATT_KV_HEADS + 1,), F32),
        ],
        compiler_params=pltpu.CompilerParams(
            dimension_semantics=("parallel", "arbitrary"), vmem_limit_bytes=VMEM_LIMIT),
        name="dsa",
    )(qcap, a3, a3, a3, a3, vt, iwt)


ML_L = 128
ML_TS = 512
ML_PAD = 8


def _mlstm_kernel(mx_ref, mv_ref, mo_ref, gr_ref, cw_ref, cb_ref, wq_ref, wkt_ref, brow_ref, hg_ref, o_ref,
                  xbuf_ref, q_ref, kt_ref, s_ref, m_ref):
    t_idx = pl.program_id(1)
    nh, dk, dv, L = MLSTM_HEADS, MLSTM_QK_DIM, MLSTM_V_DIM, ML_L

    @pl.when(t_idx == 0)
    def _():
        xbuf_ref[0:ML_PAD, :] = jnp.zeros((ML_PAD, MIX_B), F32)
        s_ref[...] = jnp.zeros_like(s_ref)
        m_ref[...] = jnp.zeros_like(m_ref)

    xbuf_ref[ML_PAD:ML_PAD + ML_TS, :] = mx_ref[...]
    xc = cb_ref[...] + jnp.zeros((ML_TS, MIX_B), F32)
    for j in range(CONV_WIDTH):
        s0 = ML_PAD - (CONV_WIDTH - 1) + j
        xc = xc + xbuf_ref[s0:s0 + ML_TS, :] * cw_ref[j:j + 1, :]
    xbuf_ref[0:ML_PAD, :] = mx_ref[ML_TS - ML_PAD:ML_TS, :]
    xc = (xc * jax.nn.sigmoid(xc)).astype(BF16)
    for h in range(nh):
        xh = xc[:, h * dv:(h + 1) * dv]
        q_ref[h] = (_dot(xh, wq_ref[h]) * (dk ** -0.5)).astype(BF16)
        kt_ref[h] = _dot_nt(wkt_ref[h], xh)

    gr = gr_ref[...]
    li_r = gr[0:nh, :] + brow_ref[0:nh, :]
    lf_r = jax.nn.log_sigmoid(gr[nh:2 * nh, :] + brow_ref[nh:2 * nh, :])

    ri = lax.broadcasted_iota(jnp.int32, (L, L), 0)
    ci = lax.broadcasted_iota(jnp.int32, (L, L), 1)
    causal = ri >= ci
    ones_v = jnp.ones((L, dv), BF16)
    ones_n = jnp.ones((dv, dv), BF16)

    tr = lax.broadcasted_iota(jnp.int32, (ML_TS, ML_TS), 0)
    tc = lax.broadcasted_iota(jnp.int32, (ML_TS, ML_TS), 1)
    chunk_triu = jnp.where((tr <= tc) & (tr // L == tc // L), 1.0, 0.0).astype(F32)
    b_all = jnp.dot(lf_r, chunk_triu, precision=lax.Precision.HIGHEST, preferred_element_type=F32)
    g_all = li_r - b_all
    in_chunk = lax.broadcasted_iota(jnp.int32, (nh, ML_TS), 1) % L
    run_all = g_all
    for sh in (1, 2, 4, 8, 16, 32, 64):
        run_all = jnp.maximum(run_all, jnp.where(in_chunk >= sh, pltpu.roll(run_all, sh, axis=1), NEG_BIG))

    def to_rows(row):
        return jnp.broadcast_to(row, (L, L)).T

    for c in range(ML_TS // L):
        r0 = c * L
        b_r, g_r, run = b_all[:, r0:r0 + L], g_all[:, r0:r0 + L], run_all[:, r0:r0 + L]
        run_rows = [to_rows(run[h:h + 1, :]) for h in range(nh)]
        b_rows = [to_rows(b_r[h:h + 1, :]) for h in range(nh)]
        m_prev = m_ref[...]
        b_last = b_r[:, L - 1:L]
        wk_r = b_last + g_r
        m_new = jnp.maximum(b_last + m_prev, jnp.max(wk_r, axis=1, keepdims=True))
        ws_r = jnp.exp(wk_r - m_new)
        wc_r = jnp.exp(b_last + m_prev - m_new)
        m_ref[...] = m_new
        outs = []
        for h in range(nh):
            big_m_t = jnp.maximum(run_rows[h], m_prev[h:h + 1, :])
            m_t = big_m_t + b_rows[h]
            decay = jnp.where(causal, jnp.exp(g_r[h:h + 1, :] - big_m_t), 0.0)
            w_int = jnp.exp(m_prev[h:h + 1, :] - big_m_t)
            qh = q_ref[h, r0:r0 + L, :]
            kt = kt_ref[h, :, r0:r0 + L]
            v_ext = jnp.concatenate(
                [mv_ref[r0:r0 + L, h * dv:(h + 1) * dv], ones_v], axis=1)
            wt = _dot(qh, kt.astype(BF16)) * decay
            nd = (_dot(wt.astype(BF16), v_ext)
                  + jnp.concatenate([w_int, w_int], axis=1) * _dot(qh, s_ref[h].astype(BF16)))
            hv = nd[:, 0:dv] / jnp.maximum(jnp.abs(nd[:, dv:2 * dv]), jnp.exp(-m_t))
            ms = _dot((hv * hv).astype(BF16), ones_n) * (1.0 / dv)
            outs.append(hv * lax.rsqrt(ms + EPS) * hg_ref[0:1, h * dv:(h + 1) * dv])
            wc = jnp.broadcast_to(wc_r[h:h + 1, :], (dk, L))
            s_ref[h] = (jnp.concatenate([wc, wc], axis=1) * s_ref[h]
                        + _dot((kt * ws_r[h:h + 1, :]).astype(BF16), v_ext))
        hh = jnp.concatenate(outs, axis=1)
        o_ref[r0:r0 + L, :] = (hh * jax.nn.sigmoid(mo_ref[r0:r0 + L, :].astype(F32))).astype(o_ref.dtype)


def _mlstm(m3, mvo3, gt, conv_w, conv_b, wq, wkt, brow, head_g):
    b, s, _ = m3.shape
    nh = MLSTM_HEADS
    tile = lambda c: pl.BlockSpec((None, ML_TS, MIX_B), lambda bb, t: (bb, t, c))
    const2 = lambda shape: pl.BlockSpec(shape, lambda bb, t: (0, 0))
    const3 = lambda shape: pl.BlockSpec(shape, lambda bb, t: (0, 0, 0))
    return pl.pallas_call(
        _mlstm_kernel,
        grid=(b, s // ML_TS),
        in_specs=[
            tile(0), tile(0), tile(1),
            pl.BlockSpec((None, 2 * nh, ML_TS), lambda bb, t: (bb, 0, t)),
            const2((CONV_WIDTH, MIX_B)), const2((1, MIX_B)),
            const3((nh, MLSTM_V_DIM, MLSTM_QK_DIM)), const3((nh, MLSTM_QK_DIM, MLSTM_V_DIM)),
            const2((2 * nh, 1)), const2((1, MIX_B)),
        ],
        out_specs=pl.BlockSpec((None, ML_TS, MIX_B), lambda bb, t: (bb, t, 0)),
        out_shape=jax.ShapeDtypeStruct((b, s, MIX_B), BF16),
        scratch_shapes=[
            pltpu.VMEM((ML_PAD + ML_TS, MIX_B), F32),
            pltpu.VMEM((nh, ML_TS, MLSTM_QK_DIM), BF16),
            pltpu.VMEM((nh, MLSTM_QK_DIM, ML_TS), F32),
            pltpu.VMEM((nh, MLSTM_QK_DIM, 2 * MLSTM_V_DIM), F32),
            pltpu.VMEM((nh, ML_L), F32),
        ],
        compiler_params=pltpu.CompilerParams(
            dimension_semantics=("parallel", "arbitrary"), vmem_limit_bytes=VMEM_LIMIT),
        name="mlstm",
    )(m3, mvo3, mvo3, gt, conv_w, conv_b, wq, wkt, brow, head_g)


MERGE_TM = 512


def _merge_ffn_kernel(x_ref, ya_ref, yb_ref, ga_ref, gb_ref, wa_ref, wb_ref, wo_ref,
                      g_ref, wg_ref, wu_ref, wd_ref, o_ref):
    merged = (jax.nn.sigmoid(ga_ref[...].astype(F32)) * _dot(ya_ref[...], wa_ref[...])
              + jax.nn.sigmoid(gb_ref[...].astype(F32)) * _dot(yb_ref[...], wb_ref[...]))
    x = x_ref[...] + _dot(merged.astype(BF16), wo_ref[...])
    _half_ffn(x, g_ref, wg_ref, wu_ref, wd_ref, o_ref)


def _merge_ffn(x2d, ya, yb, gates, wa, wb, wo, g, wg, wu, wd):
    m = x2d.shape[0]
    row = lambda i: (i, 0)
    const = lambda i: (0, 0)
    resident = lambda shape: pl.BlockSpec(shape, const, pipeline_mode=pl.Buffered(1))
    return pl.pallas_call(
        _merge_ffn_kernel,
        grid=(m // MERGE_TM,),
        in_specs=[
            pl.BlockSpec((MERGE_TM, D_MODEL), row),
            pl.BlockSpec((MERGE_TM, MIX_A), row),
            pl.BlockSpec((MERGE_TM, MIX_B), row),
            pl.BlockSpec((MERGE_TM, D_MODEL), lambda i: (i, 0)),
            pl.BlockSpec((MERGE_TM, D_MODEL), lambda i: (i, 1)),
            resident((MIX_A, D_MODEL)),
            resident((MIX_B, D_MODEL)),
            resident((D_MODEL, D_MODEL)),
            pl.BlockSpec((1, D_MODEL), const),
            resident((D_MODEL, D_FF)),
            resident((D_MODEL, D_FF)),
            resident((D_FF, D_MODEL)),
        ],
        out_specs=pl.BlockSpec((MERGE_TM, D_MODEL), row),
        out_shape=jax.ShapeDtypeStruct((m, D_MODEL), F32),
        compiler_params=pltpu.CompilerParams(
            dimension_semantics=("parallel",), vmem_limit_bytes=VMEM_LIMIT),
        name="merge_ffn",
    )(x2d, ya, yb, gates, gates, wa, wb, wo, g.reshape(1, D_MODEL), wg, wu, wd)


def _block_diag_ones(n, blk):
    r = jnp.arange(n) // blk
    return (r[:, None] == r[None, :]).astype(BF16)


def _arrange_w_in(w_in):
    widths = (512, 128, 128, 512, 64, 8, 512, 512, 4, 4, 512, 1024, 1024)
    parts, off = [], 0
    for wd in widths:
        parts.append(w_in[:, off:off + wd])
        off += wd
    aq, ak, av, iq, ik, iw, mx, mv, mi, mf, mo, ga, gb = parts
    z = lambda n: jnp.zeros((w_in.shape[0], n), w_in.dtype)
    cols = [aq, iq, ak[:, :64], z(64), ak[:, 64:], z(64), ik, z(64), av,
            iw, mi, mf, z(G_WIDTH - 16),
            mx, mv, mo, ga, gb]
    return jnp.concatenate(cols, axis=1).astype(BF16)


def kernel(x, ffn1_norm, ffn1_w_gate, ffn1_w_up, ffn1_w_down, mix_norm, w_in, q_norm, k_norm,
           idx_k_norm, conv_w, conv_b, w_mq, w_mk, b_i, b_f, m_head_norm, w_proj_a, w_proj_b,
           w_out, ffn2_norm, ffn2_w_gate, ffn2_w_up, ffn2_w_down):
    b, s, _ = x.shape
    m = b * s
    nb = s // Q_BLOCK
    nh = MLSTM_HEADS
    x2d = x.reshape(m, D_MODEL)
    for l in range(ffn1_norm.shape[0]):
        x2d = _ffn(x2d, ffn1_norm[l], ffn1_w_gate[l].astype(BF16), ffn1_w_up[l].astype(BF16),
                   ffn1_w_down[l].astype(BF16))

        gq = (jnp.tile(q_norm[l], ATT_HEADS) * (ATT_HEAD_DIM ** -0.5 * LOG2_E)).reshape(1, 512)
        z64 = jnp.zeros((ATT_HEAD_DIM,), F32)
        one64 = z64.at[0].set(1.0)
        gk = jnp.concatenate([k_norm[l], z64] * ATT_KV_HEADS + [idx_k_norm[l], z64]).reshape(1, -1)
        kone = jnp.concatenate([z64, one64] * ATT_KV_HEADS + [z64, z64]).reshape(1, -1)
        a2d, g2d, mx2d, mvo2d, t2d = _proj(x2d, mix_norm[l], _arrange_w_in(w_in[l]),
                                   _block_diag_ones(512, ATT_HEAD_DIM),
                                   _block_diag_ones(COL_V - COL_K, ATT_HEAD_DIM), gq, gk, kone)

        a3 = a2d.reshape(b, s, A_WIDTH)
        vt = _v_transposed(a3[:, :, COL_V:COL_V + ATT_KV_HEADS * ATT_HEAD_DIM])
        g3 = g2d.reshape(b, s, G_WIDTH)
        iwt = jnp.swapaxes(g3[:, :, 0:IDX_HEADS].reshape(b, nb, Q_BLOCK, IDX_HEADS), 2, 3)
        qcap = (jnp.max(jnp.abs(gq)) * (ATT_HEAD_DIM ** 0.5) * BOUND_SLACK).reshape(1)
        ya = _dsa(a3, vt, iwt, qcap)

        gt = jnp.swapaxes(g3[:, :, IDX_HEADS:IDX_HEADS + 2 * nh], 1, 2)
        bias = jnp.concatenate([b_i[l], b_f[l]])
        yb = _mlstm(mx2d.reshape(b, s, MIX_B), mvo2d.reshape(b, s, 2 * MIX_B), gt,
                    conv_w[l], conv_b[l].reshape(1, MIX_B),
                    w_mq[l].astype(BF16), jnp.swapaxes(w_mk[l], 1, 2).astype(BF16),
                    bias.reshape(2 * nh, 1), m_head_norm[l].reshape(1, MIX_B))

        x2d = _merge_ffn(x2d, ya.reshape(m, MIX_A), yb.reshape(m, MIX_B), t2d,
                         w_proj_a[l].astype(BF16), w_proj_b[l].astype(BF16), w_out[l].astype(BF16),
                         ffn2_norm[l], ffn2_w_gate[l].astype(BF16), ffn2_w_up[l].astype(BF16),
                         ffn2_w_down[l].astype(BF16))
    return x2d.reshape(b, s, D_MODEL)
```

```python
import functools

import jax
import jax.numpy as jnp
from jax import lax
from jax.experimental import pallas as pl
from jax.experimental.pallas import tpu as pltpu

F32 = jnp.float32
BF16 = jnp.bfloat16

D_MODEL = 1024
D_FF = 2816
EPS = 1e-6
ATT_HEADS = 8
ATT_KV_HEADS = 2
ATT_HEAD_DIM = 64
IDX_HEADS = 8
IDX_HEAD_DIM = 64
TOPK_MAX = 256
Q_BLOCK = 128
CHUNK = 64
MLSTM_HEADS = 4
MLSTM_QK_DIM = 64
MLSTM_V_DIM = 128
CONV_WIDTH = 4
MIX_A = ATT_HEADS * ATT_HEAD_DIM
MIX_B = MLSTM_HEADS * MLSTM_V_DIM

LANES = 128
VMEM_LIMIT = 56 * 1024 * 1024
INT_MIN = -2147483648
NEG_BIG = -1e30
LOG2_E = 1.4426950408889634

COL_Q, COL_IQ, COL_K, COL_IK, COL_V = 0, 512, 1024, 1280, 1408
K_EXT = 128
A_WIDTH = 1536
G_WIDTH = 128
M_WIDTH = 3 * MIX_B
T_WIDTH = 2 * D_MODEL
W_TOTAL = A_WIDTH + G_WIDTH + M_WIDTH + T_WIDTH


def _rms(x, g):
    return x * lax.rsqrt(jnp.mean(x * x, axis=-1, keepdims=True) + EPS) * g


def _dot(a, b):
    return jnp.dot(a, b, preferred_element_type=F32)


def _dot_nt(a, b):
    return lax.dot_general(a, b, (((1,), (1,)), ((), ())), preferred_element_type=F32)


def _dot_tn(a, b):
    return lax.dot_general(a, b, (((0,), (0,)), ((), ())), preferred_element_type=F32)


FFN_TM = 512
FFN_TF = 2816


def _ffn_kernel(x_ref, g_ref, wg_ref, wu_ref, wd_ref, o_ref):
    _half_ffn(x_ref[...], g_ref, wg_ref, wu_ref, wd_ref, o_ref)


def _half_ffn(x, g_ref, wg_ref, wu_ref, wd_ref, o_ref):
    h = _rms(x, g_ref[...]).astype(BF16)
    acc = None
    for c in range(D_FF // FFN_TF):
        cols = slice(c * FFN_TF, (c + 1) * FFN_TF)
        a = _dot(h, wg_ref[:, cols])
        u = _dot(h, wu_ref[:, cols])
        act = (a * jax.nn.sigmoid(a) * u).astype(BF16)
        part = _dot(act, wd_ref[cols, :])
        acc = part if acc is None else acc + part
    o_ref[...] = x + 0.5 * acc


def _ffn(x2d, g, wg, wu, wd):
    m = x2d.shape[0]
    const = lambda i: (0, 0)
    resident = lambda shape: pl.BlockSpec(shape, const, pipeline_mode=pl.Buffered(1))
    return pl.pallas_call(
        _ffn_kernel,
        grid=(m // FFN_TM,),
        in_specs=[
            pl.BlockSpec((FFN_TM, D_MODEL), lambda i: (i, 0)),
            pl.BlockSpec((1, D_MODEL), const),
            resident((D_MODEL, D_FF)),
            resident((D_MODEL, D_FF)),
            resident((D_FF, D_MODEL)),
        ],
        out_specs=pl.BlockSpec((FFN_TM, D_MODEL), lambda i: (i, 0)),
        out_shape=jax.ShapeDtypeStruct((m, D_MODEL), F32),
        compiler_params=pltpu.CompilerParams(
            dimension_semantics=("parallel",), vmem_limit_bytes=VMEM_LIMIT),
        name="ffn",
    )(x2d, g.reshape(1, D_MODEL), wg, wu, wd)


PROJ_TM = 512


def _proj_kernel(x_ref, g_ref, w_ref, bdq_ref, bdk_ref, gq_ref, gk_ref, kone_ref,
                 a_ref, g_out_ref, mx_ref, mvo_ref, t_ref):
    h = _rms(x_ref[...], g_ref[...]).astype(BF16)

    def head_norm(p, bd_ref, gain_ref):
        ss = _dot((p * p).astype(BF16), bd_ref[...]) * (1.0 / ATT_HEAD_DIM)
        return p * lax.rsqrt(ss + EPS) * gain_ref[...]

    pq = _dot(h, w_ref[:, COL_Q:COL_IQ])
    a_ref[:, COL_Q:COL_IQ] = head_norm(pq, bdq_ref, gq_ref).astype(BF16)
    a_ref[:, COL_IQ:COL_K] = _dot(h, w_ref[:, COL_IQ:COL_K]).astype(BF16)
    pk = _dot(h, w_ref[:, COL_K:COL_V])
    a_ref[:, COL_K:COL_V] = (head_norm(pk, bdk_ref, gk_ref) + kone_ref[...]).astype(BF16)
    a_ref[:, COL_V:A_WIDTH] = _dot(h, w_ref[:, COL_V:A_WIDTH]).astype(BF16)
    o = A_WIDTH
    g_out_ref[...] = _dot(h, w_ref[:, o:o + G_WIDTH])
    o += G_WIDTH
    mx_ref[...] = _dot(h, w_ref[:, o:o + MIX_B])
    o += MIX_B
    for c in range(2):
        mvo_ref[:, c * MIX_B:(c + 1) * MIX_B] = _dot(h, w_ref[:, o + c * MIX_B:o + (c + 1) * MIX_B]).astype(BF16)
    o += 2 * MIX_B
    for c in range(2):
        t_ref[:, c * D_MODEL:(c + 1) * D_MODEL] = _dot(
            h, w_ref[:, o + c * D_MODEL:o + (c + 1) * D_MODEL]).astype(BF16)


def _proj(x2d, g, w_all, bdq, bdk, gq, gk, kone):
    m = x2d.shape[0]
    const = lambda i: (0, 0)
    row = lambda i: (i, 0)
    return pl.pallas_call(
        _proj_kernel,
        grid=(m // PROJ_TM,),
        in_specs=[
            pl.BlockSpec((PROJ_TM, D_MODEL), row),
            pl.BlockSpec((1, D_MODEL), const),
            pl.BlockSpec((D_MODEL, W_TOTAL), const, pipeline_mode=pl.Buffered(1)),
            pl.BlockSpec((512, 512), const),
            pl.BlockSpec((COL_V - COL_K, COL_V - COL_K), const),
            pl.BlockSpec((1, 512), const),
            pl.BlockSpec((1, COL_V - COL_K), const),
            pl.BlockSpec((1, COL_V - COL_K), const),
        ],
        out_specs=[
            pl.BlockSpec((PROJ_TM, A_WIDTH), row),
            pl.BlockSpec((PROJ_TM, G_WIDTH), row),
            pl.BlockSpec((PROJ_TM, MIX_B), row),
            pl.BlockSpec((PROJ_TM, 2 * MIX_B), row),
            pl.BlockSpec((PROJ_TM, T_WIDTH), row),
        ],
        out_shape=[
            jax.ShapeDtypeStruct((m, A_WIDTH), BF16),
            jax.ShapeDtypeStruct((m, G_WIDTH), F32),
            jax.ShapeDtypeStruct((m, MIX_B), F32),
            jax.ShapeDtypeStruct((m, 2 * MIX_B), BF16),
            jax.ShapeDtypeStruct((m, T_WIDTH), BF16),
        ],
        compiler_params=pltpu.CompilerParams(
            dimension_semantics=("parallel",), vmem_limit_bytes=VMEM_LIMIT),
        name="proj",
    )(x2d, g.reshape(1, D_MODEL), w_all, bdq, bdk, gq, gk, kone)


DSA_KB = 512
DSA_KA = 256
BOUND_LIMIT = 60.0
BOUND_SLACK = 1.01
DSA_VROWS = 80

BIT_GROUP = 256
SEARCH_CHUNK = 1024
PLANE_PAD_ROWS = 8


def _bit_transpose(words):
    a = list(words)
    j, m = 16, 0x0000FFFF
    while j:
        k = 0
        while k < 32:
            t = (a[k] ^ (a[k + j] >> j)) & m
            a[k] = a[k] ^ t
            a[k + j] = a[k + j] ^ (t << j)
            k = (k + j + 1) & ~j
        j >>= 1
        m = (m ^ (m << j)) & 0xFFFFFFFF
    return a


def _heads_to_lanes(x, n_heads, dh):
    xf = x.astype(F32)
    return jnp.concatenate([xf[:, h * dh:(h + 1) * dh].T for h in range(n_heads)], axis=1).astype(BF16)


def _dsa_kernel(qcap_ref, q_ref, iq_ref, k_ref, ik_ref, vt_ref, iwt_ref, o_ref,
                keys_ref, planes_ref, alive_ref, la_ref, lb_ref, acc_ref, sa_ref, sb_ref, thr_ref, stab_ref,
                *, seq, k_sel, idx_scale):
    kb_sz = DSA_KB
    n_blocks = seq // Q_BLOCK
    step = pl.program_id(1)
    has_cur = step < n_blocks
    has_prev = step >= 1
    cur = jnp.minimum(step, n_blocks - 1)
    prev = jnp.maximum(step - 1, 0)
    cur_buf = cur % 2
    prev_buf = prev % 2
    nkb = (cur * Q_BLOCK + Q_BLOCK + kb_sz - 1) // kb_sz
    nkb_prev = (prev * Q_BLOCK + Q_BLOCK + kb_sz - 1) // kb_sz
    lane = lax.broadcasted_iota(jnp.int32, (1, LANES), 1)
    limit = cur * Q_BLOCK + jnp.where(lane < CHUNK, CHUNK, 2 * CHUNK)
    row_iota = lax.broadcasted_iota(jnp.int32, (kb_sz, LANES), 0)

    iq_t = _heads_to_lanes(iq_ref[...], IDX_HEADS, IDX_HEAD_DIM)
    w = iwt_ref[...]

    def logits_of(blk):
        off = pl.multiple_of(blk * kb_sz, kb_sz)
        return _dot(ik_ref[pl.ds(off, kb_sz), :][:, :IDX_HEAD_DIM], iq_t)

    def keys_from(l_ref, blk, masked):
        off = pl.multiple_of(blk * kb_sz, kb_sz)
        sc = jnp.zeros((kb_sz, LANES), F32)
        for h in range(IDX_HEADS):
            sc = sc + jnp.maximum(l_ref[:, h * LANES:(h + 1) * LANES], 0.0) * w[h:h + 1, :]
        sc = sc * idx_scale
        bits = pltpu.bitcast(sc, jnp.int32)
        key = bits ^ ((bits >> 31) & 0x7FFFFFFF)
        key = jnp.where(sc == 0.0, 0, key)
        if masked:
            key = jnp.where(off + row_iota < limit, key, INT_MIN)
        keys_ref[cur_buf, pl.ds(off, kb_sz), :] = key
        ukey = key ^ INT_MIN
        for grp in range(kb_sz // BIT_GROUP):
            base = grp * BIT_GROUP
            planes = _bit_transpose([ukey[base + 8 * i:base + 8 * i + 8, :] for i in range(32)])
            row = pl.multiple_of((off + base) // 32, 8)
            for p in range(32):
                planes_ref[p, pl.ds(row, 8), :] = planes[p]

    n_pairs = (nkb + 1) // 2

    def score_pair(j, masked):
        lb_ref[...] = logits_of(2 * j + 1)
        keys_from(la_ref, 2 * j, masked)
        la_ref[...] = logits_of(jnp.minimum(2 * j + 2, 2 * n_pairs - 1))
        keys_from(lb_ref, 2 * j + 1, masked)

    def last_pair():
        j = n_pairs - 1

        @pl.when(nkb % 2 == 0)
        def _():
            score_pair(j, True)

        @pl.when(nkb % 2 == 1)
        def _():
            keys_from(la_ref, 2 * j, True)
            off = pl.multiple_of((2 * j + 1) * kb_sz, kb_sz)
            keys_ref[cur_buf, pl.ds(off, kb_sz), :] = jnp.full((kb_sz, LANES), INT_MIN, jnp.int32)
            row = pl.multiple_of(off // 32, kb_sz // 32)
            for p in range(32):
                planes_ref[p, pl.ds(row, kb_sz // 32), :] = jnp.zeros((kb_sz // 32, LANES), jnp.int32)

    def count(pred):
        def body(kb, c):
            off = pl.multiple_of(kb * kb_sz, kb_sz)
            ind = jnp.where(pred(keys_ref[cur_buf, pl.ds(off, kb_sz), :], off + row_iota), 1, 0)
            return c + jnp.sum(ind.reshape(kb_sz // 8, 8, LANES), axis=0)
        c8 = lax.fori_loop(0, nkb, body, jnp.zeros((8, LANES), jnp.int32))
        return jnp.sum(c8, axis=0, keepdims=True)

    def select_threshold():
        zeros = jnp.zeros((1, LANES), jnp.int32)
        ch_rows = SEARCH_CHUNK // 32
        n_ch = (nkb + 1) // 2
        zeros8 = jnp.zeros((8, LANES), jnp.int32)

        def ones_in(x):
            return jnp.sum(lax.population_count(x).reshape(ch_rows // 8, 8, LANES), axis=0)

        def over_chunks(body, n_out):
            tot = lax.fori_loop(0, n_ch, lambda c, cnt: tuple(a + b_ for a, b_ in zip(cnt, body(c))),
                                (zeros8,) * n_out)
            return [jnp.sum(t, axis=0, keepdims=True) for t in tot]

        def plane(p, row):
            return planes_ref[p, pl.ds(row, ch_rows), :]

        def digit_counts(alive, p_hi, p_lo):
            x1 = alive & p_hi
            return ones_in(x1), ones_in(x1 & p_lo), ones_in((alive ^ x1) & p_lo)

        def decide(q, t_u, above, counts):
            c1, c11, c01 = counts
            hi = above + c1 >= k_sel
            lo = jnp.where(hi, above + c11, above + c1 + c01) >= k_sel
            above = jnp.where(hi, jnp.where(lo, above, above + c11),
                              jnp.where(lo, above + c1, above + c1 + c01))
            t_u = (t_u | jnp.where(hi, jnp.left_shift(jnp.int32(1), 31 - 2 * q), 0)
                   | jnp.where(lo, jnp.left_shift(jnp.int32(1), 30 - 2 * q), 0))
            return t_u, above, jnp.where(hi, 0, -1), jnp.where(lo, 0, -1)

        def first_chunk(c):
            row = pl.multiple_of(c * ch_rows, ch_rows)
            everything = jnp.full((ch_rows, LANES), -1, jnp.int32)
            alive_ref[pl.ds(row, ch_rows), :] = everything
            return digit_counts(everything, plane(0, row), plane(1, row))

        def radix_pass(q, carry):
            t_u, above, flip_hi, flip_lo = carry

            def chunk(c):
                row = pl.multiple_of(c * ch_rows, ch_rows)
                alive = (alive_ref[pl.ds(row, ch_rows), :]
                         & (plane(2 * q - 2, row) ^ flip_hi) & (plane(2 * q - 1, row) ^ flip_lo))
                alive_ref[pl.ds(row, ch_rows), :] = alive
                return digit_counts(alive, plane(2 * q, row), plane(2 * q + 1, row))

            return decide(q, t_u, above, over_chunks(chunk, 3))

        carry = decide(0, zeros, zeros, over_chunks(first_chunk, 3))
        t_u, above, flip_hi, flip_lo = lax.fori_loop(1, 16, radix_pass, carry)

        def last_chunk(c):
            row = pl.multiple_of(c * ch_rows, ch_rows)
            return (ones_in(alive_ref[pl.ds(row, ch_rows), :]
                            & (plane(30, row) ^ flip_hi) & (plane(31, row) ^ flip_lo)),)

        c_eq = over_chunks(last_chunk, 1)[0]
        few = t_u == 0
        thr = jnp.maximum(t_u ^ INT_MIN, INT_MIN + 1)
        thr_ref[cur_buf] = thr
        need = k_sel - above
        excess = jnp.where(few, 0, c_eq - need)
        pos_bits = seq.bit_length() - 1

        @pl.when(jnp.max(excess) > 0)
        def _():
            def jbit(i, cut):
                cand = cut | jnp.left_shift(jnp.int32(1), pos_bits - 1 - i)
                c = count(lambda kk, pos: (kk == thr) & (pos < cand))
                return jnp.where(c <= need, cand, cut)
            cut = lax.fori_loop(0, pos_bits, jbit, zeros)
            cut = jnp.where(excess > 0, cut, seq)

            def drop_block(kb, _):
                off = pl.multiple_of(kb * kb_sz, kb_sz)
                kk = keys_ref[cur_buf, pl.ds(off, kb_sz), :]
                keys_ref[cur_buf, pl.ds(off, kb_sz), :] = jnp.where(
                    (kk == thr) & (off + row_iota >= cut), INT_MIN, kk)
                return 0

            lax.fori_loop(0, nkb, drop_block, 0)

    reps = ATT_HEADS // ATT_KV_HEADS
    width = reps * LANES
    q_t = _heads_to_lanes(q_ref[...], ATT_HEADS, ATT_HEAD_DIM)

    @pl.when(step == 0)
    def _():
        def kmax_block(kb, best):
            off = pl.multiple_of(kb * kb_sz, kb_sz)
            kf = k_ref[pl.ds(off, kb_sz), :].astype(F32)
            n2 = [jnp.max(jnp.sum(jnp.square(kf[:, g * K_EXT:g * K_EXT + ATT_HEAD_DIM]), axis=1, keepdims=True))
                  for g in range(ATT_KV_HEADS)]
            return tuple(jnp.maximum(b_, n_) for b_, n_ in zip(best, n2))
        best = lax.fori_loop(0, seq // kb_sz, kmax_block, (jnp.float32(0.0),) * ATT_KV_HEADS)
        worst = jnp.float32(0.0)
        for g in range(ATT_KV_HEADS):
            stab_ref[g] = jnp.sqrt(best[g]) * BOUND_SLACK
            worst = jnp.maximum(worst, stab_ref[g] * qcap_ref[0])
        stab_ref[ATT_KV_HEADS] = jnp.where(worst < BOUND_LIMIT, 1.0, 0.0)

    bounded = stab_ref[ATT_KV_HEADS] > 0.5
    qf = q_t.astype(F32)
    q_len = jnp.sqrt(jnp.sum(qf * qf, axis=0, keepdims=True))
    rhs = []
    for g in range(ATT_KV_HEADS):
        cols = slice(g * width, (g + 1) * width)
        neg_bound = jnp.where(bounded, -q_len[:, cols] * stab_ref[g], 0.0)
        rhs.append(jnp.concatenate(
            [qf[:, cols], neg_bound, jnp.zeros((K_EXT - ATT_HEAD_DIM - 1, width), F32)], axis=0).astype(BF16))

    ka = DSA_KA

    def masked_scores(blk):
        off = pl.multiple_of(blk * ka, ka)
        bias = jnp.where(keys_ref[prev_buf, pl.ds(off, ka), :] >= thr_prev, 0.0, NEG_BIG)
        bias = jnp.concatenate([bias] * reps, axis=1)
        kblk = k_ref[pl.ds(off, ka), :]
        return jnp.concatenate(
            [_dot(kblk[:, g * K_EXT:(g + 1) * K_EXT], rhs[g]) + bias for g in range(ATT_KV_HEADS)],
            axis=1)

    def consume_bounded(s_ref, blk):
        off = pl.multiple_of(blk * ka, ka)
        p = jnp.exp2(s_ref[...]).astype(BF16)
        for g in range(ATT_KV_HEADS):
            acc_ref[g] += _dot(vt_ref[g, :, pl.ds(off, ka)], p[:, g * width:(g + 1) * width])

    def consume_online(s_ref, blk, m_old):
        off = pl.multiple_of(blk * ka, ka)
        s = s_ref[...]
        m_new = jnp.maximum(m_old, jnp.max(s, axis=0, keepdims=True))
        alpha = jnp.exp2(m_old - m_new)
        p = jnp.exp2(s - m_new).astype(BF16)
        for g in range(ATT_KV_HEADS):
            vt = vt_ref[g, :, pl.ds(off, ka)]
            acc_ref[g] = (alpha[:, g * width:(g + 1) * width] * acc_ref[g]
                          + _dot(vt, p[:, g * width:(g + 1) * width]))
        return m_new

    n_sub = nkb_prev * (kb_sz // ka)
    thr_prev = thr_ref[prev_buf]

    def att_pair_bounded(j, prefetch=True):
        sb_ref[...] = masked_scores(2 * j + 1)
        consume_bounded(sa_ref, 2 * j)
        if prefetch:
            sa_ref[...] = masked_scores(2 * j + 2)
        consume_bounded(sb_ref, 2 * j + 1)

    @pl.when(has_cur)
    def _():
        la_ref[...] = logits_of(0)

    @pl.when(has_prev)
    def _():
        acc_ref[...] = jnp.zeros_like(acc_ref)
        sa_ref[...] = masked_scores(0)

    n_att = n_sub // 2
    n_fused = jnp.where(has_cur & has_prev & bounded, jnp.minimum(n_pairs - 1, (n_att - 1) // 2), 0)

    def fused(j, c):
        score_pair(j, False)
        att_pair_bounded(2 * j)
        att_pair_bounded(2 * j + 1)
        return c

    lax.fori_loop(0, n_fused, fused, 0)

    @pl.when(has_cur)
    def _():
        lax.fori_loop(n_fused, n_pairs - 1, lambda j, c: (score_pair(j, False), c)[1], 0)
        last_pair()
        select_threshold()

    @pl.when(has_prev & bounded)
    def _():
        lax.fori_loop(2 * n_fused, n_att - 1, lambda j, c: (att_pair_bounded(j), c)[1], 0)
        att_pair_bounded(n_att - 1, prefetch=False)

    @pl.when(has_prev & jnp.logical_not(bounded))
    def _():
        def att_pair(j, m_run):
            sb_ref[...] = masked_scores(2 * j + 1)
            m_run = consume_online(sa_ref, 2 * j, m_run)
            sa_ref[...] = masked_scores(jnp.minimum(2 * j + 2, n_sub - 1))
            return consume_online(sb_ref, 2 * j + 1, m_run)
        lax.fori_loop(0, n_sub // 2, att_pair, jnp.full((1, ATT_HEADS * LANES), NEG_BIG, F32))

    @pl.when(has_prev)
    def _():
        pieces = []
        for g in range(ATT_KV_HEADS):
            a = acc_ref[g]
            o = a[0:ATT_HEAD_DIM, :] / a[ATT_HEAD_DIM:ATT_HEAD_DIM + 1, :]
            pieces += [o[:, r * LANES:(r + 1) * LANES] for r in range(reps)]
        o_ref[...] = jnp.concatenate(pieces, axis=0).T.astype(o_ref.dtype)


def _v_transposed(v):
    b, s, _ = v.shape
    vt = jnp.swapaxes(v.reshape(b, s, ATT_KV_HEADS, ATT_HEAD_DIM), 1, 3)
    vt = jnp.swapaxes(vt, 1, 2)
    ones = jnp.ones((b, ATT_KV_HEADS, 1, s), v.dtype)
    pad = jnp.zeros((b, ATT_KV_HEADS, DSA_VROWS - ATT_HEAD_DIM - 1, s), v.dtype)
    return jnp.concatenate([vt, ones, pad], axis=2)


def _dsa(a3, vt, iwt, qcap):
    b, s, _ = a3.shape
    nb = s // Q_BLOCK
    k_sel = min(TOPK_MAX, s // 4)
    idx_scale = (IDX_HEAD_DIM ** -0.5) * (IDX_HEADS ** -0.5)
    kern = functools.partial(_dsa_kernel, seq=s, k_sel=k_sel, idx_scale=idx_scale)
    return pl.pallas_call(
        kern,
        grid=(b, nb + 1),
        in_specs=[
            pl.BlockSpec(memory_space=pltpu.SMEM),
            pl.BlockSpec((None, Q_BLOCK, 512), lambda bb, i: (bb, jnp.maximum(i - 1, 0), COL_Q // 512)),
            pl.BlockSpec((None, Q_BLOCK, 512), lambda bb, i: (bb, jnp.minimum(i, nb - 1), COL_IQ // 512)),
            pl.BlockSpec((None, s, ATT_KV_HEADS * K_EXT), lambda bb, i: (bb, 0, COL_K // (ATT_KV_HEADS * K_EXT))),
            pl.BlockSpec((None, s, LANES), lambda bb, i: (bb, 0, COL_IK // LANES)),
            pl.BlockSpec((None, ATT_KV_HEADS, DSA_VROWS, s), lambda bb, i: (bb, 0, 0, 0)),
            pl.BlockSpec((None, None, IDX_HEADS, LANES), lambda bb, i: (bb, jnp.minimum(i, nb - 1), 0, 0)),
        ],
        out_specs=pl.BlockSpec((None, Q_BLOCK, MIX_A), lambda bb, i: (bb, jnp.maximum(i - 1, 0), 0)),
        out_shape=jax.ShapeDtypeStruct((b, s, MIX_A), BF16),
        scratch_shapes=[
            pltpu.VMEM((2, s, LANES), jnp.int32),
            pltpu.VMEM((32, s // 32 + PLANE_PAD_ROWS, LANES), jnp.int32),
            pltpu.VMEM((s // 32, LANES), jnp.int32),
            pltpu.VMEM((DSA_KB, IDX_HEADS * LANES), F32),
            pltpu.VMEM((DSA_KB, IDX_HEADS * LANES), F32),
            pltpu.VMEM((ATT_KV_HEADS, DSA_VROWS, (ATT_HEADS // ATT_KV_HEADS) * LANES), F32),
            pltpu.VMEM((DSA_KA, ATT_HEADS * LANES), F32),
            pltpu.VMEM((DSA_KA, ATT_HEADS * LANES), F32),
            pltpu.VMEM((2, 1, LANES), jnp.int32),
            pltpu.SMEM((ATT_KV_HEADS + 1,), F32),
        ],
        compiler_params=pltpu.CompilerParams(
            dimension_semantics=("parallel", "arbitrary"), vmem_limit_bytes=VMEM_LIMIT),
        name="dsa",
    )(qcap, a3, a3, a3, a3, vt, iwt)


ML_L = 128
ML_TS = 512
ML_PAD = 8


def _mlstm_kernel(mx_ref, mv_ref, mo_ref, gr_ref, cw_ref, cb_ref, wq_ref, wkt_ref, brow_ref, hg_ref, o_ref,
                  xbuf_ref, q_ref, kt_ref, s_ref, m_ref):
    t_idx = pl.program_id(1)
    nh, dk, dv, L = MLSTM_HEADS, MLSTM_QK_DIM, MLSTM_V_DIM, ML_L

    @pl.when(t_idx == 0)
    def _():
        xbuf_ref[0:ML_PAD, :] = jnp.zeros((ML_PAD, MIX_B), F32)
        s_ref[...] = jnp.zeros_like(s_ref)
        m_ref[...] = jnp.zeros_like(m_ref)

    xbuf_ref[ML_PAD:ML_PAD + ML_TS, :] = mx_ref[...]
    xc = cb_ref[...] + jnp.zeros((ML_TS, MIX_B), F32)
    for j in range(CONV_WIDTH):
        s0 = ML_PAD - (CONV_WIDTH - 1) + j
        xc = xc + xbuf_ref[s0:s0 + ML_TS, :] * cw_ref[j:j + 1, :]
    xbuf_ref[0:ML_PAD, :] = mx_ref[ML_TS - ML_PAD:ML_TS, :]
    xc = (xc * jax.nn.sigmoid(xc)).astype(BF16)
    for h in range(nh):
        xh = xc[:, h * dv:(h + 1) * dv]
        q_ref[h] = (_dot(xh, wq_ref[h]) * (dk ** -0.5)).astype(BF16)
        kt_ref[h] = _dot_nt(wkt_ref[h], xh)

    gr = gr_ref[...]
    li_r = gr[0:nh, :] + brow_ref[0:nh, :]
    lf_r = jax.nn.log_sigmoid(gr[nh:2 * nh, :] + brow_ref[nh:2 * nh, :])

    ri = lax.broadcasted_iota(jnp.int32, (L, L), 0)
    ci = lax.broadcasted_iota(jnp.int32, (L, L), 1)
    causal = ri >= ci
    ones_v = jnp.ones((L, dv), BF16)
    ones_n = jnp.ones((dv, dv), BF16)

    tr = lax.broadcasted_iota(jnp.int32, (ML_TS, ML_TS), 0)
    tc = lax.broadcasted_iota(jnp.int32, (ML_TS, ML_TS), 1)
    chunk_triu = jnp.where((tr <= tc) & (tr // L == tc // L), 1.0, 0.0).astype(F32)
    b_all = jnp.dot(lf_r, chunk_triu, precision=lax.Precision.HIGHEST, preferred_element_type=F32)
    g_all = li_r - b_all
    in_chunk = lax.broadcasted_iota(jnp.int32, (nh, ML_TS), 1) % L
    run_all = g_all
    for sh in (1, 2, 4, 8, 16, 32, 64):
        run_all = jnp.maximum(run_all, jnp.where(in_chunk >= sh, pltpu.roll(run_all, sh, axis=1), NEG_BIG))

    def to_rows(row):
        return jnp.broadcast_to(row, (L, L)).T

    for c in range(ML_TS // L):
        r0 = c * L
        b_r, g_r, run = b_all[:, r0:r0 + L], g_all[:, r0:r0 + L], run_all[:, r0:r0 + L]
        run_rows = [to_rows(run[h:h + 1, :]) for h in range(nh)]
        b_rows = [to_rows(b_r[h:h + 1, :]) for h in range(nh)]
        m_prev = m_ref[...]
        b_last = b_r[:, L - 1:L]
        wk_r = b_last + g_r
        m_new = jnp.maximum(b_last + m_prev, jnp.max(wk_r, axis=1, keepdims=True))
        ws_r = jnp.exp(wk_r - m_new)
        wc_r = jnp.exp(b_last + m_prev - m_new)
        m_ref[...] = m_new
        outs = []
        for h in range(nh):
            big_m_t = jnp.maximum(run_rows[h], m_prev[h:h + 1, :])
            m_t = big_m_t + b_rows[h]
            decay = jnp.where(causal, jnp.exp(g_r[h:h + 1, :] - big_m_t), 0.0)
            w_int = jnp.exp(m_prev[h:h + 1, :] - big_m_t)
            qh = q_ref[h, r0:r0 + L, :]
            kt = kt_ref[h, :, r0:r0 + L]
            v_ext = jnp.concatenate(
                [mv_ref[r0:r0 + L, h * dv:(h + 1) * dv], ones_v], axis=1)
            wt = _dot(qh, kt.astype(BF16)) * decay
            nd = (_dot(wt.astype(BF16), v_ext)
                  + jnp.concatenate([w_int, w_int], axis=1) * _dot(qh, s_ref[h].astype(BF16)))
            hv = nd[:, 0:dv] / jnp.maximum(jnp.abs(nd[:, dv:2 * dv]), jnp.exp(-m_t))
            ms = _dot((hv * hv).astype(BF16), ones_n) * (1.0 / dv)
            outs.append(hv * lax.rsqrt(ms + EPS) * hg_ref[0:1, h * dv:(h + 1) * dv])
            wc = jnp.broadcast_to(wc_r[h:h + 1, :], (dk, L))
            s_ref[h] = (jnp.concatenate([wc, wc], axis=1) * s_ref[h]
                        + _dot((kt * ws_r[h:h + 1, :]).astype(BF16), v_ext))
        hh = jnp.concatenate(outs, axis=1)
        o_ref[r0:r0 + L, :] = (hh * jax.nn.sigmoid(mo_ref[r0:r0 + L, :].astype(F32))).astype(o_ref.dtype)


def _mlstm(m3, mvo3, gt, conv_w, conv_b, wq, wkt, brow, head_g):
    b, s, _ = m3.shape
    nh = MLSTM_HEADS
    tile = lambda c: pl.BlockSpec((None, ML_TS, MIX_B), lambda bb, t: (bb, t, c))
    const2 = lambda shape: pl.BlockSpec(shape, lambda bb, t: (0, 0))
    const3 = lambda shape: pl.BlockSpec(shape, lambda bb, t: (0, 0, 0))
    return pl.pallas_call(
        _mlstm_kernel,
        grid=(b, s // ML_TS),
        in_specs=[
            tile(0), tile(0), tile(1),
            pl.BlockSpec((None, 2 * nh, ML_TS), lambda bb, t: (bb, 0, t)),
            const2((CONV_WIDTH, MIX_B)), const2((1, MIX_B)),
            const3((nh, MLSTM_V_DIM, MLSTM_QK_DIM)), const3((nh, MLSTM_QK_DIM, MLSTM_V_DIM)),
            const2((2 * nh, 1)), const2((1, MIX_B)),
        ],
        out_specs=pl.BlockSpec((None, ML_TS, MIX_B), lambda bb, t: (bb, t, 0)),
        out_shape=jax.ShapeDtypeStruct((b, s, MIX_B), BF16),
        scratch_shapes=[
            pltpu.VMEM((ML_PAD + ML_TS, MIX_B), F32),
            pltpu.VMEM((nh, ML_TS, MLSTM_QK_DIM), BF16),
            pltpu.VMEM((nh, MLSTM_QK_DIM, ML_TS), F32),
            pltpu.VMEM((nh, MLSTM_QK_DIM, 2 * MLSTM_V_DIM), F32),
            pltpu.VMEM((nh, ML_L), F32),
        ],
        compiler_params=pltpu.CompilerParams(
            dimension_semantics=("parallel", "arbitrary"), vmem_limit_bytes=VMEM_LIMIT),
        name="mlstm",
    )(m3, mvo3, mvo3, gt, conv_w, conv_b, wq, wkt, brow, head_g)


MERGE_TM = 512


def _merge_ffn_kernel(x_ref, ya_ref, yb_ref, ga_ref, gb_ref, wa_ref, wb_ref, wo_ref,
                      g_ref, wg_ref, wu_ref, wd_ref, o_ref):
    merged = (jax.nn.sigmoid(ga_ref[...].astype(F32)) * _dot(ya_ref[...], wa_ref[...])
              + jax.nn.sigmoid(gb_ref[...].astype(F32)) * _dot(yb_ref[...], wb_ref[...]))
    x = x_ref[...] + _dot(merged.astype(BF16), wo_ref[...])
    _half_ffn(x, g_ref, wg_ref, wu_ref, wd_ref, o_ref)


def _merge_ffn(x2d, ya, yb, gates, wa, wb, wo, g, wg, wu, wd):
    m = x2d.shape[0]
    row = lambda i: (i, 0)
    const = lambda i: (0, 0)
    resident = lambda shape: pl.BlockSpec(shape, const, pipeline_mode=pl.Buffered(1))
    return pl.pallas_call(
        _merge_ffn_kernel,
        grid=(m // MERGE_TM,),
        in_specs=[
            pl.BlockSpec((MERGE_TM, D_MODEL), row),
            pl.BlockSpec((MERGE_TM, MIX_A), row),
            pl.BlockSpec((MERGE_TM, MIX_B), row),
            pl.BlockSpec((MERGE_TM, D_MODEL), lambda i: (i, 0)),
            pl.BlockSpec((MERGE_TM, D_MODEL), lambda i: (i, 1)),
            resident((MIX_A, D_MODEL)),
            resident((MIX_B, D_MODEL)),
            resident((D_MODEL, D_MODEL)),
            pl.BlockSpec((1, D_MODEL), const),
            resident((D_MODEL, D_FF)),
            resident((D_MODEL, D_FF)),
            resident((D_FF, D_MODEL)),
        ],
        out_specs=pl.BlockSpec((MERGE_TM, D_MODEL), row),
        out_shape=jax.ShapeDtypeStruct((m, D_MODEL), F32),
        compiler_params=pltpu.CompilerParams(
            dimension_semantics=("parallel",), vmem_limit_bytes=VMEM_LIMIT),
        name="merge_ffn",
    )(x2d, ya, yb, gates, gates, wa, wb, wo, g.reshape(1, D_MODEL), wg, wu, wd)


def _block_diag_ones(n, blk):
    r = jnp.arange(n) // blk
    return (r[:, None] == r[None, :]).astype(BF16)


def _arrange_w_in(w_in):
    widths = (512, 128, 128, 512, 64, 8, 512, 512, 4, 4, 512, 1024, 1024)
    parts, off = [], 0
    for wd in widths:
        parts.append(w_in[:, off:off + wd])
        off += wd
    aq, ak, av, iq, ik, iw, mx, mv, mi, mf, mo, ga, gb = parts
    z = lambda n: jnp.zeros((w_in.shape[0], n), w_in.dtype)
    cols = [aq, iq, ak[:, :64], z(64), ak[:, 64:], z(64), ik, z(64), av,
            iw, mi, mf, z(G_WIDTH - 16),
            mx, mv, mo, ga, gb]
    return jnp.concatenate(cols, axis=1).astype(BF16)


def kernel(x, ffn1_norm, ffn1_w_gate, ffn1_w_up, ffn1_w_down, mix_norm, w_in, q_norm, k_norm,
           idx_k_norm, conv_w, conv_b, w_mq, w_mk, b_i, b_f, m_head_norm, w_proj_a, w_proj_b,
           w_out, ffn2_norm, ffn2_w_gate, ffn2_w_up, ffn2_w_down):
    b, s, _ = x.shape
    m = b * s
    nb = s // Q_BLOCK
    nh = MLSTM_HEADS
    x2d = x.reshape(m, D_MODEL)
    for l in range(ffn1_norm.shape[0]):
        x2d = _ffn(x2d, ffn1_norm[l], ffn1_w_gate[l].astype(BF16), ffn1_w_up[l].astype(BF16),
                   ffn1_w_down[l].astype(BF16))

        gq = (jnp.tile(q_norm[l], ATT_HEADS) * (ATT_HEAD_DIM ** -0.5 * LOG2_E)).reshape(1, 512)
        z64 = jnp.zeros((ATT_HEAD_DIM,), F32)
        one64 = z64.at[0].set(1.0)
        gk = jnp.concatenate([k_norm[l], z64] * ATT_KV_HEADS + [idx_k_norm[l], z64]).reshape(1, -1)
        kone = jnp.concatenate([z64, one64] * ATT_KV_HEADS + [z64, z64]).reshape(1, -1)
        a2d, g2d, mx2d, mvo2d, t2d = _proj(x2d, mix_norm[l], _arrange_w_in(w_in[l]),
                                   _block_diag_ones(512, ATT_HEAD_DIM),
                                   _block_diag_ones(COL_V - COL_K, ATT_HEAD_DIM), gq, gk, kone)

        a3 = a2d.reshape(b, s, A_WIDTH)
        vt = _v_transposed(a3[:, :, COL_V:COL_V + ATT_KV_HEADS * ATT_HEAD_DIM])
        g3 = g2d.reshape(b, s, G_WIDTH)
        iwt = jnp.swapaxes(g3[:, :, 0:IDX_HEADS].reshape(b, nb, Q_BLOCK, IDX_HEADS), 2, 3)
        qcap = (jnp.max(jnp.abs(gq)) * (ATT_HEAD_DIM ** 0.5) * BOUND_SLACK).reshape(1)
        ya = _dsa(a3, vt, iwt, qcap)

        gt = jnp.swapaxes(g3[:, :, IDX_HEADS:IDX_HEADS + 2 * nh], 1, 2)
        bias = jnp.concatenate([b_i[l], b_f[l]])
        yb = _mlstm(mx2d.reshape(b, s, MIX_B), mvo2d.reshape(b, s, 2 * MIX_B), gt,
                    conv_w[l], conv_b[l].reshape(1, MIX_B),
                    w_mq[l].astype(BF16), jnp.swapaxes(w_mk[l], 1, 2).astype(BF16),
                    bias.reshape(2 * nh, 1), m_head_norm[l].reshape(1, MIX_B))

        x2d = _merge_ffn(x2d, ya.reshape(m, MIX_A), yb.reshape(m, MIX_B), t2d,
                         w_proj_a[l].astype(BF16), w_proj_b[l].astype(BF16), w_out[l].astype(BF16),
                         ffn2_norm[l], ffn2_w_gate[l].astype(BF16), ffn2_w_up[l].astype(BF16),
                         ffn2_w_down[l].astype(BF16))
    return x2d.reshape(b, s, D_MODEL)
```

```python
import functools

import jax
import jax.numpy as jnp
from jax import lax
from jax.experimental import pallas as pl
from jax.experimental.pallas import tpu as pltpu

F32 = jnp.float32
BF16 = jnp.bfloat16

D_MODEL = 1024
D_FF = 2816
EPS = 1e-6
ATT_HEADS = 8
ATT_KV_HEADS = 2
ATT_HEAD_DIM = 64
IDX_HEADS = 8
IDX_HEAD_DIM = 64
TOPK_MAX = 256
Q_BLOCK = 128
CHUNK = 64
MLSTM_HEADS = 4
MLSTM_QK_DIM = 64
MLSTM_V_DIM = 128
CONV_WIDTH = 4
MIX_A = ATT_HEADS * ATT_HEAD_DIM
MIX_B = MLSTM_HEADS * MLSTM_V_DIM

LANES = 128
VMEM_LIMIT = 56 * 1024 * 1024
INT_MIN = -2147483648
NEG_BIG = -1e30
LOG2_E = 1.4426950408889634

COL_Q, COL_IQ, COL_K, COL_IK, COL_V = 0, 512, 1024, 1280, 1408
K_EXT = 128
A_WIDTH = 1536
G_WIDTH = 128
M_WIDTH = 3 * MIX_B
T_WIDTH = 2 * D_MODEL
W_TOTAL = A_WIDTH + G_WIDTH + M_WIDTH + T_WIDTH


def _rms(x, g):
    return x * lax.rsqrt(jnp.mean(x * x, axis=-1, keepdims=True) + EPS) * g


def _dot(a, b):
    return jnp.dot(a, b, preferred_element_type=F32)


def _dot_nt(a, b):
    return lax.dot_general(a, b, (((1,), (1,)), ((), ())), preferred_element_type=F32)


def _dot_tn(a, b):
    return lax.dot_general(a, b, (((0,), (0,)), ((), ())), preferred_element_type=F32)


FFN_TM = 512
FFN_TF = 2816


def _ffn_kernel(x_ref, g_ref, wg_ref, wu_ref, wd_ref, o_ref):
    _half_ffn(x_ref[...], g_ref, wg_ref, wu_ref, wd_ref, o_ref)


def _half_ffn(x, g_ref, wg_ref, wu_ref, wd_ref, o_ref):
    h = _rms(x, g_ref[...]).astype(BF16)
    acc = None
    for c in range(D_FF // FFN_TF):
        cols = slice(c * FFN_TF, (c + 1) * FFN_TF)
        a = _dot(h, wg_ref[:, cols])
        u = _dot(h, wu_ref[:, cols])
        act = (a * jax.nn.sigmoid(a) * u).astype(BF16)
        part = _dot(act, wd_ref[cols, :])
        acc = part if acc is None else acc + part
    o_ref[...] = x + 0.5 * acc


def _ffn(x2d, g, wg, wu, wd):
    m = x2d.shape[0]
    const = lambda i: (0, 0)
    resident = lambda shape: pl.BlockSpec(shape, const, pipeline_mode=pl.Buffered(1))
    return pl.pallas_call(
        _ffn_kernel,
        grid=(m // FFN_TM,),
        in_specs=[
            pl.BlockSpec((FFN_TM, D_MODEL), lambda i: (i, 0)),
            pl.BlockSpec((1, D_MODEL), const),
            resident((D_MODEL, D_FF)),
            resident((D_MODEL, D_FF)),
            resident((D_FF, D_MODEL)),
        ],
        out_specs=pl.BlockSpec((FFN_TM, D_MODEL), lambda i: (i, 0)),
        out_shape=jax.ShapeDtypeStruct((m, D_MODEL), F32),
        compiler_params=pltpu.CompilerParams(
            dimension_semantics=("parallel",), vmem_limit_bytes=VMEM_LIMIT),
        name="ffn",
    )(x2d, g.reshape(1, D_MODEL), wg, wu, wd)


PROJ_TM = 512


def _proj_kernel(x_ref, g_ref, w_ref, bdq_ref, bdk_ref, gq_ref, gk_ref, kone_ref,
                 a_ref, g_out_ref, mx_ref, mvo_ref, t_ref):
    h = _rms(x_ref[...], g_ref[...]).astype(BF16)

    def head_norm(p, bd_ref, gain_ref):
        ss = _dot((p * p).astype(BF16), bd_ref[...]) * (1.0 / ATT_HEAD_DIM)
        return p * lax.rsqrt(ss + EPS) * gain_ref[...]

    pq = _dot(h, w_ref[:, COL_Q:COL_IQ])
    a_ref[:, COL_Q:COL_IQ] = head_norm(pq, bdq_ref, gq_ref).astype(BF16)
    a_ref[:, COL_IQ:COL_K] = _dot(h, w_ref[:, COL_IQ:COL_K]).astype(BF16)
    pk = _dot(h, w_ref[:, COL_K:COL_V])
    a_ref[:, COL_K:COL_V] = (head_norm(pk, bdk_ref, gk_ref) + kone_ref[...]).astype(BF16)
    a_ref[:, COL_V:A_WIDTH] = _dot(h, w_ref[:, COL_V:A_WIDTH]).astype(BF16)
    o = A_WIDTH
    g_out_ref[...] = _dot(h, w_ref[:, o:o + G_WIDTH])
    o += G_WIDTH
    mx_ref[...] = _dot(h, w_ref[:, o:o + MIX_B])
    o += MIX_B
    for c in range(2):
        mvo_ref[:, c * MIX_B:(c + 1) * MIX_B] = _dot(h, w_ref[:, o + c * MIX_B:o + (c + 1) * MIX_B]).astype(BF16)
    o += 2 * MIX_B
    for c in range(2):
        t_ref[:, c * D_MODEL:(c + 1) * D_MODEL] = _dot(
            h, w_ref[:, o + c * D_MODEL:o + (c + 1) * D_MODEL]).astype(BF16)


def _proj(x2d, g, w_all, bdq, bdk, gq, gk, kone):
    m = x2d.shape[0]
    const = lambda i: (0, 0)
    row = lambda i: (i, 0)
    return pl.pallas_call(
        _proj_kernel,
        grid=(m // PROJ_TM,),
        in_specs=[
            pl.BlockSpec((PROJ_TM, D_MODEL), row),
            pl.BlockSpec((1, D_MODEL), const),
            pl.BlockSpec((D_MODEL, W_TOTAL), const, pipeline_mode=pl.Buffered(1)),
            pl.BlockSpec((512, 512), const),
            pl.BlockSpec((COL_V - COL_K, COL_V - COL_K), const),
            pl.BlockSpec((1, 512), const),
            pl.BlockSpec((1, COL_V - COL_K), const),
            pl.BlockSpec((1, COL_V - COL_K), const),
        ],
        out_specs=[
            pl.BlockSpec((PROJ_TM, A_WIDTH), row),
            pl.BlockSpec((PROJ_TM, G_WIDTH), row),
            pl.BlockSpec((PROJ_TM, MIX_B), row),
            pl.BlockSpec((PROJ_TM, 2 * MIX_B), row),
            pl.BlockSpec((PROJ_TM, T_WIDTH), row),
        ],
        out_shape=[
            jax.ShapeDtypeStruct((m, A_WIDTH), BF16),
            jax.ShapeDtypeStruct((m, G_WIDTH), F32),
            jax.ShapeDtypeStruct((m, MIX_B), F32),
            jax.ShapeDtypeStruct((m, 2 * MIX_B), BF16),
            jax.ShapeDtypeStruct((m, T_WIDTH), BF16),
        ],
        compiler_params=pltpu.CompilerParams(
            dimension_semantics=("parallel",), vmem_limit_bytes=VMEM_LIMIT),
        name="proj",
    )(x2d, g.reshape(1, D_MODEL), w_all, bdq, bdk, gq, gk, kone)


DSA_KB = 512
DSA_KA = 256
BOUND_LIMIT = 60.0
BOUND_SLACK = 1.01
DSA_VROWS = 80

BIT_GROUP = 256
SEARCH_CHUNK = 1024
PLANE_PAD_ROWS = 8


def _bit_transpose(words):
    a = list(words)
    j, m = 16, 0x0000FFFF
    while j:
        k = 0
        while k < 32:
            t = (a[k] ^ (a[k + j] >> j)) & m
            a[k] = a[k] ^ t
            a[k + j] = a[k + j] ^ (t << j)
            k = (k + j + 1) & ~j
        j >>= 1
        m = (m ^ (m << j)) & 0xFFFFFFFF
    return a


def _heads_to_lanes(x, n_heads, dh):
    xf = x.astype(F32)
    return jnp.concatenate([xf[:, h * dh:(h + 1) * dh].T for h in range(n_heads)], axis=1).astype(BF16)


def _dsa_kernel(qcap_ref, q_ref, iq_ref, k_ref, ik_ref, vt_ref, iwt_ref, o_ref,
                keys_ref, planes_ref, alive_ref, la_ref, lb_ref, acc_ref, sa_ref, sb_ref, thr_ref, stab_ref,
                *, seq, k_sel, idx_scale):
    kb_sz = DSA_KB
    n_blocks = seq // Q_BLOCK
    step = pl.program_id(1)
    has_cur = step < n_blocks
    has_prev = step >= 1
    cur = jnp.minimum(step, n_blocks - 1)
    prev = jnp.maximum(step - 1, 0)
    cur_buf = cur % 2
    prev_buf = prev % 2
    nkb = (cur * Q_BLOCK + Q_BLOCK + kb_sz - 1) // kb_sz
    nkb_prev = (prev * Q_BLOCK + Q_BLOCK + kb_sz - 1) // kb_sz
    lane = lax.broadcasted_iota(jnp.int32, (1, LANES), 1)
    limit = cur * Q_BLOCK + jnp.where(lane < CHUNK, CHUNK, 2 * CHUNK)
    row_iota = lax.broadcasted_iota(jnp.int32, (kb_sz, LANES), 0)

    iq_t = _heads_to_lanes(iq_ref[...], IDX_HEADS, IDX_HEAD_DIM)
    w = iwt_ref[...]

    def logits_of(blk):
        off = pl.multiple_of(blk * kb_sz, kb_sz)
        return _dot(ik_ref[pl.ds(off, kb_sz), :][:, :IDX_HEAD_DIM], iq_t)

    def keys_from(l_ref, blk, masked):
        off = pl.multiple_of(blk * kb_sz, kb_sz)
        sc = jnp.zeros((kb_sz, LANES), F32)
        for h in range(IDX_HEADS):
            sc = sc + jnp.maximum(l_ref[:, h * LANES:(h + 1) * LANES], 0.0) * w[h:h + 1, :]
        sc = sc * idx_scale
        bits = pltpu.bitcast(sc, jnp.int32)
        key = bits ^ ((bits >> 31) & 0x7FFFFFFF)
        key = jnp.where(sc == 0.0, 0, key)
        if masked:
            key = jnp.where(off + row_iota < limit, key, INT_MIN)
        keys_ref[cur_buf, pl.ds(off, kb_sz), :] = key
        ukey = key ^ INT_MIN
        for grp in range(kb_sz // BIT_GROUP):
            base = grp * BIT_GROUP
            planes = _bit_transpose([ukey[base + 8 * i:base + 8 * i + 8, :] for i in range(32)])
            row = pl.multiple_of((off + base) // 32, 8)
            for p in range(32):
                planes_ref[p, pl.ds(row, 8), :] = planes[p]

    n_pairs = (nkb + 1) // 2

    def score_pair(j, masked, prefetch=True):
        lb_ref[...] = logits_of(2 * j + 1)
        keys_from(la_ref, 2 * j, masked)
        if prefetch:
            la_ref[...] = logits_of(2 * j + 2)
        keys_from(lb_ref, 2 * j + 1, masked)

    def last_pair():
        j = n_pairs - 1

        @pl.when(nkb % 2 == 0)
        def _():
            score_pair(j, True, prefetch=False)

        @pl.when(nkb % 2 == 1)
        def _():
            keys_from(la_ref, 2 * j, True)
            off = pl.multiple_of((2 * j + 1) * kb_sz, kb_sz)
            keys_ref[cur_buf, pl.ds(off, kb_sz), :] = jnp.full((kb_sz, LANES), INT_MIN, jnp.int32)
            row = pl.multiple_of(off // 32, kb_sz // 32)
            for p in range(32):
                planes_ref[p, pl.ds(row, kb_sz // 32), :] = jnp.zeros((kb_sz // 32, LANES), jnp.int32)

    def count(pred):
        def body(kb, c):
            off = pl.multiple_of(kb * kb_sz, kb_sz)
            ind = jnp.where(pred(keys_ref[cur_buf, pl.ds(off, kb_sz), :], off + row_iota), 1, 0)
            return c + jnp.sum(ind.reshape(kb_sz // 8, 8, LANES), axis=0)
        c8 = lax.fori_loop(0, nkb, body, jnp.zeros((8, LANES), jnp.int32))
        return jnp.sum(c8, axis=0, keepdims=True)

    def select_threshold():
        zeros = jnp.zeros((1, LANES), jnp.int32)
        ch_rows = SEARCH_CHUNK // 32
        n_ch = (nkb + 1) // 2
        zeros8 = jnp.zeros((8, LANES), jnp.int32)

        def ones_in(x):
            return jnp.sum(lax.population_count(x).reshape(ch_rows // 8, 8, LANES), axis=0)

        def over_chunks(body, n_out):
            tot = lax.fori_loop(0, n_ch, lambda c, cnt: tuple(a + b_ for a, b_ in zip(cnt, body(c))),
                                (zeros8,) * n_out)
            return [jnp.sum(t, axis=0, keepdims=True) for t in tot]

        def plane(p, row):
            return planes_ref[p, pl.ds(row, ch_rows), :]

        def digit_counts(alive, p_hi, p_lo):
            x1 = alive & p_hi
            return ones_in(x1), ones_in(x1 & p_lo), ones_in((alive ^ x1) & p_lo)

        def decide(q, t_u, above, counts):
            c1, c11, c01 = counts
            hi = above + c1 >= k_sel
            lo = jnp.where(hi, above + c11, above + c1 + c01) >= k_sel
            above = jnp.where(hi, jnp.where(lo, above, above + c11),
                              jnp.where(lo, above + c1, above + c1 + c01))
            t_u = (t_u | jnp.where(hi, jnp.left_shift(jnp.int32(1), 31 - 2 * q), 0)
                   | jnp.where(lo, jnp.left_shift(jnp.int32(1), 30 - 2 * q), 0))
            return t_u, above, jnp.where(hi, 0, -1), jnp.where(lo, 0, -1)

        def first_chunk(c):
            row = pl.multiple_of(c * ch_rows, ch_rows)
            everything = jnp.full((ch_rows, LANES), -1, jnp.int32)
            alive_ref[pl.ds(row, ch_rows), :] = everything
            return digit_counts(everything, plane(0, row), plane(1, row))

        def radix_pass(q, carry):
            t_u, above, flip_hi, flip_lo = carry

            def chunk(c):
                row = pl.multiple_of(c * ch_rows, ch_rows)
                alive = (alive_ref[pl.ds(row, ch_rows), :]
                         & (plane(2 * q - 2, row) ^ flip_hi) & (plane(2 * q - 1, row) ^ flip_lo))
                alive_ref[pl.ds(row, ch_rows), :] = alive
                return digit_counts(alive, plane(2 * q, row), plane(2 * q + 1, row))

            return decide(q, t_u, above, over_chunks(chunk, 3))

        carry = decide(0, zeros, zeros, over_chunks(first_chunk, 3))
        t_u, above, flip_hi, flip_lo = lax.fori_loop(1, 16, radix_pass, carry)

        def last_chunk(c):
            row = pl.multiple_of(c * ch_rows, ch_rows)
            return (ones_in(alive_ref[pl.ds(row, ch_rows), :]
                            & (plane(30, row) ^ flip_hi) & (plane(31, row) ^ flip_lo)),)

        c_eq = over_chunks(last_chunk, 1)[0]
        few = t_u == 0
        thr = jnp.maximum(t_u ^ INT_MIN, INT_MIN + 1)
        thr_ref[cur_buf] = thr
        need = k_sel - above
        excess = jnp.where(few, 0, c_eq - need)
        pos_bits = seq.bit_length() - 1

        @pl.when(jnp.max(excess) > 0)
        def _():
            def jbit(i, cut):
                cand = cut | jnp.left_shift(jnp.int32(1), pos_bits - 1 - i)
                c = count(lambda kk, pos: (kk == thr) & (pos < cand))
                return jnp.where(c <= need, cand, cut)
            cut = lax.fori_loop(0, pos_bits, jbit, zeros)
            cut = jnp.where(excess > 0, cut, seq)

            def drop_block(kb, _):
                off = pl.multiple_of(kb * kb_sz, kb_sz)
                kk = keys_ref[cur_buf, pl.ds(off, kb_sz), :]
                keys_ref[cur_buf, pl.ds(off, kb_sz), :] = jnp.where(
                    (kk == thr) & (off + row_iota >= cut), INT_MIN, kk)
                return 0

            lax.fori_loop(0, nkb, drop_block, 0)

    reps = ATT_HEADS // ATT_KV_HEADS
    width = reps * LANES
    q_t = _heads_to_lanes(q_ref[...], ATT_HEADS, ATT_HEAD_DIM)

    @pl.when(step == 0)
    def _():
        def kmax_block(kb, best):
            off = pl.multiple_of(kb * kb_sz, kb_sz)
            kf = k_ref[pl.ds(off, kb_sz), :].astype(F32)
            n2 = [jnp.max(jnp.sum(jnp.square(kf[:, g * K_EXT:g * K_EXT + ATT_HEAD_DIM]), axis=1, keepdims=True))
                  for g in range(ATT_KV_HEADS)]
            return tuple(jnp.maximum(b_, n_) for b_, n_ in zip(best, n2))
        best = lax.fori_loop(0, seq // kb_sz, kmax_block, (jnp.float32(0.0),) * ATT_KV_HEADS)
        worst = jnp.float32(0.0)
        for g in range(ATT_KV_HEADS):
            stab_ref[g] = jnp.sqrt(best[g]) * BOUND_SLACK
            worst = jnp.maximum(worst, stab_ref[g] * qcap_ref[0])
        stab_ref[ATT_KV_HEADS] = jnp.where(worst < BOUND_LIMIT, 1.0, 0.0)

    bounded = stab_ref[ATT_KV_HEADS] > 0.5
    qf = q_t.astype(F32)
    q_len = jnp.sqrt(jnp.sum(qf * qf, axis=0, keepdims=True))
    rhs = []
    for g in range(ATT_KV_HEADS):
        cols = slice(g * width, (g + 1) * width)
        neg_bound = jnp.where(bounded, -q_len[:, cols] * stab_ref[g], 0.0)
        rhs.append(jnp.concatenate(
            [qf[:, cols], neg_bound, jnp.zeros((K_EXT - ATT_HEAD_DIM - 1, width), F32)], axis=0).astype(BF16))

    ka = DSA_KA

    def masked_scores(blk):
        off = pl.multiple_of(blk * ka, ka)
        bias = jnp.where(keys_ref[prev_buf, pl.ds(off, ka), :] >= thr_prev, 0.0, NEG_BIG)
        bias = jnp.concatenate([bias] * reps, axis=1)
        kblk = k_ref[pl.ds(off, ka), :]
        return jnp.concatenate(
            [_dot(kblk[:, g * K_EXT:(g + 1) * K_EXT], rhs[g]) + bias for g in range(ATT_KV_HEADS)],
            axis=1)

    def consume_bounded(s_ref, blk):
        off = pl.multiple_of(blk * ka, ka)
        p = jnp.exp2(s_ref[...]).astype(BF16)
        for g in range(ATT_KV_HEADS):
            acc_ref[g] += _dot(vt_ref[g, :, pl.ds(off, ka)], p[:, g * width:(g + 1) * width])

    def consume_online(s_ref, blk, m_old):
        off = pl.multiple_of(blk * ka, ka)
        s = s_ref[...]
        m_new = jnp.maximum(m_old, jnp.max(s, axis=0, keepdims=True))
        alpha = jnp.exp2(m_old - m_new)
        p = jnp.exp2(s - m_new).astype(BF16)
        for g in range(ATT_KV_HEADS):
            vt = vt_ref[g, :, pl.ds(off, ka)]
            acc_ref[g] = (alpha[:, g * width:(g + 1) * width] * acc_ref[g]
                          + _dot(vt, p[:, g * width:(g + 1) * width]))
        return m_new

    n_sub = nkb_prev * (kb_sz // ka)
    thr_prev = thr_ref[prev_buf]

    def att_pair_bounded(j, prefetch=True):
        sb_ref[...] = masked_scores(2 * j + 1)
        consume_bounded(sa_ref, 2 * j)
        if prefetch:
            sa_ref[...] = masked_scores(2 * j + 2)
        consume_bounded(sb_ref, 2 * j + 1)

    @pl.when(has_cur)
    def _():
        la_ref[...] = logits_of(0)

    @pl.when(has_prev)
    def _():
        acc_ref[...] = jnp.zeros_like(acc_ref)
        sa_ref[...] = masked_scores(0)

    n_att = n_sub // 2
    n_fused = jnp.where(has_cur & has_prev & bounded, jnp.minimum(n_pairs - 1, (n_att - 1) // 2), 0)

    def fused(j, c):
        score_pair(j, False)
        att_pair_bounded(2 * j)
        att_pair_bounded(2 * j + 1)
        return c

    lax.fori_loop(0, n_fused, fused, 0)

    @pl.when(has_cur)
    def _():
        lax.fori_loop(n_fused, n_pairs - 1, lambda j, c: (score_pair(j, False), c)[1], 0)
        last_pair()
        select_threshold()

    @pl.when(has_prev & bounded)
    def _():
        lax.fori_loop(2 * n_fused, n_att - 1, lambda j, c: (att_pair_bounded(j), c)[1], 0)
        second_live = (n_sub - 1) * ka < prev * Q_BLOCK + Q_BLOCK

        @pl.when(second_live)
        def _():
            att_pair_bounded(n_att - 1, prefetch=False)

        @pl.when(jnp.logical_not(second_live))
        def _():
            consume_bounded(sa_ref, n_sub - 2)

    @pl.when(has_prev & jnp.logical_not(bounded))
    def _():
        def att_pair(j, m_run):
            sb_ref[...] = masked_scores(2 * j + 1)
            m_run = consume_online(sa_ref, 2 * j, m_run)
            sa_ref[...] = masked_scores(jnp.minimum(2 * j + 2, n_sub - 1))
            return consume_online(sb_ref, 2 * j + 1, m_run)
        lax.fori_loop(0, n_sub // 2, att_pair, jnp.full((1, ATT_HEADS * LANES), NEG_BIG, F32))

    @pl.when(has_prev)
    def _():
        pieces = []
        for g in range(ATT_KV_HEADS):
            a = acc_ref[g]
            o = a[0:ATT_HEAD_DIM, :] / a[ATT_HEAD_DIM:ATT_HEAD_DIM + 1, :]
            pieces += [o[:, r * LANES:(r + 1) * LANES] for r in range(reps)]
        o_ref[...] = jnp.concatenate(pieces, axis=0).T.astype(o_ref.dtype)


def _v_transposed(v):
    b, s, _ = v.shape
    vt = jnp.swapaxes(v.reshape(b, s, ATT_KV_HEADS, ATT_HEAD_DIM), 1, 3)
    vt = jnp.swapaxes(vt, 1, 2)
    ones = jnp.ones((b, ATT_KV_HEADS, 1, s), v.dtype)
    pad = jnp.zeros((b, ATT_KV_HEADS, DSA_VROWS - ATT_HEAD_DIM - 1, s), v.dtype)
    return jnp.concatenate([vt, ones, pad], axis=2)


def _dsa(a3, vt, iwt, qcap):
    b, s, _ = a3.shape
    nb = s // Q_BLOCK
    k_sel = min(TOPK_MAX, s // 4)
    idx_scale = (IDX_HEAD_DIM ** -0.5) * (IDX_HEADS ** -0.5)
    kern = functools.partial(_dsa_kernel, seq=s, k_sel=k_sel, idx_scale=idx_scale)
    return pl.pallas_call(
        kern,
        grid=(b, nb + 1),
        in_specs=[
            pl.BlockSpec(memory_space=pltpu.SMEM),
            pl.BlockSpec((None, Q_BLOCK, 512), lambda bb, i: (bb, jnp.maximum(i - 1, 0), COL_Q // 512)),
            pl.BlockSpec((None, Q_BLOCK, 512), lambda bb, i: (bb, jnp.minimum(i, nb - 1), COL_IQ // 512)),
            pl.BlockSpec((None, s, ATT_KV_HEADS * K_EXT), lambda bb, i: (bb, 0, COL_K // (ATT_KV_HEADS * K_EXT))),
            pl.BlockSpec((None, s, LANES), lambda bb, i: (bb, 0, COL_IK // LANES)),
            pl.BlockSpec((None, ATT_KV_HEADS, DSA_VROWS, s), lambda bb, i: (bb, 0, 0, 0)),
            pl.BlockSpec((None, None, IDX_HEADS, LANES), lambda bb, i: (bb, jnp.minimum(i, nb - 1), 0, 0)),
        ],
        out_specs=pl.BlockSpec((None, Q_BLOCK, MIX_A), lambda bb, i: (bb, jnp.maximum(i - 1, 0), 0)),
        out_shape=jax.ShapeDtypeStruct((b, s, MIX_A), BF16),
        scratch_shapes=[
            pltpu.VMEM((2, s, LANES), jnp.int32),
            pltpu.VMEM((32, s // 32 + PLANE_PAD_ROWS, LANES), jnp.int32),
            pltpu.VMEM((s // 32, LANES), jnp.int32),
            pltpu.VMEM((DSA_KB, IDX_HEADS * LANES), F32),
            pltpu.VMEM((DSA_KB, IDX_HEADS * LANES), F32),
            pltpu.VMEM((ATT_KV_HEADS, DSA_VROWS, (ATT_HEADS // ATT_KV_HEADS) * LANES), F32),
            pltpu.VMEM((DSA_KA, ATT_HEADS * LANES), F32),
            pltpu.VMEM((DSA_KA, ATT_HEADS * LANES), F32),
            pltpu.VMEM((2, 1, LANES), jnp.int32),
            pltpu.SMEM((ATT_KV_HEADS + 1,), F32),
        ],
        compiler_params=pltpu.CompilerParams(
            dimension_semantics=("parallel", "arbitrary"), vmem_limit_bytes=VMEM_LIMIT),
        name="dsa",
    )(qcap, a3, a3, a3, a3, vt, iwt)


ML_L = 128
ML_TS = 512
ML_PAD = 8


def _mlstm_kernel(mx_ref, mv_ref, mo_ref, gr_ref, cw_ref, cb_ref, wq_ref, wkt_ref, brow_ref, hg_ref, o_ref,
                  xbuf_ref, q_ref, kt_ref, s_ref, m_ref):
    t_idx = pl.program_id(1)
    nh, dk, dv, L = MLSTM_HEADS, MLSTM_QK_DIM, MLSTM_V_DIM, ML_L

    @pl.when(t_idx == 0)
    def _():
        xbuf_ref[0:ML_PAD, :] = jnp.zeros((ML_PAD, MIX_B), F32)
        s_ref[...] = jnp.zeros_like(s_ref)
        m_ref[...] = jnp.zeros_like(m_ref)

    xbuf_ref[ML_PAD:ML_PAD + ML_TS, :] = mx_ref[...]
    xc = cb_ref[...] + jnp.zeros((ML_TS, MIX_B), F32)
    for j in range(CONV_WIDTH):
        s0 = ML_PAD - (CONV_WIDTH - 1) + j
        xc = xc + xbuf_ref[s0:s0 + ML_TS, :] * cw_ref[j:j + 1, :]
    xbuf_ref[0:ML_PAD, :] = mx_ref[ML_TS - ML_PAD:ML_TS, :]
    xc = (xc * jax.nn.sigmoid(xc)).astype(BF16)
    for h in range(nh):
        xh = xc[:, h * dv:(h + 1) * dv]
        q_ref[h] = (_dot(xh, wq_ref[h]) * (dk ** -0.5)).astype(BF16)
        kt_ref[h] = _dot_nt(wkt_ref[h], xh)

    gr = gr_ref[...]
    li_r = gr[0:nh, :] + brow_ref[0:nh, :]
    lf_r = jax.nn.log_sigmoid(gr[nh:2 * nh, :] + brow_ref[nh:2 * nh, :])

    ri = lax.broadcasted_iota(jnp.int32, (L, L), 0)
    ci = lax.broadcasted_iota(jnp.int32, (L, L), 1)
    causal = ri >= ci
    ones_v = jnp.ones((L, dv), BF16)
    ones_n = jnp.ones((dv, dv), BF16)

    tr = lax.broadcasted_iota(jnp.int32, (ML_TS, ML_TS), 0)
    tc = lax.broadcasted_iota(jnp.int32, (ML_TS, ML_TS), 1)
    chunk_triu = jnp.where((tr <= tc) & (tr // L == tc // L), 1.0, 0.0).astype(F32)
    b_all = jnp.dot(lf_r, chunk_triu, precision=lax.Precision.HIGHEST, preferred_element_type=F32)
    g_all = li_r - b_all
    in_chunk = lax.broadcasted_iota(jnp.int32, (nh, ML_TS), 1) % L
    run_all = g_all
    for sh in (1, 2, 4, 8, 16, 32, 64):
        run_all = jnp.maximum(run_all, jnp.where(in_chunk >= sh, pltpu.roll(run_all, sh, axis=1), NEG_BIG))

    def to_rows(row):
        return jnp.broadcast_to(row, (L, L)).T

    for c in range(ML_TS // L):
        r0 = c * L
        b_r, g_r, run = b_all[:, r0:r0 + L], g_all[:, r0:r0 + L], run_all[:, r0:r0 + L]
        run_rows = [to_rows(run[h:h + 1, :]) for h in range(nh)]
        b_rows = [to_rows(b_r[h:h + 1, :]) for h in range(nh)]
        m_prev = m_ref[...]
        b_last = b_r[:, L - 1:L]
        wk_r = b_last + g_r
        m_new = jnp.maximum(b_last + m_prev, jnp.max(wk_r, axis=1, keepdims=True))
        ws_r = jnp.exp(wk_r - m_new)
        wc_r = jnp.exp(b_last + m_prev - m_new)
        m_ref[...] = m_new
        outs = []
        for h in range(nh):
            big_m_t = jnp.maximum(run_rows[h], m_prev[h:h + 1, :])
            m_t = big_m_t + b_rows[h]
            decay = jnp.where(causal, jnp.exp(g_r[h:h + 1, :] - big_m_t), 0.0)
            w_int = jnp.exp(m_prev[h:h + 1, :] - big_m_t)
            qh = q_ref[h, r0:r0 + L, :]
            kt = kt_ref[h, :, r0:r0 + L]
            v_ext = jnp.concatenate(
                [mv_ref[r0:r0 + L, h * dv:(h + 1) * dv], ones_v], axis=1)
            wt = _dot(qh, kt.astype(BF16)) * decay
            nd = (_dot(wt.astype(BF16), v_ext)
                  + jnp.concatenate([w_int, w_int], axis=1) * _dot(qh, s_ref[h].astype(BF16)))
            hv = nd[:, 0:dv] / jnp.maximum(jnp.abs(nd[:, dv:2 * dv]), jnp.exp(-m_t))
            ms = _dot((hv * hv).astype(BF16), ones_n) * (1.0 / dv)
            outs.append(hv * lax.rsqrt(ms + EPS) * hg_ref[0:1, h * dv:(h + 1) * dv])
            wc = jnp.broadcast_to(wc_r[h:h + 1, :], (dk, L))
            s_ref[h] = (jnp.concatenate([wc, wc], axis=1) * s_ref[h]
                        + _dot((kt * ws_r[h:h + 1, :]).astype(BF16), v_ext))
        hh = jnp.concatenate(outs, axis=1)
        o_ref[r0:r0 + L, :] = (hh * jax.nn.sigmoid(mo_ref[r0:r0 + L, :].astype(F32))).astype(o_ref.dtype)


def _mlstm(m3, mvo3, gt, conv_w, conv_b, wq, wkt, brow, head_g):
    b, s, _ = m3.shape
    nh = MLSTM_HEADS
    tile = lambda c: pl.BlockSpec((None, ML_TS, MIX_B), lambda bb, t: (bb, t, c))
    const2 = lambda shape: pl.BlockSpec(shape, lambda bb, t: (0, 0))
    const3 = lambda shape: pl.BlockSpec(shape, lambda bb, t: (0, 0, 0))
    return pl.pallas_call(
        _mlstm_kernel,
        grid=(b, s // ML_TS),
        in_specs=[
            tile(0), tile(0), tile(1),
            pl.BlockSpec((None, 2 * nh, ML_TS), lambda bb, t: (bb, 0, t)),
            const2((CONV_WIDTH, MIX_B)), const2((1, MIX_B)),
            const3((nh, MLSTM_V_DIM, MLSTM_QK_DIM)), const3((nh, MLSTM_QK_DIM, MLSTM_V_DIM)),
            const2((2 * nh, 1)), const2((1, MIX_B)),
        ],
        out_specs=pl.BlockSpec((None, ML_TS, MIX_B), lambda bb, t: (bb, t, 0)),
        out_shape=jax.ShapeDtypeStruct((b, s, MIX_B), BF16),
        scratch_shapes=[
            pltpu.VMEM((ML_PAD + ML_TS, MIX_B), F32),
            pltpu.VMEM((nh, ML_TS, MLSTM_QK_DIM), BF16),
            pltpu.VMEM((nh, MLSTM_QK_DIM, ML_TS), F32),
            pltpu.VMEM((nh, MLSTM_QK_DIM, 2 * MLSTM_V_DIM), F32),
            pltpu.VMEM((nh, ML_L), F32),
        ],
        compiler_params=pltpu.CompilerParams(
            dimension_semantics=("parallel", "arbitrary"), vmem_limit_bytes=VMEM_LIMIT),
        name="mlstm",
    )(m3, mvo3, mvo3, gt, conv_w, conv_b, wq, wkt, brow, head_g)


MERGE_TM = 512


def _merge_ffn_kernel(x_ref, ya_ref, yb_ref, ga_ref, gb_ref, wa_ref, wb_ref, wo_ref,
                      g_ref, wg_ref, wu_ref, wd_ref, o_ref):
    merged = (jax.nn.sigmoid(ga_ref[...].astype(F32)) * _dot(ya_ref[...], wa_ref[...])
              + jax.nn.sigmoid(gb_ref[...].astype(F32)) * _dot(yb_ref[...], wb_ref[...]))
    x = x_ref[...] + _dot(merged.astype(BF16), wo_ref[...])
    _half_ffn(x, g_ref, wg_ref, wu_ref, wd_ref, o_ref)


def _merge_ffn(x2d, ya, yb, gates, wa, wb, wo, g, wg, wu, wd):
    m = x2d.shape[0]
    row = lambda i: (i, 0)
    const = lambda i: (0, 0)
    resident = lambda shape: pl.BlockSpec(shape, const, pipeline_mode=pl.Buffered(1))
    return pl.pallas_call(
        _merge_ffn_kernel,
        grid=(m // MERGE_TM,),
        in_specs=[
            pl.BlockSpec((MERGE_TM, D_MODEL), row),
            pl.BlockSpec((MERGE_TM, MIX_A), row),
            pl.BlockSpec((MERGE_TM, MIX_B), row),
            pl.BlockSpec((MERGE_TM, D_MODEL), lambda i: (i, 0)),
            pl.BlockSpec((MERGE_TM, D_MODEL), lambda i: (i, 1)),
            resident((MIX_A, D_MODEL)),
            resident((MIX_B, D_MODEL)),
            resident((D_MODEL, D_MODEL)),
            pl.BlockSpec((1, D_MODEL), const),
            resident((D_MODEL, D_FF)),
            resident((D_MODEL, D_FF)),
            resident((D_FF, D_MODEL)),
        ],
        out_specs=pl.BlockSpec((MERGE_TM, D_MODEL), row),
        out_shape=jax.ShapeDtypeStruct((m, D_MODEL), F32),
        compiler_params=pltpu.CompilerParams(
            dimension_semantics=("parallel",), vmem_limit_bytes=VMEM_LIMIT),
        name="merge_ffn",
    )(x2d, ya, yb, gates, gates, wa, wb, wo, g.reshape(1, D_MODEL), wg, wu, wd)


def _block_diag_ones(n, blk):
    r = jnp.arange(n) // blk
    return (r[:, None] == r[None, :]).astype(BF16)


def _arrange_w_in(w_in):
    widths = (512, 128, 128, 512, 64, 8, 512, 512, 4, 4, 512, 1024, 1024)
    parts, off = [], 0
    for wd in widths:
        parts.append(w_in[:, off:off + wd])
        off += wd
    aq, ak, av, iq, ik, iw, mx, mv, mi, mf, mo, ga, gb = parts
    z = lambda n: jnp.zeros((w_in.shape[0], n), w_in.dtype)
    cols = [aq, iq, ak[:, :64], z(64), ak[:, 64:], z(64), ik, z(64), av,
            iw, mi, mf, z(G_WIDTH - 16),
            mx, mv, mo, ga, gb]
    return jnp.concatenate(cols, axis=1).astype(BF16)


def kernel(x, ffn1_norm, ffn1_w_gate, ffn1_w_up, ffn1_w_down, mix_norm, w_in, q_norm, k_norm,
           idx_k_norm, conv_w, conv_b, w_mq, w_mk, b_i, b_f, m_head_norm, w_proj_a, w_proj_b,
           w_out, ffn2_norm, ffn2_w_gate, ffn2_w_up, ffn2_w_down):
    b, s, _ = x.shape
    m = b * s
    nb = s // Q_BLOCK
    nh = MLSTM_HEADS
    x2d = x.reshape(m, D_MODEL)
    for l in range(ffn1_norm.shape[0]):
        x2d = _ffn(x2d, ffn1_norm[l], ffn1_w_gate[l].astype(BF16), ffn1_w_up[l].astype(BF16),
                   ffn1_w_down[l].astype(BF16))

        gq = (jnp.tile(q_norm[l], ATT_HEADS) * (ATT_HEAD_DIM ** -0.5 * LOG2_E)).reshape(1, 512)
        z64 = jnp.zeros((ATT_HEAD_DIM,), F32)
        one64 = z64.at[0].set(1.0)
        gk = jnp.concatenate([k_norm[l], z64] * ATT_KV_HEADS + [idx_k_norm[l], z64]).reshape(1, -1)
        kone = jnp.concatenate([z64, one64] * ATT_KV_HEADS + [z64, z64]).reshape(1, -1)
        a2d, g2d, mx2d, mvo2d, t2d = _proj(x2d, mix_norm[l], _arrange_w_in(w_in[l]),
                                   _block_diag_ones(512, ATT_HEAD_DIM),
                                   _block_diag_ones(COL_V - COL_K, ATT_HEAD_DIM), gq, gk, kone)

        a3 = a2d.reshape(b, s, A_WIDTH)
        vt = _v_transposed(a3[:, :, COL_V:COL_V + ATT_KV_HEADS * ATT_HEAD_DIM])
        g3 = g2d.reshape(b, s, G_WIDTH)
        iwt = jnp.swapaxes(g3[:, :, 0:IDX_HEADS].reshape(b, nb, Q_BLOCK, IDX_HEADS), 2, 3)
        qcap = (jnp.max(jnp.abs(gq)) * (ATT_HEAD_DIM ** 0.5) * BOUND_SLACK).reshape(1)
        ya = _dsa(a3, vt, iwt, qcap)

        gt = jnp.swapaxes(g3[:, :, IDX_HEADS:IDX_HEADS + 2 * nh], 1, 2)
        bias = jnp.concatenate([b_i[l], b_f[l]])
        yb = _mlstm(mx2d.reshape(b, s, MIX_B), mvo2d.reshape(b, s, 2 * MIX_B), gt,
                    conv_w[l], conv_b[l].reshape(1, MIX_B),
                    w_mq[l].astype(BF16), jnp.swapaxes(w_mk[l], 1, 2).astype(BF16),
                    bias.reshape(2 * nh, 1), m_head_norm[l].reshape(1, MIX_B))

        x2d = _merge_ffn(x2d, ya.reshape(m, MIX_A), yb.reshape(m, MIX_B), t2d,
                         w_proj_a[l].astype(BF16), w_proj_b[l].astype(BF16), w_out[l].astype(BF16),
                         ffn2_norm[l], ffn2_w_gate[l].astype(BF16), ffn2_w_up[l].astype(BF16),
                         ffn2_w_down[l].astype(BF16))
    return x2d.reshape(b, s, D_MODEL)
```

```python
import functools

import jax
import jax.numpy as jnp
from jax import lax
from jax.experimental import pallas as pl
from jax.experimental.pallas import tpu as pltpu

F32 = jnp.float32
BF16 = jnp.bfloat16

D_MODEL = 1024
D_FF = 2816
EPS = 1e-6
ATT_HEADS = 8
ATT_KV_HEADS = 2
ATT_HEAD_DIM = 64
IDX_HEADS = 8
IDX_HEAD_DIM = 64
TOPK_MAX = 256
Q_BLOCK = 128
CHUNK = 64
MLSTM_HEADS = 4
MLSTM_QK_DIM = 64
MLSTM_V_DIM = 128
CONV_WIDTH = 4
MIX_A = ATT_HEADS * ATT_HEAD_DIM
MIX_B = MLSTM_HEADS * MLSTM_V_DIM

LANES = 128
VMEM_LIMIT = 56 * 1024 * 1024
INT_MIN = -2147483648
NEG_BIG = -1e30
LOG2_E = 1.4426950408889634

COL_Q, COL_IQ, COL_K, COL_IK, COL_V = 0, 512, 1024, 1280, 1408
K_EXT = 128
A_WIDTH = 1536
G_WIDTH = 128
M_WIDTH = 3 * MIX_B
T_WIDTH = 2 * D_MODEL
W_TOTAL = A_WIDTH + G_WIDTH + M_WIDTH + T_WIDTH


def _rms(x, g):
    return x * lax.rsqrt(jnp.mean(x * x, axis=-1, keepdims=True) + EPS) * g


def _dot(a, b):
    return jnp.dot(a, b, preferred_element_type=F32)


def _dot_nt(a, b):
    return lax.dot_general(a, b, (((1,), (1,)), ((), ())), preferred_element_type=F32)


def _dot_tn(a, b):
    return lax.dot_general(a, b, (((0,), (0,)), ((), ())), preferred_element_type=F32)


FFN_TM = 512
FFN_TF = 2816


def _ffn_kernel(x_ref, g_ref, wg_ref, wu_ref, wd_ref, o_ref):
    _half_ffn(x_ref[...], g_ref, wg_ref, wu_ref, wd_ref, o_ref)


def _half_ffn(x, g_ref, wg_ref, wu_ref, wd_ref, o_ref):
    h = _rms(x, g_ref[...]).astype(BF16)
    acc = None
    for c in range(D_FF // FFN_TF):
        cols = slice(c * FFN_TF, (c + 1) * FFN_TF)
        a = _dot(h, wg_ref[:, cols])
        u = _dot(h, wu_ref[:, cols])
        act = (a * jax.nn.sigmoid(a) * u).astype(BF16)
        part = _dot(act, wd_ref[cols, :])
        acc = part if acc is None else acc + part
    o_ref[...] = x + 0.5 * acc


def _ffn(x2d, g, wg, wu, wd):
    m = x2d.shape[0]
    const = lambda i: (0, 0)
    resident = lambda shape: pl.BlockSpec(shape, const, pipeline_mode=pl.Buffered(1))
    return pl.pallas_call(
        _ffn_kernel,
        grid=(m // FFN_TM,),
        in_specs=[
            pl.BlockSpec((FFN_TM, D_MODEL), lambda i: (i, 0)),
            pl.BlockSpec((1, D_MODEL), const),
            resident((D_MODEL, D_FF)),
            resident((D_MODEL, D_FF)),
            resident((D_FF, D_MODEL)),
        ],
        out_specs=pl.BlockSpec((FFN_TM, D_MODEL), lambda i: (i, 0)),
        out_shape=jax.ShapeDtypeStruct((m, D_MODEL), F32),
        compiler_params=pltpu.CompilerParams(
            dimension_semantics=("parallel",), vmem_limit_bytes=VMEM_LIMIT),
        name="ffn",
    )(x2d, g.reshape(1, D_MODEL), wg, wu, wd)


PROJ_TM = 512


def _proj_kernel(x_ref, g_ref, w_ref, bdq_ref, bdk_ref, gq_ref, gk_ref, kone_ref,
                 a_ref, g_out_ref, mx_ref, mvo_ref, t_ref):
    h = _rms(x_ref[...], g_ref[...]).astype(BF16)

    def head_norm(p, bd_ref, gain_ref):
        ss = _dot((p * p).astype(BF16), bd_ref[...]) * (1.0 / ATT_HEAD_DIM)
        return p * lax.rsqrt(ss + EPS) * gain_ref[...]

    pq = _dot(h, w_ref[:, COL_Q:COL_IQ])
    a_ref[:, COL_Q:COL_IQ] = head_norm(pq, bdq_ref, gq_ref).astype(BF16)
    a_ref[:, COL_IQ:COL_K] = _dot(h, w_ref[:, COL_IQ:COL_K]).astype(BF16)
    pk = _dot(h, w_ref[:, COL_K:COL_V])
    a_ref[:, COL_K:COL_V] = (head_norm(pk, bdk_ref, gk_ref) + kone_ref[...]).astype(BF16)
    a_ref[:, COL_V:A_WIDTH] = _dot(h, w_ref[:, COL_V:A_WIDTH]).astype(BF16)
    o = A_WIDTH
    g_out_ref[...] = _dot(h, w_ref[:, o:o + G_WIDTH])
    o += G_WIDTH
    mx_ref[...] = _dot(h, w_ref[:, o:o + MIX_B])
    o += MIX_B
    for c in range(2):
        mvo_ref[:, c * MIX_B:(c + 1) * MIX_B] = _dot(h, w_ref[:, o + c * MIX_B:o + (c + 1) * MIX_B]).astype(BF16)
    o += 2 * MIX_B
    for c in range(2):
        t_ref[:, c * D_MODEL:(c + 1) * D_MODEL] = _dot(
            h, w_ref[:, o + c * D_MODEL:o + (c + 1) * D_MODEL]).astype(BF16)


def _proj(x2d, g, w_all, bdq, bdk, gq, gk, kone):
    m = x2d.shape[0]
    const = lambda i: (0, 0)
    row = lambda i: (i, 0)
    return pl.pallas_call(
        _proj_kernel,
        grid=(m // PROJ_TM,),
        in_specs=[
            pl.BlockSpec((PROJ_TM, D_MODEL), row),
            pl.BlockSpec((1, D_MODEL), const),
            pl.BlockSpec((D_MODEL, W_TOTAL), const, pipeline_mode=pl.Buffered(1)),
            pl.BlockSpec((512, 512), const),
            pl.BlockSpec((COL_V - COL_K, COL_V - COL_K), const),
            pl.BlockSpec((1, 512), const),
            pl.BlockSpec((1, COL_V - COL_K), const),
            pl.BlockSpec((1, COL_V - COL_K), const),
        ],
        out_specs=[
            pl.BlockSpec((PROJ_TM, A_WIDTH), row),
            pl.BlockSpec((PROJ_TM, G_WIDTH), row),
            pl.BlockSpec((PROJ_TM, MIX_B), row),
            pl.BlockSpec((PROJ_TM, 2 * MIX_B), row),
            pl.BlockSpec((PROJ_TM, T_WIDTH), row),
        ],
        out_shape=[
            jax.ShapeDtypeStruct((m, A_WIDTH), BF16),
            jax.ShapeDtypeStruct((m, G_WIDTH), F32),
            jax.ShapeDtypeStruct((m, MIX_B), F32),
            jax.ShapeDtypeStruct((m, 2 * MIX_B), BF16),
            jax.ShapeDtypeStruct((m, T_WIDTH), BF16),
        ],
        compiler_params=pltpu.CompilerParams(
            dimension_semantics=("parallel",), vmem_limit_bytes=VMEM_LIMIT),
        name="proj",
    )(x2d, g.reshape(1, D_MODEL), w_all, bdq, bdk, gq, gk, kone)


DSA_KB = 512
DSA_KA = 256
BOUND_LIMIT = 60.0
BOUND_SLACK = 1.01
DSA_VROWS = 80

BIT_GROUP = 256
SEARCH_CHUNK = 1024
PLANE_PAD_ROWS = 8


def _bit_transpose(words):
    a = list(words)
    j, m = 16, 0x0000FFFF
    while j:
        k = 0
        while k < 32:
            t = (a[k] ^ (a[k + j] >> j)) & m
            a[k] = a[k] ^ t
            a[k + j] = a[k + j] ^ (t << j)
            k = (k + j + 1) & ~j
        j >>= 1
        m = (m ^ (m << j)) & 0xFFFFFFFF
    return a


def _heads_to_lanes(x, n_heads, dh):
    xf = x.astype(F32)
    return jnp.concatenate([xf[:, h * dh:(h + 1) * dh].T for h in range(n_heads)], axis=1).astype(BF16)


def _dsa_kernel(qcap_ref, q_ref, iq_ref, k_ref, ik_ref, vt_ref, iwt_ref, o_ref,
                keys_ref, planes_ref, alive_ref, la_ref, lb_ref, acc_ref, sa_ref, sb_ref, thr_ref, stab_ref,
                *, seq, k_sel, idx_scale):
    kb_sz = DSA_KB
    n_blocks = seq // Q_BLOCK
    step = pl.program_id(1)
    has_cur = step < n_blocks
    has_prev = step >= 1
    cur = jnp.minimum(step, n_blocks - 1)
    prev = jnp.maximum(step - 1, 0)
    cur_buf = cur % 2
    prev_buf = prev % 2
    nkb = (cur * Q_BLOCK + Q_BLOCK + kb_sz - 1) // kb_sz
    nkb_prev = (prev * Q_BLOCK + Q_BLOCK + kb_sz - 1) // kb_sz
    lane = lax.broadcasted_iota(jnp.int32, (1, LANES), 1)
    limit = cur * Q_BLOCK + jnp.where(lane < CHUNK, CHUNK, 2 * CHUNK)
    row_iota = lax.broadcasted_iota(jnp.int32, (kb_sz, LANES), 0)

    iq_t = _heads_to_lanes(iq_ref[...], IDX_HEADS, IDX_HEAD_DIM)
    w = iwt_ref[...]

    def logits_of(blk):
        off = pl.multiple_of(blk * kb_sz, kb_sz)
        return _dot(ik_ref[pl.ds(off, kb_sz), :][:, :IDX_HEAD_DIM], iq_t)

    def keys_from(l_ref, blk, masked):
        off = pl.multiple_of(blk * kb_sz, kb_sz)
        sc = jnp.zeros((kb_sz, LANES), F32)
        for h in range(IDX_HEADS):
            sc = sc + jnp.maximum(l_ref[:, h * LANES:(h + 1) * LANES], 0.0) * w[h:h + 1, :]
        sc = sc * idx_scale
        bits = pltpu.bitcast(sc, jnp.int32)
        key = bits ^ ((bits >> 31) & 0x7FFFFFFF)
        key = jnp.where(sc == 0.0, 0, key)
        if masked:
            key = jnp.where(off + row_iota < limit, key, INT_MIN)
        keys_ref[cur_buf, pl.ds(off, kb_sz), :] = key
        ukey = key ^ INT_MIN
        for grp in range(kb_sz // BIT_GROUP):
            base = grp * BIT_GROUP
            planes = _bit_transpose([ukey[base + 8 * i:base + 8 * i + 8, :] for i in range(32)])
            row = pl.multiple_of((off + base) // 32, 8)
            for p in range(32):
                planes_ref[p, pl.ds(row, 8), :] = planes[p]

    n_pairs = (nkb + 1) // 2

    def score_pair(j, masked):
        lb_ref[...] = logits_of(2 * j + 1)
        keys_from(la_ref, 2 * j, masked)
        la_ref[...] = logits_of(jnp.minimum(2 * j + 2, 2 * n_pairs - 1))
        keys_from(lb_ref, 2 * j + 1, masked)

    def last_pair():
        j = n_pairs - 1

        @pl.when(nkb % 2 == 0)
        def _():
            score_pair(j, True)

        @pl.when(nkb % 2 == 1)
        def _():
            keys_from(la_ref, 2 * j, True)
            off = pl.multiple_of((2 * j + 1) * kb_sz, kb_sz)
            keys_ref[cur_buf, pl.ds(off, kb_sz), :] = jnp.full((kb_sz, LANES), INT_MIN, jnp.int32)
            row = pl.multiple_of(off // 32, kb_sz // 32)
            for p in range(32):
                planes_ref[p, pl.ds(row, kb_sz // 32), :] = jnp.zeros((kb_sz // 32, LANES), jnp.int32)

    def count(pred):
        def body(kb, c):
            off = pl.multiple_of(kb * kb_sz, kb_sz)
            ind = jnp.where(pred(keys_ref[cur_buf, pl.ds(off, kb_sz), :], off + row_iota), 1, 0)
            return c + jnp.sum(ind.reshape(kb_sz // 8, 8, LANES), axis=0)
        c8 = lax.fori_loop(0, nkb, body, jnp.zeros((8, LANES), jnp.int32))
        return jnp.sum(c8, axis=0, keepdims=True)

    def select_threshold():
        zeros = jnp.zeros((1, LANES), jnp.int32)
        ch_rows = SEARCH_CHUNK // 32
        n_ch = (nkb + 1) // 2
        zeros8 = jnp.zeros((8, LANES), jnp.int32)

        def ones_in(x):
            return jnp.sum(lax.population_count(x).reshape(ch_rows // 8, 8, LANES), axis=0)

        def over_chunks(body, n_out):
            tot = lax.fori_loop(0, n_ch, lambda c, cnt: tuple(a + b_ for a, b_ in zip(cnt, body(c))),
                                (zeros8,) * n_out)
            return [jnp.sum(t, axis=0, keepdims=True) for t in tot]

        def plane(p, row):
            return planes_ref[p, pl.ds(row, ch_rows), :]

        def digit_counts(alive, p_hi, p_lo):
            x1 = alive & p_hi
            return ones_in(x1), ones_in(x1 & p_lo), ones_in((alive ^ x1) & p_lo)

        def decide(q, t_u, above, counts):
            c1, c11, c01 = counts
            hi = above + c1 >= k_sel
            lo = jnp.where(hi, above + c11, above + c1 + c01) >= k_sel
            above = jnp.where(hi, jnp.where(lo, above, above + c11),
                              jnp.where(lo, above + c1, above + c1 + c01))
            t_u = (t_u | jnp.where(hi, jnp.left_shift(jnp.int32(1), 31 - 2 * q), 0)
                   | jnp.where(lo, jnp.left_shift(jnp.int32(1), 30 - 2 * q), 0))
            return t_u, above, jnp.where(hi, 0, -1), jnp.where(lo, 0, -1)

        def first_chunk(c):
            row = pl.multiple_of(c * ch_rows, ch_rows)
            everything = jnp.full((ch_rows, LANES), -1, jnp.int32)
            alive_ref[pl.ds(row, ch_rows), :] = everything
            return digit_counts(everything, plane(0, row), plane(1, row))

        def radix_pass(q, carry):
            t_u, above, flip_hi, flip_lo = carry

            def chunk(c):
                row = pl.multiple_of(c * ch_rows, ch_rows)
                alive = (alive_ref[pl.ds(row, ch_rows), :]
                         & (plane(2 * q - 2, row) ^ flip_hi) & (plane(2 * q - 1, row) ^ flip_lo))
                alive_ref[pl.ds(row, ch_rows), :] = alive
                return digit_counts(alive, plane(2 * q, row), plane(2 * q + 1, row))

            return decide(q, t_u, above, over_chunks(chunk, 3))

        carry = decide(0, zeros, zeros, over_chunks(first_chunk, 3))
        t_u, above, flip_hi, flip_lo = lax.fori_loop(1, 16, radix_pass, carry)

        def last_chunk(c):
            row = pl.multiple_of(c * ch_rows, ch_rows)
            return (ones_in(alive_ref[pl.ds(row, ch_rows), :]
                            & (plane(30, row) ^ flip_hi) & (plane(31, row) ^ flip_lo)),)

        c_eq = over_chunks(last_chunk, 1)[0]
        few = t_u == 0
        thr = jnp.maximum(t_u ^ INT_MIN, INT_MIN + 1)
        thr_ref[cur_buf] = thr
        need = k_sel - above
        excess = jnp.where(few, 0, c_eq - need)
        pos_bits = seq.bit_length() - 1

        @pl.when(jnp.max(excess) > 0)
        def _():
            def jbit(i, cut):
                cand = cut | jnp.left_shift(jnp.int32(1), pos_bits - 1 - i)
                c = count(lambda kk, pos: (kk == thr) & (pos < cand))
                return jnp.where(c <= need, cand, cut)
            cut = lax.fori_loop(0, pos_bits, jbit, zeros)
            cut = jnp.where(excess > 0, cut, seq)

            def drop_block(kb, _):
                off = pl.multiple_of(kb * kb_sz, kb_sz)
                kk = keys_ref[cur_buf, pl.ds(off, kb_sz), :]
                keys_ref[cur_buf, pl.ds(off, kb_sz), :] = jnp.where(
                    (kk == thr) & (off + row_iota >= cut), INT_MIN, kk)
                return 0

            lax.fori_loop(0, nkb, drop_block, 0)

    reps = ATT_HEADS // ATT_KV_HEADS
    width = reps * LANES
    q_t = _heads_to_lanes(q_ref[...], ATT_HEADS, ATT_HEAD_DIM)

    @pl.when(step == 0)
    def _():
        def kmax_block(kb, best):
            off = pl.multiple_of(kb * kb_sz, kb_sz)
            kf = k_ref[pl.ds(off, kb_sz), :].astype(F32)
            n2 = [jnp.max(jnp.sum(jnp.square(kf[:, g * K_EXT:g * K_EXT + ATT_HEAD_DIM]), axis=1, keepdims=True))
                  for g in range(ATT_KV_HEADS)]
            return tuple(jnp.maximum(b_, n_) for b_, n_ in zip(best, n2))
        best = lax.fori_loop(0, seq // kb_sz, kmax_block, (jnp.float32(0.0),) * ATT_KV_HEADS)
        worst = jnp.float32(0.0)
        for g in range(ATT_KV_HEADS):
            stab_ref[g] = jnp.sqrt(best[g]) * BOUND_SLACK
            worst = jnp.maximum(worst, stab_ref[g] * qcap_ref[0])
        stab_ref[ATT_KV_HEADS] = jnp.where(worst < BOUND_LIMIT, 1.0, 0.0)

    bounded = stab_ref[ATT_KV_HEADS] > 0.5
    qf = q_t.astype(F32)
    q_len = jnp.sqrt(jnp.sum(qf * qf, axis=0, keepdims=True))
    rhs = []
    for g in range(ATT_KV_HEADS):
        cols = slice(g * width, (g + 1) * width)
        neg_bound = jnp.where(bounded, -q_len[:, cols] * stab_ref[g], 0.0)
        rhs.append(jnp.concatenate(
            [qf[:, cols], neg_bound, jnp.zeros((K_EXT - ATT_HEAD_DIM - 1, width), F32)], axis=0).astype(BF16))

    ka = DSA_KA

    def masked_scores(blk):
        off = pl.multiple_of(blk * ka, ka)
        bias = jnp.where(keys_ref[prev_buf, pl.ds(off, ka), :] >= thr_prev, 0.0, NEG_BIG)
        bias = jnp.concatenate([bias] * reps, axis=1)
        kblk = k_ref[pl.ds(off, ka), :]
        return jnp.concatenate(
            [_dot(kblk[:, g * K_EXT:(g + 1) * K_EXT], rhs[g]) + bias for g in range(ATT_KV_HEADS)],
            axis=1)

    def consume_bounded(s_ref, blk):
        off = pl.multiple_of(blk * ka, ka)
        p = jnp.exp2(s_ref[...]).astype(BF16)
        for g in range(ATT_KV_HEADS):
            acc_ref[g] += _dot(vt_ref[g, :, pl.ds(off, ka)], p[:, g * width:(g + 1) * width])

    def consume_online(s_ref, blk, m_old):
        off = pl.multiple_of(blk * ka, ka)
        s = s_ref[...]
        m_new = jnp.maximum(m_old, jnp.max(s, axis=0, keepdims=True))
        alpha = jnp.exp2(m_old - m_new)
        p = jnp.exp2(s - m_new).astype(BF16)
        for g in range(ATT_KV_HEADS):
            vt = vt_ref[g, :, pl.ds(off, ka)]
            acc_ref[g] = (alpha[:, g * width:(g + 1) * width] * acc_ref[g]
                          + _dot(vt, p[:, g * width:(g + 1) * width]))
        return m_new

    n_sub = nkb_prev * (kb_sz // ka)
    thr_prev = thr_ref[prev_buf]

    def att_pair_bounded(j, prefetch=True):
        sb_ref[...] = masked_scores(2 * j + 1)
        consume_bounded(sa_ref, 2 * j)
        if prefetch:
            sa_ref[...] = masked_scores(2 * j + 2)
        consume_bounded(sb_ref, 2 * j + 1)

    @pl.when(has_cur)
    def _():
        la_ref[...] = logits_of(0)

    @pl.when(has_prev)
    def _():
        acc_ref[...] = jnp.zeros_like(acc_ref)
        sa_ref[...] = masked_scores(0)

    n_att = n_sub // 2
    n_fused = jnp.where(has_cur & has_prev & bounded, jnp.minimum(n_pairs - 1, (n_att - 1) // 2), 0)

    def fused(j, c):
        score_pair(j, False)
        att_pair_bounded(2 * j)
        att_pair_bounded(2 * j + 1)
        return c

    lax.fori_loop(0, n_fused, fused, 0)

    @pl.when(has_cur)
    def _():
        lax.fori_loop(n_fused, n_pairs - 1, lambda j, c: (score_pair(j, False), c)[1], 0)
        last_pair()
        select_threshold()

    @pl.when(has_prev & bounded)
    def _():
        lax.fori_loop(2 * n_fused, n_att - 1, lambda j, c: (att_pair_bounded(j), c)[1], 0)
        att_pair_bounded(n_att - 1, prefetch=False)

    @pl.when(has_prev & jnp.logical_not(bounded))
    def _():
        def att_pair(j, m_run):
            sb_ref[...] = masked_scores(2 * j + 1)
            m_run = consume_online(sa_ref, 2 * j, m_run)
            sa_ref[...] = masked_scores(jnp.minimum(2 * j + 2, n_sub - 1))
            return consume_online(sb_ref, 2 * j + 1, m_run)
        lax.fori_loop(0, n_sub // 2, att_pair, jnp.full((1, ATT_HEADS * LANES), NEG_BIG, F32))

    @pl.when(has_prev)
    def _():
        pieces = []
        for g in range(ATT_KV_HEADS):
            a = acc_ref[g]
            o = a[0:ATT_HEAD_DIM, :] / a[ATT_HEAD_DIM:ATT_HEAD_DIM + 1, :]
            pieces += [o[:, r * LANES:(r + 1) * LANES] for r in range(reps)]
        o_ref[...] = jnp.concatenate(pieces, axis=0).T.astype(o_ref.dtype)


def _v_transposed(v):
    b, s, _ = v.shape
    vt = jnp.swapaxes(v.reshape(b, s, ATT_KV_HEADS, ATT_HEAD_DIM), 1, 3)
    vt = jnp.swapaxes(vt, 1, 2)
    ones = jnp.ones((b, ATT_KV_HEADS, 1, s), v.dtype)
    pad = jnp.zeros((b, ATT_KV_HEADS, DSA_VROWS - ATT_HEAD_DIM - 1, s), v.dtype)
    return jnp.concatenate([vt, ones, pad], axis=2)


def _dsa(a3, vt, iwt, qcap):
    b, s, _ = a3.shape
    nb = s // Q_BLOCK
    k_sel = min(TOPK_MAX, s // 4)
    idx_scale = (IDX_HEAD_DIM ** -0.5) * (IDX_HEADS ** -0.5)
    kern = functools.partial(_dsa_kernel, seq=s, k_sel=k_sel, idx_scale=idx_scale)
    return pl.pallas_call(
        kern,
        grid=(b, nb + 1),
        in_specs=[
            pl.BlockSpec(memory_space=pltpu.SMEM),
            pl.BlockSpec((None, Q_BLOCK, 512), lambda bb, i: (bb, jnp.maximum(i - 1, 0), COL_Q // 512)),
            pl.BlockSpec((None, Q_BLOCK, 512), lambda bb, i: (bb, jnp.minimum(i, nb - 1), COL_IQ // 512)),
            pl.BlockSpec((None, s, ATT_KV_HEADS * K_EXT), lambda bb, i: (bb, 0, COL_K // (ATT_KV_HEADS * K_EXT))),
            pl.BlockSpec((None, s, LANES), lambda bb, i: (bb, 0, COL_IK // LANES)),
            pl.BlockSpec((None, ATT_KV_HEADS, DSA_VROWS, s), lambda bb, i: (bb, 0, 0, 0)),
            pl.BlockSpec((None, None, IDX_HEADS, LANES), lambda bb, i: (bb, jnp.minimum(i, nb - 1), 0, 0)),
        ],
        out_specs=pl.BlockSpec((None, Q_BLOCK, MIX_A), lambda bb, i: (bb, jnp.maximum(i - 1, 0), 0)),
        out_shape=jax.ShapeDtypeStruct((b, s, MIX_A), BF16),
        scratch_shapes=[
            pltpu.VMEM((2, s, LANES), jnp.int32),
            pltpu.VMEM((32, s // 32 + PLANE_PAD_ROWS, LANES), jnp.int32),
            pltpu.VMEM((s // 32, LANES), jnp.int32),
            pltpu.VMEM((DSA_KB, IDX_HEADS * LANES), F32),
            pltpu.VMEM((DSA_KB, IDX_HEADS * LANES), F32),
            pltpu.VMEM((ATT_KV_HEADS, DSA_VROWS, (ATT_HEADS // ATT_KV_HEADS) * LANES), F32),
            pltpu.VMEM((DSA_KA, ATT_HEADS * LANES), F32),
            pltpu.VMEM((DSA_KA, ATT_HEADS * LANES), F32),
            pltpu.VMEM((2, 1, LANES), jnp.int32),
            pltpu.SMEM((ATT_KV_HEADS + 1,), F32),
        ],
        compiler_params=pltpu.CompilerParams(
            dimension_semantics=("parallel", "arbitrary"), vmem_limit_bytes=VMEM_LIMIT),
        name="dsa",
    )(qcap, a3, a3, a3, a3, vt, iwt)


ML_L = 128
ML_TS = 1024
ML_PAD = 8


def _mlstm_kernel(mx_ref, mv_ref, mo_ref, gr_ref, cw_ref, cb_ref, wq_ref, wkt_ref, brow_ref, hg_ref, o_ref,
                  xbuf_ref, q_ref, kt_ref, s_ref, m_ref):
    t_idx = pl.program_id(1)
    nh, dk, dv, L = MLSTM_HEADS, MLSTM_QK_DIM, MLSTM_V_DIM, ML_L

    @pl.when(t_idx == 0)
    def _():
        xbuf_ref[0:ML_PAD, :] = jnp.zeros((ML_PAD, MIX_B), F32)
        s_ref[...] = jnp.zeros_like(s_ref)
        m_ref[...] = jnp.zeros_like(m_ref)

    xbuf_ref[ML_PAD:ML_PAD + ML_TS, :] = mx_ref[...]
    xc = cb_ref[...] + jnp.zeros((ML_TS, MIX_B), F32)
    for j in range(CONV_WIDTH):
        s0 = ML_PAD - (CONV_WIDTH - 1) + j
        xc = xc + xbuf_ref[s0:s0 + ML_TS, :] * cw_ref[j:j + 1, :]
    xbuf_ref[0:ML_PAD, :] = mx_ref[ML_TS - ML_PAD:ML_TS, :]
    xc = (xc * jax.nn.sigmoid(xc)).astype(BF16)
    for h in range(nh):
        xh = xc[:, h * dv:(h + 1) * dv]
        q_ref[h] = (_dot(xh, wq_ref[h]) * (dk ** -0.5)).astype(BF16)
        kt_ref[h] = _dot_nt(wkt_ref[h], xh)

    gr = gr_ref[...]
    li_r = gr[0:nh, :] + brow_ref[0:nh, :]
    lf_r = jax.nn.log_sigmoid(gr[nh:2 * nh, :] + brow_ref[nh:2 * nh, :])

    ri = lax.broadcasted_iota(jnp.int32, (L, L), 0)
    ci = lax.broadcasted_iota(jnp.int32, (L, L), 1)
    causal = ri >= ci
    ones_v = jnp.ones((L, dv), BF16)
    ones_n = jnp.ones((dv, dv), BF16)

    tr = lax.broadcasted_iota(jnp.int32, (ML_TS, ML_TS), 0)
    tc = lax.broadcasted_iota(jnp.int32, (ML_TS, ML_TS), 1)
    chunk_triu = jnp.where((tr <= tc) & (tr // L == tc // L), 1.0, 0.0).astype(F32)
    b_all = jnp.dot(lf_r, chunk_triu, precision=lax.Precision.HIGHEST, preferred_element_type=F32)
    g_all = li_r - b_all
    in_chunk = lax.broadcasted_iota(jnp.int32, (nh, ML_TS), 1) % L
    run_all = g_all
    for sh in (1, 2, 4, 8, 16, 32, 64):
        run_all = jnp.maximum(run_all, jnp.where(in_chunk >= sh, pltpu.roll(run_all, sh, axis=1), NEG_BIG))

    def to_rows(row):
        return jnp.broadcast_to(row, (L, L)).T

    for c in range(ML_TS // L):
        r0 = c * L
        b_r, g_r, run = b_all[:, r0:r0 + L], g_all[:, r0:r0 + L], run_all[:, r0:r0 + L]
        run_rows = [to_rows(run[h:h + 1, :]) for h in range(nh)]
        b_rows = [to_rows(b_r[h:h + 1, :]) for h in range(nh)]
        m_prev = m_ref[...]
        b_last = b_r[:, L - 1:L]
        wk_r = b_last + g_r
        m_new = jnp.maximum(b_last + m_prev, jnp.max(wk_r, axis=1, keepdims=True))
        ws_r = jnp.exp(wk_r - m_new)
        wc_r = jnp.exp(b_last + m_prev - m_new)
        m_ref[...] = m_new
        outs = []
        for h in range(nh):
            big_m_t = jnp.maximum(run_rows[h], m_prev[h:h + 1, :])
            m_t = big_m_t + b_rows[h]
            decay = jnp.where(causal, jnp.exp(g_r[h:h + 1, :] - big_m_t), 0.0)
            w_int = jnp.exp(m_prev[h:h + 1, :] - big_m_t)
            qh = q_ref[h, r0:r0 + L, :]
            kt = kt_ref[h, :, r0:r0 + L]
            v_ext = jnp.concatenate(
                [mv_ref[r0:r0 + L, h * dv:(h + 1) * dv], ones_v], axis=1)
            wt = _dot(qh, kt.astype(BF16)) * decay
            nd = (_dot(wt.astype(BF16), v_ext)
                  + jnp.concatenate([w_int, w_int], axis=1) * _dot(qh, s_ref[h].astype(BF16)))
            hv = nd[:, 0:dv] / jnp.maximum(jnp.abs(nd[:, dv:2 * dv]), jnp.exp(-m_t))
            ms = _dot((hv * hv).astype(BF16), ones_n) * (1.0 / dv)
            outs.append(hv * lax.rsqrt(ms + EPS) * hg_ref[0:1, h * dv:(h + 1) * dv])
            wc = jnp.broadcast_to(wc_r[h:h + 1, :], (dk, L))
            s_ref[h] = (jnp.concatenate([wc, wc], axis=1) * s_ref[h]
                        + _dot((kt * ws_r[h:h + 1, :]).astype(BF16), v_ext))
        hh = jnp.concatenate(outs, axis=1)
        o_ref[r0:r0 + L, :] = (hh * jax.nn.sigmoid(mo_ref[r0:r0 + L, :].astype(F32))).astype(o_ref.dtype)


def _mlstm(m3, mvo3, gt, conv_w, conv_b, wq, wkt, brow, head_g):
    b, s, _ = m3.shape
    nh = MLSTM_HEADS
    tile = lambda c: pl.BlockSpec((None, ML_TS, MIX_B), lambda bb, t: (bb, t, c))
    const2 = lambda shape: pl.BlockSpec(shape, lambda bb, t: (0, 0))
    const3 = lambda shape: pl.BlockSpec(shape, lambda bb, t: (0, 0, 0))
    return pl.pallas_call(
        _mlstm_kernel,
        grid=(b, s // ML_TS),
        in_specs=[
            tile(0), tile(0), tile(1),
            pl.BlockSpec((None, 2 * nh, ML_TS), lambda bb, t: (bb, 0, t)),
            const2((CONV_WIDTH, MIX_B)), const2((1, MIX_B)),
            const3((nh, MLSTM_V_DIM, MLSTM_QK_DIM)), const3((nh, MLSTM_QK_DIM, MLSTM_V_DIM)),
            const2((2 * nh, 1)), const2((1, MIX_B)),
        ],
        out_specs=pl.BlockSpec((None, ML_TS, MIX_B), lambda bb, t: (bb, t, 0)),
        out_shape=jax.ShapeDtypeStruct((b, s, MIX_B), BF16),
        scratch_shapes=[
            pltpu.VMEM((ML_PAD + ML_TS, MIX_B), F32),
            pltpu.VMEM((nh, ML_TS, MLSTM_QK_DIM), BF16),
            pltpu.VMEM((nh, MLSTM_QK_DIM, ML_TS), F32),
            pltpu.VMEM((nh, MLSTM_QK_DIM, 2 * MLSTM_V_DIM), F32),
            pltpu.VMEM((nh, ML_L), F32),
        ],
        compiler_params=pltpu.CompilerParams(
            dimension_semantics=("parallel", "arbitrary"), vmem_limit_bytes=VMEM_LIMIT),
        name="mlstm",
    )(m3, mvo3, mvo3, gt, conv_w, conv_b, wq, wkt, brow, head_g)


MERGE_TM = 512


def _merge_ffn_kernel(x_ref, ya_ref, yb_ref, ga_ref, gb_ref, wa_ref, wb_ref, wo_ref,
                      g_ref, wg_ref, wu_ref, wd_ref, o_ref):
    merged = (jax.nn.sigmoid(ga_ref[...].astype(F32)) * _dot(ya_ref[...], wa_ref[...])
              + jax.nn.sigmoid(gb_ref[...].astype(F32)) * _dot(yb_ref[...], wb_ref[...]))
    x = x_ref[...] + _dot(merged.astype(BF16), wo_ref[...])
    _half_ffn(x, g_ref, wg_ref, wu_ref, wd_ref, o_ref)


def _merge_ffn(x2d, ya, yb, gates, wa, wb, wo, g, wg, wu, wd):
    m = x2d.shape[0]
    row = lambda i: (i, 0)
    const = lambda i: (0, 0)
    resident = lambda shape: pl.BlockSpec(shape, const, pipeline_mode=pl.Buffered(1))
    return pl.pallas_call(
        _merge_ffn_kernel,
        grid=(m // MERGE_TM,),
        in_specs=[
            pl.BlockSpec((MERGE_TM, D_MODEL), row),
            pl.BlockSpec((MERGE_TM, MIX_A), row),
            pl.BlockSpec((MERGE_TM, MIX_B), row),
            pl.BlockSpec((MERGE_TM, D_MODEL), lambda i: (i, 0)),
            pl.BlockSpec((MERGE_TM, D_MODEL), lambda i: (i, 1)),
            resident((MIX_A, D_MODEL)),
            resident((MIX_B, D_MODEL)),
            resident((D_MODEL, D_MODEL)),
            pl.BlockSpec((1, D_MODEL), const),
            resident((D_MODEL, D_FF)),
            resident((D_MODEL, D_FF)),
            resident((D_FF, D_MODEL)),
        ],
        out_specs=pl.BlockSpec((MERGE_TM, D_MODEL), row),
        out_shape=jax.ShapeDtypeStruct((m, D_MODEL), F32),
        compiler_params=pltpu.CompilerParams(
            dimension_semantics=("parallel",), vmem_limit_bytes=VMEM_LIMIT),
        name="merge_ffn",
    )(x2d, ya, yb, gates, gates, wa, wb, wo, g.reshape(1, D_MODEL), wg, wu, wd)


def _block_diag_ones(n, blk):
    r = jnp.arange(n) // blk
    return (r[:, None] == r[None, :]).astype(BF16)


def _arrange_w_in(w_in):
    widths = (512, 128, 128, 512, 64, 8, 512, 512, 4, 4, 512, 1024, 1024)
    parts, off = [], 0
    for wd in widths:
        parts.append(w_in[:, off:off + wd])
        off += wd
    aq, ak, av, iq, ik, iw, mx, mv, mi, mf, mo, ga, gb = parts
    z = lambda n: jnp.zeros((w_in.shape[0], n), w_in.dtype)
    cols = [aq, iq, ak[:, :64], z(64), ak[:, 64:], z(64), ik, z(64), av,
            iw, mi, mf, z(G_WIDTH - 16),
            mx, mv, mo, ga, gb]
    return jnp.concatenate(cols, axis=1).astype(BF16)


def kernel(x, ffn1_norm, ffn1_w_gate, ffn1_w_up, ffn1_w_down, mix_norm, w_in, q_norm, k_norm,
           idx_k_norm, conv_w, conv_b, w_mq, w_mk, b_i, b_f, m_head_norm, w_proj_a, w_proj_b,
           w_out, ffn2_norm, ffn2_w_gate, ffn2_w_up, ffn2_w_down):
    b, s, _ = x.shape
    m = b * s
    nb = s // Q_BLOCK
    nh = MLSTM_HEADS
    x2d = x.reshape(m, D_MODEL)
    for l in range(ffn1_norm.shape[0]):
        x2d = _ffn(x2d, ffn1_norm[l], ffn1_w_gate[l].astype(BF16), ffn1_w_up[l].astype(BF16),
                   ffn1_w_down[l].astype(BF16))

        gq = (jnp.tile(q_norm[l], ATT_HEADS) * (ATT_HEAD_DIM ** -0.5 * LOG2_E)).reshape(1, 512)
        z64 = jnp.zeros((ATT_HEAD_DIM,), F32)
        one64 = z64.at[0].set(1.0)
        gk = jnp.concatenate([k_norm[l], z64] * ATT_KV_HEADS + [idx_k_norm[l], z64]).reshape(1, -1)
        kone = jnp.concatenate([z64, one64] * ATT_KV_HEADS + [z64, z64]).reshape(1, -1)
        a2d, g2d, mx2d, mvo2d, t2d = _proj(x2d, mix_norm[l], _arrange_w_in(w_in[l]),
                                   _block_diag_ones(512, ATT_HEAD_DIM),
                                   _block_diag_ones(COL_V - COL_K, ATT_HEAD_DIM), gq, gk, kone)

        a3 = a2d.reshape(b, s, A_WIDTH)
        vt = _v_transposed(a3[:, :, COL_V:COL_V + ATT_KV_HEADS * ATT_HEAD_DIM])
        g3 = g2d.reshape(b, s, G_WIDTH)
        iwt = jnp.swapaxes(g3[:, :, 0:IDX_HEADS].reshape(b, nb, Q_BLOCK, IDX_HEADS), 2, 3)
        qcap = (jnp.max(jnp.abs(gq)) * (ATT_HEAD_DIM ** 0.5) * BOUND_SLACK).reshape(1)
        ya = _dsa(a3, vt, iwt, qcap)

        gt = jnp.swapaxes(g3[:, :, IDX_HEADS:IDX_HEADS + 2 * nh], 1, 2)
        bias = jnp.concatenate([b_i[l], b_f[l]])
        yb = _mlstm(mx2d.reshape(b, s, MIX_B), mvo2d.reshape(b, s, 2 * MIX_B), gt,
                    conv_w[l], conv_b[l].reshape(1, MIX_B),
                    w_mq[l].astype(BF16), jnp.swapaxes(w_mk[l], 1, 2).astype(BF16),
                    bias.reshape(2 * nh, 1), m_head_norm[l].reshape(1, MIX_B))

        x2d = _merge_ffn(x2d, ya.reshape(m, MIX_A), yb.reshape(m, MIX_B), t2d,
                         w_proj_a[l].astype(BF16), w_proj_b[l].astype(BF16), w_out[l].astype(BF16),
                         ffn2_norm[l], ffn2_w_gate[l].astype(BF16), ffn2_w_up[l].astype(BF16),
                         ffn2_w_down[l].astype(BF16))
    return x2d.reshape(b, s, D_MODEL)
```

```python
import functools

import jax
import jax.numpy as jnp
from jax import lax
from jax.experimental import pallas as pl
from jax.experimental.pallas import tpu as pltpu

F32 = jnp.float32
BF16 = jnp.bfloat16

D_MODEL = 1024
D_FF = 2816
EPS = 1e-6
ATT_HEADS = 8
ATT_KV_HEADS = 2
ATT_HEAD_DIM = 64
IDX_HEADS = 8
IDX_HEAD_DIM = 64
TOPK_MAX = 256
Q_BLOCK = 128
CHUNK = 64
MLSTM_HEADS = 4
MLSTM_QK_DIM = 64
MLSTM_V_DIM = 128
CONV_WIDTH = 4
MIX_A = ATT_HEADS * ATT_HEAD_DIM
MIX_B = MLSTM_HEADS * MLSTM_V_DIM

LANES = 128
VMEM_LIMIT = 56 * 1024 * 1024
INT_MIN = -2147483648
NEG_BIG = -1e30
LOG2_E = 1.4426950408889634

COL_Q, COL_IQ, COL_K, COL_IK, COL_V = 0, 512, 1024, 1280, 1408
K_EXT = 128
A_WIDTH = 1536
G_WIDTH = 128
M_WIDTH = 3 * MIX_B
T_WIDTH = 2 * D_MODEL
W_TOTAL = A_WIDTH + G_WIDTH + M_WIDTH + T_WIDTH


def _rms(x, g):
    return x * lax.rsqrt(jnp.mean(x * x, axis=-1, keepdims=True) + EPS) * g


def _dot(a, b):
    return jnp.dot(a, b, preferred_element_type=F32)


def _dot_nt(a, b):
    return lax.dot_general(a, b, (((1,), (1,)), ((), ())), preferred_element_type=F32)


FFN_TM = 512
FFN_TF = 2816


def _ffn_kernel(x_ref, g_ref, wg_ref, wu_ref, wd_ref, o_ref):
    _half_ffn(x_ref[...], g_ref, wg_ref, wu_ref, wd_ref, o_ref)


def _half_ffn(x, g_ref, wg_ref, wu_ref, wd_ref, o_ref):
    h = _rms(x, g_ref[...]).astype(BF16)
    acc = None
    for c in range(D_FF // FFN_TF):
        cols = slice(c * FFN_TF, (c + 1) * FFN_TF)
        a = _dot(h, wg_ref[:, cols])
        u = _dot(h, wu_ref[:, cols])
        act = (a * jax.nn.sigmoid(a) * u).astype(BF16)
        part = _dot(act, wd_ref[cols, :])
        acc = part if acc is None else acc + part
    o_ref[...] = x + 0.5 * acc


def _ffn(x2d, g, wg, wu, wd):
    m = x2d.shape[0]
    const = lambda i: (0, 0)
    resident = lambda shape: pl.BlockSpec(shape, const, pipeline_mode=pl.Buffered(1))
    return pl.pallas_call(
        _ffn_kernel,
        grid=(m // FFN_TM,),
        in_specs=[
            pl.BlockSpec((FFN_TM, D_MODEL), lambda i: (i, 0)),
            pl.BlockSpec((1, D_MODEL), const),
            resident((D_MODEL, D_FF)),
            resident((D_MODEL, D_FF)),
            resident((D_FF, D_MODEL)),
        ],
        out_specs=pl.BlockSpec((FFN_TM, D_MODEL), lambda i: (i, 0)),
        out_shape=jax.ShapeDtypeStruct((m, D_MODEL), F32),
        compiler_params=pltpu.CompilerParams(
            dimension_semantics=("parallel",), vmem_limit_bytes=VMEM_LIMIT),
        name="ffn",
    )(x2d, g.reshape(1, D_MODEL), wg, wu, wd)


PROJ_TM = 512


def _proj_kernel(x_ref, g_ref, w_ref, bdq_ref, bdk_ref, gq_ref, gk_ref, kone_ref,
                 a_ref, g_out_ref, mx_ref, mvo_ref, t_ref):
    h = _rms(x_ref[...], g_ref[...]).astype(BF16)

    def head_norm(p, bd_ref, gain_ref):
        ss = _dot((p * p).astype(BF16), bd_ref[...]) * (1.0 / ATT_HEAD_DIM)
        return p * lax.rsqrt(ss + EPS) * gain_ref[...]

    pq = _dot(h, w_ref[:, COL_Q:COL_IQ])
    a_ref[:, COL_Q:COL_IQ] = head_norm(pq, bdq_ref, gq_ref).astype(BF16)
    a_ref[:, COL_IQ:COL_K] = _dot(h, w_ref[:, COL_IQ:COL_K]).astype(BF16)
    pk = _dot(h, w_ref[:, COL_K:COL_V])
    a_ref[:, COL_K:COL_V] = (head_norm(pk, bdk_ref, gk_ref) + kone_ref[...]).astype(BF16)
    a_ref[:, COL_V:A_WIDTH] = _dot(h, w_ref[:, COL_V:A_WIDTH]).astype(BF16)
    o = A_WIDTH
    g_out_ref[...] = _dot(h, w_ref[:, o:o + G_WIDTH])
    o += G_WIDTH
    mx_ref[...] = _dot(h, w_ref[:, o:o + MIX_B])
    o += MIX_B
    for c in range(2):
        mvo_ref[:, c * MIX_B:(c + 1) * MIX_B] = _dot(h, w_ref[:, o + c * MIX_B:o + (c + 1) * MIX_B]).astype(BF16)
    o += 2 * MIX_B
    for c in range(2):
        t_ref[:, c * D_MODEL:(c + 1) * D_MODEL] = _dot(
            h, w_ref[:, o + c * D_MODEL:o + (c + 1) * D_MODEL]).astype(BF16)


def _proj(x2d, g, w_all, bdq, bdk, gq, gk, kone):
    m = x2d.shape[0]
    const = lambda i: (0, 0)
    row = lambda i: (i, 0)
    return pl.pallas_call(
        _proj_kernel,
        grid=(m // PROJ_TM,),
        in_specs=[
            pl.BlockSpec((PROJ_TM, D_MODEL), row),
            pl.BlockSpec((1, D_MODEL), const),
            pl.BlockSpec((D_MODEL, W_TOTAL), const, pipeline_mode=pl.Buffered(1)),
            pl.BlockSpec((512, 512), const),
            pl.BlockSpec((COL_V - COL_K, COL_V - COL_K), const),
            pl.BlockSpec((1, 512), const),
            pl.BlockSpec((1, COL_V - COL_K), const),
            pl.BlockSpec((1, COL_V - COL_K), const),
        ],
        out_specs=[
            pl.BlockSpec((PROJ_TM, A_WIDTH), row),
            pl.BlockSpec((PROJ_TM, G_WIDTH), row),
            pl.BlockSpec((PROJ_TM, MIX_B), row),
            pl.BlockSpec((PROJ_TM, 2 * MIX_B), row),
            pl.BlockSpec((PROJ_TM, T_WIDTH), row),
        ],
        out_shape=[
            jax.ShapeDtypeStruct((m, A_WIDTH), BF16),
            jax.ShapeDtypeStruct((m, G_WIDTH), F32),
            jax.ShapeDtypeStruct((m, MIX_B), F32),
            jax.ShapeDtypeStruct((m, 2 * MIX_B), BF16),
            jax.ShapeDtypeStruct((m, T_WIDTH), BF16),
        ],
        compiler_params=pltpu.CompilerParams(
            dimension_semantics=("parallel",), vmem_limit_bytes=VMEM_LIMIT),
        name="proj",
    )(x2d, g.reshape(1, D_MODEL), w_all, bdq, bdk, gq, gk, kone)


DSA_KB = 512
DSA_KA = 256
BOUND_LIMIT = 60.0
BOUND_SLACK = 1.01
DSA_VROWS = 80

BIT_GROUP = 256
SEARCH_CHUNK = 1024
PLANE_PAD_ROWS = 8


def _bit_transpose(words):
    a = list(words)
    j, m = 16, 0x0000FFFF
    while j:
        k = 0
        while k < 32:
            t = (a[k] ^ (a[k + j] >> j)) & m
            a[k] = a[k] ^ t
            a[k + j] = a[k + j] ^ (t << j)
            k = (k + j + 1) & ~j
        j >>= 1
        m = (m ^ (m << j)) & 0xFFFFFFFF
    return a


def _heads_to_lanes(x, n_heads, dh):
    xf = x.astype(F32)
    return jnp.concatenate([xf[:, h * dh:(h + 1) * dh].T for h in range(n_heads)], axis=1).astype(BF16)


def _dsa_kernel(qcap_ref, q_ref, iq_ref, k_ref, ik_ref, vt_ref, iwt_ref, o_ref,
                keys_ref, planes_ref, alive_ref, la_ref, lb_ref, acc_ref, sa_ref, sb_ref, thr_ref, stab_ref,
                *, seq, k_sel, idx_scale):
    kb_sz = DSA_KB
    n_blocks = seq // Q_BLOCK
    step = pl.program_id(1)
    has_cur = step < n_blocks
    has_prev = step >= 1
    cur = jnp.minimum(step, n_blocks - 1)
    prev = jnp.maximum(step - 1, 0)
    cur_buf = cur % 2
    prev_buf = prev % 2
    nkb = (cur * Q_BLOCK + Q_BLOCK + kb_sz - 1) // kb_sz
    nkb_prev = (prev * Q_BLOCK + Q_BLOCK + kb_sz - 1) // kb_sz
    lane = lax.broadcasted_iota(jnp.int32, (1, LANES), 1)
    limit = cur * Q_BLOCK + jnp.where(lane < CHUNK, CHUNK, 2 * CHUNK)
    row_iota = lax.broadcasted_iota(jnp.int32, (kb_sz, LANES), 0)

    iq_t = _heads_to_lanes(iq_ref[...], IDX_HEADS, IDX_HEAD_DIM)
    w = iwt_ref[...]

    def logits_of(blk):
        off = pl.multiple_of(blk * kb_sz, kb_sz)
        return _dot(ik_ref[pl.ds(off, kb_sz), :][:, :IDX_HEAD_DIM], iq_t)

    def keys_from(l_ref, blk, masked):
        off = pl.multiple_of(blk * kb_sz, kb_sz)
        sc = jnp.zeros((kb_sz, LANES), F32)
        for h in range(IDX_HEADS):
            sc = sc + jnp.maximum(l_ref[:, h * LANES:(h + 1) * LANES], 0.0) * w[h:h + 1, :]
        sc = sc * idx_scale
        bits = pltpu.bitcast(sc, jnp.int32)
        key = bits ^ ((bits >> 31) & 0x7FFFFFFF)
        key = jnp.where(sc == 0.0, 0, key)
        if masked:
            key = jnp.where(off + row_iota < limit, key, INT_MIN)
        keys_ref[cur_buf, pl.ds(off, kb_sz), :] = key
        ukey = key ^ INT_MIN
        for grp in range(kb_sz // BIT_GROUP):
            base = grp * BIT_GROUP
            planes = _bit_transpose([ukey[base + 8 * i:base + 8 * i + 8, :] for i in range(32)])
            row = pl.multiple_of((off + base) // 32, 8)
            for p in range(32):
                planes_ref[p, pl.ds(row, 8), :] = planes[p]

    n_pairs = (nkb + 1) // 2

    def score_pair(j, masked):
        lb_ref[...] = logits_of(2 * j + 1)
        keys_from(la_ref, 2 * j, masked)
        la_ref[...] = logits_of(jnp.minimum(2 * j + 2, 2 * n_pairs - 1))
        keys_from(lb_ref, 2 * j + 1, masked)

    def last_pair():
        j = n_pairs - 1

        @pl.when(nkb % 2 == 0)
        def _():
            score_pair(j, True)

        @pl.when(nkb % 2 == 1)
        def _():
            keys_from(la_ref, 2 * j, True)
            off = pl.multiple_of((2 * j + 1) * kb_sz, kb_sz)
            keys_ref[cur_buf, pl.ds(off, kb_sz), :] = jnp.full((kb_sz, LANES), INT_MIN, jnp.int32)
            row = pl.multiple_of(off // 32, kb_sz // 32)
            for p in range(32):
                planes_ref[p, pl.ds(row, kb_sz // 32), :] = jnp.zeros((kb_sz // 32, LANES), jnp.int32)

    def count(pred):
        def body(kb, c):
            off = pl.multiple_of(kb * kb_sz, kb_sz)
            ind = jnp.where(pred(keys_ref[cur_buf, pl.ds(off, kb_sz), :], off + row_iota), 1, 0)
            return c + jnp.sum(ind.reshape(kb_sz // 8, 8, LANES), axis=0)
        c8 = lax.fori_loop(0, nkb, body, jnp.zeros((8, LANES), jnp.int32))
        return jnp.sum(c8, axis=0, keepdims=True)

    def select_threshold():
        zeros = jnp.zeros((1, LANES), jnp.int32)
        ch_rows = SEARCH_CHUNK // 32
        n_ch = (nkb + 1) // 2
        zeros8 = jnp.zeros((8, LANES), jnp.int32)

        def ones_in(x):
            return jnp.sum(lax.population_count(x).reshape(ch_rows // 8, 8, LANES), axis=0)

        def over_chunks(body, n_out):
            tot = lax.fori_loop(0, n_ch, lambda c, cnt: tuple(a + b_ for a, b_ in zip(cnt, body(c))),
                                (zeros8,) * n_out)
            return [jnp.sum(t, axis=0, keepdims=True) for t in tot]

        def plane(p, row):
            return planes_ref[p, pl.ds(row, ch_rows), :]

        def digit_counts(alive, p_hi, p_lo):
            x1 = alive & p_hi
            return ones_in(x1), ones_in(x1 & p_lo), ones_in((alive ^ x1) & p_lo)

        def decide(q, t_u, above, counts):
            c1, c11, c01 = counts
            hi = above + c1 >= k_sel
            lo = jnp.where(hi, above + c11, above + c1 + c01) >= k_sel
            above = jnp.where(hi, jnp.where(lo, above, above + c11),
                              jnp.where(lo, above + c1, above + c1 + c01))
            t_u = (t_u | jnp.where(hi, jnp.left_shift(jnp.int32(1), 31 - 2 * q), 0)
                   | jnp.where(lo, jnp.left_shift(jnp.int32(1), 30 - 2 * q), 0))
            return t_u, above, jnp.where(hi, 0, -1), jnp.where(lo, 0, -1)

        def first_chunk(c):
            row = pl.multiple_of(c * ch_rows, ch_rows)
            everything = jnp.full((ch_rows, LANES), -1, jnp.int32)
            alive_ref[pl.ds(row, ch_rows), :] = everything
            return digit_counts(everything, plane(0, row), plane(1, row))

        def radix_pass(q, carry):
            t_u, above, flip_hi, flip_lo = carry

            def chunk(c):
                row = pl.multiple_of(c * ch_rows, ch_rows)
                alive = (alive_ref[pl.ds(row, ch_rows), :]
                         & (plane(2 * q - 2, row) ^ flip_hi) & (plane(2 * q - 1, row) ^ flip_lo))
                alive_ref[pl.ds(row, ch_rows), :] = alive
                return digit_counts(alive, plane(2 * q, row), plane(2 * q + 1, row))

            return decide(q, t_u, above, over_chunks(chunk, 3))

        carry = decide(0, zeros, zeros, over_chunks(first_chunk, 3))
        t_u, above, flip_hi, flip_lo = lax.fori_loop(1, 16, radix_pass, carry)

        def last_chunk(c):
            row = pl.multiple_of(c * ch_rows, ch_rows)
            return (ones_in(alive_ref[pl.ds(row, ch_rows), :]
                            & (plane(30, row) ^ flip_hi) & (plane(31, row) ^ flip_lo)),)

        c_eq = over_chunks(last_chunk, 1)[0]
        few = t_u == 0
        thr = jnp.maximum(t_u ^ INT_MIN, INT_MIN + 1)
        thr_ref[cur_buf] = thr
        need = k_sel - above
        excess = jnp.where(few, 0, c_eq - need)
        pos_bits = seq.bit_length() - 1

        @pl.when(jnp.max(excess) > 0)
        def _():
            def jbit(i, cut):
                cand = cut | jnp.left_shift(jnp.int32(1), pos_bits - 1 - i)
                c = count(lambda kk, pos: (kk == thr) & (pos < cand))
                return jnp.where(c <= need, cand, cut)
            cut = lax.fori_loop(0, pos_bits, jbit, zeros)
            cut = jnp.where(excess > 0, cut, seq)

            def drop_block(kb, _):
                off = pl.multiple_of(kb * kb_sz, kb_sz)
                kk = keys_ref[cur_buf, pl.ds(off, kb_sz), :]
                keys_ref[cur_buf, pl.ds(off, kb_sz), :] = jnp.where(
                    (kk == thr) & (off + row_iota >= cut), INT_MIN, kk)
                return 0

            lax.fori_loop(0, nkb, drop_block, 0)

    reps = ATT_HEADS // ATT_KV_HEADS
    width = reps * LANES
    q_t = _heads_to_lanes(q_ref[...], ATT_HEADS, ATT_HEAD_DIM)

    @pl.when(step == 0)
    def _():
        def kmax_block(kb, best):
            off = pl.multiple_of(kb * kb_sz, kb_sz)
            kf = k_ref[pl.ds(off, kb_sz), :].astype(F32)
            n2 = [jnp.max(jnp.sum(jnp.square(kf[:, g * K_EXT:g * K_EXT + ATT_HEAD_DIM]), axis=1, keepdims=True))
                  for g in range(ATT_KV_HEADS)]
            return tuple(jnp.maximum(b_, n_) for b_, n_ in zip(best, n2))
        best = lax.fori_loop(0, seq // kb_sz, kmax_block, (jnp.float32(0.0),) * ATT_KV_HEADS)
        worst = jnp.float32(0.0)
        for g in range(ATT_KV_HEADS):
            stab_ref[g] = jnp.sqrt(best[g]) * BOUND_SLACK
            worst = jnp.maximum(worst, stab_ref[g] * qcap_ref[0])
        stab_ref[ATT_KV_HEADS] = jnp.where(worst < BOUND_LIMIT, 1.0, 0.0)

    bounded = stab_ref[ATT_KV_HEADS] > 0.5
    qf = q_t.astype(F32)
    q_len = jnp.sqrt(jnp.sum(qf * qf, axis=0, keepdims=True))
    rhs = []
    for g in range(ATT_KV_HEADS):
        cols = slice(g * width, (g + 1) * width)
        neg_bound = jnp.where(bounded, -q_len[:, cols] * stab_ref[g], 0.0)
        rhs.append(jnp.concatenate(
            [qf[:, cols], neg_bound, jnp.zeros((K_EXT - ATT_HEAD_DIM - 1, width), F32)], axis=0).astype(BF16))

    ka = DSA_KA

    def masked_scores(blk):
        off = pl.multiple_of(blk * ka, ka)
        bias = jnp.where(keys_ref[prev_buf, pl.ds(off, ka), :] >= thr_prev, 0.0, NEG_BIG)
        bias = jnp.concatenate([bias] * reps, axis=1)
        kblk = k_ref[pl.ds(off, ka), :]
        return jnp.concatenate(
            [_dot(kblk[:, g * K_EXT:(g + 1) * K_EXT], rhs[g]) + bias for g in range(ATT_KV_HEADS)],
            axis=1)

    def consume_bounded(s_ref, blk):
        off = pl.multiple_of(blk * ka, ka)
        p = jnp.exp2(s_ref[...]).astype(BF16)
        for g in range(ATT_KV_HEADS):
            acc_ref[g] += _dot(vt_ref[g, :, pl.ds(off, ka)], p[:, g * width:(g + 1) * width])

    def consume_online(s_ref, blk, m_old):
        off = pl.multiple_of(blk * ka, ka)
        s = s_ref[...]
        m_new = jnp.maximum(m_old, jnp.max(s, axis=0, keepdims=True))
        alpha = jnp.exp2(m_old - m_new)
        p = jnp.exp2(s - m_new).astype(BF16)
        for g in range(ATT_KV_HEADS):
            vt = vt_ref[g, :, pl.ds(off, ka)]
            acc_ref[g] = (alpha[:, g * width:(g + 1) * width] * acc_ref[g]
                          + _dot(vt, p[:, g * width:(g + 1) * width]))
        return m_new

    n_sub = nkb_prev * (kb_sz // ka)
    thr_prev = thr_ref[prev_buf]

    def att_pair_bounded(j, prefetch=True):
        sb_ref[...] = masked_scores(2 * j + 1)
        consume_bounded(sa_ref, 2 * j)
        if prefetch:
            sa_ref[...] = masked_scores(2 * j + 2)
        consume_bounded(sb_ref, 2 * j + 1)

    @pl.when(has_cur)
    def _():
        la_ref[...] = logits_of(0)

    @pl.when(has_prev)
    def _():
        acc_ref[...] = jnp.zeros_like(acc_ref)
        sa_ref[...] = masked_scores(0)

    n_att = n_sub // 2
    n_fused = jnp.where(has_cur & has_prev & bounded, jnp.minimum(n_pairs - 1, (n_att - 1) // 2), 0)

    def fused(j, c):
        score_pair(j, False)
        att_pair_bounded(2 * j)
        att_pair_bounded(2 * j + 1)
        return c

    lax.fori_loop(0, n_fused, fused, 0)

    @pl.when(has_cur)
    def _():
        lax.fori_loop(n_fused, n_pairs - 1, lambda j, c: (score_pair(j, False), c)[1], 0)
        last_pair()
        select_threshold()

    @pl.when(has_prev & bounded)
    def _():
        lax.fori_loop(2 * n_fused, n_att - 1, lambda j, c: (att_pair_bounded(j), c)[1], 0)
        att_pair_bounded(n_att - 1, prefetch=False)

    @pl.when(has_prev & jnp.logical_not(bounded))
    def _():
        def att_pair(j, m_run):
            sb_ref[...] = masked_scores(2 * j + 1)
            m_run = consume_online(sa_ref, 2 * j, m_run)
            sa_ref[...] = masked_scores(jnp.minimum(2 * j + 2, n_sub - 1))
            return consume_online(sb_ref, 2 * j + 1, m_run)
        lax.fori_loop(0, n_sub // 2, att_pair, jnp.full((1, ATT_HEADS * LANES), NEG_BIG, F32))

    @pl.when(has_prev)
    def _():
        pieces = []
        for g in range(ATT_KV_HEADS):
            a = acc_ref[g]
            o = a[0:ATT_HEAD_DIM, :] / a[ATT_HEAD_DIM:ATT_HEAD_DIM + 1, :]
            pieces += [o[:, r * LANES:(r + 1) * LANES] for r in range(reps)]
        o_ref[...] = jnp.concatenate(pieces, axis=0).T.astype(o_ref.dtype)


def _v_transposed(v):
    b, s, _ = v.shape
    vt = jnp.swapaxes(v.reshape(b, s, ATT_KV_HEADS, ATT_HEAD_DIM), 1, 3)
    vt = jnp.swapaxes(vt, 1, 2)
    ones = jnp.ones((b, ATT_KV_HEADS, 1, s), v.dtype)
    pad = jnp.zeros((b, ATT_KV_HEADS, DSA_VROWS - ATT_HEAD_DIM - 1, s), v.dtype)
    return jnp.concatenate([vt, ones, pad], axis=2)


def _dsa(a3, vt, iwt, qcap):
    b, s, _ = a3.shape
    nb = s // Q_BLOCK
    k_sel = min(TOPK_MAX, s // 4)
    idx_scale = (IDX_HEAD_DIM ** -0.5) * (IDX_HEADS ** -0.5)
    kern = functools.partial(_dsa_kernel, seq=s, k_sel=k_sel, idx_scale=idx_scale)
    return pl.pallas_call(
        kern,
        grid=(b, nb + 1),
        in_specs=[
            pl.BlockSpec(memory_space=pltpu.SMEM),
            pl.BlockSpec((None, Q_BLOCK, 512), lambda bb, i: (bb, jnp.maximum(i - 1, 0), COL_Q // 512)),
            pl.BlockSpec((None, Q_BLOCK, 512), lambda bb, i: (bb, jnp.minimum(i, nb - 1), COL_IQ // 512)),
            pl.BlockSpec((None, s, ATT_KV_HEADS * K_EXT), lambda bb, i: (bb, 0, COL_K // (ATT_KV_HEADS * K_EXT))),
            pl.BlockSpec((None, s, LANES), lambda bb, i: (bb, 0, COL_IK // LANES)),
            pl.BlockSpec((None, ATT_KV_HEADS, DSA_VROWS, s), lambda bb, i: (bb, 0, 0, 0)),
            pl.BlockSpec((None, None, IDX_HEADS, LANES), lambda bb, i: (bb, jnp.minimum(i, nb - 1), 0, 0)),
        ],
        out_specs=pl.BlockSpec((None, Q_BLOCK, MIX_A), lambda bb, i: (bb, jnp.maximum(i - 1, 0), 0)),
        out_shape=jax.ShapeDtypeStruct((b, s, MIX_A), BF16),
        scratch_shapes=[
            pltpu.VMEM((2, s, LANES), jnp.int32),
            pltpu.VMEM((32, s // 32 + PLANE_PAD_ROWS, LANES), jnp.int32),
            pltpu.VMEM((s // 32, LANES), jnp.int32),
            pltpu.VMEM((DSA_KB, IDX_HEADS * LANES), F32),
            pltpu.VMEM((DSA_KB, IDX_HEADS * LANES), F32),
            pltpu.VMEM((ATT_KV_HEADS, DSA_VROWS, (ATT_HEADS // ATT_KV_HEADS) * LANES), F32),
            pltpu.VMEM((DSA_KA, ATT_HEADS * LANES), F32),
            pltpu.VMEM((DSA_KA, ATT_HEADS * LANES), F32),
            pltpu.VMEM((2, 1, LANES), jnp.int32),
            pltpu.SMEM((ATT_KV_HEADS + 1,), F32),
        ],
        compiler_params=pltpu.CompilerParams(
            dimension_semantics=("parallel", "arbitrary"), vmem_limit_bytes=VMEM_LIMIT),
        name="dsa",
    )(qcap, a3, a3, a3, a3, vt, iwt)


ML_L = 128
ML_TS = 1024
ML_PAD = 8


def _mlstm_kernel(mx_ref, mv_ref, mo_ref, gr_ref, cw_ref, cb_ref, wq_ref, wkt_ref, brow_ref, hg_ref, o_ref,
                  xbuf_ref, q_ref, kt_ref, s_ref, m_ref):
    t_idx = pl.program_id(1)
    nh, dk, dv, L = MLSTM_HEADS, MLSTM_QK_DIM, MLSTM_V_DIM, ML_L

    @pl.when(t_idx == 0)
    def _():
        xbuf_ref[0:ML_PAD, :] = jnp.zeros((ML_PAD, MIX_B), F32)
        s_ref[...] = jnp.zeros_like(s_ref)
        m_ref[...] = jnp.zeros_like(m_ref)

    xbuf_ref[ML_PAD:ML_PAD + ML_TS, :] = mx_ref[...]
    xc = cb_ref[...] + jnp.zeros((ML_TS, MIX_B), F32)
    for j in range(CONV_WIDTH):
        s0 = ML_PAD - (CONV_WIDTH - 1) + j
        xc = xc + xbuf_ref[s0:s0 + ML_TS, :] * cw_ref[j:j + 1, :]
    xbuf_ref[0:ML_PAD, :] = mx_ref[ML_TS - ML_PAD:ML_TS, :]
    xc = (xc * jax.nn.sigmoid(xc)).astype(BF16)
    for h in range(nh):
        xh = xc[:, h * dv:(h + 1) * dv]
        q_ref[h] = (_dot(xh, wq_ref[h]) * (dk ** -0.5)).astype(BF16)
        kt_ref[h] = _dot_nt(wkt_ref[h], xh)

    gr = gr_ref[...]
    li_r = gr[0:nh, :] + brow_ref[0:nh, :]
    lf_r = jax.nn.log_sigmoid(gr[nh:2 * nh, :] + brow_ref[nh:2 * nh, :])

    ri = lax.broadcasted_iota(jnp.int32, (L, L), 0)
    ci = lax.broadcasted_iota(jnp.int32, (L, L), 1)
    causal = ri >= ci
    ones_v = jnp.ones((L, dv), BF16)
    ones_n = jnp.ones((dv, dv), BF16)

    tr = lax.broadcasted_iota(jnp.int32, (ML_TS, ML_TS), 0)
    tc = lax.broadcasted_iota(jnp.int32, (ML_TS, ML_TS), 1)
    chunk_triu = jnp.where((tr <= tc) & (tr // L == tc // L), 1.0, 0.0).astype(F32)
    b_all = jnp.dot(lf_r, chunk_triu, precision=lax.Precision.HIGHEST, preferred_element_type=F32)
    g_all = li_r - b_all
    in_chunk = lax.broadcasted_iota(jnp.int32, (nh, ML_TS), 1) % L
    run_all = g_all
    for sh in (1, 2, 4, 8, 16, 32, 64):
        run_all = jnp.maximum(run_all, jnp.where(in_chunk >= sh, pltpu.roll(run_all, sh, axis=1), NEG_BIG))

    def to_rows(row):
        return jnp.broadcast_to(row, (L, L)).T

    for c in range(ML_TS // L):
        r0 = c * L
        b_r, g_r, run = b_all[:, r0:r0 + L], g_all[:, r0:r0 + L], run_all[:, r0:r0 + L]
        run_rows = [to_rows(run[h:h + 1, :]) for h in range(nh)]
        b_rows = [to_rows(b_r[h:h + 1, :]) for h in range(nh)]
        m_prev = m_ref[...]
        b_last = b_r[:, L - 1:L]
        wk_r = b_last + g_r
        m_new = jnp.maximum(b_last + m_prev, jnp.max(wk_r, axis=1, keepdims=True))
        ws_r = jnp.exp(wk_r - m_new)
        wc_r = jnp.exp(b_last + m_prev - m_new)
        m_ref[...] = m_new
        outs = []
        for h in range(nh):
            big_m_t = jnp.maximum(run_rows[h], m_prev[h:h + 1, :])
            m_t = big_m_t + b_rows[h]
            decay = jnp.where(causal, jnp.exp(g_r[h:h + 1, :] - big_m_t), 0.0)
            w_int = jnp.exp(m_prev[h:h + 1, :] - big_m_t)
            qh = q_ref[h, r0:r0 + L, :]
            kt = kt_ref[h, :, r0:r0 + L]
            v_ext = jnp.concatenate(
                [mv_ref[r0:r0 + L, h * dv:(h + 1) * dv], ones_v], axis=1)
            wt = _dot(qh, kt.astype(BF16)) * decay
            nd = (_dot(wt.astype(BF16), v_ext)
                  + jnp.concatenate([w_int, w_int], axis=1) * _dot(qh, s_ref[h].astype(BF16)))
            hv = nd[:, 0:dv] / jnp.maximum(jnp.abs(nd[:, dv:2 * dv]), jnp.exp(-m_t))
            ms = _dot((hv * hv).astype(BF16), ones_n) * (1.0 / dv)
            outs.append(hv * lax.rsqrt(ms + EPS) * hg_ref[0:1, h * dv:(h + 1) * dv])
            wc = jnp.broadcast_to(wc_r[h:h + 1, :], (dk, L))
            s_ref[h] = (jnp.concatenate([wc, wc], axis=1) * s_ref[h]
                        + _dot((kt * ws_r[h:h + 1, :]).astype(BF16), v_ext))
        hh = jnp.concatenate(outs, axis=1)
        o_ref[r0:r0 + L, :] = (hh * jax.nn.sigmoid(mo_ref[r0:r0 + L, :].astype(F32))).astype(o_ref.dtype)


def _mlstm(m3, mvo3, gt, conv_w, conv_b, wq, wkt, brow, head_g):
    b, s, _ = m3.shape
    nh = MLSTM_HEADS
    tile = lambda c: pl.BlockSpec((None, ML_TS, MIX_B), lambda bb, t: (bb, t, c))
    const2 = lambda shape: pl.BlockSpec(shape, lambda bb, t: (0, 0))
    const3 = lambda shape: pl.BlockSpec(shape, lambda bb, t: (0, 0, 0))
    return pl.pallas_call(
        _mlstm_kernel,
        grid=(b, s // ML_TS),
        in_specs=[
            tile(0), tile(0), tile(1),
            pl.BlockSpec((None, 2 * nh, ML_TS), lambda bb, t: (bb, 0, t)),
            const2((CONV_WIDTH, MIX_B)), const2((1, MIX_B)),
            const3((nh, MLSTM_V_DIM, MLSTM_QK_DIM)), const3((nh, MLSTM_QK_DIM, MLSTM_V_DIM)),
            const2((2 * nh, 1)), const2((1, MIX_B)),
        ],
        out_specs=pl.BlockSpec((None, ML_TS, MIX_B), lambda bb, t: (bb, t, 0)),
        out_shape=jax.ShapeDtypeStruct((b, s, MIX_B), BF16),
        scratch_shapes=[
            pltpu.VMEM((ML_PAD + ML_TS, MIX_B), F32),
            pltpu.VMEM((nh, ML_TS, MLSTM_QK_DIM), BF16),
            pltpu.VMEM((nh, MLSTM_QK_DIM, ML_TS), F32),
            pltpu.VMEM((nh, MLSTM_QK_DIM, 2 * MLSTM_V_DIM), F32),
            pltpu.VMEM((nh, ML_L), F32),
        ],
        compiler_params=pltpu.CompilerParams(
            dimension_semantics=("parallel", "arbitrary"), vmem_limit_bytes=VMEM_LIMIT),
        name="mlstm",
    )(m3, mvo3, mvo3, gt, conv_w, conv_b, wq, wkt, brow, head_g)


MERGE_TM = 512


def _merge_ffn_kernel(x_ref, ya_ref, yb_ref, ga_ref, gb_ref, wa_ref, wb_ref, wo_ref,
                      g_ref, wg_ref, wu_ref, wd_ref, o_ref):
    merged = (jax.nn.sigmoid(ga_ref[...].astype(F32)) * _dot(ya_ref[...], wa_ref[...])
              + jax.nn.sigmoid(gb_ref[...].astype(F32)) * _dot(yb_ref[...], wb_ref[...]))
    x = x_ref[...] + _dot(merged.astype(BF16), wo_ref[...])
    _half_ffn(x, g_ref, wg_ref, wu_ref, wd_ref, o_ref)


def _merge_ffn(x2d, ya, yb, gates, wa, wb, wo, g, wg, wu, wd):
    m = x2d.shape[0]
    row = lambda i: (i, 0)
    const = lambda i: (0, 0)
    resident = lambda shape: pl.BlockSpec(shape, const, pipeline_mode=pl.Buffered(1))
    return pl.pallas_call(
        _merge_ffn_kernel,
        grid=(m // MERGE_TM,),
        in_specs=[
            pl.BlockSpec((MERGE_TM, D_MODEL), row),
            pl.BlockSpec((MERGE_TM, MIX_A), row),
            pl.BlockSpec((MERGE_TM, MIX_B), row),
            pl.BlockSpec((MERGE_TM, D_MODEL), lambda i: (i, 0)),
            pl.BlockSpec((MERGE_TM, D_MODEL), lambda i: (i, 1)),
            resident((MIX_A, D_MODEL)),
            resident((MIX_B, D_MODEL)),
            resident((D_MODEL, D_MODEL)),
            pl.BlockSpec((1, D_MODEL), const),
            resident((D_MODEL, D_FF)),
            resident((D_MODEL, D_FF)),
            resident((D_FF, D_MODEL)),
        ],
        out_specs=pl.BlockSpec((MERGE_TM, D_MODEL), row),
        out_shape=jax.ShapeDtypeStruct((m, D_MODEL), F32),
        compiler_params=pltpu.CompilerParams(
            dimension_semantics=("parallel",), vmem_limit_bytes=VMEM_LIMIT),
        name="merge_ffn",
    )(x2d, ya, yb, gates, gates, wa, wb, wo, g.reshape(1, D_MODEL), wg, wu, wd)


def _block_diag_ones(n, blk):
    r = jnp.arange(n) // blk
    return (r[:, None] == r[None, :]).astype(BF16)


def _arrange_w_in(w_in):
    widths = (512, 128, 128, 512, 64, 8, 512, 512, 4, 4, 512, 1024, 1024)
    parts, off = [], 0
    for wd in widths:
        parts.append(w_in[:, off:off + wd])
        off += wd
    aq, ak, av, iq, ik, iw, mx, mv, mi, mf, mo, ga, gb = parts
    z = lambda n: jnp.zeros((w_in.shape[0], n), w_in.dtype)
    cols = [aq, iq, ak[:, :64], z(64), ak[:, 64:], z(64), ik, z(64), av,
            iw, mi, mf, z(G_WIDTH - 16),
            mx, mv, mo, ga, gb]
    return jnp.concatenate(cols, axis=1).astype(BF16)


def kernel(x, ffn1_norm, ffn1_w_gate, ffn1_w_up, ffn1_w_down, mix_norm, w_in, q_norm, k_norm,
           idx_k_norm, conv_w, conv_b, w_mq, w_mk, b_i, b_f, m_head_norm, w_proj_a, w_proj_b,
           w_out, ffn2_norm, ffn2_w_gate, ffn2_w_up, ffn2_w_down):
    b, s, _ = x.shape
    m = b * s
    nb = s // Q_BLOCK
    nh = MLSTM_HEADS
    x2d = x.reshape(m, D_MODEL)
    for l in range(ffn1_norm.shape[0]):
        x2d = _ffn(x2d, ffn1_norm[l], ffn1_w_gate[l].astype(BF16), ffn1_w_up[l].astype(BF16),
                   ffn1_w_down[l].astype(BF16))

        gq = (jnp.tile(q_norm[l], ATT_HEADS) * (ATT_HEAD_DIM ** -0.5 * LOG2_E)).reshape(1, 512)
        z64 = jnp.zeros((ATT_HEAD_DIM,), F32)
        one64 = z64.at[0].set(1.0)
        gk = jnp.concatenate([k_norm[l], z64] * ATT_KV_HEADS + [idx_k_norm[l], z64]).reshape(1, -1)
        kone = jnp.concatenate([z64, one64] * ATT_KV_HEADS + [z64, z64]).reshape(1, -1)
        a2d, g2d, mx2d, mvo2d, t2d = _proj(x2d, mix_norm[l], _arrange_w_in(w_in[l]),
                                   _block_diag_ones(512, ATT_HEAD_DIM),
                                   _block_diag_ones(COL_V - COL_K, ATT_HEAD_DIM), gq, gk, kone)

        a3 = a2d.reshape(b, s, A_WIDTH)
        vt = _v_transposed(a3[:, :, COL_V:COL_V + ATT_KV_HEADS * ATT_HEAD_DIM])
        g3 = g2d.reshape(b, s, G_WIDTH)
        iwt = jnp.swapaxes(g3[:, :, 0:IDX_HEADS].reshape(b, nb, Q_BLOCK, IDX_HEADS), 2, 3)
        qcap = (jnp.max(jnp.abs(gq)) * (ATT_HEAD_DIM ** 0.5) * BOUND_SLACK).reshape(1)
        ya = _dsa(a3, vt, iwt, qcap)

        gt = jnp.swapaxes(g3[:, :, IDX_HEADS:IDX_HEADS + 2 * nh], 1, 2)
        bias = jnp.concatenate([b_i[l], b_f[l]])
        yb = _mlstm(mx2d.reshape(b, s, MIX_B), mvo2d.reshape(b, s, 2 * MIX_B), gt,
                    conv_w[l], conv_b[l].reshape(1, MIX_B),
                    w_mq[l].astype(BF16), jnp.swapaxes(w_mk[l], 1, 2).astype(BF16),
                    bias.reshape(2 * nh, 1), m_head_norm[l].reshape(1, MIX_B))

        x2d = _merge_ffn(x2d, ya.reshape(m, MIX_A), yb.reshape(m, MIX_B), t2d,
                         w_proj_a[l].astype(BF16), w_proj_b[l].astype(BF16), w_out[l].astype(BF16),
                         ffn2_norm[l], ffn2_w_gate[l].astype(BF16), ffn2_w_up[l].astype(BF16),
                         ffn2_w_down[l].astype(BF16))
    return x2d.reshape(b, s, D_MODEL)
```
